```python
import jax, jax.numpy as jnp
from jax import lax
import numpy as np

D_MODEL = 2048
BATCH = 4
SEQ = 2048
DEPTH = 1
DEC_BATCH = 128
DEC_SEQ = 8
PAST_LEN = 16384
PAGE_SIZE = 128

MIX_WIDTH = D_MODEL
CONV_WIDTH = MIX_WIDTH // 2
CONV_GROUPS = 16
CONV_K = 3
GLA_HEADS = 8
GLA_DV = (MIX_WIDTH - CONV_WIDTH) // GLA_HEADS
GLA_DK = GLA_DV // 2
GLA_RANK = 16
GLA_TAU = 16.0
GLA_CHUNK = 64
N_GROUPS = 4
EXPERTS_PER_GROUP = 8
N_EXPERTS = N_GROUPS * EXPERTS_PER_GROUP
TOP_K = 2
D_EXPERT = D_MODEL // 4
EPS = 1e-6

PROJ_SIZES = (CONV_WIDTH, CONV_WIDTH, CONV_WIDTH,
              GLA_HEADS * GLA_DK, GLA_HEADS * GLA_DK, GLA_HEADS * GLA_DV,
              GLA_HEADS * GLA_DV, GLA_RANK)
PROJ_DIM = sum(PROJ_SIZES)
SPLIT_POINTS = tuple(int(s) for s in np.cumsum(PROJ_SIZES)[:-1])

kernel_name = "hymba_conv_gla_hmoe_adaln_step"


def _rmsnorm(x, g):
    xf = x.astype(jnp.float32)
    xf = xf * lax.rsqrt(jnp.mean(xf * xf, axis=-1, keepdims=True) + EPS)
    return xf.astype(x.dtype) * g


def _short_conv(u, buf, conv_w, conv_b):
    L = u.shape[1]
    full = jnp.concatenate([buf.astype(u.dtype), u], axis=1)
    y = conv_b + sum(conv_w[j] * full[:, j:j + L] for j in range(CONV_K))
    return y, full[:, L:]


def _gla(q, k, v, loga, S0):
    Bn, L, H, DK = q.shape
    DV = v.shape[-1]
    C = GLA_CHUNK if L % GLA_CHUNK == 0 else L
    nC = L // C
    f32 = jnp.float32
    q = q.astype(f32).reshape(Bn, nC, C, H, DK)
    k = k.astype(f32).reshape(Bn, nC, C, H, DK)
    v = v.astype(f32).reshape(Bn, nC, C, H, DV)
    la = loga.astype(f32).reshape(Bn, nC, C, H, DK)
    b = jnp.cumsum(la, axis=2)
    b_last = b[:, :, -1:]
    qt = q * jnp.exp(b) * (DK ** -0.5)
    kt = k * jnp.exp(-b)
    kend = k * jnp.exp(b_last - b)
    mask = jnp.tril(jnp.ones((C, C), dtype=bool))
    scores = jnp.einsum('bnthk,bnshk->bnhts', qt, kt)
    scores = jnp.where(mask, scores, 0.0)
    o_intra = jnp.einsum('bnhts,bnshv->bnthv', scores, v)
    kv = jnp.einsum('bnshk,bnshv->nbhkv', kend, v)
    decay = jnp.exp(jnp.moveaxis(b_last[:, :, 0], 1, 0))

    def step(S, xs):
        d, kvc = xs
        return d[..., None] * S + kvc, S

    S_final, S_starts = lax.scan(step, S0.astype(f32), (decay, kv))
    o_inter = jnp.einsum('bnthk,nbhkv->bnthv', qt, S_starts)
    o = (o_intra + o_inter).reshape(Bn, L, H, DV)
    return o, S_final


def _mixer(u, conv_buf, gla_state, w_in, conv_w, conv_b, w_gate_up, b_gate,
           norm_conv, norm_gla, w_out):
    Bn, L, _ = u.shape
    proj = u @ w_in
    bg, cg, hx, q, k, v, g, a_low = jnp.split(proj, SPLIT_POINTS, axis=-1)
    conv_y, new_buf = _short_conv(cg * hx, conv_buf, conv_w, conv_b)
    y_conv = _rmsnorm(bg * conv_y, norm_conv)
    q = q.reshape(Bn, L, GLA_HEADS, GLA_DK)
    k = k.reshape(Bn, L, GLA_HEADS, GLA_DK)
    v = v.reshape(Bn, L, GLA_HEADS, GLA_DV)
    z = (a_low @ w_gate_up + b_gate).astype(jnp.float32)
    loga = (jax.nn.log_sigmoid(z) / GLA_TAU).reshape(Bn, L, GLA_HEADS, GLA_DK)
    o, new_state = _gla(q, k, v, loga, gla_state)
    o = _rmsnorm(o.astype(u.dtype), norm_gla).reshape(Bn, L, GLA_HEADS * GLA_DV)
    y_gla = o * jax.nn.silu(g)
    out = jnp.concatenate([y_conv, y_gla], axis=-1) @ w_out
    return out, new_buf, new_state


def _moe(h, w_coarse, b_coarse, w_fine, b_fine, w_eg, w_eu, w_ed):
    shape = h.shape
    hf = h.reshape(-1, shape[-1])
    f32 = jnp.float32
    h32 = hf.astype(f32)
    coarse = h32 @ w_coarse.astype(f32) + b_coarse.astype(f32)
    p_grp = jax.nn.softmax(coarse, axis=-1)
    grp = jnp.argmax(coarse, axis=-1)
    p_sel = jnp.take_along_axis(p_grp, grp[:, None], axis=1)
    fine = (h32 @ w_fine.astype(f32) + b_fine.astype(f32)).reshape(-1, N_GROUPS, EXPERTS_PER_GROUP)
    fine_sel = jnp.take_along_axis(fine, grp[:, None, None], axis=1)[:, 0]
    p_exp = jax.nn.softmax(fine_sel, axis=-1)
    top_p, top_i = lax.top_k(p_exp, TOP_K)
    wts = top_p / jnp.sum(top_p, axis=-1, keepdims=True) * p_sel
    eidx = grp[:, None] * EXPERTS_PER_GROUP + top_i
    combine = jnp.sum(jax.nn.one_hot(eidx, N_EXPERTS, dtype=f32) * wts[..., None], axis=1)
    out = jnp.zeros_like(hf)
    for e in range(N_EXPERTS):
        ye = (jax.nn.silu(hf @ w_eg[e]) * (hf @ w_eu[e])) @ w_ed[e]
        out = out + combine[:, e:e + 1].astype(hf.dtype) * ye
    return out.reshape(shape)


def _layer(x, c, conv_buf, gla_state, w_ada, b_ada, norm_mix, w_in, conv_w, conv_b,
           w_gate_up, b_gate, norm_conv, norm_gla, w_out, norm_ffn, w_coarse, b_coarse,
           w_fine, b_fine, w_eg, w_eu, w_ed):
    mod = (jax.nn.silu(c) @ w_ada + b_ada)[:, None, :]
    sh1, sc1, gt1, sh2, sc2, gt2 = jnp.split(mod, 6, axis=-1)
    h = _rmsnorm(x, norm_mix) * (1.0 + sc1) + sh1
    mix, new_buf, new_state = _mixer(h, conv_buf, gla_state, w_in, conv_w, conv_b,
                                     w_gate_up, b_gate, norm_conv, norm_gla, w_out)
    x = x + gt1 * mix
    h2 = _rmsnorm(x, norm_ffn) * (1.0 + sc2) + sh2
    x = x + gt2 * _moe(h2, w_coarse, b_coarse, w_fine, b_fine, w_eg, w_eu, w_ed)
    return x, new_buf, new_state


def setup_inputs(seed: int = 0) -> dict:
    key = jax.random.key(seed)
    ks = jax.random.split(key, 32)
    D = D_MODEL
    nrm = jax.random.normal

    def gain(k, shape):
        return 1.0 + 0.02 * nrm(k, shape, jnp.float32)

    return {
        "x_prompt": nrm(ks[0], (BATCH, SEQ, D), jnp.float32),
        "x_sample": nrm(ks[1], (DEC_BATCH, DEC_SEQ, D), jnp.float32),
        "c_prompt": nrm(ks[2], (BATCH, D), jnp.float32),
        "c_sample": nrm(ks[3], (DEC_BATCH, D), jnp.float32),
        "state_conv": nrm(ks[4], (DEPTH, DEC_BATCH, CONV_K - 1, CONV_WIDTH), jnp.float32),
        "state_gla": 0.5 * nrm(ks[5], (DEPTH, DEC_BATCH, GLA_HEADS, GLA_DK, GLA_DV), jnp.float32),
        "w_ada": nrm(ks[6], (DEPTH, D, 6 * D), jnp.float32) * D ** -0.5,
        "b_ada": 0.02 * nrm(ks[7], (DEPTH, 6 * D), jnp.float32),
        "norm_mix": gain(ks[8], (DEPTH, D)),
        "w_in": nrm(ks[9], (DEPTH, D, PROJ_DIM), jnp.float32) * D ** -0.5,
        "conv_w": nrm(ks[10], (DEPTH, CONV_K, CONV_WIDTH), jnp.float32) * CONV_K ** -0.5,
        "conv_b": 0.02 * nrm(ks[11], (DEPTH, CONV_WIDTH), jnp.float32),
        "w_gate_up": nrm(ks[12], (DEPTH, GLA_RANK, GLA_HEADS * GLA_DK), jnp.float32) * GLA_RANK ** -0.5,
        "b_gate": 0.1 * nrm(ks[13], (DEPTH, GLA_HEADS * GLA_DK), jnp.float32),
        "norm_conv": gain(ks[14], (DEPTH, CONV_WIDTH)),
        "norm_gla": gain(ks[15], (DEPTH, GLA_DV)),
        "w_out": nrm(ks[16], (DEPTH, MIX_WIDTH, D), jnp.float32) * MIX_WIDTH ** -0.5,
        "norm_ffn": gain(ks[17], (DEPTH, D)),
        "w_coarse": nrm(ks[18], (DEPTH, D, N_GROUPS), jnp.float32) * D ** -0.5,
        "b_coarse": 0.01 * nrm(ks[19], (DEPTH, N_GROUPS), jnp.float32),
        "w_fine": nrm(ks[20], (DEPTH, D, N_EXPERTS), jnp.float32) * D ** -0.5,
        "b_fine": 0.01 * nrm(ks[21], (DEPTH, N_EXPERTS), jnp.float32),
        "w_exp_gate": nrm(ks[22], (DEPTH, N_EXPERTS, D, D_EXPERT), jnp.float32) * D ** -0.5,
        "w_exp_up": nrm(ks[23], (DEPTH, N_EXPERTS, D, D_EXPERT), jnp.float32) * D ** -0.5,
        "w_exp_down": nrm(ks[24], (DEPTH, N_EXPERTS, D_EXPERT, D), jnp.float32) * D_EXPERT ** -0.5,
        "norm_final": gain(ks[25], (D,)),
    }


def reference(x_prompt, x_sample, c_prompt, c_sample, state_conv, state_gla,
              w_ada, b_ada, norm_mix, w_in, conv_w, conv_b, w_gate_up, b_gate,
              norm_conv, norm_gla, w_out, norm_ffn, w_coarse, b_coarse, w_fine, b_fine,
              w_exp_gate, w_exp_up, w_exp_down, norm_final):
    xp, xs = x_prompt, x_sample
    conv_p, gla_p, conv_s, gla_s = [], [], [], []
    for l in range(DEPTH):
        lw = (w_ada[l], b_ada[l], norm_mix[l], w_in[l], conv_w[l], conv_b[l], w_gate_up[l],
              b_gate[l], norm_conv[l], norm_gla[l], w_out[l], norm_ffn[l], w_coarse[l],
              b_coarse[l], w_fine[l], b_fine[l], w_exp_gate[l], w_exp_up[l], w_exp_down[l])
        buf0 = jnp.zeros((xp.shape[0], CONV_K - 1, CONV_WIDTH), xp.dtype)
        st0 = jnp.zeros((xp.shape[0], GLA_HEADS, GLA_DK, GLA_DV), jnp.float32)
        xp, nbp, nsp = _layer(xp, c_prompt, buf0, st0, *lw)
        xs, nbs, nss = _layer(xs, c_sample, state_conv[l], state_gla[l], *lw)
        conv_p.append(nbp.astype(state_conv.dtype))
        gla_p.append(nsp.astype(state_gla.dtype))
        conv_s.append(nbs.astype(state_conv.dtype))
        gla_s.append(nss.astype(state_gla.dtype))
    y_prompt = _rmsnorm(xp, norm_final)
    y_sample = _rmsnorm(xs, norm_final)
    new_conv_prompt = jnp.stack(conv_p, axis=0)
    new_gla_prompt = jnp.stack(gla_p, axis=0)
    new_conv_sample = jnp.stack(conv_s, axis=0)
    new_gla_sample = jnp.stack(gla_s, axis=0)
    return (y_prompt, y_sample, new_conv_prompt, new_gla_prompt, new_conv_sample, new_gla_sample)
```

```python
import functools

import jax
import jax.numpy as jnp
from jax import lax
from jax.experimental import pallas as pl
from jax.experimental.pallas import tpu as pltpu

F32 = jnp.float32
BF16 = jnp.bfloat16
I32 = jnp.int32

D = 2048
CW = 1024
H = 8
DK = 64
DV = 128
HK = H * DK
HV = H * DV
RANK = 16
TAU = 16.0
N_GROUPS = 4
EPG = 8
NE = N_GROUPS * EPG
DE = 512
EPS = 1e-6

B_P, L_P = 4, 2048
B_S, L_S = 128, 8
T_P = B_P * L_P
T_S = B_S * L_S
T = T_P + T_S

TM = 256
CH = 64
N_TILES = T // TM
N_TILES_P = T_P // TM
TILES_PER_SEQ = L_P // TM
SEQ_PER_TILE = TM // L_S
SEQ_PER_CHUNK = CH // L_S

PROJ_MAIN = 3 * CW + 2 * HK + 2 * HV
PROJ_PAD = PROJ_MAIN + 128
C_ROWS = 136
P_ROW_BLOCK = B_S // 8
R_ROWS = 40

E_TILE = 256
E_TILES_MAX = (2 * T) // E_TILE + NE
VMEM_LIMIT = 60 * 1024 * 1024


def _cparams(n_axes=1, vmem=VMEM_LIMIT):
    return pltpu.CompilerParams(dimension_semantics=("arbitrary",) * n_axes, vmem_limit_bytes=vmem)


def _rms(x, g):
    return x * lax.rsqrt(jnp.mean(x * x, axis=-1, keepdims=True) + EPS) * g


def _sigmoid(x):
    return 1.0 / (1.0 + jnp.exp(-x))


def _silu(x):
    return x * _sigmoid(x)


def _log_sigmoid(x):
    return jnp.minimum(x, 0.0) - jnp.log1p(jnp.exp(-jnp.abs(x)))


def _expand_rows(ref, n, reps):
    return jnp.concatenate([jnp.broadcast_to(ref[j:j + 1, :], (reps, ref.shape[-1])) for j in range(n)], axis=0)


def _mod_rows(ref, prompt):
    if prompt:
        b = pl.program_id(0) // TILES_PER_SEQ
        return ref[pl.ds(b, 1), :]
    return _expand_rows(ref, SEQ_PER_TILE, L_S)


def _mod_spec(col, prompt):
    if prompt:
        return pl.BlockSpec((8, D), lambda i: (P_ROW_BLOCK, col))
    return pl.BlockSpec((SEQ_PER_TILE, D), lambda i: (i, col))


def _const_spec(shape):
    zeros = (0,) * len(shape)
    return pl.BlockSpec(shape, lambda *_: zeros, pipeline_mode=pl.Buffered(1))


def _ada_kernel(c_ref, w_ref, b_ref, o_ref):
    s = _silu(c_ref[...]).astype(BF16)
    o_ref[...] = jnp.dot(s, w_ref[...].astype(BF16), preferred_element_type=F32) + b_ref[...]


def _ada(c_all, w_ada, b_ada):
    tn = 512
    return pl.pallas_call(
        _ada_kernel,
        out_shape=jax.ShapeDtypeStruct((C_ROWS, 6 * D), F32),
        grid=(6 * D // tn,),
        in_specs=[pl.BlockSpec((C_ROWS, D), lambda j: (0, 0)),
                  pl.BlockSpec((D, tn), lambda j: (0, j)),
                  pl.BlockSpec((1, tn), lambda j: (0, j))],
        out_specs=pl.BlockSpec((C_ROWS, tn), lambda j: (0, j)),
        compiler_params=_cparams(),
        name="ada",
    )(c_all, w_ada, b_ada)


def _inproj_kernel(x_ref, sh_ref, sc_ref, g_ref, w_ref, o_ref, *, prompt):
    h = _rms(x_ref[...], g_ref[...]) * (1.0 + _mod_rows(sc_ref, prompt)) + _mod_rows(sh_ref, prompt)
    o_ref[...] = jnp.dot(h.astype(BF16), w_ref[...], preferred_element_type=F32)


def _inproj(x, mod, g_mix, w_in_b, prompt):
    rows = x.shape[0]
    return pl.pallas_call(
        functools.partial(_inproj_kernel, prompt=prompt),
        out_shape=jax.ShapeDtypeStruct((rows, PROJ_PAD), F32),
        grid=(rows // TM,),
        in_specs=[pl.BlockSpec((TM, D), lambda i: (i, 0)),
                  _mod_spec(0, prompt), _mod_spec(1, prompt),
                  _const_spec((1, D)), _const_spec((D, PROJ_PAD))],
        out_specs=pl.BlockSpec((TM, PROJ_PAD), lambda i: (i, 0)),
        compiler_params=_cparams(),
        name="inproj_p" if prompt else "inproj_s",
    )(x, mod, mod, g_mix, w_in_b)


def _iota2(shape, axis):
    return lax.broadcasted_iota(I32, shape, axis)


def _gla_prep(p_ref, wgu_ref, bg_ref, rows, seg):
    q = p_ref[:, 3 * CW:3 * CW + HK]
    k = p_ref[:, 3 * CW + HK:3 * CW + 2 * HK]
    a = p_ref[:, PROJ_MAIN:PROJ_PAD].astype(BF16)
    z = jnp.dot(a, wgu_ref[...], preferred_element_type=F32) + bg_ref[...]
    la = _log_sigmoid(z) * (1.0 / TAU)
    r = _iota2((rows, rows), 0)
    c = _iota2((rows, rows), 1)
    same = (r // seg) == (c // seg)
    tri = jnp.where(same & (c <= r), 1.0, 0.0).astype(F32)
    tot = jnp.where(same, 1.0, 0.0).astype(F32)
    b = jnp.dot(tri, la, preferred_element_type=F32, precision=lax.Precision.HIGHEST)
    bl = jnp.dot(tot, la, preferred_element_type=F32, precision=lax.Precision.HIGHEST)
    qt = q * jnp.exp(b) * (DK ** -0.5)
    kt = k * jnp.exp(-b)
    kend = k * jnp.exp(bl - b)
    return qt, kt, kend, bl


def _head_stack(x):
    rows = x.shape[0]
    t = jnp.concatenate([x] * H, axis=0)
    keep = (_iota2((H * rows, HK), 0) // rows) == (_iota2((H * rows, HK), 1) // DK)
    return jnp.where(keep, t, 0.0)


def _gla_out(o_h, g_h, ngl):
    on = o_h * lax.rsqrt(jnp.mean(o_h * o_h, axis=-1, keepdims=True) + EPS) * ngl
    return on * _silu(g_h)


def _conv_out(bg, u, um1, um2, cw_ref, cb_ref, ncv_ref):
    conv_y = cb_ref[...] + cw_ref[0:1, :] * um2 + cw_ref[1:2, :] * um1 + cw_ref[2:3, :] * u
    return _rms(bg * conv_y, ncv_ref[...])


def _mixer_p_kernel(p_ref, cw_ref, cb_ref, wgu_ref, bg_ref, ncv_ref, ngl_ref,
                    y_ref, nconv_ref, nst_ref, ubuf, st_ref):
    j = pl.program_id(0) % TILES_PER_SEQ

    @pl.when(j == 0)
    def _():
        ubuf[0:8, :] = jnp.zeros((8, CW), F32)
        st_ref[...] = jnp.zeros((DV, HK), F32)

    bgate = p_ref[:, 0:CW]
    u = p_ref[:, CW:2 * CW] * p_ref[:, 2 * CW:3 * CW]
    ubuf[8:8 + TM, :] = u
    um1 = ubuf[7:7 + TM, :]
    um2 = ubuf[6:6 + TM, :]
    y_ref[:, 0:CW] = _conv_out(bgate, u, um1, um2, cw_ref, cb_ref, ncv_ref).astype(BF16)
    ubuf[6:8, :] = u[TM - 2:TM, :]

    @pl.when(j == TILES_PER_SEQ - 1)
    def _():
        nconv_ref[0] = u[TM - 2:TM, :]

    qt, kt, kend, bl = _gla_prep(p_ref, wgu_ref, bg_ref, TM, CH)
    causal = _iota2((H * CH, CH), 0) % CH >= _iota2((H * CH, CH), 1)
    ngl = ngl_ref[...]
    for c in range(TM // CH):
        r0 = c * CH
        lhs = _head_stack(qt[r0:r0 + CH]).astype(BF16)
        sc = lax.dot_general(lhs, kt[r0:r0 + CH].astype(BF16), (((1,), (1,)), ((), ())),
                             preferred_element_type=F32)
        sc = jnp.where(causal, sc, 0.0).astype(BF16)
        st = st_ref[...]
        o_inter = lax.dot_general(lhs, st.astype(BF16), (((1,), (1,)), ((), ())),
                                  preferred_element_type=F32)
        vs = []
        for h in range(H):
            v_h = p_ref[r0:r0 + CH, 3 * CW + 2 * HK + h * DV:3 * CW + 2 * HK + (h + 1) * DV].astype(BF16)
            g_h = p_ref[r0:r0 + CH, 3 * CW + 2 * HK + HV + h * DV:3 * CW + 2 * HK + HV + (h + 1) * DV]
            vs.append(v_h)
            o_h = jnp.dot(sc[h * CH:(h + 1) * CH], v_h, preferred_element_type=F32) + o_inter[h * CH:(h + 1) * CH]
            y_ref[r0:r0 + CH, CW + h * DV:CW + (h + 1) * DV] = _gla_out(o_h, g_h, ngl).astype(BF16)
        vstack = jnp.concatenate(vs, axis=0)
        kstack = _head_stack(kend[r0:r0 + CH]).astype(BF16)
        kv_t = lax.dot_general(vstack, kstack, (((0,), (0,)), ((), ())), preferred_element_type=F32)
        st_ref[...] = st * jnp.exp(bl[r0:r0 + 1, :]) + kv_t

    @pl.when(j == TILES_PER_SEQ - 1)
    def _():
        nst_ref[0] = st_ref[...]


def _mixer_p(proj, conv_w, conv_b, wgu_b, b_gate, n_conv, n_gla):
    return pl.pallas_call(
        _mixer_p_kernel,
        out_shape=(jax.ShapeDtypeStruct((T_P, D), BF16),
                   jax.ShapeDtypeStruct((B_P, 2, CW), F32),
                   jax.ShapeDtypeStruct((B_P, DV, HK), F32)),
        grid=(N_TILES_P,),
        in_specs=[pl.BlockSpec((TM, PROJ_PAD), lambda i: (i, 0)),
                  _const_spec((3, CW)), _const_spec((1, CW)), _const_spec((128, HK)), _const_spec((1, HK)),
                  _const_spec((1, CW)), _const_spec((1, DV))],
        out_specs=(pl.BlockSpec((TM, D), lambda i: (i, 0)),
                   pl.BlockSpec((1, 2, CW), lambda i: (i // TILES_PER_SEQ, 0, 0)),
                   pl.BlockSpec((1, DV, HK), lambda i: (i // TILES_PER_SEQ, 0, 0))),
        scratch_shapes=[pltpu.VMEM((8 + TM, CW), F32), pltpu.VMEM((DV, HK), F32)],
        compiler_params=_cparams(),
        name="mixer_p",
    )(proj, conv_w, conv_b, wgu_b, b_gate, n_conv, n_gla)


def _mixer_s_kernel(p_ref, sconv_ref, sst_ref, cw_ref, cb_ref, wgu_ref, bg_ref, ncv_ref, ngl_ref,
                    y_ref, nconv_ref, nst_ref, ubuf):
    nseq = SEQ_PER_CHUNK
    bgate = p_ref[:, 0:CW]
    u = p_ref[:, CW:2 * CW] * p_ref[:, 2 * CW:3 * CW]
    ubuf[0:8, :] = jnp.zeros((8, CW), F32)
    ubuf[8:8 + CH, :] = u
    tpos = _iota2((CH, CW), 0) % L_S
    s0 = jnp.concatenate([jnp.broadcast_to(sconv_ref[s, 0:1, :], (L_S, CW)) for s in range(nseq)], axis=0)
    s1 = jnp.concatenate([jnp.broadcast_to(sconv_ref[s, 1:2, :], (L_S, CW)) for s in range(nseq)], axis=0)
    um1 = jnp.where(tpos == 0, s1, ubuf[7:7 + CH, :])
    um2 = jnp.where(tpos == 0, s0, jnp.where(tpos == 1, s1, ubuf[6:6 + CH, :]))
    y_ref[:, 0:CW] = _conv_out(bgate, u, um1, um2, cw_ref, cb_ref, ncv_ref).astype(BF16)
    for s in range(nseq):
        nconv_ref[s] = u[s * L_S + L_S - 2:(s + 1) * L_S, :]

    qt, kt, kend, bl = _gla_prep(p_ref, wgu_ref, bg_ref, CH, L_S)
    rr = _iota2((H * CH, CH), 0) % CH
    cc = _iota2((H * CH, CH), 1)
    causal = (rr >= cc) & ((rr // L_S) == (cc // L_S))
    ngl = ngl_ref[...]
    lhs_f = _head_stack(qt)
    sc = lax.dot_general(lhs_f.astype(BF16), kt.astype(BF16), (((1,), (1,)), ((), ())),
                         preferred_element_type=F32)
    sc = jnp.where(causal, sc, 0.0).astype(BF16)
    kstack_f = _head_stack(kend)
    v_all = p_ref[:, 3 * CW + 2 * HK:3 * CW + 2 * HK + HV]
    o_inter = []
    for s in range(nseq):
        rs = [slice(h * CH + s * L_S, h * CH + (s + 1) * L_S) for h in range(H)]
        lhs_s = jnp.concatenate([lhs_f[r] for r in rs], axis=0).astype(BF16)
        k_s = jnp.concatenate([kstack_f[r] for r in rs], axis=0).astype(BF16)
        v_s = jnp.concatenate([v_all[s * L_S:(s + 1) * L_S, h * DV:(h + 1) * DV] for h in range(H)],
                              axis=0).astype(BF16)
        st = sst_ref[s]
        o_inter.append(lax.dot_general(lhs_s, st.astype(BF16), (((1,), (1,)), ((), ())),
                                       preferred_element_type=F32))
        kv_t = lax.dot_general(v_s, k_s, (((0,), (0,)), ((), ())), preferred_element_type=F32)
        nst_ref[s] = st * jnp.exp(bl[s * L_S:s * L_S + 1, :]) + kv_t
    for h in range(H):
        v_h = v_all[:, h * DV:(h + 1) * DV].astype(BF16)
        g_h = p_ref[:, 3 * CW + 2 * HK + HV + h * DV:3 * CW + 2 * HK + HV + (h + 1) * DV]
        oi_h = jnp.concatenate([o_inter[s][h * L_S:(h + 1) * L_S] for s in range(nseq)], axis=0)
        o_h = jnp.dot(sc[h * CH:(h + 1) * CH], v_h, preferred_element_type=F32) + oi_h
        y_ref[:, CW + h * DV:CW + (h + 1) * DV] = _gla_out(o_h, g_h, ngl).astype(BF16)


def _mixer_s(proj, sconv, sst, conv_w, conv_b, wgu_b, b_gate, n_conv, n_gla):
    nseq = SEQ_PER_CHUNK
    return pl.pallas_call(
        _mixer_s_kernel,
        out_shape=(jax.ShapeDtypeStruct((T_S, D), BF16),
                   jax.ShapeDtypeStruct((B_S, 2, CW), F32),
                   jax.ShapeDtypeStruct((B_S, DV, HK), F32)),
        grid=(T_S // CH,),
        in_specs=[pl.BlockSpec((CH, PROJ_PAD), lambda i: (i, 0)),
                  pl.BlockSpec((nseq, 2, CW), lambda i: (i, 0, 0)),
                  pl.BlockSpec((nseq, DV, HK), lambda i: (i, 0, 0)),
                  _const_spec((3, CW)), _const_spec((1, CW)), _const_spec((128, HK)), _const_spec((1, HK)),
                  _const_spec((1, CW)), _const_spec((1, DV))],
        out_specs=(pl.BlockSpec((CH, D), lambda i: (i, 0)),
                   pl.BlockSpec((nseq, 2, CW), lambda i: (i, 0, 0)),
                   pl.BlockSpec((nseq, DV, HK), lambda i: (i, 0, 0))),
        scratch_shapes=[pltpu.VMEM((8 + CH, CW), F32)],
        compiler_params=_cparams(),
        name="mixer_s",
    )(proj, sconv, sst, conv_w, conv_b, wgu_b, b_gate, n_conv, n_gla)


def _outproj_kernel(*refs, prompt, aliased):
    if aliased:
        refs = refs[:9] + refs[13:]
    (yc_ref, x_ref, gt_ref, sh_ref, sc_ref, wo_ref, g_ref, wr_ref, br_ref,
     x1_ref, h2_ref, ri_ref, rw_ref) = refs
    mix = jnp.dot(yc_ref[...], wo_ref[...], preferred_element_type=F32)
    x1 = x_ref[...] + _mod_rows(gt_ref, prompt) * mix
    x1_ref[...] = x1
    h2 = _rms(x1, g_ref[...]) * (1.0 + _mod_rows(sc_ref, prompt)) + _mod_rows(sh_ref, prompt)
    h2_ref[...] = h2

    lt = lax.dot_general(wr_ref[...], h2, (((1,), (1,)), ((), ())), preferred_element_type=F32,
                         precision=lax.Precision.HIGHEST) + br_ref[...]
    coarse = lt[0:N_GROUPS]
    cmax = jnp.max(coarse, axis=0, keepdims=True)
    gi = _iota2((N_GROUPS, TM), 0)
    grp = jnp.min(jnp.where(coarse == cmax, gi, N_GROUPS), axis=0, keepdims=True)
    p_sel = 1.0 / jnp.sum(jnp.exp(coarse - cmax), axis=0, keepdims=True)
    fine = jnp.zeros((EPG, TM), F32)
    for g in range(N_GROUPS):
        fine = jnp.where(grp == g, lt[8 + g * EPG:8 + (g + 1) * EPG], fine)
    ei = _iota2((EPG, TM), 0)
    f1 = jnp.max(fine, axis=0, keepdims=True)
    i1 = jnp.min(jnp.where(fine == f1, ei, EPG), axis=0, keepdims=True)
    rest = jnp.where(ei == i1, -jnp.inf, fine)
    f2 = jnp.max(rest, axis=0, keepdims=True)
    i2 = jnp.min(jnp.where(rest == f2, ei, EPG), axis=0, keepdims=True)
    e2 = jnp.exp(f2 - f1)
    w1 = p_sel / (1.0 + e2)
    w2 = p_sel * e2 / (1.0 + e2)
    row = _iota2((8, TM), 0)
    ri_ref[...] = jnp.where(row == 0, grp * EPG + i1, jnp.where(row == 1, grp * EPG + i2, 0))
    rw_ref[...] = jnp.where(row == 0, w1, jnp.where(row == 1, w2, 0.0))


def _outproj(ycat, x, mod, w_out_b, g_ffn, w_r, b_r, prompt, prev=None):
    rows = x.shape[0]
    off = 0 if prompt else N_TILES_P
    out_shape = (jax.ShapeDtypeStruct((T, D), F32), jax.ShapeDtypeStruct((T, D), F32),
                 jax.ShapeDtypeStruct((8, T), I32), jax.ShapeDtypeStruct((8, T), F32))
    in_specs = [pl.BlockSpec((TM, D), lambda i: (i, 0)), pl.BlockSpec((TM, D), lambda i: (i, 0)),
                _mod_spec(2, prompt), _mod_spec(3, prompt), _mod_spec(4, prompt),
                _const_spec((D, D)), _const_spec((1, D)), _const_spec((R_ROWS, D)), _const_spec((R_ROWS, 1))]
    args = [ycat, x, mod, mod, mod, w_out_b, g_ffn, w_r, b_r]
    aliases = {}
    if prev is not None:
        in_specs += [pl.BlockSpec(memory_space=pl.ANY)] * 4
        args += list(prev)
        aliases = {9 + k: k for k in range(4)}
    return pl.pallas_call(
        functools.partial(_outproj_kernel, prompt=prompt, aliased=prev is not None),
        out_shape=out_shape,
        grid=(rows // TM,),
        in_specs=in_specs,
        out_specs=(pl.BlockSpec((TM, D), lambda i: (i + off, 0)), pl.BlockSpec((TM, D), lambda i: (i + off, 0)),
                   pl.BlockSpec((8, TM), lambda i: (0, i + off)), pl.BlockSpec((8, TM), lambda i: (0, i + off))),
        input_output_aliases=aliases,
        compiler_params=_cparams(),
        name="outproj_p" if prompt else "outproj_s",
    )(*args)


def _moe_kernel(te_ref, nt_ref, tok_ref, dst_ref, rw_ref, h2_ref, wg_ref, wu_ref, wd_ref, out_ref,
                xg, yb, gsem, ssem):
    i = pl.program_id(0)

    @pl.when(i < nt_ref[0])
    def _():
        def gather(r, carry):
            pltpu.make_async_copy(h2_ref.at[pl.ds(tok_ref[0, 0, r], 1), :], xg.at[pl.ds(r, 1), :], gsem).start()
            return carry
        lax.fori_loop(0, E_TILE, gather, 0)
        pltpu.make_async_copy(h2_ref.at[pl.ds(0, E_TILE), :], xg, gsem).wait()

        x = xg[...].astype(BF16)
        a = jnp.dot(x, wg_ref[0].astype(BF16), preferred_element_type=F32)
        b = jnp.dot(x, wu_ref[0].astype(BF16), preferred_element_type=F32)
        hid = (_silu(a) * b).astype(BF16)
        y = jnp.dot(hid, wd_ref[0].astype(BF16), preferred_element_type=F32)
        yb[...] = y * rw_ref[...]

        def scatter(r, carry):
            pltpu.make_async_copy(yb.at[pl.ds(r, 1), :], out_ref.at[pl.ds(dst_ref[0, 0, r], 1), :], ssem).start()
            return carry
        lax.fori_loop(0, E_TILE, scatter, 0)
        pltpu.make_async_copy(yb, out_ref.at[pl.ds(0, E_TILE), :], ssem).wait()


def _moe(tile_expert, n_tiles, row_tok, row_dst, row_w, h2, w_eg, w_eu, w_ed):
    grid_spec = pltpu.PrefetchScalarGridSpec(
        num_scalar_prefetch=2,
        grid=(E_TILES_MAX,),
        in_specs=[pl.BlockSpec((1, 1, E_TILE), lambda i, te, nt: (i, 0, 0), memory_space=pltpu.SMEM),
                  pl.BlockSpec((1, 1, E_TILE), lambda i, te, nt: (i, 0, 0), memory_space=pltpu.SMEM),
                  pl.BlockSpec((E_TILE, 1), lambda i, te, nt: (i, 0)),
                  pl.BlockSpec(memory_space=pl.ANY),
                  pl.BlockSpec((1, D, DE), lambda i, te, nt: (te[i], 0, 0)),
                  pl.BlockSpec((1, D, DE), lambda i, te, nt: (te[i], 0, 0)),
                  pl.BlockSpec((1, DE, D), lambda i, te, nt: (te[i], 0, 0))],
        out_specs=pl.BlockSpec(memory_space=pl.ANY),
        scratch_shapes=[pltpu.VMEM((E_TILE, D), F32), pltpu.VMEM((E_TILE, D), F32),
                        pltpu.SemaphoreType.DMA(()), pltpu.SemaphoreType.DMA(())],
    )
    return pl.pallas_call(
        _moe_kernel,
        out_shape=jax.ShapeDtypeStruct((2 * T + E_TILE, D), F32),
        grid_spec=grid_spec,
        compiler_params=_cparams(),
        name="moe",
    )(tile_expert, n_tiles, row_tok, row_dst, row_w, h2, w_eg, w_eu, w_ed)


def _dispatch(ridx, rwts):
    e = ridx[:2].reshape(-1)
    w = rwts[:2].reshape(-1)
    onehot = (e[:, None] == jnp.arange(NE, dtype=I32)[None, :]).astype(I32)
    csum = jnp.cumsum(onehot, axis=0)
    rank = jnp.take_along_axis(csum, e[:, None], axis=1)[:, 0] - 1
    counts = csum[-1]
    ntile = (counts + E_TILE - 1) // E_TILE
    tend = jnp.cumsum(ntile)
    pos = (tend - ntile)[e] * E_TILE + rank
    n_rows = E_TILES_MAX * E_TILE
    pair = jnp.arange(2 * T, dtype=I32)
    row_tok = jnp.zeros((n_rows,), I32).at[pos].set(pair % T)
    row_dst = (2 * T + jnp.arange(n_rows, dtype=I32) % E_TILE).at[pos].set(pair)
    row_w = jnp.zeros((n_rows,), F32).at[pos].set(w)
    total = tend[-1]
    tile_id = jnp.minimum(jnp.arange(E_TILES_MAX, dtype=I32), total - 1)
    tile_expert = jnp.minimum(jnp.searchsorted(tend, tile_id, side="right"), NE - 1).astype(I32)
    return (tile_expert, total.reshape(1).astype(I32), row_tok.reshape(E_TILES_MAX, 1, E_TILE),
            row_dst.reshape(E_TILES_MAX, 1, E_TILE), row_w.reshape(n_rows, 1))


def _final_kernel(x1_ref, m0_ref, m1_ref, gt_ref, g_ref, y_ref, *, prompt):
    x2 = x1_ref[...] + _mod_rows(gt_ref, prompt) * (m0_ref[...] + m1_ref[...])
    y_ref[...] = _rms(x2, g_ref[...])


def _final(x1, moe2, mod, g_fin, prompt):
    rows = T_P if prompt else T_S
    off = 0 if prompt else N_TILES_P
    return pl.pallas_call(
        functools.partial(_final_kernel, prompt=prompt),
        out_shape=jax.ShapeDtypeStruct((rows, D), F32),
        grid=(rows // TM,),
        in_specs=[pl.BlockSpec((TM, D), lambda i: (i + off, 0)),
                  pl.BlockSpec((TM, D), lambda i: (i + off, 0)),
                  pl.BlockSpec((TM, D), lambda i: (i + off + N_TILES, 0)),
                  _mod_spec(5, prompt), _const_spec((1, D))],
        out_specs=pl.BlockSpec((TM, D), lambda i: (i, 0)),
        compiler_params=_cparams(),
        name="final_p" if prompt else "final_s",
    )(x1, moe2, moe2, mod, g_fin)


def kernel(x_prompt, x_sample, c_prompt, c_sample, state_conv, state_gla, w_ada, b_ada, norm_mix, w_in, conv_w,
           conv_b, w_gate_up, b_gate, norm_conv, norm_gla, w_out, norm_ffn, w_coarse, b_coarse, w_fine, b_fine,
           w_exp_gate, w_exp_up, w_exp_down, norm_final):
    xp = x_prompt.reshape(T_P, D)
    xs = x_sample.reshape(T_S, D)
    c_all = jnp.concatenate([c_sample, c_prompt, jnp.zeros((C_ROWS - B_S - B_P, D), F32)], axis=0)
    mod = _ada(c_all, w_ada[0], b_ada[0][None, :])

    w_in_b = jnp.pad(w_in[0], ((0, 0), (0, PROJ_PAD - w_in.shape[-1]))).astype(BF16)
    w_out_b = w_out[0].astype(BF16)
    wgu_b = jnp.pad(w_gate_up[0], ((0, 128 - RANK), (0, 0))).astype(BF16)
    w_r = jnp.concatenate([w_coarse[0].T, jnp.zeros((8 - N_GROUPS, D), F32), w_fine[0].T], axis=0)
    b_r = jnp.concatenate([b_coarse[0], jnp.zeros((8 - N_GROUPS,), F32), b_fine[0]])[:, None]
    g_mix, g_ffn, g_fin = norm_mix[0][None, :], norm_ffn[0][None, :], norm_final[None, :]
    mix_consts = (conv_w[0], conv_b[0][None, :], wgu_b, b_gate[0][None, :], norm_conv[0][None, :],
                  norm_gla[0][None, :])

    proj_p = _inproj(xp, mod, g_mix, w_in_b, True)
    proj_s = _inproj(xs, mod, g_mix, w_in_b, False)
    ycat_p, nconv_p, nst_p = _mixer_p(proj_p, *mix_consts)
    sst = state_gla[0].transpose(0, 3, 1, 2).reshape(B_S, DV, HK)
    ycat_s, nconv_s, nst_s = _mixer_s(proj_s, state_conv[0], sst, *mix_consts)

    outs = _outproj(ycat_p, xp, mod, w_out_b, g_ffn, w_r, b_r, True)
    x1, h2, ridx, rwts = _outproj(ycat_s, xs, mod, w_out_b, g_ffn, w_r, b_r, False, prev=outs)

    moe2 = _moe(*_dispatch(ridx, rwts), h2, w_exp_gate[0], w_exp_up[0], w_exp_down[0])
    y_p = _final(x1, moe2, mod, g_fin, True)
    y_s = _final(x1, moe2, mod, g_fin, False)

    def untranspose(st, nb):
        return st.reshape(nb, DV, H, DK).transpose(0, 2, 3, 1)[None]

    return (y_p.reshape(B_P, L_P, D), y_s.reshape(B_S, L_S, D), nconv_p[None], untranspose(nst_p, B_P),
            nconv_s[None], untranspose(nst_s, B_S))
```

```python
import functools

import jax
import jax.numpy as jnp
from jax import lax
from jax.experimental import pallas as pl
from jax.experimental.pallas import tpu as pltpu

F32 = jnp.float32
BF16 = jnp.bfloat16
I32 = jnp.int32
U32 = jnp.uint32

D = 2048
DH = D // 2
CW = 1024
H = 8
DK = 64
DV = 128
HK = H * DK
HV = H * DV
RANK = 16
TAU = 16.0
N_GROUPS = 4
EPG = 8
NE = N_GROUPS * EPG
DE = 512
EPS = 1e-6

B_P, L_P = 4, 2048
B_S, L_S = 128, 8
T_P = B_P * L_P
T_S = B_S * L_S
T = T_P + T_S

TM = 256
CH = 64
N_TILES = T // TM
N_TILES_P = T_P // TM
TILES_PER_SEQ = L_P // TM
SEQ_PER_TILE = TM // L_S
SEQ_PER_CHUNK = CH // L_S

PROJ_MAIN = 3 * CW + 2 * HK + 2 * HV
PROJ_PAD = PROJ_MAIN + 128
C_ROWS = 136
P_ROW_BLOCK = B_S // 8
R_ROWS = 40

E_TILE = 256
E_TILES_MAX = (2 * T) // E_TILE + NE
E_ROWS = E_TILES_MAX * E_TILE
VMEM_LIMIT = 60 * 1024 * 1024


def _cparams(n_axes=1, vmem=VMEM_LIMIT):
    return pltpu.CompilerParams(dimension_semantics=("arbitrary",) * n_axes, vmem_limit_bytes=vmem)


def _rms(x, g):
    return x * lax.rsqrt(jnp.mean(x * x, axis=-1, keepdims=True) + EPS) * g


def _sigmoid(x):
    return 1.0 / (1.0 + jnp.exp(-x))


def _silu(x):
    return x * _sigmoid(x)


def _log_sigmoid(x):
    return jnp.minimum(x, 0.0) - jnp.log1p(jnp.exp(-jnp.abs(x)))


def _iota2(shape, axis):
    return lax.broadcasted_iota(I32, shape, axis)


def _expand_rows(ref, n, reps):
    return jnp.concatenate([jnp.broadcast_to(ref[j:j + 1, :], (reps, ref.shape[-1])) for j in range(n)], axis=0)


def _prompt_row(ref):
    return ref[pl.ds(pl.program_id(0) // TILES_PER_SEQ, 1), :]


def _sample_rows(ref):
    return _expand_rows(ref, SEQ_PER_TILE, L_S)


def _p_tile(i):
    return jnp.minimum(i, N_TILES_P - 1)


def _s_tile(i):
    return jnp.maximum(i - N_TILES_P, 0)


def _mod_specs(col):
    return [pl.BlockSpec((8, D), lambda i, *_: (P_ROW_BLOCK, col)),
            pl.BlockSpec((SEQ_PER_TILE, D), lambda i, *_: (_s_tile(i), col))]


def _const_spec(shape):
    zeros = (0,) * len(shape)
    return pl.BlockSpec(shape, lambda *_: zeros, pipeline_mode=pl.Buffered(1))


def _pack_halves(x):
    lo = lax.bitcast_convert_type(x[:, :DH].astype(BF16).astype(F32), U32)
    hi = lax.bitcast_convert_type(x[:, DH:].astype(BF16).astype(F32), U32)
    return (hi & jnp.uint32(0xFFFF0000)) | (lo >> 16)


def _unpack_halves(p):
    lo = lax.bitcast_convert_type(p << 16, F32)
    hi = lax.bitcast_convert_type(p & jnp.uint32(0xFFFF0000), F32)
    return lo, hi


def _ada_kernel(c_ref, w_ref, b_ref, o_ref):
    s = _silu(c_ref[...]).astype(BF16)
    o_ref[...] = jnp.dot(s, w_ref[...].astype(BF16), preferred_element_type=F32) + b_ref[...]


def _ada(c_all, w_ada, b_ada):
    tn = 512
    return pl.pallas_call(
        _ada_kernel,
        out_shape=jax.ShapeDtypeStruct((C_ROWS, 6 * D), F32),
        grid=(6 * D // tn,),
        in_specs=[pl.BlockSpec((C_ROWS, D), lambda j: (0, 0)),
                  pl.BlockSpec((D, tn), lambda j: (0, j)),
                  pl.BlockSpec((1, tn), lambda j: (0, j))],
        out_specs=pl.BlockSpec((C_ROWS, tn), lambda j: (0, j)),
        compiler_params=_cparams(),
        name="ada",
    )(c_all, w_ada, b_ada)


def _inproj_kernel(xp_ref, xs_ref, shp_ref, shs_ref, scp_ref, scs_ref, g_ref, w_ref, o_ref, h_scr):
    i = pl.program_id(0)

    @pl.when(i < N_TILES_P)
    def _():
        h = _rms(xp_ref[...], g_ref[...]) * (1.0 + _prompt_row(scp_ref)) + _prompt_row(shp_ref)
        h_scr[...] = h.astype(BF16)

    @pl.when(i >= N_TILES_P)
    def _():
        h = _rms(xs_ref[...], g_ref[...]) * (1.0 + _sample_rows(scs_ref)) + _sample_rows(shs_ref)
        h_scr[...] = h.astype(BF16)

    o_ref[...] = jnp.dot(h_scr[...], w_ref[...], preferred_element_type=F32)


def _inproj(xp, xs, mod, g_mix, w_in_b):
    return pl.pallas_call(
        _inproj_kernel,
        out_shape=jax.ShapeDtypeStruct((T, PROJ_PAD), F32),
        grid=(N_TILES,),
        in_specs=[pl.BlockSpec((TM, D), lambda i: (_p_tile(i), 0)),
                  pl.BlockSpec((TM, D), lambda i: (_s_tile(i), 0)),
                  *_mod_specs(0), *_mod_specs(1),
                  _const_spec((1, D)), _const_spec((D, PROJ_PAD))],
        out_specs=pl.BlockSpec((TM, PROJ_PAD), lambda i: (i, 0)),
        scratch_shapes=[pltpu.VMEM((TM, D), BF16)],
        compiler_params=_cparams(),
        name="inproj",
    )(xp, xs, mod, mod, mod, mod, g_mix, w_in_b)


def _gla_prep(p_ref, wgu_ref, bg_ref, rows, seg):
    q = p_ref[:, 3 * CW:3 * CW + HK]
    k = p_ref[:, 3 * CW + HK:3 * CW + 2 * HK]
    a = p_ref[:, PROJ_MAIN:PROJ_PAD].astype(BF16)
    z = jnp.dot(a, wgu_ref[...], preferred_element_type=F32) + bg_ref[...]
    la = _log_sigmoid(z) * (1.0 / TAU)
    r = _iota2((rows, rows), 0)
    c = _iota2((rows, rows), 1)
    same = (r // seg) == (c // seg)
    tri = jnp.where(same & (c <= r), 1.0, 0.0).astype(F32)
    tot = jnp.where(same, 1.0, 0.0).astype(F32)
    b = jnp.dot(tri, la, preferred_element_type=F32, precision=lax.Precision.HIGHEST)
    bl = jnp.dot(tot, la, preferred_element_type=F32, precision=lax.Precision.HIGHEST)
    qt = q * jnp.exp(b) * (DK ** -0.5)
    kt = k * jnp.exp(-b)
    kend = k * jnp.exp(bl - b)
    return qt, kt, kend, bl


def _head_stack(x):
    rows = x.shape[0]
    t = jnp.concatenate([x] * H, axis=0)
    keep = (_iota2((H * rows, HK), 0) // rows) == (_iota2((H * rows, HK), 1) // DK)
    return jnp.where(keep, t, 0.0)


def _gla_out(o_h, g_h, ngl):
    on = o_h * lax.rsqrt(jnp.mean(o_h * o_h, axis=-1, keepdims=True) + EPS) * ngl
    return on * _silu(g_h)


def _conv_out(bg, u, um1, um2, cw_ref, cb_ref, ncv_ref):
    conv_y = cb_ref[...] + cw_ref[0:1, :] * um2 + cw_ref[1:2, :] * um1 + cw_ref[2:3, :] * u
    return _rms(bg * conv_y, ncv_ref[...])


V_OFF = 3 * CW + 2 * HK
G_OFF = V_OFF + HV


def _mixer_p_kernel(p_ref, cw_ref, cb_ref, wgu_ref, bg_ref, ncv_ref, ngl_ref,
                    y_ref, nconv_ref, nst_ref, ubuf, st_ref):
    j = pl.program_id(0) % TILES_PER_SEQ

    @pl.when(j == 0)
    def _():
        ubuf[0:8, :] = jnp.zeros((8, CW), F32)
        st_ref[...] = jnp.zeros((DV, HK), F32)

    bgate = p_ref[:, 0:CW]
    u = p_ref[:, CW:2 * CW] * p_ref[:, 2 * CW:3 * CW]
    ubuf[8:8 + TM, :] = u
    um1 = ubuf[7:7 + TM, :]
    um2 = ubuf[6:6 + TM, :]
    y_ref[:, 0:CW] = _conv_out(bgate, u, um1, um2, cw_ref, cb_ref, ncv_ref).astype(BF16)
    ubuf[6:8, :] = u[TM - 2:TM, :]

    @pl.when(j == TILES_PER_SEQ - 1)
    def _():
        nconv_ref[0] = u[TM - 2:TM, :]

    qt, kt, kend, bl = _gla_prep(p_ref, wgu_ref, bg_ref, TM, CH)
    causal = _iota2((H * CH, CH), 0) % CH >= _iota2((H * CH, CH), 1)
    ngl = ngl_ref[...]
    for c in range(TM // CH):
        r0 = c * CH
        lhs = _head_stack(qt[r0:r0 + CH]).astype(BF16)
        sc = lax.dot_general(lhs, kt[r0:r0 + CH].astype(BF16), (((1,), (1,)), ((), ())),
                             preferred_element_type=F32)
        sc = jnp.where(causal, sc, 0.0).astype(BF16)
        st = st_ref[...]
        o_inter = lax.dot_general(lhs, st.astype(BF16), (((1,), (1,)), ((), ())),
                                  preferred_element_type=F32)
        vs = []
        for h in range(H):
            v_h = p_ref[r0:r0 + CH, V_OFF + h * DV:V_OFF + (h + 1) * DV].astype(BF16)
            g_h = p_ref[r0:r0 + CH, G_OFF + h * DV:G_OFF + (h + 1) * DV]
            vs.append(v_h)
            o_h = jnp.dot(sc[h * CH:(h + 1) * CH], v_h, preferred_element_type=F32) + o_inter[h * CH:(h + 1) * CH]
            y_ref[r0:r0 + CH, CW + h * DV:CW + (h + 1) * DV] = _gla_out(o_h, g_h, ngl).astype(BF16)
        vstack = jnp.concatenate(vs, axis=0)
        kstack = _head_stack(kend[r0:r0 + CH]).astype(BF16)
        kv_t = lax.dot_general(vstack, kstack, (((0,), (0,)), ((), ())), preferred_element_type=F32)
        st_ref[...] = st * jnp.exp(bl[r0:r0 + 1, :]) + kv_t

    @pl.when(j == TILES_PER_SEQ - 1)
    def _():
        nst_ref[0] = st_ref[...]


def _mixer_p(proj, conv_w, conv_b, wgu_b, b_gate, n_conv, n_gla):
    return pl.pallas_call(
        _mixer_p_kernel,
        out_shape=(jax.ShapeDtypeStruct((T_P, D), BF16),
                   jax.ShapeDtypeStruct((B_P, 2, CW), F32),
                   jax.ShapeDtypeStruct((B_P, DV, HK), F32)),
        grid=(N_TILES_P,),
        in_specs=[pl.BlockSpec((TM, PROJ_PAD), lambda i: (i, 0)),
                  _const_spec((3, CW)), _const_spec((1, CW)), _const_spec((128, HK)), _const_spec((1, HK)),
                  _const_spec((1, CW)), _const_spec((1, DV))],
        out_specs=(pl.BlockSpec((TM, D), lambda i: (i, 0)),
                   pl.BlockSpec((1, 2, CW), lambda i: (i // TILES_PER_SEQ, 0, 0)),
                   pl.BlockSpec((1, DV, HK), lambda i: (i // TILES_PER_SEQ, 0, 0))),
        scratch_shapes=[pltpu.VMEM((8 + TM, CW), F32), pltpu.VMEM((DV, HK), F32)],
        compiler_params=_cparams(),
        name="mixer_p",
    )(proj, conv_w, conv_b, wgu_b, b_gate, n_conv, n_gla)


def _mixer_s_kernel(p_ref, sconv_ref, sst_ref, cw_ref, cb_ref, wgu_ref, bg_ref, ncv_ref, ngl_ref,
                    y_ref, nconv_ref, nst_ref, ubuf):
    nseq = SEQ_PER_CHUNK
    bgate = p_ref[:, 0:CW]
    u = p_ref[:, CW:2 * CW] * p_ref[:, 2 * CW:3 * CW]
    ubuf[0:8, :] = jnp.zeros((8, CW), F32)
    ubuf[8:8 + CH, :] = u
    tpos = _iota2((CH, CW), 0) % L_S
    s0 = jnp.concatenate([jnp.broadcast_to(sconv_ref[s, 0:1, :], (L_S, CW)) for s in range(nseq)], axis=0)
    s1 = jnp.concatenate([jnp.broadcast_to(sconv_ref[s, 1:2, :], (L_S, CW)) for s in range(nseq)], axis=0)
    um1 = jnp.where(tpos == 0, s1, ubuf[7:7 + CH, :])
    um2 = jnp.where(tpos == 0, s0, jnp.where(tpos == 1, s1, ubuf[6:6 + CH, :]))
    y_ref[:, 0:CW] = _conv_out(bgate, u, um1, um2, cw_ref, cb_ref, ncv_ref).astype(BF16)
    for s in range(nseq):
        nconv_ref[s] = u[s * L_S + L_S - 2:(s + 1) * L_S, :]

    qt, kt, kend, bl = _gla_prep(p_ref, wgu_ref, bg_ref, CH, L_S)
    rr = _iota2((H * CH, CH), 0) % CH
    cc = _iota2((H * CH, CH), 1)
    causal = (rr >= cc) & ((rr // L_S) == (cc // L_S))
    ngl = ngl_ref[...]
    lhs_f = _head_stack(qt)
    sc = lax.dot_general(lhs_f.astype(BF16), kt.astype(BF16), (((1,), (1,)), ((), ())),
                         preferred_element_type=F32)
    sc = jnp.where(causal, sc, 0.0).astype(BF16)
    kstack_f = _head_stack(kend)
    v_all = p_ref[:, V_OFF:V_OFF + HV]
    decay_t = jnp.exp(jnp.concatenate([bl, jnp.zeros((128 - CH, HK), F32)], axis=0).T)
    o_inter = []
    for s in range(nseq):
        rs = [slice(h * CH + s * L_S, h * CH + (s + 1) * L_S) for h in range(H)]
        lhs_s = jnp.concatenate([lhs_f[r] for r in rs], axis=0).astype(BF16)
        k_s = jnp.concatenate([kstack_f[r] for r in rs], axis=0).astype(BF16)
        v_s = jnp.concatenate([v_all[s * L_S:(s + 1) * L_S, h * DV:(h + 1) * DV] for h in range(H)],
                              axis=0).astype(BF16)
        st = sst_ref[s]
        o_inter.append(jnp.dot(lhs_s, st.astype(BF16), preferred_element_type=F32))
        kv = lax.dot_general(k_s, v_s, (((0,), (0,)), ((), ())), preferred_element_type=F32)
        nst_ref[s] = st * decay_t[:, s * L_S:s * L_S + 1] + kv
    for h in range(H):
        v_h = v_all[:, h * DV:(h + 1) * DV].astype(BF16)
        g_h = p_ref[:, G_OFF + h * DV:G_OFF + (h + 1) * DV]
        oi_h = jnp.concatenate([o_inter[s][h * L_S:(h + 1) * L_S] for s in range(nseq)], axis=0)
        o_h = jnp.dot(sc[h * CH:(h + 1) * CH], v_h, preferred_element_type=F32) + oi_h
        y_ref[:, CW + h * DV:CW + (h + 1) * DV] = _gla_out(o_h, g_h, ngl).astype(BF16)


def _mixer_s(proj, sconv, sst, conv_w, conv_b, wgu_b, b_gate, n_conv, n_gla):
    nseq = SEQ_PER_CHUNK
    off = T_P // CH
    return pl.pallas_call(
        _mixer_s_kernel,
        out_shape=(jax.ShapeDtypeStruct((T_S, D), BF16),
                   jax.ShapeDtypeStruct((B_S, 2, CW), F32),
                   jax.ShapeDtypeStruct((B_S, HK, DV), F32)),
        grid=(T_S // CH,),
        in_specs=[pl.BlockSpec((CH, PROJ_PAD), lambda i: (i + off, 0)),
                  pl.BlockSpec((nseq, 2, CW), lambda i: (i, 0, 0)),
                  pl.BlockSpec((nseq, HK, DV), lambda i: (i, 0, 0)),
                  _const_spec((3, CW)), _const_spec((1, CW)), _const_spec((128, HK)), _const_spec((1, HK)),
                  _const_spec((1, CW)), _const_spec((1, DV))],
        out_specs=(pl.BlockSpec((CH, D), lambda i: (i, 0)),
                   pl.BlockSpec((nseq, 2, CW), lambda i: (i, 0, 0)),
                   pl.BlockSpec((nseq, HK, DV), lambda i: (i, 0, 0))),
        scratch_shapes=[pltpu.VMEM((8 + CH, CW), F32)],
        compiler_params=_cparams(),
        name="mixer_s",
    )(proj, sconv, sst, conv_w, conv_b, wgu_b, b_gate, n_conv, n_gla)


def _route(h2, wr_ref, br_ref, run_ref, ri_ref, rk_ref, rw_ref, cnt_ref):
    h_hi = h2.astype(BF16)
    h_lo = (h2 - h_hi.astype(F32)).astype(BF16)
    lt4 = lax.dot_general(wr_ref[...], jnp.concatenate([h_hi, h_lo], axis=0), (((1,), (1,)), ((), ())),
                          preferred_element_type=F32)
    lt = (lt4[0:R_ROWS, 0:TM] + lt4[R_ROWS:, 0:TM] + lt4[0:R_ROWS, TM:] + lt4[R_ROWS:, TM:]) + br_ref[...]

    coarse = lt[0:N_GROUPS]
    cmax = jnp.max(coarse, axis=0, keepdims=True)
    gi = _iota2((N_GROUPS, TM), 0)
    grp = jnp.min(jnp.where(coarse == cmax, gi, N_GROUPS), axis=0, keepdims=True)
    p_sel = 1.0 / jnp.sum(jnp.exp(coarse - cmax), axis=0, keepdims=True)
    fine = jnp.zeros((EPG, TM), F32)
    for g in range(N_GROUPS):
        fine = jnp.where(grp == g, lt[8 + g * EPG:8 + (g + 1) * EPG], fine)
    ei = _iota2((EPG, TM), 0)
    f1 = jnp.max(fine, axis=0, keepdims=True)
    i1 = jnp.min(jnp.where(fine == f1, ei, EPG), axis=0, keepdims=True)
    rest = jnp.where(ei == i1, -jnp.inf, fine)
    f2 = jnp.max(rest, axis=0, keepdims=True)
    i2 = jnp.min(jnp.where(rest == f2, ei, EPG), axis=0, keepdims=True)
    e2 = jnp.exp(f2 - f1)
    w1 = p_sel / (1.0 + e2)
    w2 = p_sel * e2 / (1.0 + e2)
    x1 = grp * EPG + i1
    x2 = grp * EPG + i2

    er = _iota2((NE, TM), 0)
    oh1 = jnp.where(er == x1, 1.0, 0.0)
    oh2 = jnp.where(er == x2, 1.0, 0.0)
    oh = oh1 + oh2
    before = jnp.where(_iota2((TM, TM), 0) < _iota2((TM, TM), 1), 1.0, 0.0).astype(BF16)
    cum = jnp.dot(oh.astype(BF16), before, preferred_element_type=F32) + run_ref[:, 0:1]
    r1 = jnp.sum(oh1 * cum, axis=0, keepdims=True).astype(I32)
    r2 = jnp.sum(oh2 * cum, axis=0, keepdims=True).astype(I32)
    run_new = run_ref[...] + jnp.sum(oh, axis=1, keepdims=True)
    run_ref[...] = run_new
    cnt_ref[...] = run_new.astype(I32)

    row = _iota2((8, TM), 0)
    ri_ref[0] = jnp.where(row == 0, x1, jnp.where(row == 1, x2, 0))
    rk_ref[0] = jnp.where(row == 0, r1, jnp.where(row == 1, r2, 0))
    wrow = _iota2((128, TM), 0)
    rw_ref[...] = jnp.where(wrow == 0, w1, jnp.where(wrow == 1, w2, 0.0)).T


def _outproj_kernel(ycp_ref, ycs_ref, xp_ref, xs_ref, gtp_ref, gts_ref, shp_ref, shs_ref, scp_ref, scs_ref,
                    wo_ref, g_ref, wr_ref, br_ref,
                    x1_ref, hp_ref, ri_ref, rk_ref, rw_ref, cnt_ref, run_ref):
    i = pl.program_id(0)

    @pl.when(i == 0)
    def _():
        run_ref[...] = jnp.zeros((NE, 128), F32)

    def tile(yc, x, gt, sh, sc):
        mix = jnp.dot(yc, wo_ref[...], preferred_element_type=F32)
        x1 = x + gt * mix
        x1_ref[...] = x1
        h2 = _rms(x1, g_ref[...]) * (1.0 + sc) + sh
        hp_ref[...] = _pack_halves(h2)
        _route(h2, wr_ref, br_ref, run_ref, ri_ref, rk_ref, rw_ref, cnt_ref)

    @pl.when(i < N_TILES_P)
    def _():
        tile(ycp_ref[...], xp_ref[...], _prompt_row(gtp_ref), _prompt_row(shp_ref), _prompt_row(scp_ref))

    @pl.when(i >= N_TILES_P)
    def _():
        tile(ycs_ref[...], xs_ref[...], _sample_rows(gts_ref), _sample_rows(shs_ref), _sample_rows(scs_ref))


def _outproj(ycp, ycs, xp, xs, mod, w_out_b, g_ffn, w_r, b_r):
    return pl.pallas_call(
        _outproj_kernel,
        out_shape=(jax.ShapeDtypeStruct((T, D), F32), jax.ShapeDtypeStruct((T, DH), U32),
                   jax.ShapeDtypeStruct((N_TILES, 8, TM), I32), jax.ShapeDtypeStruct((N_TILES, 8, TM), I32),
                   jax.ShapeDtypeStruct((T, 128), F32), jax.ShapeDtypeStruct((NE, 128), I32)),
        grid=(N_TILES,),
        in_specs=[pl.BlockSpec((TM, D), lambda i: (_p_tile(i), 0)), pl.BlockSpec((TM, D), lambda i: (_s_tile(i), 0)),
                  pl.BlockSpec((TM, D), lambda i: (_p_tile(i), 0)), pl.BlockSpec((TM, D), lambda i: (_s_tile(i), 0)),
                  *_mod_specs(2), *_mod_specs(3), *_mod_specs(4),
                  _const_spec((D, D)), _const_spec((1, D)), _const_spec((2 * R_ROWS, D)), _const_spec((R_ROWS, 1))],
        out_specs=(pl.BlockSpec((TM, D), lambda i: (i, 0)), pl.BlockSpec((TM, DH), lambda i: (i, 0)),
                   pl.BlockSpec((1, 8, TM), lambda i: (i, 0, 0)), pl.BlockSpec((1, 8, TM), lambda i: (i, 0, 0)),
                   pl.BlockSpec((TM, 128), lambda i: (i, 0)), pl.BlockSpec((NE, 128), lambda i: (0, 0))),
        scratch_shapes=[pltpu.VMEM((NE, 128), F32)],
        compiler_params=_cparams(),
        name="outproj",
    )(ycp, ycs, xp, xs, mod, mod, mod, mod, mod, mod, w_out_b, g_ffn, w_r, b_r)


def _dispatch_kernel(start_ref, ri_ref, rk_ref, hp_ref, xs_ref, sem):
    i = pl.program_id(0)

    def issue(r, carry):
        for slot in range(2):
            pos = start_ref[ri_ref[0, slot, r]] + rk_ref[0, slot, r]
            pltpu.make_async_copy(hp_ref.at[pl.ds(i * TM + r, 1), :], xs_ref.at[pl.ds(pos, 1), :], sem).start()
        return carry

    lax.fori_loop(0, TM, issue, 0)
    pltpu.make_async_copy(hp_ref.at[pl.ds(0, 2 * TM), :], xs_ref.at[pl.ds(0, 2 * TM), :], sem).wait()


def _dispatch(row_start, ridx, rrank, hp):
    grid_spec = pltpu.PrefetchScalarGridSpec(
        num_scalar_prefetch=1,
        grid=(N_TILES,),
        in_specs=[pl.BlockSpec((1, 8, TM), lambda i, s: (i, 0, 0), memory_space=pltpu.SMEM),
                  pl.BlockSpec((1, 8, TM), lambda i, s: (i, 0, 0), memory_space=pltpu.SMEM),
                  pl.BlockSpec(memory_space=pl.ANY)],
        out_specs=pl.BlockSpec(memory_space=pl.ANY),
        scratch_shapes=[pltpu.SemaphoreType.DMA(())],
    )
    return pl.pallas_call(
        _dispatch_kernel,
        out_shape=jax.ShapeDtypeStruct((E_ROWS, DH), U32),
        grid_spec=grid_spec,
        compiler_params=_cparams(),
        name="dispatch",
    )(row_start, ridx, rrank, hp)


def _moe_kernel(te_ref, first_ref, nval_ref, nt_ref, x_ref, wg_ref, wu_ref, wd_ref, y_ref, wgu_b, wd_b):
    i = pl.program_id(0)

    @pl.when(i < nt_ref[0])
    def _():
        @pl.when(first_ref[i] == 1)
        def _():
            wgu_b[:, 0:DE] = wg_ref[0].astype(BF16)
            wgu_b[:, DE:2 * DE] = wu_ref[0].astype(BF16)
            wd_b[...] = wd_ref[0].astype(BF16)

        valid = _iota2((E_TILE, DH), 0) < nval_ref[i]
        lo, hi = _unpack_halves(jnp.where(valid, x_ref[...], jnp.uint32(0)))
        x = jnp.concatenate([lo.astype(BF16), hi.astype(BF16)], axis=1)
        ab = jnp.dot(x, wgu_b[...], preferred_element_type=F32)
        hid = (_silu(ab[:, 0:DE]) * ab[:, DE:2 * DE]).astype(BF16)
        y_ref[...] = _pack_halves(jnp.dot(hid, wd_b[...], preferred_element_type=F32))


def _moe(tile_expert, tile_first, tile_nvalid, n_tiles, xs, w_eg, w_eu, w_ed):
    def tile_map(i, te, fi, nv, nt):
        return (jnp.minimum(i, nt[0] - 1), 0)

    def w_map(i, te, fi, nv, nt):
        return (te[i], 0, 0)

    grid_spec = pltpu.PrefetchScalarGridSpec(
        num_scalar_prefetch=4,
        grid=(E_TILES_MAX,),
        in_specs=[pl.BlockSpec((E_TILE, DH), tile_map),
                  pl.BlockSpec((1, D, DE), w_map), pl.BlockSpec((1, D, DE), w_map), pl.BlockSpec((1, DE, D), w_map)],
        out_specs=pl.BlockSpec((E_TILE, DH), tile_map),
        scratch_shapes=[pltpu.VMEM((D, 2 * DE), BF16), pltpu.VMEM((DE, D), BF16)],
    )
    return pl.pallas_call(
        _moe_kernel,
        out_shape=jax.ShapeDtypeStruct((E_ROWS, DH), U32),
        grid_spec=grid_spec,
        compiler_params=_cparams(),
        name="moe",
    )(tile_expert, tile_first, tile_nvalid, n_tiles, xs, w_eg, w_eu, w_ed)


def _plan(counts):
    ntile = (counts + E_TILE - 1) // E_TILE
    tend = jnp.cumsum(ntile)
    tbeg = tend - ntile
    n_tiles = tend[-1]
    tid = jnp.minimum(jnp.arange(E_TILES_MAX, dtype=I32), n_tiles - 1)
    tile_expert = jnp.minimum(jnp.sum((tend[None, :] <= tid[:, None]).astype(I32), axis=1), NE - 1)
    tile_first = (tid == tbeg[tile_expert]).astype(I32)
    tile_nvalid = jnp.minimum(counts[tile_expert] - (tid - tbeg[tile_expert]) * E_TILE, E_TILE)
    return tbeg * E_TILE, tile_expert, tile_first, tile_nvalid, n_tiles.reshape(1)


def _final_kernel(start_ref, ri_ref, rk_ref, x1_ref, rw_ref, gtp_ref, gts_ref, g_ref, ys_ref,
                  yp_ref, ysm_ref, buf0, buf1, sem):
    i = pl.program_id(0)
    bufs = (buf0, buf1)

    def issue(r, carry):
        for slot in range(2):
            pos = start_ref[ri_ref[0, slot, r]] + rk_ref[0, slot, r]
            pltpu.make_async_copy(ys_ref.at[pl.ds(pos, 1), :], bufs[slot].at[pl.ds(r, 1), :], sem).start()
        return carry

    lax.fori_loop(0, TM, issue, 0)
    for slot in range(2):
        pltpu.make_async_copy(ys_ref.at[pl.ds(0, TM), :], bufs[slot], sem).wait()

    lo0, hi0 = _unpack_halves(buf0[...])
    lo1, hi1 = _unpack_halves(buf1[...])
    w0 = rw_ref[:, 0:1]
    w1 = rw_ref[:, 1:2]
    moe = jnp.concatenate([w0 * lo0 + w1 * lo1, w0 * hi0 + w1 * hi1], axis=1)

    @pl.when(i < N_TILES_P)
    def _():
        yp_ref[...] = _rms(x1_ref[...] + _prompt_row(gtp_ref) * moe, g_ref[...])

    @pl.when(i >= N_TILES_P)
    def _():
        ysm_ref[...] = _rms(x1_ref[...] + _sample_rows(gts_ref) * moe, g_ref[...])


def _final(row_start, ridx, rrank, x1, rw, mod, g_fin, ys):
    grid_spec = pltpu.PrefetchScalarGridSpec(
        num_scalar_prefetch=1,
        grid=(N_TILES,),
        in_specs=[pl.BlockSpec((1, 8, TM), lambda i, s: (i, 0, 0), memory_space=pltpu.SMEM),
                  pl.BlockSpec((1, 8, TM), lambda i, s: (i, 0, 0), memory_space=pltpu.SMEM),
                  pl.BlockSpec((TM, D), lambda i, s: (i, 0)),
                  pl.BlockSpec((TM, 128), lambda i, s: (i, 0)),
                  *_mod_specs(5),
                  pl.BlockSpec((1, D), lambda i, s: (0, 0)),
                  pl.BlockSpec(memory_space=pl.ANY)],
        out_specs=(pl.BlockSpec((TM, D), lambda i, s: (_p_tile(i), 0)),
                   pl.BlockSpec((TM, D), lambda i, s: (_s_tile(i), 0))),
        scratch_shapes=[pltpu.VMEM((TM, DH), U32), pltpu.VMEM((TM, DH), U32), pltpu.SemaphoreType.DMA(())],
    )
    return pl.pallas_call(
        _final_kernel,
        out_shape=(jax.ShapeDtypeStruct((T_P, D), F32), jax.ShapeDtypeStruct((T_S, D), F32)),
        grid_spec=grid_spec,
        compiler_params=_cparams(),
        name="final",
    )(row_start, ridx, rrank, x1, rw, mod, mod, g_fin, ys)


def kernel(x_prompt, x_sample, c_prompt, c_sample, state_conv, state_gla, w_ada, b_ada, norm_mix, w_in, conv_w,
           conv_b, w_gate_up, b_gate, norm_conv, norm_gla, w_out, norm_ffn, w_coarse, b_coarse, w_fine, b_fine,
           w_exp_gate, w_exp_up, w_exp_down, norm_final):
    xp = x_prompt.reshape(T_P, D)
    xs = x_sample.reshape(T_S, D)
    c_all = jnp.concatenate([c_sample, c_prompt, jnp.zeros((C_ROWS - B_S - B_P, D), F32)], axis=0)
    mod = _ada(c_all, w_ada[0], b_ada[0][None, :])

    w_in_b = jnp.pad(w_in[0], ((0, 0), (0, PROJ_PAD - w_in.shape[-1]))).astype(BF16)
    w_out_b = w_out[0].astype(BF16)
    wgu_b = jnp.pad(w_gate_up[0], ((0, 128 - RANK), (0, 0))).astype(BF16)
    w_r = jnp.concatenate([w_coarse[0].T, jnp.zeros((8 - N_GROUPS, D), F32), w_fine[0].T], axis=0)
    w_r_hi = w_r.astype(BF16)
    w_r_split = jnp.concatenate([w_r_hi, (w_r - w_r_hi.astype(F32)).astype(BF16)], axis=0)
    b_r = jnp.concatenate([b_coarse[0], jnp.zeros((8 - N_GROUPS,), F32), b_fine[0]])[:, None]
    g_mix, g_ffn, g_fin = norm_mix[0][None, :], norm_ffn[0][None, :], norm_final[None, :]
    mix_consts = (conv_w[0], conv_b[0][None, :], wgu_b, b_gate[0][None, :], norm_conv[0][None, :],
                  norm_gla[0][None, :])

    proj = _inproj(xp, xs, mod, g_mix, w_in_b)
    ycat_p, nconv_p, nst_p = _mixer_p(proj, *mix_consts)
    ycat_s, nconv_s, nst_s = _mixer_s(proj, state_conv[0], state_gla[0].reshape(B_S, HK, DV), *mix_consts)

    x1, hp, ridx, rrank, rw, counts = _outproj(ycat_p, ycat_s, xp, xs, mod, w_out_b, g_ffn, w_r_split, b_r)
    row_start, tile_expert, tile_first, tile_nvalid, n_tiles = _plan(counts[:, 0])
    x_sorted = _dispatch(row_start, ridx, rrank, hp)
    y_sorted = _moe(tile_expert, tile_first, tile_nvalid, n_tiles, x_sorted,
                    w_exp_gate[0], w_exp_up[0], w_exp_down[0])
    y_p, y_s = _final(row_start, ridx, rrank, x1, rw, mod, g_fin, y_sorted)

    new_gla_p = nst_p.reshape(B_P, DV, H, DK).transpose(0, 2, 3, 1)[None]
    new_gla_s = nst_s.reshape(1, B_S, H, DK, DV)
    return (y_p.reshape(B_P, L_P, D), y_s.reshape(B_S, L_S, D), nconv_p[None], new_gla_p,
            nconv_s[None], new_gla_s)
```

```python
import functools

import jax
import jax.numpy as jnp
from jax import lax
from jax.experimental import pallas as pl
from jax.experimental.pallas import tpu as pltpu

F32 = jnp.float32
BF16 = jnp.bfloat16
I32 = jnp.int32

D = 2048
CW = 1024
H = 8
DK = 64
DV = 128
HK = H * DK
HV = H * DV
RANK = 16
TAU = 16.0
N_GROUPS = 4
EPG = 8
NE = N_GROUPS * EPG
DE = 512
EPS = 1e-6

B_P, L_P = 4, 2048
B_S, L_S = 128, 8
T_P = B_P * L_P
T_S = B_S * L_S
T = T_P + T_S

TM = 256
CH = 64
N_TILES = T // TM
N_TILES_P = T_P // TM
TILES_PER_SEQ = L_P // TM
SEQ_PER_TILE = TM // L_S
SEQ_PER_CHUNK = CH // L_S

PROJ_MAIN = 3 * CW + 2 * HK + 2 * HV
PROJ_PAD = PROJ_MAIN + 128
C_ROWS = 136
P_ROW_BLOCK = B_S // 8
R_ROWS = 40

E_TILE = 256
E_TILES_MAX = (2 * T) // E_TILE + NE
E_CAP = T
E_CAP_TILES = E_CAP // E_TILE
E_ROWS = NE * E_CAP
VMEM_LIMIT = 60 * 1024 * 1024


def _cparams(n_axes=1, vmem=VMEM_LIMIT):
    return pltpu.CompilerParams(dimension_semantics=("arbitrary",) * n_axes, vmem_limit_bytes=vmem)


def _rms(x, g):
    return x * lax.rsqrt(jnp.mean(x * x, axis=-1, keepdims=True) + EPS) * g


def _sigmoid(x):
    return 1.0 / (1.0 + jnp.exp(-x))


def _silu(x):
    return x * _sigmoid(x)


def _log_sigmoid(x):
    return jnp.minimum(x, 0.0) - jnp.log1p(jnp.exp(-jnp.abs(x)))


def _iota2(shape, axis):
    return lax.broadcasted_iota(I32, shape, axis)


def _expand_rows(ref, n, reps):
    return jnp.concatenate([jnp.broadcast_to(ref[j:j + 1, :], (reps, ref.shape[-1])) for j in range(n)], axis=0)


def _prompt_row(ref):
    return ref[pl.ds(pl.program_id(0) // TILES_PER_SEQ, 1), :]


def _sample_rows(ref):
    return _expand_rows(ref, SEQ_PER_TILE, L_S)


def _p_tile(i):
    return jnp.minimum(i, N_TILES_P - 1)


def _s_tile(i):
    return jnp.maximum(i - N_TILES_P, 0)


def _mod_specs(col):
    return [pl.BlockSpec((8, D), lambda i, *_: (P_ROW_BLOCK, col)),
            pl.BlockSpec((SEQ_PER_TILE, D), lambda i, *_: (_s_tile(i), col))]


def _const_spec(shape):
    zeros = (0,) * len(shape)
    return pl.BlockSpec(shape, lambda *_: zeros, pipeline_mode=pl.Buffered(1))


def _ada_kernel(c_ref, w_ref, b_ref, o_ref):
    s = _silu(c_ref[...]).astype(BF16)
    o_ref[...] = jnp.dot(s, w_ref[...].astype(BF16), preferred_element_type=F32) + b_ref[...]


def _ada(c_all, w_ada, b_ada):
    tn = 512
    return pl.pallas_call(
        _ada_kernel,
        out_shape=jax.ShapeDtypeStruct((C_ROWS, 6 * D), F32),
        grid=(6 * D // tn,),
        in_specs=[pl.BlockSpec((C_ROWS, D), lambda j: (0, 0)),
                  pl.BlockSpec((D, tn), lambda j: (0, j)),
                  pl.BlockSpec((1, tn), lambda j: (0, j))],
        out_specs=pl.BlockSpec((C_ROWS, tn), lambda j: (0, j)),
        compiler_params=_cparams(),
        name="ada",
    )(c_all, w_ada, b_ada)


def _inproj_kernel(xp_ref, xs_ref, shp_ref, shs_ref, scp_ref, scs_ref, g_ref, w_ref, o_ref, h_scr):
    i = pl.program_id(0)

    @pl.when(i < N_TILES_P)
    def _():
        h = _rms(xp_ref[...], g_ref[...]) * (1.0 + _prompt_row(scp_ref)) + _prompt_row(shp_ref)
        h_scr[...] = h.astype(BF16)

    @pl.when(i >= N_TILES_P)
    def _():
        h = _rms(xs_ref[...], g_ref[...]) * (1.0 + _sample_rows(scs_ref)) + _sample_rows(shs_ref)
        h_scr[...] = h.astype(BF16)

    o_ref[...] = jnp.dot(h_scr[...], w_ref[...], preferred_element_type=F32)


def _inproj(xp, xs, mod, g_mix, w_in_b):
    return pl.pallas_call(
        _inproj_kernel,
        out_shape=jax.ShapeDtypeStruct((T, PROJ_PAD), F32),
        grid=(N_TILES,),
        in_specs=[pl.BlockSpec((TM, D), lambda i: (_p_tile(i), 0)),
                  pl.BlockSpec((TM, D), lambda i: (_s_tile(i), 0)),
                  *_mod_specs(0), *_mod_specs(1),
                  _const_spec((1, D)), _const_spec((D, PROJ_PAD))],
        out_specs=pl.BlockSpec((TM, PROJ_PAD), lambda i: (i, 0)),
        scratch_shapes=[pltpu.VMEM((TM, D), BF16)],
        compiler_params=_cparams(),
        name="inproj",
    )(xp, xs, mod, mod, mod, mod, g_mix, w_in_b)


def _gla_prep(p_ref, wgu_ref, bg_ref, rows, seg):
    q = p_ref[:, 3 * CW:3 * CW + HK]
    k = p_ref[:, 3 * CW + HK:3 * CW + 2 * HK]
    a = p_ref[:, PROJ_MAIN:PROJ_PAD].astype(BF16)
    z = jnp.dot(a, wgu_ref[...], preferred_element_type=F32) + bg_ref[...]
    la = _log_sigmoid(z) * (1.0 / TAU)
    r = _iota2((rows, rows), 0)
    c = _iota2((rows, rows), 1)
    same = (r // seg) == (c // seg)
    tri = jnp.where(same & (c <= r), 1.0, 0.0).astype(F32)
    tot = jnp.where(same, 1.0, 0.0).astype(F32)
    b = jnp.dot(tri, la, preferred_element_type=F32, precision=lax.Precision.HIGHEST)
    bl = jnp.dot(tot, la, preferred_element_type=F32, precision=lax.Precision.HIGHEST)
    qt = q * jnp.exp(b) * (DK ** -0.5)
    kt = k * jnp.exp(-b)
    kend = k * jnp.exp(bl - b)
    return qt, kt, kend, bl


def _head_stack(x):
    rows = x.shape[0]
    t = jnp.concatenate([x] * H, axis=0)
    keep = (_iota2((H * rows, HK), 0) // rows) == (_iota2((H * rows, HK), 1) // DK)
    return jnp.where(keep, t, 0.0)


def _gla_out(o_h, g_h, ngl):
    on = o_h * lax.rsqrt(jnp.mean(o_h * o_h, axis=-1, keepdims=True) + EPS) * ngl
    return on * _silu(g_h)


def _conv_out(bg, u, um1, um2, cw_ref, cb_ref, ncv_ref):
    conv_y = cb_ref[...] + cw_ref[0:1, :] * um2 + cw_ref[1:2, :] * um1 + cw_ref[2:3, :] * u
    return _rms(bg * conv_y, ncv_ref[...])


V_OFF = 3 * CW + 2 * HK
G_OFF = V_OFF + HV


def _mixer_p_kernel(p_ref, cw_ref, cb_ref, wgu_ref, bg_ref, ncv_ref, ngl_ref,
                    y_ref, nconv_ref, nst_ref, ubuf, st_ref):
    j = pl.program_id(0) % TILES_PER_SEQ

    @pl.when(j == 0)
    def _():
        ubuf[0:8, :] = jnp.zeros((8, CW), F32)
        st_ref[...] = jnp.zeros((DV, HK), F32)

    bgate = p_ref[:, 0:CW]
    u = p_ref[:, CW:2 * CW] * p_ref[:, 2 * CW:3 * CW]
    ubuf[8:8 + TM, :] = u
    um1 = ubuf[7:7 + TM, :]
    um2 = ubuf[6:6 + TM, :]
    y_ref[:, 0:CW] = _conv_out(bgate, u, um1, um2, cw_ref, cb_ref, ncv_ref).astype(BF16)
    ubuf[6:8, :] = u[TM - 2:TM, :]

    @pl.when(j == TILES_PER_SEQ - 1)
    def _():
        nconv_ref[0] = u[TM - 2:TM, :]

    qt, kt, kend, bl = _gla_prep(p_ref, wgu_ref, bg_ref, TM, CH)
    causal = _iota2((H * CH, CH), 0) % CH >= _iota2((H * CH, CH), 1)
    ngl = ngl_ref[...]
    for c in range(TM // CH):
        r0 = c * CH
        lhs = _head_stack(qt[r0:r0 + CH]).astype(BF16)
        sc = lax.dot_general(lhs, kt[r0:r0 + CH].astype(BF16), (((1,), (1,)), ((), ())),
                             preferred_element_type=F32)
        sc = jnp.where(causal, sc, 0.0).astype(BF16)
        st = st_ref[...]
        o_inter = lax.dot_general(lhs, st.astype(BF16), (((1,), (1,)), ((), ())),
                                  preferred_element_type=F32)
        vs = []
        for h in range(H):
            v_h = p_ref[r0:r0 + CH, V_OFF + h * DV:V_OFF + (h + 1) * DV].astype(BF16)
            g_h = p_ref[r0:r0 + CH, G_OFF + h * DV:G_OFF + (h + 1) * DV]
            vs.append(v_h)
            o_h = jnp.dot(sc[h * CH:(h + 1) * CH], v_h, preferred_element_type=F32) + o_inter[h * CH:(h + 1) * CH]
            y_ref[r0:r0 + CH, CW + h * DV:CW + (h + 1) * DV] = _gla_out(o_h, g_h, ngl).astype(BF16)
        vstack = jnp.concatenate(vs, axis=0)
        kstack = _head_stack(kend[r0:r0 + CH]).astype(BF16)
        kv_t = lax.dot_general(vstack, kstack, (((0,), (0,)), ((), ())), preferred_element_type=F32)
        st_ref[...] = st * jnp.exp(bl[r0:r0 + 1, :]) + kv_t

    @pl.when(j == TILES_PER_SEQ - 1)
    def _():
        nst_ref[0] = st_ref[...]


def _mixer_p(proj, conv_w, conv_b, wgu_b, b_gate, n_conv, n_gla):
    return pl.pallas_call(
        _mixer_p_kernel,
        out_shape=(jax.ShapeDtypeStruct((T_P, D), BF16),
                   jax.ShapeDtypeStruct((B_P, 2, CW), F32),
                   jax.ShapeDtypeStruct((B_P, DV, HK), F32)),
        grid=(N_TILES_P,),
        in_specs=[pl.BlockSpec((TM, PROJ_PAD), lambda i: (i, 0)),
                  _const_spec((3, CW)), _const_spec((1, CW)), _const_spec((128, HK)), _const_spec((1, HK)),
                  _const_spec((1, CW)), _const_spec((1, DV))],
        out_specs=(pl.BlockSpec((TM, D), lambda i: (i, 0)),
                   pl.BlockSpec((1, 2, CW), lambda i: (i // TILES_PER_SEQ, 0, 0)),
                   pl.BlockSpec((1, DV, HK), lambda i: (i // TILES_PER_SEQ, 0, 0))),
        scratch_shapes=[pltpu.VMEM((8 + TM, CW), F32), pltpu.VMEM((DV, HK), F32)],
        compiler_params=_cparams(),
        name="mixer_p",
    )(proj, conv_w, conv_b, wgu_b, b_gate, n_conv, n_gla)


def _mixer_s_kernel(p_ref, sconv_ref, sst_ref, cw_ref, cb_ref, wgu_ref, bg_ref, ncv_ref, ngl_ref,
                    y_ref, nconv_ref, nst_ref, ubuf):
    nseq = SEQ_PER_CHUNK
    bgate = p_ref[:, 0:CW]
    u = p_ref[:, CW:2 * CW] * p_ref[:, 2 * CW:3 * CW]
    ubuf[0:8, :] = jnp.zeros((8, CW), F32)
    ubuf[8:8 + CH, :] = u
    tpos = _iota2((CH, CW), 0) % L_S
    s0 = jnp.concatenate([jnp.broadcast_to(sconv_ref[s, 0:1, :], (L_S, CW)) for s in range(nseq)], axis=0)
    s1 = jnp.concatenate([jnp.broadcast_to(sconv_ref[s, 1:2, :], (L_S, CW)) for s in range(nseq)], axis=0)
    um1 = jnp.where(tpos == 0, s1, ubuf[7:7 + CH, :])
    um2 = jnp.where(tpos == 0, s0, jnp.where(tpos == 1, s1, ubuf[6:6 + CH, :]))
    y_ref[:, 0:CW] = _conv_out(bgate, u, um1, um2, cw_ref, cb_ref, ncv_ref).astype(BF16)
    for s in range(nseq):
        nconv_ref[s] = u[s * L_S + L_S - 2:(s + 1) * L_S, :]

    qt, kt, kend, bl = _gla_prep(p_ref, wgu_ref, bg_ref, CH, L_S)
    rr = _iota2((H * CH, CH), 0) % CH
    cc = _iota2((H * CH, CH), 1)
    causal = (rr >= cc) & ((rr // L_S) == (cc // L_S))
    ngl = ngl_ref[...]
    lhs_f = _head_stack(qt)
    sc = lax.dot_general(lhs_f.astype(BF16), kt.astype(BF16), (((1,), (1,)), ((), ())),
                         preferred_element_type=F32)
    sc = jnp.where(causal, sc, 0.0).astype(BF16)
    kstack_f = _head_stack(kend)
    v_all = p_ref[:, V_OFF:V_OFF + HV]
    decay_t = jnp.exp(jnp.concatenate([bl, jnp.zeros((128 - CH, HK), F32)], axis=0).T)
    o_inter = []
    for s in range(nseq):
        rs = [slice(h * CH + s * L_S, h * CH + (s + 1) * L_S) for h in range(H)]
        lhs_s = jnp.concatenate([lhs_f[r] for r in rs], axis=0).astype(BF16)
        k_s = jnp.concatenate([kstack_f[r] for r in rs], axis=0).astype(BF16)
        v_s = jnp.concatenate([v_all[s * L_S:(s + 1) * L_S, h * DV:(h + 1) * DV] for h in range(H)],
                              axis=0).astype(BF16)
        st = sst_ref[s]
        o_inter.append(jnp.dot(lhs_s, st.astype(BF16), preferred_element_type=F32))
        kv = lax.dot_general(k_s, v_s, (((0,), (0,)), ((), ())), preferred_element_type=F32)
        nst_ref[s] = st * decay_t[:, s * L_S:s * L_S + 1] + kv
    for h in range(H):
        v_h = v_all[:, h * DV:(h + 1) * DV].astype(BF16)
        g_h = p_ref[:, G_OFF + h * DV:G_OFF + (h + 1) * DV]
        oi_h = jnp.concatenate([o_inter[s][h * L_S:(h + 1) * L_S] for s in range(nseq)], axis=0)
        o_h = jnp.dot(sc[h * CH:(h + 1) * CH], v_h, preferred_element_type=F32) + oi_h
        y_ref[:, CW + h * DV:CW + (h + 1) * DV] = _gla_out(o_h, g_h, ngl).astype(BF16)


def _mixer_s(proj, sconv, sst, conv_w, conv_b, wgu_b, b_gate, n_conv, n_gla):
    nseq = SEQ_PER_CHUNK
    off = T_P // CH
    return pl.pallas_call(
        _mixer_s_kernel,
        out_shape=(jax.ShapeDtypeStruct((T_S, D), BF16),
                   jax.ShapeDtypeStruct((B_S, 2, CW), F32),
                   jax.ShapeDtypeStruct((B_S, HK, DV), F32)),
        grid=(T_S // CH,),
        in_specs=[pl.BlockSpec((CH, PROJ_PAD), lambda i: (i + off, 0)),
                  pl.BlockSpec((nseq, 2, CW), lambda i: (i, 0, 0)),
                  pl.BlockSpec((nseq, HK, DV), lambda i: (i, 0, 0)),
                  _const_spec((3, CW)), _const_spec((1, CW)), _const_spec((128, HK)), _const_spec((1, HK)),
                  _const_spec((1, CW)), _const_spec((1, DV))],
        out_specs=(pl.BlockSpec((CH, D), lambda i: (i, 0)),
                   pl.BlockSpec((nseq, 2, CW), lambda i: (i, 0, 0)),
                   pl.BlockSpec((nseq, HK, DV), lambda i: (i, 0, 0))),
        scratch_shapes=[pltpu.VMEM((8 + CH, CW), F32)],
        compiler_params=_cparams(),
        name="mixer_s",
    )(proj, sconv, sst, conv_w, conv_b, wgu_b, b_gate, n_conv, n_gla)


def _route(h2, wr_ref, br_ref, run_ref, pos_ref, rw_ref, cnt_ref):
    lt = lax.dot_general(wr_ref[...], h2.astype(BF16), (((1,), (1,)), ((), ())),
                         preferred_element_type=F32) + br_ref[...]

    coarse = lt[0:N_GROUPS]
    cmax = jnp.max(coarse, axis=0, keepdims=True)
    gi = _iota2((N_GROUPS, TM), 0)
    grp = jnp.min(jnp.where(coarse == cmax, gi, N_GROUPS), axis=0, keepdims=True)
    p_sel = 1.0 / jnp.sum(jnp.exp(coarse - cmax), axis=0, keepdims=True)
    fine = jnp.zeros((EPG, TM), F32)
    for g in range(N_GROUPS):
        fine = jnp.where(grp == g, lt[8 + g * EPG:8 + (g + 1) * EPG], fine)
    ei = _iota2((EPG, TM), 0)
    f1 = jnp.max(fine, axis=0, keepdims=True)
    i1 = jnp.min(jnp.where(fine == f1, ei, EPG), axis=0, keepdims=True)
    rest = jnp.where(ei == i1, -jnp.inf, fine)
    f2 = jnp.max(rest, axis=0, keepdims=True)
    i2 = jnp.min(jnp.where(rest == f2, ei, EPG), axis=0, keepdims=True)
    e2 = jnp.exp(f2 - f1)
    w1 = p_sel / (1.0 + e2)
    w2 = p_sel * e2 / (1.0 + e2)
    x1 = grp * EPG + i1
    x2 = grp * EPG + i2

    er = _iota2((NE, TM), 0)
    oh1 = jnp.where(er == x1, 1.0, 0.0)
    oh2 = jnp.where(er == x2, 1.0, 0.0)
    oh = oh1 + oh2
    before = jnp.where(_iota2((TM, TM), 0) < _iota2((TM, TM), 1), 1.0, 0.0).astype(BF16)
    cum = jnp.dot(oh.astype(BF16), before, preferred_element_type=F32) + run_ref[:, 0:1]
    r1 = jnp.sum(oh1 * cum, axis=0, keepdims=True).astype(I32)
    r2 = jnp.sum(oh2 * cum, axis=0, keepdims=True).astype(I32)
    run_new = run_ref[...] + jnp.sum(oh, axis=1, keepdims=True)
    run_ref[...] = run_new
    cnt_ref[...] = run_new.astype(I32)

    row = _iota2((8, TM), 0)
    pos_ref[...] = jnp.where(row == 0, x1 * E_CAP + r1, jnp.where(row == 1, x2 * E_CAP + r2, 0))
    wrow = _iota2((128, TM), 0)
    rw_ref[...] = jnp.where(wrow == 0, w1, jnp.where(wrow == 1, w2, 0.0)).T


def _outproj_kernel(ycp_ref, ycs_ref, xp_ref, xs_ref, gtp_ref, gts_ref, shp_ref, shs_ref, scp_ref, scs_ref,
                    wo_ref, g_ref, wr_ref, br_ref,
                    x1_ref, pos_ref, rw_ref, cnt_ref, xs_out,
                    run_ref, h2_scr, pos_v, pos_s, sem, psem):
    i = pl.program_id(0)

    @pl.when(i == 0)
    def _():
        run_ref[...] = jnp.zeros((NE, 128), F32)

    def tile(yc, x, gt, sh, sc):
        mix = jnp.dot(yc, wo_ref[...], preferred_element_type=F32)
        x1 = x + gt * mix
        x1_ref[...] = x1
        h2 = _rms(x1, g_ref[...]) * (1.0 + sc) + sh
        h2_scr[...] = h2
        _route(h2, wr_ref, br_ref, run_ref, pos_v, rw_ref, cnt_ref)

    @pl.when(i < N_TILES_P)
    def _():
        tile(ycp_ref[...], xp_ref[...], _prompt_row(gtp_ref), _prompt_row(shp_ref), _prompt_row(scp_ref))

    @pl.when(i >= N_TILES_P)
    def _():
        tile(ycs_ref[...], xs_ref[...], _sample_rows(gts_ref), _sample_rows(shs_ref), _sample_rows(scs_ref))

    pos_ref[0] = pos_v[...]
    to_smem = pltpu.make_async_copy(pos_v, pos_s, psem)
    to_smem.start()
    to_smem.wait()

    def issue(r, carry):
        for slot in range(2):
            pltpu.make_async_copy(h2_scr.at[pl.ds(r, 1), :], xs_out.at[pl.ds(pos_s[slot, r], 1), :], sem).start()
        return carry

    lax.fori_loop(0, TM, issue, 0)
    for slot in range(2):
        pltpu.make_async_copy(h2_scr, xs_out.at[pl.ds(0, TM), :], sem).wait()


def _outproj(ycp, ycs, xp, xs, mod, w_out_b, g_ffn, w_r, b_r):
    return pl.pallas_call(
        _outproj_kernel,
        out_shape=(jax.ShapeDtypeStruct((T, D), F32), jax.ShapeDtypeStruct((N_TILES, 8, TM), I32),
                   jax.ShapeDtypeStruct((T, 128), F32), jax.ShapeDtypeStruct((NE, 128), I32),
                   jax.ShapeDtypeStruct((E_ROWS, D), F32)),
        grid=(N_TILES,),
        in_specs=[pl.BlockSpec((TM, D), lambda i: (_p_tile(i), 0)), pl.BlockSpec((TM, D), lambda i: (_s_tile(i), 0)),
                  pl.BlockSpec((TM, D), lambda i: (_p_tile(i), 0)), pl.BlockSpec((TM, D), lambda i: (_s_tile(i), 0)),
                  *_mod_specs(2), *_mod_specs(3), *_mod_specs(4),
                  _const_spec((D, D)), _const_spec((1, D)), _const_spec((R_ROWS, D)), _const_spec((R_ROWS, 1))],
        out_specs=(pl.BlockSpec((TM, D), lambda i: (i, 0)), pl.BlockSpec((1, 8, TM), lambda i: (i, 0, 0)),
                   pl.BlockSpec((TM, 128), lambda i: (i, 0)), pl.BlockSpec((NE, 128), lambda i: (0, 0)),
                   pl.BlockSpec(memory_space=pl.ANY)),
        scratch_shapes=[pltpu.VMEM((NE, 128), F32), pltpu.VMEM((TM, D), F32), pltpu.VMEM((8, TM), I32),
                        pltpu.SMEM((8, TM), I32), pltpu.SemaphoreType.DMA(()), pltpu.SemaphoreType.DMA(())],
        compiler_params=_cparams(),
        name="outproj",
    )(ycp, ycs, xp, xs, mod, mod, mod, mod, mod, mod, w_out_b, g_ffn, w_r, b_r)


def _moe_kernel(tb_ref, te_ref, first_ref, nval_ref, nt_ref, x_ref, wg_ref, wu_ref, wd_ref, y_ref, wgu_b, wd_b):
    i = pl.program_id(0)

    @pl.when(i < nt_ref[0])
    def _():
        @pl.when(first_ref[i] == 1)
        def _():
            wgu_b[:, 0:DE] = wg_ref[0].astype(BF16)
            wgu_b[:, DE:2 * DE] = wu_ref[0].astype(BF16)
            wd_b[...] = wd_ref[0].astype(BF16)

        valid = _iota2((E_TILE, D), 0) < nval_ref[i]
        x = jnp.where(valid, x_ref[...], 0.0).astype(BF16)
        ab = jnp.dot(x, wgu_b[...], preferred_element_type=F32)
        hid = (_silu(ab[:, 0:DE]) * ab[:, DE:2 * DE]).astype(BF16)
        y_ref[...] = jnp.dot(hid, wd_b[...], preferred_element_type=F32)


def _moe(tile_block, tile_expert, tile_first, tile_nvalid, n_tiles, xs, w_eg, w_eu, w_ed):
    def tile_map(i, tb, te, fi, nv, nt):
        return (tb[i], 0)

    def w_map(i, tb, te, fi, nv, nt):
        return (te[i], 0, 0)

    grid_spec = pltpu.PrefetchScalarGridSpec(
        num_scalar_prefetch=5,
        grid=(E_TILES_MAX,),
        in_specs=[pl.BlockSpec((E_TILE, D), tile_map),
                  pl.BlockSpec((1, D, DE), w_map), pl.BlockSpec((1, D, DE), w_map), pl.BlockSpec((1, DE, D), w_map)],
        out_specs=pl.BlockSpec((E_TILE, D), tile_map),
        scratch_shapes=[pltpu.VMEM((D, 2 * DE), BF16), pltpu.VMEM((DE, D), BF16)],
    )
    return pl.pallas_call(
        _moe_kernel,
        out_shape=jax.ShapeDtypeStruct((E_ROWS, D), F32),
        grid_spec=grid_spec,
        compiler_params=_cparams(),
        name="moe",
    )(tile_block, tile_expert, tile_first, tile_nvalid, n_tiles, xs, w_eg, w_eu, w_ed)


def _plan(counts):
    ntile = (counts + E_TILE - 1) // E_TILE
    tend = jnp.cumsum(ntile)
    tbeg = tend - ntile
    n_tiles = tend[-1]
    tid = jnp.minimum(jnp.arange(E_TILES_MAX, dtype=I32), n_tiles - 1)
    tile_expert = jnp.minimum(jnp.sum((tend[None, :] <= tid[:, None]).astype(I32), axis=1), NE - 1)
    tile_in_expert = tid - tbeg[tile_expert]
    tile_block = tile_expert * E_CAP_TILES + tile_in_expert
    tile_first = (tile_in_expert == 0).astype(I32)
    tile_nvalid = jnp.minimum(counts[tile_expert] - tile_in_expert * E_TILE, E_TILE)
    return tile_block, tile_expert, tile_first, tile_nvalid, n_tiles.reshape(1)


def _final_kernel(pos_ref, x1_ref, rw_ref, gtp_ref, gts_ref, g_ref, ys_ref, yp_ref, ysm_ref, buf0, buf1, sem):
    i = pl.program_id(0)
    bufs = (buf0, buf1)

    def issue(r, carry):
        for slot in range(2):
            pltpu.make_async_copy(ys_ref.at[pl.ds(pos_ref[0, slot, r], 1), :], bufs[slot].at[pl.ds(r, 1), :],
                                  sem).start()
        return carry

    lax.fori_loop(0, TM, issue, 0)
    for slot in range(2):
        pltpu.make_async_copy(ys_ref.at[pl.ds(0, TM), :], bufs[slot], sem).wait()

    moe = rw_ref[:, 0:1] * buf0[...] + rw_ref[:, 1:2] * buf1[...]

    @pl.when(i < N_TILES_P)
    def _():
        yp_ref[...] = _rms(x1_ref[...] + _prompt_row(gtp_ref) * moe, g_ref[...])

    @pl.when(i >= N_TILES_P)
    def _():
        ysm_ref[...] = _rms(x1_ref[...] + _sample_rows(gts_ref) * moe, g_ref[...])


def _final(pos, x1, rw, mod, g_fin, ys):
    return pl.pallas_call(
        _final_kernel,
        out_shape=(jax.ShapeDtypeStruct((T_P, D), F32), jax.ShapeDtypeStruct((T_S, D), F32)),
        grid=(N_TILES,),
        in_specs=[pl.BlockSpec((1, 8, TM), lambda i: (i, 0, 0), memory_space=pltpu.SMEM),
                  pl.BlockSpec((TM, D), lambda i: (i, 0)),
                  pl.BlockSpec((TM, 128), lambda i: (i, 0)),
                  *_mod_specs(5),
                  _const_spec((1, D)),
                  pl.BlockSpec(memory_space=pl.ANY)],
        out_specs=(pl.BlockSpec((TM, D), lambda i: (_p_tile(i), 0)),
                   pl.BlockSpec((TM, D), lambda i: (_s_tile(i), 0))),
        scratch_shapes=[pltpu.VMEM((TM, D), F32), pltpu.VMEM((TM, D), F32), pltpu.SemaphoreType.DMA(())],
        compiler_params=_cparams(),
        name="final",
    )(pos, x1, rw, mod, mod, g_fin, ys)


def kernel(x_prompt, x_sample, c_prompt, c_sample, state_conv, state_gla, w_ada, b_ada, norm_mix, w_in, conv_w,
           conv_b, w_gate_up, b_gate, norm_conv, norm_gla, w_out, norm_ffn, w_coarse, b_coarse, w_fine, b_fine,
           w_exp_gate, w_exp_up, w_exp_down, norm_final):
    xp = x_prompt.reshape(T_P, D)
    xs = x_sample.reshape(T_S, D)
    c_all = jnp.concatenate([c_sample, c_prompt, jnp.zeros((C_ROWS - B_S - B_P, D), F32)], axis=0)
    mod = _ada(c_all, w_ada[0], b_ada[0][None, :])

    w_in_b = jnp.pad(w_in[0], ((0, 0), (0, PROJ_PAD - w_in.shape[-1]))).astype(BF16)
    w_out_b = w_out[0].astype(BF16)
    wgu_b = jnp.pad(w_gate_up[0], ((0, 128 - RANK), (0, 0))).astype(BF16)
    w_r = jnp.concatenate([w_coarse[0].T, jnp.zeros((8 - N_GROUPS, D), F32), w_fine[0].T], axis=0)
    b_r = jnp.concatenate([b_coarse[0], jnp.zeros((8 - N_GROUPS,), F32), b_fine[0]])[:, None]
    g_mix, g_ffn, g_fin = norm_mix[0][None, :], norm_ffn[0][None, :], norm_final[None, :]
    mix_consts = (conv_w[0], conv_b[0][None, :], wgu_b, b_gate[0][None, :], norm_conv[0][None, :],
                  norm_gla[0][None, :])

    proj = _inproj(xp, xs, mod, g_mix, w_in_b)
    ycat_p, nconv_p, nst_p = _mixer_p(proj, *mix_consts)
    ycat_s, nconv_s, nst_s = _mixer_s(proj, state_conv[0], state_gla[0].reshape(B_S, HK, DV), *mix_consts)

    x1, pos, rw, counts, x_sorted = _outproj(ycat_p, ycat_s, xp, xs, mod, w_out_b, g_ffn, w_r.astype(BF16), b_r)
    y_sorted = _moe(*_plan(counts[:, 0]), x_sorted, w_exp_gate[0], w_exp_up[0], w_exp_down[0])
    y_p, y_s = _final(pos, x1, rw, mod, g_fin, y_sorted)

    new_gla_p = nst_p.reshape(B_P, DV, H, DK).transpose(0, 2, 3, 1)[None]
    new_gla_s = nst_s.reshape(1, B_S, H, DK, DV)
    return (y_p.reshape(B_P, L_P, D), y_s.reshape(B_S, L_S, D), nconv_p[None], new_gla_p,
            nconv_s[None], new_gla_s)
```

```python
import functools

import jax
import jax.numpy as jnp
from jax import lax
from jax.experimental import pallas as pl
from jax.experimental.pallas import tpu as pltpu

F32 = jnp.float32
BF16 = jnp.bfloat16
I32 = jnp.int32

D = 2048
CW = 1024
H = 8
DK = 64
DV = 128
HK = H * DK
HV = H * DV
RANK = 16
TAU = 16.0
N_GROUPS = 4
EPG = 8
NE = N_GROUPS * EPG
DE = 512
EPS = 1e-6

B_P, L_P = 4, 2048
B_S, L_S = 128, 8
T_P = B_P * L_P
T_S = B_S * L_S
T = T_P + T_S

TM = 256
CH = 64
N_TILES = T // TM
N_TILES_P = T_P // TM
TILES_PER_SEQ = L_P // TM
SEQ_PER_TILE = TM // L_S
SEQ_PER_CHUNK = CH // L_S

PROJ_MAIN = 3 * CW + 2 * HK + 2 * HV
PROJ_PAD = PROJ_MAIN + 128
C_ROWS = 136
P_ROW_BLOCK = B_S // 8
R_ROWS = 40

E_TILE = 256
E_TILES_MAX = (2 * T) // E_TILE + NE
E_CAP = T
E_CAP_TILES = E_CAP // E_TILE
E_ROWS = NE * E_CAP
VMEM_LIMIT = 60 * 1024 * 1024


def _cparams(n_axes=1, vmem=VMEM_LIMIT):
    return pltpu.CompilerParams(dimension_semantics=("arbitrary",) * n_axes, vmem_limit_bytes=vmem)


def _rms(x, g):
    return x * lax.rsqrt(jnp.mean(x * x, axis=-1, keepdims=True) + EPS) * g


def _sigmoid(x):
    return 1.0 / (1.0 + jnp.exp(-x))


def _silu(x):
    return x * _sigmoid(x)


def _log_sigmoid(x):
    return jnp.minimum(x, 0.0) - jnp.log1p(jnp.exp(-jnp.abs(x)))


def _iota2(shape, axis):
    return lax.broadcasted_iota(I32, shape, axis)


def _expand_rows(ref, n, reps):
    return jnp.concatenate([jnp.broadcast_to(ref[j:j + 1, :], (reps, ref.shape[-1])) for j in range(n)], axis=0)


def _prompt_row(ref):
    return ref[pl.ds(pl.program_id(0) // TILES_PER_SEQ, 1), :]


def _sample_rows(ref):
    return _expand_rows(ref, SEQ_PER_TILE, L_S)


def _p_tile(i):
    return jnp.minimum(i, N_TILES_P - 1)


def _s_tile(i):
    return jnp.maximum(i - N_TILES_P, 0)


def _mod_specs(col):
    return [pl.BlockSpec((8, D), lambda i, *_: (P_ROW_BLOCK, col)),
            pl.BlockSpec((SEQ_PER_TILE, D), lambda i, *_: (_s_tile(i), col))]


def _const_spec(shape):
    zeros = (0,) * len(shape)
    return pl.BlockSpec(shape, lambda *_: zeros, pipeline_mode=pl.Buffered(1))


def _ada_kernel(c_ref, w_ref, b_ref, o_ref):
    s = _silu(c_ref[...]).astype(BF16)
    o_ref[...] = jnp.dot(s, w_ref[...].astype(BF16), preferred_element_type=F32) + b_ref[...]


def _ada(c_all, w_ada, b_ada):
    tn = 512
    return pl.pallas_call(
        _ada_kernel,
        out_shape=jax.ShapeDtypeStruct((C_ROWS, 6 * D), F32),
        grid=(6 * D // tn,),
        in_specs=[pl.BlockSpec((C_ROWS, D), lambda j: (0, 0)),
                  pl.BlockSpec((D, tn), lambda j: (0, j)),
                  pl.BlockSpec((1, tn), lambda j: (0, j))],
        out_specs=pl.BlockSpec((C_ROWS, tn), lambda j: (0, j)),
        compiler_params=_cparams(),
        name="ada",
    )(c_all, w_ada, b_ada)


W_CHUNK = 256
N_W_CHUNKS = PROJ_MAIN // W_CHUNK


def _inproj_kernel(xp_ref, xs_ref, shp_ref, shs_ref, scp_ref, scs_ref, g_ref, w_hbm, wa_ref, o_ref,
                   h_scr, w_scr, stage, wsem):
    i = pl.program_id(0)

    @pl.when(i == 0)
    def _():
        def chunk(c):
            return pltpu.make_async_copy(w_hbm.at[:, pl.ds(c * W_CHUNK, W_CHUNK)], stage.at[c % 2], wsem.at[c % 2])

        chunk(0).start()
        for c in range(N_W_CHUNKS):
            if c + 1 < N_W_CHUNKS:
                chunk(c + 1).start()
            chunk(c).wait()
            w_scr[:, c * W_CHUNK:(c + 1) * W_CHUNK] = stage[c % 2].astype(BF16)

    @pl.when(i < N_TILES_P)
    def _():
        h = _rms(xp_ref[...], g_ref[...]) * (1.0 + _prompt_row(scp_ref)) + _prompt_row(shp_ref)
        h_scr[...] = h.astype(BF16)

    @pl.when(i >= N_TILES_P)
    def _():
        h = _rms(xs_ref[...], g_ref[...]) * (1.0 + _sample_rows(scs_ref)) + _sample_rows(shs_ref)
        h_scr[...] = h.astype(BF16)

    h = h_scr[...]
    o_ref[:, 0:PROJ_MAIN] = jnp.dot(h, w_scr[...], preferred_element_type=F32)
    o_ref[:, PROJ_MAIN:PROJ_PAD] = jnp.dot(h, wa_ref[...], preferred_element_type=F32)


def _inproj(xp, xs, mod, g_mix, w_in, w_low_b):
    return pl.pallas_call(
        _inproj_kernel,
        out_shape=jax.ShapeDtypeStruct((T, PROJ_PAD), F32),
        grid=(N_TILES,),
        in_specs=[pl.BlockSpec((TM, D), lambda i: (_p_tile(i), 0)),
                  pl.BlockSpec((TM, D), lambda i: (_s_tile(i), 0)),
                  *_mod_specs(0), *_mod_specs(1),
                  _const_spec((1, D)), pl.BlockSpec(memory_space=pl.ANY), _const_spec((D, 128))],
        out_specs=pl.BlockSpec((TM, PROJ_PAD), lambda i: (i, 0)),
        scratch_shapes=[pltpu.VMEM((TM, D), BF16), pltpu.VMEM((D, PROJ_MAIN), BF16),
                        pltpu.VMEM((2, D, W_CHUNK), F32), pltpu.SemaphoreType.DMA((2,))],
        compiler_params=_cparams(),
        name="inproj",
    )(xp, xs, mod, mod, mod, mod, g_mix, w_in, w_low_b)


def _gla_prep(p_ref, wgu_ref, bg_ref, rows, seg):
    q = p_ref[:, 3 * CW:3 * CW + HK]
    k = p_ref[:, 3 * CW + HK:3 * CW + 2 * HK]
    a = p_ref[:, PROJ_MAIN:PROJ_PAD].astype(BF16)
    z = jnp.dot(a, wgu_ref[...], preferred_element_type=F32) + bg_ref[...]
    la = _log_sigmoid(z) * (1.0 / TAU)
    r = _iota2((rows, rows), 0)
    c = _iota2((rows, rows), 1)
    same = (r // seg) == (c // seg)
    tri = jnp.where(same & (c <= r), 1.0, 0.0).astype(F32)
    tot = jnp.where(same, 1.0, 0.0).astype(F32)
    b = jnp.dot(tri, la, preferred_element_type=F32, precision=lax.Precision.HIGHEST)
    bl = jnp.dot(tot, la, preferred_element_type=F32, precision=lax.Precision.HIGHEST)
    qt = q * jnp.exp(b) * (DK ** -0.5)
    kt = k * jnp.exp(-b)
    kend = k * jnp.exp(bl - b)
    return qt, kt, kend, bl


def _head_stack(x):
    rows = x.shape[0]
    t = jnp.concatenate([x] * H, axis=0)
    keep = (_iota2((H * rows, HK), 0) // rows) == (_iota2((H * rows, HK), 1) // DK)
    return jnp.where(keep, t, 0.0)


def _gla_out(o_h, g_h, ngl):
    on = o_h * lax.rsqrt(jnp.mean(o_h * o_h, axis=-1, keepdims=True) + EPS) * ngl
    return on * _silu(g_h)


def _conv_out(bg, u, um1, um2, cw_ref, cb_ref, ncv_ref):
    conv_y = cb_ref[...] + cw_ref[0:1, :] * um2 + cw_ref[1:2, :] * um1 + cw_ref[2:3, :] * u
    return _rms(bg * conv_y, ncv_ref[...])


V_OFF = 3 * CW + 2 * HK
G_OFF = V_OFF + HV


def _mixer_p_kernel(p_ref, cw_ref, cb_ref, wgu_ref, bg_ref, ncv_ref, ngl_ref,
                    y_ref, nconv_ref, nst_ref, ubuf, st_ref):
    j = pl.program_id(0) % TILES_PER_SEQ

    @pl.when(j == 0)
    def _():
        ubuf[0:8, :] = jnp.zeros((8, CW), F32)
        st_ref[...] = jnp.zeros((DV, HK), F32)

    bgate = p_ref[:, 0:CW]
    u = p_ref[:, CW:2 * CW] * p_ref[:, 2 * CW:3 * CW]
    ubuf[8:8 + TM, :] = u
    um1 = ubuf[7:7 + TM, :]
    um2 = ubuf[6:6 + TM, :]
    y_ref[:, 0:CW] = _conv_out(bgate, u, um1, um2, cw_ref, cb_ref, ncv_ref).astype(BF16)
    ubuf[6:8, :] = u[TM - 2:TM, :]

    @pl.when(j == TILES_PER_SEQ - 1)
    def _():
        nconv_ref[0] = u[TM - 2:TM, :]

    qt, kt, kend, bl = _gla_prep(p_ref, wgu_ref, bg_ref, TM, CH)
    causal = _iota2((H * CH, CH), 0) % CH >= _iota2((H * CH, CH), 1)
    ngl = ngl_ref[...]
    for c in range(TM // CH):
        r0 = c * CH
        lhs = _head_stack(qt[r0:r0 + CH]).astype(BF16)
        sc = lax.dot_general(lhs, kt[r0:r0 + CH].astype(BF16), (((1,), (1,)), ((), ())),
                             preferred_element_type=F32)
        sc = jnp.where(causal, sc, 0.0).astype(BF16)
        st = st_ref[...]
        o_inter = lax.dot_general(lhs, st.astype(BF16), (((1,), (1,)), ((), ())),
                                  preferred_element_type=F32)
        vs = []
        for h in range(H):
            v_h = p_ref[r0:r0 + CH, V_OFF + h * DV:V_OFF + (h + 1) * DV].astype(BF16)
            g_h = p_ref[r0:r0 + CH, G_OFF + h * DV:G_OFF + (h + 1) * DV]
            vs.append(v_h)
            o_h = jnp.dot(sc[h * CH:(h + 1) * CH], v_h, preferred_element_type=F32) + o_inter[h * CH:(h + 1) * CH]
            y_ref[r0:r0 + CH, CW + h * DV:CW + (h + 1) * DV] = _gla_out(o_h, g_h, ngl).astype(BF16)
        vstack = jnp.concatenate(vs, axis=0)
        kstack = _head_stack(kend[r0:r0 + CH]).astype(BF16)
        kv_t = lax.dot_general(vstack, kstack, (((0,), (0,)), ((), ())), preferred_element_type=F32)
        st_ref[...] = st * jnp.exp(bl[r0:r0 + 1, :]) + kv_t

    @pl.when(j == TILES_PER_SEQ - 1)
    def _():
        nst_ref[0] = st_ref[...]


def _mixer_p(proj, conv_w, conv_b, wgu_b, b_gate, n_conv, n_gla):
    return pl.pallas_call(
        _mixer_p_kernel,
        out_shape=(jax.ShapeDtypeStruct((T_P, D), BF16),
                   jax.ShapeDtypeStruct((B_P, 2, CW), F32),
                   jax.ShapeDtypeStruct((B_P, DV, HK), F32)),
        grid=(N_TILES_P,),
        in_specs=[pl.BlockSpec((TM, PROJ_PAD), lambda i: (i, 0)),
                  _const_spec((3, CW)), _const_spec((1, CW)), _const_spec((128, HK)), _const_spec((1, HK)),
                  _const_spec((1, CW)), _const_spec((1, DV))],
        out_specs=(pl.BlockSpec((TM, D), lambda i: (i, 0)),
                   pl.BlockSpec((1, 2, CW), lambda i: (i // TILES_PER_SEQ, 0, 0)),
                   pl.BlockSpec((1, DV, HK), lambda i: (i // TILES_PER_SEQ, 0, 0))),
        scratch_shapes=[pltpu.VMEM((8 + TM, CW), F32), pltpu.VMEM((DV, HK), F32)],
        compiler_params=_cparams(),
        name="mixer_p",
    )(proj, conv_w, conv_b, wgu_b, b_gate, n_conv, n_gla)


def _mixer_s_kernel(p_ref, sconv_ref, sst_ref, cw_ref, cb_ref, wgu_ref, bg_ref, ncv_ref, ngl_ref,
                    y_ref, nconv_ref, nst_ref, ubuf):
    nseq = SEQ_PER_CHUNK
    bgate = p_ref[:, 0:CW]
    u = p_ref[:, CW:2 * CW] * p_ref[:, 2 * CW:3 * CW]
    ubuf[0:8, :] = jnp.zeros((8, CW), F32)
    ubuf[8:8 + CH, :] = u
    tpos = _iota2((CH, CW), 0) % L_S
    s0 = jnp.concatenate([jnp.broadcast_to(sconv_ref[s, 0:1, :], (L_S, CW)) for s in range(nseq)], axis=0)
    s1 = jnp.concatenate([jnp.broadcast_to(sconv_ref[s, 1:2, :], (L_S, CW)) for s in range(nseq)], axis=0)
    um1 = jnp.where(tpos == 0, s1, ubuf[7:7 + CH, :])
    um2 = jnp.where(tpos == 0, s0, jnp.where(tpos == 1, s1, ubuf[6:6 + CH, :]))
    y_ref[:, 0:CW] = _conv_out(bgate, u, um1, um2, cw_ref, cb_ref, ncv_ref).astype(BF16)
    for s in range(nseq):
        nconv_ref[s] = u[s * L_S + L_S - 2:(s + 1) * L_S, :]

    qt, kt, kend, bl = _gla_prep(p_ref, wgu_ref, bg_ref, CH, L_S)
    rr = _iota2((H * CH, CH), 0) % CH
    cc = _iota2((H * CH, CH), 1)
    causal = (rr >= cc) & ((rr // L_S) == (cc // L_S))
    ngl = ngl_ref[...]
    lhs_f = _head_stack(qt)
    sc = lax.dot_general(lhs_f.astype(BF16), kt.astype(BF16), (((1,), (1,)), ((), ())),
                         preferred_element_type=F32)
    sc = jnp.where(causal, sc, 0.0).astype(BF16)
    kstack_f = _head_stack(kend)
    v_all = p_ref[:, V_OFF:V_OFF + HV]
    decay_t = jnp.exp(jnp.concatenate([bl, jnp.zeros((128 - CH, HK), F32)], axis=0).T)
    o_inter = []
    for s in range(nseq):
        rs = [slice(h * CH + s * L_S, h * CH + (s + 1) * L_S) for h in range(H)]
        lhs_s = jnp.concatenate([lhs_f[r] for r in rs], axis=0).astype(BF16)
        k_s = jnp.concatenate([kstack_f[r] for r in rs], axis=0).astype(BF16)
        v_s = jnp.concatenate([v_all[s * L_S:(s + 1) * L_S, h * DV:(h + 1) * DV] for h in range(H)],
                              axis=0).astype(BF16)
        st = sst_ref[s]
        o_inter.append(jnp.dot(lhs_s, st.astype(BF16), preferred_element_type=F32))
        kv = lax.dot_general(k_s, v_s, (((0,), (0,)), ((), ())), preferred_element_type=F32)
        nst_ref[s] = st * decay_t[:, s * L_S:s * L_S + 1] + kv
    for h in range(H):
        v_h = v_all[:, h * DV:(h + 1) * DV].astype(BF16)
        g_h = p_ref[:, G_OFF + h * DV:G_OFF + (h + 1) * DV]
        oi_h = jnp.concatenate([o_inter[s][h * L_S:(h + 1) * L_S] for s in range(nseq)], axis=0)
        o_h = jnp.dot(sc[h * CH:(h + 1) * CH], v_h, preferred_element_type=F32) + oi_h
        y_ref[:, CW + h * DV:CW + (h + 1) * DV] = _gla_out(o_h, g_h, ngl).astype(BF16)


def _mixer_s(proj, sconv, sst, conv_w, conv_b, wgu_b, b_gate, n_conv, n_gla):
    nseq = SEQ_PER_CHUNK
    off = T_P // CH
    return pl.pallas_call(
        _mixer_s_kernel,
        out_shape=(jax.ShapeDtypeStruct((T_S, D), BF16),
                   jax.ShapeDtypeStruct((B_S, 2, CW), F32),
                   jax.ShapeDtypeStruct((B_S, HK, DV), F32)),
        grid=(T_S // CH,),
        in_specs=[pl.BlockSpec((CH, PROJ_PAD), lambda i: (i + off, 0)),
                  pl.BlockSpec((nseq, 2, CW), lambda i: (i, 0, 0)),
                  pl.BlockSpec((nseq, HK, DV), lambda i: (i, 0, 0)),
                  _const_spec((3, CW)), _const_spec((1, CW)), _const_spec((128, HK)), _const_spec((1, HK)),
                  _const_spec((1, CW)), _const_spec((1, DV))],
        out_specs=(pl.BlockSpec((CH, D), lambda i: (i, 0)),
                   pl.BlockSpec((nseq, 2, CW), lambda i: (i, 0, 0)),
                   pl.BlockSpec((nseq, HK, DV), lambda i: (i, 0, 0))),
        scratch_shapes=[pltpu.VMEM((8 + CH, CW), F32)],
        compiler_params=_cparams(),
        name="mixer_s",
    )(proj, sconv, sst, conv_w, conv_b, wgu_b, b_gate, n_conv, n_gla)


def _route(h2, wr_ref, br_ref, run_ref, pos_ref, rw_ref, cnt_ref):
    lt = lax.dot_general(wr_ref[...], h2.astype(BF16), (((1,), (1,)), ((), ())),
                         preferred_element_type=F32) + br_ref[...]

    coarse = lt[0:N_GROUPS]
    cmax = jnp.max(coarse, axis=0, keepdims=True)
    gi = _iota2((N_GROUPS, TM), 0)
    grp = jnp.min(jnp.where(coarse == cmax, gi, N_GROUPS), axis=0, keepdims=True)
    p_sel = 1.0 / jnp.sum(jnp.exp(coarse - cmax), axis=0, keepdims=True)
    fine = jnp.zeros((EPG, TM), F32)
    for g in range(N_GROUPS):
        fine = jnp.where(grp == g, lt[8 + g * EPG:8 + (g + 1) * EPG], fine)
    ei = _iota2((EPG, TM), 0)
    f1 = jnp.max(fine, axis=0, keepdims=True)
    i1 = jnp.min(jnp.where(fine == f1, ei, EPG), axis=0, keepdims=True)
    rest = jnp.where(ei == i1, -jnp.inf, fine)
    f2 = jnp.max(rest, axis=0, keepdims=True)
    i2 = jnp.min(jnp.where(rest == f2, ei, EPG), axis=0, keepdims=True)
    e2 = jnp.exp(f2 - f1)
    w1 = p_sel / (1.0 + e2)
    w2 = p_sel * e2 / (1.0 + e2)
    x1 = grp * EPG + i1
    x2 = grp * EPG + i2

    er = _iota2((NE, TM), 0)
    oh1 = jnp.where(er == x1, 1.0, 0.0)
    oh2 = jnp.where(er == x2, 1.0, 0.0)
    oh = oh1 + oh2
    before = jnp.where(_iota2((TM, TM), 0) < _iota2((TM, TM), 1), 1.0, 0.0).astype(BF16)
    cum = jnp.dot(oh.astype(BF16), before, preferred_element_type=F32) + run_ref[:, 0:1]
    r1 = jnp.sum(oh1 * cum, axis=0, keepdims=True).astype(I32)
    r2 = jnp.sum(oh2 * cum, axis=0, keepdims=True).astype(I32)
    run_new = run_ref[...] + jnp.sum(oh, axis=1, keepdims=True)
    run_ref[...] = run_new
    cnt_ref[...] = run_new.astype(I32)

    row = _iota2((8, TM), 0)
    pos_ref[...] = jnp.where(row == 0, x1 * E_CAP + r1, jnp.where(row == 1, x2 * E_CAP + r2, 0))
    wrow = _iota2((128, TM), 0)
    rw_ref[...] = jnp.where(wrow == 0, w1, jnp.where(wrow == 1, w2, 0.0)).T


def _outproj_kernel(ycp_ref, ycs_ref, xp_ref, xs_ref, gtp_ref, gts_ref, shp_ref, shs_ref, scp_ref, scs_ref,
                    wo_ref, g_ref, wr_ref, br_ref,
                    x1_ref, pos_ref, rw_ref, cnt_ref, xs_out,
                    run_ref, h2_scr, pos_v, pos_s, sem, psem):
    i = pl.program_id(0)
    par = i % 2

    @pl.when(i == 0)
    def _():
        run_ref[...] = jnp.zeros((NE, 128), F32)

    def tile(yc, x, gt, sh, sc):
        mix = jnp.dot(yc, wo_ref[...], preferred_element_type=F32)
        x1 = x + gt * mix
        x1_ref[...] = x1
        h2 = _rms(x1, g_ref[...]) * (1.0 + sc) + sh
        h2_scr[par] = h2
        _route(h2, wr_ref, br_ref, run_ref, pos_v, rw_ref, cnt_ref)

    @pl.when(i < N_TILES_P)
    def _():
        tile(ycp_ref[...], xp_ref[...], _prompt_row(gtp_ref), _prompt_row(shp_ref), _prompt_row(scp_ref))

    @pl.when(i >= N_TILES_P)
    def _():
        tile(ycs_ref[...], xs_ref[...], _sample_rows(gts_ref), _sample_rows(shs_ref), _sample_rows(scs_ref))

    pos_ref[0] = pos_v[...]
    to_smem = pltpu.make_async_copy(pos_v, pos_s, psem)
    to_smem.start()
    to_smem.wait()
    for r in range(TM):
        for slot in range(2):
            pltpu.make_async_copy(h2_scr.at[par, pl.ds(r, 1), :], xs_out.at[pl.ds(pos_s[slot, r], 1), :],
                                  sem.at[par]).start(priority=r % 2)

    def drain(p):
        for _ in range(2):
            pltpu.make_async_copy(h2_scr.at[p], xs_out.at[pl.ds(0, TM), :], sem.at[p]).wait()

    @pl.when(i > 0)
    def _():
        drain(1 - par)

    @pl.when(i == N_TILES - 1)
    def _():
        drain(par)


def _outproj(ycp, ycs, xp, xs, mod, w_out_b, g_ffn, w_r, b_r):
    return pl.pallas_call(
        _outproj_kernel,
        out_shape=(jax.ShapeDtypeStruct((T, D), F32), jax.ShapeDtypeStruct((N_TILES, 8, TM), I32),
                   jax.ShapeDtypeStruct((T, 128), F32), jax.ShapeDtypeStruct((NE, 128), I32),
                   jax.ShapeDtypeStruct((E_ROWS, D), F32)),
        grid=(N_TILES,),
        in_specs=[pl.BlockSpec((TM, D), lambda i: (_p_tile(i), 0)), pl.BlockSpec((TM, D), lambda i: (_s_tile(i), 0)),
                  pl.BlockSpec((TM, D), lambda i: (_p_tile(i), 0)), pl.BlockSpec((TM, D), lambda i: (_s_tile(i), 0)),
                  *_mod_specs(2), *_mod_specs(3), *_mod_specs(4),
                  _const_spec((D, D)), _const_spec((1, D)), _const_spec((R_ROWS, D)), _const_spec((R_ROWS, 1))],
        out_specs=(pl.BlockSpec((TM, D), lambda i: (i, 0)), pl.BlockSpec((1, 8, TM), lambda i: (i, 0, 0)),
                   pl.BlockSpec((TM, 128), lambda i: (i, 0)), pl.BlockSpec((NE, 128), lambda i: (0, 0)),
                   pl.BlockSpec(memory_space=pl.ANY)),
        scratch_shapes=[pltpu.VMEM((NE, 128), F32), pltpu.VMEM((2, TM, D), F32), pltpu.VMEM((8, TM), I32),
                        pltpu.SMEM((8, TM), I32), pltpu.SemaphoreType.DMA((2,)), pltpu.SemaphoreType.DMA(())],
        compiler_params=_cparams(),
        name="outproj",
    )(ycp, ycs, xp, xs, mod, mod, mod, mod, mod, mod, w_out_b, g_ffn, w_r, b_r)


def _moe_kernel(tb_ref, te_ref, first_ref, nval_ref, nt_ref, x_ref, wg_ref, wu_ref, wd_ref, y_ref, wgu_b, wd_b):
    i = pl.program_id(0)

    @pl.when(i < nt_ref[0])
    def _():
        @pl.when(first_ref[i] == 1)
        def _():
            wgu_b[:, 0:DE] = wg_ref[0].astype(BF16)
            wgu_b[:, DE:2 * DE] = wu_ref[0].astype(BF16)
            wd_b[...] = wd_ref[0].astype(BF16)

        valid = _iota2((E_TILE, D), 0) < nval_ref[i]
        x = jnp.where(valid, x_ref[...], 0.0).astype(BF16)
        ab = jnp.dot(x, wgu_b[...], preferred_element_type=F32)
        hid = (_silu(ab[:, 0:DE]) * ab[:, DE:2 * DE]).astype(BF16)
        y_ref[...] = jnp.dot(hid, wd_b[...], preferred_element_type=F32)


def _moe(tile_block, tile_expert, tile_first, tile_nvalid, n_tiles, xs, w_eg, w_eu, w_ed):
    def tile_map(i, tb, te, fi, nv, nt):
        return (tb[i], 0)

    def w_map(i, tb, te, fi, nv, nt):
        return (te[i], 0, 0)

    grid_spec = pltpu.PrefetchScalarGridSpec(
        num_scalar_prefetch=5,
        grid=(E_TILES_MAX,),
        in_specs=[pl.BlockSpec((E_TILE, D), tile_map),
                  pl.BlockSpec((1, D, DE), w_map), pl.BlockSpec((1, D, DE), w_map), pl.BlockSpec((1, DE, D), w_map)],
        out_specs=pl.BlockSpec((E_TILE, D), tile_map),
        scratch_shapes=[pltpu.VMEM((D, 2 * DE), BF16), pltpu.VMEM((DE, D), BF16)],
    )
    return pl.pallas_call(
        _moe_kernel,
        out_shape=jax.ShapeDtypeStruct((E_ROWS, D), F32),
        grid_spec=grid_spec,
        compiler_params=_cparams(),
        name="moe",
    )(tile_block, tile_expert, tile_first, tile_nvalid, n_tiles, xs, w_eg, w_eu, w_ed)


def _plan(counts):
    ntile = (counts + E_TILE - 1) // E_TILE
    tend = jnp.cumsum(ntile)
    tbeg = tend - ntile
    n_tiles = tend[-1]
    tid = jnp.minimum(jnp.arange(E_TILES_MAX, dtype=I32), n_tiles - 1)
    tile_expert = jnp.minimum(jnp.sum((tend[None, :] <= tid[:, None]).astype(I32), axis=1), NE - 1)
    tile_in_expert = tid - tbeg[tile_expert]
    tile_block = tile_expert * E_CAP_TILES + tile_in_expert
    tile_first = (tile_in_expert == 0).astype(I32)
    tile_nvalid = jnp.minimum(counts[tile_expert] - tile_in_expert * E_TILE, E_TILE)
    return tile_block, tile_expert, tile_first, tile_nvalid, n_tiles.reshape(1)


def _final_kernel(pos0_ref, posn_ref, x1_ref, rw_ref, gtp_ref, gts_ref, g_ref, ys_ref, yp_ref, ysm_ref, buf, sem):
    i = pl.program_id(0)
    par = i % 2

    def gather(p_ref, p):
        for r in range(TM):
            for slot in range(2):
                pltpu.make_async_copy(ys_ref.at[pl.ds(p_ref[0, slot, r], 1), :], buf.at[p, slot, pl.ds(r, 1), :],
                                      sem.at[p]).start(priority=r % 2)

    @pl.when(i == 0)
    def _():
        gather(pos0_ref, 0)

    @pl.when(i < N_TILES - 1)
    def _():
        gather(posn_ref, 1 - par)

    for slot in range(2):
        pltpu.make_async_copy(ys_ref.at[pl.ds(0, TM), :], buf.at[par, slot], sem.at[par]).wait()

    moe = rw_ref[:, 0:1] * buf[par, 0] + rw_ref[:, 1:2] * buf[par, 1]

    @pl.when(i < N_TILES_P)
    def _():
        yp_ref[...] = _rms(x1_ref[...] + _prompt_row(gtp_ref) * moe, g_ref[...])

    @pl.when(i >= N_TILES_P)
    def _():
        ysm_ref[...] = _rms(x1_ref[...] + _sample_rows(gts_ref) * moe, g_ref[...])


def _final(pos, x1, rw, mod, g_fin, ys):
    return pl.pallas_call(
        _final_kernel,
        out_shape=(jax.ShapeDtypeStruct((T_P, D), F32), jax.ShapeDtypeStruct((T_S, D), F32)),
        grid=(N_TILES,),
        in_specs=[pl.BlockSpec((1, 8, TM), lambda i: (0, 0, 0), memory_space=pltpu.SMEM),
                  pl.BlockSpec((1, 8, TM), lambda i: (jnp.minimum(i + 1, N_TILES - 1), 0, 0),
                               memory_space=pltpu.SMEM),
                  pl.BlockSpec((TM, D), lambda i: (i, 0)),
                  pl.BlockSpec((TM, 128), lambda i: (i, 0)),
                  *_mod_specs(5),
                  _const_spec((1, D)),
                  pl.BlockSpec(memory_space=pl.ANY)],
        out_specs=(pl.BlockSpec((TM, D), lambda i: (_p_tile(i), 0)),
                   pl.BlockSpec((TM, D), lambda i: (_s_tile(i), 0))),
        scratch_shapes=[pltpu.VMEM((2, 2, TM, D), F32), pltpu.SemaphoreType.DMA((2,))],
        compiler_params=_cparams(),
        name="final",
    )(pos, pos, x1, rw, mod, mod, g_fin, ys)


def kernel(x_prompt, x_sample, c_prompt, c_sample, state_conv, state_gla, w_ada, b_ada, norm_mix, w_in, conv_w,
           conv_b, w_gate_up, b_gate, norm_conv, norm_gla, w_out, norm_ffn, w_coarse, b_coarse, w_fine, b_fine,
           w_exp_gate, w_exp_up, w_exp_down, norm_final):
    xp = x_prompt.reshape(T_P, D)
    xs = x_sample.reshape(T_S, D)
    c_all = jnp.concatenate([c_sample, c_prompt, jnp.zeros((C_ROWS - B_S - B_P, D), F32)], axis=0)
    mod = _ada(c_all, w_ada[0], b_ada[0][None, :])

    w_low_b = jnp.pad(w_in[0, :, PROJ_MAIN:], ((0, 0), (0, 128 - RANK))).astype(BF16)
    w_out_b = w_out[0].astype(BF16)
    wgu_b = jnp.pad(w_gate_up[0], ((0, 128 - RANK), (0, 0))).astype(BF16)
    w_r = jnp.concatenate([w_coarse[0].T, jnp.zeros((8 - N_GROUPS, D), F32), w_fine[0].T], axis=0)
    b_r = jnp.concatenate([b_coarse[0], jnp.zeros((8 - N_GROUPS,), F32), b_fine[0]])[:, None]
    g_mix, g_ffn, g_fin = norm_mix[0][None, :], norm_ffn[0][None, :], norm_final[None, :]
    mix_consts = (conv_w[0], conv_b[0][None, :], wgu_b, b_gate[0][None, :], norm_conv[0][None, :],
                  norm_gla[0][None, :])

    proj = _inproj(xp, xs, mod, g_mix, w_in[0], w_low_b)
    ycat_p, nconv_p, nst_p = _mixer_p(proj, *mix_consts)
    ycat_s, nconv_s, nst_s = _mixer_s(proj, state_conv[0], state_gla[0].reshape(B_S, HK, DV), *mix_consts)

    x1, pos, rw, counts, x_sorted = _outproj(ycat_p, ycat_s, xp, xs, mod, w_out_b, g_ffn, w_r.astype(BF16), b_r)
    y_sorted = _moe(*_plan(counts[:, 0]), x_sorted, w_exp_gate[0], w_exp_up[0], w_exp_down[0])
    y_p, y_s = _final(pos, x1, rw, mod, g_fin, y_sorted)

    new_gla_p = nst_p.reshape(B_P, DV, H, DK).transpose(0, 2, 3, 1)[None]
    new_gla_s = nst_s.reshape(1, B_S, H, DK, DV)
    return (y_p.reshape(B_P, L_P, D), y_s.reshape(B_S, L_S, D), nconv_p[None], new_gla_p,
            nconv_s[None], new_gla_s)
```

```python
import functools

import jax
import jax.numpy as jnp
from jax import lax
from jax.experimental import pallas as pl
from jax.experimental.pallas import tpu as pltpu

F32 = jnp.float32
BF16 = jnp.bfloat16
I32 = jnp.int32

D = 2048
CW = 1024
H = 8
DK = 64
DV = 128
HK = H * DK
HV = H * DV
RANK = 16
TAU = 16.0
N_GROUPS = 4
EPG = 8
NE = N_GROUPS * EPG
DE = 512
EPS = 1e-6

B_P, L_P = 4, 2048
B_S, L_S = 128, 8
T_P = B_P * L_P
T_S = B_S * L_S
T = T_P + T_S

TM = 256
CH = 64
N_TILES = T // TM
N_TILES_P = T_P // TM
TILES_PER_SEQ = L_P // TM
SEQ_PER_TILE = TM // L_S
SEQ_PER_CHUNK = CH // L_S

PROJ_MAIN = 3 * CW + 2 * HK + 2 * HV
PROJ_PAD = PROJ_MAIN + 128
C_ROWS = 136
P_ROW_BLOCK = B_S // 8
R_ROWS = 40

E_TILE = 256
E_TILES_MAX = (2 * T) // E_TILE + NE
E_CAP = T
E_CAP_TILES = E_CAP // E_TILE
E_ROWS = NE * E_CAP
VMEM_LIMIT = 60 * 1024 * 1024


def _cparams(n_axes=1, vmem=VMEM_LIMIT):
    return pltpu.CompilerParams(dimension_semantics=("arbitrary",) * n_axes, vmem_limit_bytes=vmem)


def _rms(x, g):
    return x * lax.rsqrt(jnp.mean(x * x, axis=-1, keepdims=True) + EPS) * g


def _sigmoid(x):
    return 1.0 / (1.0 + jnp.exp(-x))


def _silu(x):
    return x * _sigmoid(x)


def _log_sigmoid(x):
    return jnp.minimum(x, 0.0) - jnp.log(1.0 + jnp.exp(-jnp.abs(x)))


def _iota2(shape, axis):
    return lax.broadcasted_iota(I32, shape, axis)


def _expand_rows(ref, n, reps):
    return jnp.concatenate([jnp.broadcast_to(ref[j:j + 1, :], (reps, ref.shape[-1])) for j in range(n)], axis=0)


def _prompt_row(ref):
    return ref[pl.ds(pl.program_id(0) // TILES_PER_SEQ, 1), :]


def _sample_rows(ref):
    return _expand_rows(ref, SEQ_PER_TILE, L_S)


def _p_tile(i):
    return jnp.minimum(i, N_TILES_P - 1)


def _s_tile(i):
    return jnp.maximum(i - N_TILES_P, 0)


def _mod_specs(col):
    return [pl.BlockSpec((8, D), lambda i, *_: (P_ROW_BLOCK, col)),
            pl.BlockSpec((SEQ_PER_TILE, D), lambda i, *_: (_s_tile(i), col))]


def _const_spec(shape):
    zeros = (0,) * len(shape)
    return pl.BlockSpec(shape, lambda *_: zeros, pipeline_mode=pl.Buffered(1))


def _ada_kernel(c_ref, w_ref, b_ref, o_ref):
    s = _silu(c_ref[...]).astype(BF16)
    o_ref[...] = jnp.dot(s, w_ref[...].astype(BF16), preferred_element_type=F32) + b_ref[...]


def _ada(c_all, w_ada, b_ada):
    tn = 512
    return pl.pallas_call(
        _ada_kernel,
        out_shape=jax.ShapeDtypeStruct((C_ROWS, 6 * D), F32),
        grid=(6 * D // tn,),
        in_specs=[pl.BlockSpec((C_ROWS, D), lambda j: (0, 0)),
                  pl.BlockSpec((D, tn), lambda j: (0, j)),
                  pl.BlockSpec((1, tn), lambda j: (0, j))],
        out_specs=pl.BlockSpec((C_ROWS, tn), lambda j: (0, j)),
        compiler_params=_cparams(),
        name="ada",
    )(c_all, w_ada, b_ada)


W_CHUNK = 256
N_W_CHUNKS = PROJ_MAIN // W_CHUNK


def _inproj_kernel(xp_ref, xs_ref, shp_ref, shs_ref, scp_ref, scs_ref, g_ref, w_hbm, wa_ref, o_ref,
                   h_scr, w_scr, stage, wsem):
    i = pl.program_id(0)

    @pl.when(i == 0)
    def _():
        def chunk(c):
            return pltpu.make_async_copy(w_hbm.at[0, :, pl.ds(c * W_CHUNK, W_CHUNK)], stage.at[c % 2],
                                         wsem.at[c % 2])

        chunk(0).start()
        for c in range(N_W_CHUNKS):
            if c + 1 < N_W_CHUNKS:
                chunk(c + 1).start()
            chunk(c).wait()
            w_scr[:, c * W_CHUNK:(c + 1) * W_CHUNK] = stage[c % 2].astype(BF16)

    @pl.when(i < N_TILES_P)
    def _():
        h = _rms(xp_ref[...], g_ref[...]) * (1.0 + _prompt_row(scp_ref)) + _prompt_row(shp_ref)
        h_scr[...] = h.astype(BF16)

    @pl.when(i >= N_TILES_P)
    def _():
        h = _rms(xs_ref[...], g_ref[...]) * (1.0 + _sample_rows(scs_ref)) + _sample_rows(shs_ref)
        h_scr[...] = h.astype(BF16)

    h = h_scr[...]
    o_ref[:, 0:PROJ_MAIN] = jnp.dot(h, w_scr[...], preferred_element_type=F32)
    o_ref[:, PROJ_MAIN:PROJ_PAD] = jnp.dot(h, wa_ref[...], preferred_element_type=F32)


def _inproj(xp, xs, mod, g_mix, w_in, w_low_b):
    return pl.pallas_call(
        _inproj_kernel,
        out_shape=jax.ShapeDtypeStruct((T, PROJ_PAD), F32),
        grid=(N_TILES,),
        in_specs=[pl.BlockSpec((TM, D), lambda i: (_p_tile(i), 0)),
                  pl.BlockSpec((TM, D), lambda i: (_s_tile(i), 0)),
                  *_mod_specs(0), *_mod_specs(1),
                  _const_spec((1, D)), pl.BlockSpec(memory_space=pl.ANY), _const_spec((D, 128))],
        out_specs=pl.BlockSpec((TM, PROJ_PAD), lambda i: (i, 0)),
        scratch_shapes=[pltpu.VMEM((TM, D), BF16), pltpu.VMEM((D, PROJ_MAIN), BF16),
                        pltpu.VMEM((2, D, W_CHUNK), F32), pltpu.SemaphoreType.DMA((2,))],
        compiler_params=_cparams(),
        name="inproj",
    )(xp, xs, mod, mod, mod, mod, g_mix, w_in, w_low_b)


def _gla_prep(p_ref, wgu_ref, bg_ref, rows, seg):
    q = p_ref[:, 3 * CW:3 * CW + HK]
    k = p_ref[:, 3 * CW + HK:3 * CW + 2 * HK]
    a = p_ref[:, PROJ_MAIN:PROJ_PAD].astype(BF16)
    z = jnp.dot(a, wgu_ref[...], preferred_element_type=F32) + bg_ref[...]
    la = _log_sigmoid(z) * (1.0 / TAU)
    la0 = la.astype(BF16)
    rem = la - la0.astype(F32)
    la1 = rem.astype(BF16)
    la2 = (rem - la1.astype(F32)).astype(BF16)
    r = _iota2((2 * rows, rows), 0)
    c = _iota2((2 * rows, rows), 1)
    same = ((r % rows) // seg) == (c // seg)
    sel = jnp.where(same & ((r >= rows) | (c <= r)), 1.0, 0.0).astype(BF16)
    sums = sum(jnp.dot(sel, part, preferred_element_type=F32) for part in (la0, la1, la2))
    b, bl = sums[0:rows], sums[rows:2 * rows]
    qt = q * jnp.exp(b) * (DK ** -0.5)
    kt = k * jnp.exp(-b)
    kend = kt * jnp.exp(bl)
    return qt, kt, kend, bl


def _head_stack(x):
    rows = x.shape[0]
    t = jnp.concatenate([x] * H, axis=0)
    keep = (_iota2((H * rows, HK), 0) // rows) == (_iota2((H * rows, HK), 1) // DK)
    return jnp.where(keep, t, 0.0)


def _gla_out(o_h, g_h, ngl):
    on = o_h * lax.rsqrt(jnp.mean(o_h * o_h, axis=-1, keepdims=True) + EPS) * ngl
    return on * _silu(g_h)


def _conv_out(bg, u, um1, um2, cw_ref, cb_ref, ncv_ref):
    conv_y = cb_ref[...] + cw_ref[0:1, :] * um2 + cw_ref[1:2, :] * um1 + cw_ref[2:3, :] * u
    return _rms(bg * conv_y, ncv_ref[...])


V_OFF = 3 * CW + 2 * HK
G_OFF = V_OFF + HV


def _mixer_p_kernel(p_ref, cw_ref, cb_ref, wgu_ref, bg_ref, ncv_ref, ngl_ref,
                    y_ref, nconv_ref, nst_ref, ubuf, st_ref):
    j = pl.program_id(0) % TILES_PER_SEQ

    @pl.when(j == 0)
    def _():
        ubuf[0:8, :] = jnp.zeros((8, CW), F32)
        st_ref[...] = jnp.zeros((DV, HK), F32)

    bgate = p_ref[:, 0:CW]
    u = p_ref[:, CW:2 * CW] * p_ref[:, 2 * CW:3 * CW]
    ubuf[8:8 + TM, :] = u
    um1 = ubuf[7:7 + TM, :]
    um2 = ubuf[6:6 + TM, :]
    y_ref[:, 0:CW] = _conv_out(bgate, u, um1, um2, cw_ref, cb_ref, ncv_ref).astype(BF16)
    ubuf[6:8, :] = u[TM - 2:TM, :]

    @pl.when(j == TILES_PER_SEQ - 1)
    def _():
        nconv_ref[0] = u[TM - 2:TM, :]

    qt, kt, kend, bl = _gla_prep(p_ref, wgu_ref, bg_ref, TM, CH)
    causal = _iota2((2 * CH, CH), 0) % CH >= _iota2((2 * CH, CH), 1)
    first_head = _iota2((CH, 2 * DK), 1) < DK
    ngl = ngl_ref[...]

    def pair_stack(x):
        return jnp.concatenate([jnp.where(first_head, x, 0.0), jnp.where(first_head, 0.0, x)], axis=0).astype(BF16)

    for c in range(TM // CH):
        r0 = c * CH
        for m in range(H // 2):
            lanes = slice(m * 2 * DK, (m + 1) * 2 * DK)
            lhs = pair_stack(qt[r0:r0 + CH, lanes])
            sc = lax.dot_general(lhs, kt[r0:r0 + CH, lanes].astype(BF16), (((1,), (1,)), ((), ())),
                                 preferred_element_type=F32)
            sc = jnp.where(causal, sc, 0.0).astype(BF16)
            st = st_ref[:, lanes]
            o_inter = lax.dot_general(lhs, st.astype(BF16), (((1,), (1,)), ((), ())),
                                      preferred_element_type=F32)
            vs = []
            for hh in range(2):
                h = 2 * m + hh
                v_h = p_ref[r0:r0 + CH, V_OFF + h * DV:V_OFF + (h + 1) * DV].astype(BF16)
                g_h = p_ref[r0:r0 + CH, G_OFF + h * DV:G_OFF + (h + 1) * DV]
                vs.append(v_h)
                o_h = (jnp.dot(sc[hh * CH:(hh + 1) * CH], v_h, preferred_element_type=F32)
                       + o_inter[hh * CH:(hh + 1) * CH])
                y_ref[r0:r0 + CH, CW + h * DV:CW + (h + 1) * DV] = _gla_out(o_h, g_h, ngl).astype(BF16)
            kv_t = lax.dot_general(jnp.concatenate(vs, axis=0), pair_stack(kend[r0:r0 + CH, lanes]),
                                   (((0,), (0,)), ((), ())), preferred_element_type=F32)
            st_ref[:, lanes] = st * jnp.exp(bl[r0:r0 + 1, lanes]) + kv_t

    @pl.when(j == TILES_PER_SEQ - 1)
    def _():
        nst_ref[0] = st_ref[...]


def _mixer_p(proj, conv_w, conv_b, wgu_b, b_gate, n_conv, n_gla):
    return pl.pallas_call(
        _mixer_p_kernel,
        out_shape=(jax.ShapeDtypeStruct((T_P, D), BF16),
                   jax.ShapeDtypeStruct((B_P, 2, CW), F32),
                   jax.ShapeDtypeStruct((B_P, DV, HK), F32)),
        grid=(N_TILES_P,),
        in_specs=[pl.BlockSpec((TM, PROJ_PAD), lambda i: (i, 0)),
                  _const_spec((3, CW)), _const_spec((1, CW)), _const_spec((128, HK)), _const_spec((1, HK)),
                  _const_spec((1, CW)), _const_spec((1, DV))],
        out_specs=(pl.BlockSpec((TM, D), lambda i: (i, 0)),
                   pl.BlockSpec((1, 2, CW), lambda i: (i // TILES_PER_SEQ, 0, 0)),
                   pl.BlockSpec((1, DV, HK), lambda i: (i // TILES_PER_SEQ, 0, 0))),
        scratch_shapes=[pltpu.VMEM((8 + TM, CW), F32), pltpu.VMEM((DV, HK), F32)],
        compiler_params=_cparams(),
        name="mixer_p",
    )(proj, conv_w, conv_b, wgu_b, b_gate, n_conv, n_gla)


def _mixer_s_kernel(p_ref, sconv_ref, sst_ref, cw_ref, cb_ref, wgu_ref, bg_ref, ncv_ref, ngl_ref,
                    y_ref, nconv_ref, nst_ref, ubuf):
    nseq = SEQ_PER_CHUNK
    bgate = p_ref[:, 0:CW]
    u = p_ref[:, CW:2 * CW] * p_ref[:, 2 * CW:3 * CW]
    ubuf[0:8, :] = jnp.zeros((8, CW), F32)
    ubuf[8:8 + CH, :] = u
    tpos = _iota2((CH, CW), 0) % L_S
    s0 = jnp.concatenate([jnp.broadcast_to(sconv_ref[s, 0:1, :], (L_S, CW)) for s in range(nseq)], axis=0)
    s1 = jnp.concatenate([jnp.broadcast_to(sconv_ref[s, 1:2, :], (L_S, CW)) for s in range(nseq)], axis=0)
    um1 = jnp.where(tpos == 0, s1, ubuf[7:7 + CH, :])
    um2 = jnp.where(tpos == 0, s0, jnp.where(tpos == 1, s1, ubuf[6:6 + CH, :]))
    y_ref[:, 0:CW] = _conv_out(bgate, u, um1, um2, cw_ref, cb_ref, ncv_ref).astype(BF16)
    for s in range(nseq):
        nconv_ref[s] = u[s * L_S + L_S - 2:(s + 1) * L_S, :]

    qt, kt, kend, bl = _gla_prep(p_ref, wgu_ref, bg_ref, CH, L_S)
    rr = _iota2((H * CH, CH), 0) % CH
    cc = _iota2((H * CH, CH), 1)
    causal = (rr >= cc) & ((rr // L_S) == (cc // L_S))
    ngl = ngl_ref[...]
    lhs_f = _head_stack(qt)
    sc = lax.dot_general(lhs_f.astype(BF16), kt.astype(BF16), (((1,), (1,)), ((), ())),
                         preferred_element_type=F32)
    sc = jnp.where(causal, sc, 0.0).astype(BF16)
    kstack_f = _head_stack(kend)
    v_all = p_ref[:, V_OFF:V_OFF + HV]
    decay_t = jnp.exp(jnp.concatenate([bl, jnp.zeros((128 - CH, HK), F32)], axis=0).T)
    o_inter = []
    for s in range(nseq):
        rs = [slice(h * CH + s * L_S, h * CH + (s + 1) * L_S) for h in range(H)]
        lhs_s = jnp.concatenate([lhs_f[r] for r in rs], axis=0).astype(BF16)
        k_s = jnp.concatenate([kstack_f[r] for r in rs], axis=0).astype(BF16)
        v_s = jnp.concatenate([v_all[s * L_S:(s + 1) * L_S, h * DV:(h + 1) * DV] for h in range(H)],
                              axis=0).astype(BF16)
        st = sst_ref[s]
        o_inter.append(jnp.dot(lhs_s, st.astype(BF16), preferred_element_type=F32))
        kv = lax.dot_general(k_s, v_s, (((0,), (0,)), ((), ())), preferred_element_type=F32)
        nst_ref[s] = st * decay_t[:, s * L_S:s * L_S + 1] + kv
    for h in range(H):
        v_h = v_all[:, h * DV:(h + 1) * DV].astype(BF16)
        g_h = p_ref[:, G_OFF + h * DV:G_OFF + (h + 1) * DV]
        oi_h = jnp.concatenate([o_inter[s][h * L_S:(h + 1) * L_S] for s in range(nseq)], axis=0)
        o_h = jnp.dot(sc[h * CH:(h + 1) * CH], v_h, preferred_element_type=F32) + oi_h
        y_ref[:, CW + h * DV:CW + (h + 1) * DV] = _gla_out(o_h, g_h, ngl).astype(BF16)


def _mixer_s(proj, sconv, sst, conv_w, conv_b, wgu_b, b_gate, n_conv, n_gla):
    nseq = SEQ_PER_CHUNK
    off = T_P // CH
    return pl.pallas_call(
        _mixer_s_kernel,
        out_shape=(jax.ShapeDtypeStruct((T_S, D), BF16),
                   jax.ShapeDtypeStruct((B_S, 2, CW), F32),
                   jax.ShapeDtypeStruct((B_S, HK, DV), F32)),
        grid=(T_S // CH,),
        in_specs=[pl.BlockSpec((CH, PROJ_PAD), lambda i: (i + off, 0)),
                  pl.BlockSpec((nseq, 2, CW), lambda i: (i, 0, 0)),
                  pl.BlockSpec((nseq, HK, DV), lambda i: (i, 0, 0)),
                  _const_spec((3, CW)), _const_spec((1, CW)), _const_spec((128, HK)), _const_spec((1, HK)),
                  _const_spec((1, CW)), _const_spec((1, DV))],
        out_specs=(pl.BlockSpec((CH, D), lambda i: (i, 0)),
                   pl.BlockSpec((nseq, 2, CW), lambda i: (i, 0, 0)),
                   pl.BlockSpec((nseq, HK, DV), lambda i: (i, 0, 0))),
        scratch_shapes=[pltpu.VMEM((8 + CH, CW), F32)],
        compiler_params=_cparams(),
        name="mixer_s",
    )(proj, sconv, sst, conv_w, conv_b, wgu_b, b_gate, n_conv, n_gla)


def _route(h2, wr_ref, br_ref, run_ref, pos_ref, rw_ref, cnt_ref):
    lt = lax.dot_general(wr_ref[...], h2.astype(BF16), (((1,), (1,)), ((), ())),
                         preferred_element_type=F32) + br_ref[...]

    coarse = lt[0:N_GROUPS]
    cmax = jnp.max(coarse, axis=0, keepdims=True)
    gi = _iota2((N_GROUPS, TM), 0)
    grp = jnp.min(jnp.where(coarse == cmax, gi, N_GROUPS), axis=0, keepdims=True)
    p_sel = 1.0 / jnp.sum(jnp.exp(coarse - cmax), axis=0, keepdims=True)
    fine = jnp.zeros((EPG, TM), F32)
    for g in range(N_GROUPS):
        fine = jnp.where(grp == g, lt[8 + g * EPG:8 + (g + 1) * EPG], fine)
    ei = _iota2((EPG, TM), 0)
    f1 = jnp.max(fine, axis=0, keepdims=True)
    i1 = jnp.min(jnp.where(fine == f1, ei, EPG), axis=0, keepdims=True)
    rest = jnp.where(ei == i1, -jnp.inf, fine)
    f2 = jnp.max(rest, axis=0, keepdims=True)
    i2 = jnp.min(jnp.where(rest == f2, ei, EPG), axis=0, keepdims=True)
    e2 = jnp.exp(f2 - f1)
    w1 = p_sel / (1.0 + e2)
    w2 = p_sel * e2 / (1.0 + e2)
    x1 = grp * EPG + i1
    x2 = grp * EPG + i2

    er = _iota2((NE, TM), 0)
    oh1 = jnp.where(er == x1, 1.0, 0.0)
    oh2 = jnp.where(er == x2, 1.0, 0.0)
    oh = oh1 + oh2
    before = jnp.where(_iota2((TM, TM), 0) < _iota2((TM, TM), 1), 1.0, 0.0).astype(BF16)
    cum = jnp.dot(oh.astype(BF16), before, preferred_element_type=F32) + run_ref[:, 0:1]
    r1 = jnp.sum(oh1 * cum, axis=0, keepdims=True).astype(I32)
    r2 = jnp.sum(oh2 * cum, axis=0, keepdims=True).astype(I32)
    run_new = run_ref[...] + jnp.sum(oh, axis=1, keepdims=True)
    run_ref[...] = run_new
    cnt_ref[...] = run_new.astype(I32)

    row = _iota2((8, TM), 0)
    pos_ref[...] = jnp.where(row == 0, x1 * E_CAP + r1, jnp.where(row == 1, x2 * E_CAP + r2, 0))
    wrow = _iota2((128, TM), 0)
    rw_ref[...] = jnp.where(wrow == 0, w1, jnp.where(wrow == 1, w2, 0.0)).T


def _outproj_kernel(ycp_ref, ycs_ref, xp_ref, xs_ref, gtp_ref, gts_ref, shp_ref, shs_ref, scp_ref, scs_ref,
                    wo_ref, g_ref, wr_ref, br_ref,
                    x1_ref, pos_ref, rw_ref, cnt_ref, xs_out,
                    run_ref, h2_scr, pos_v, pos_s, sem, psem):
    i = pl.program_id(0)
    par = i % 2

    def positions_to_smem():
        cp = pltpu.make_async_copy(pos_v, pos_s, psem)
        cp.start()
        cp.wait()

    def scatter(p):
        for r in range(TM):
            for slot in range(2):
                pltpu.make_async_copy(h2_scr.at[p, pl.ds(r, 1), :], xs_out.at[pl.ds(pos_s[slot, r], 1), :],
                                      sem).start(priority=r % 2)

    def drain():
        for _ in range(2):
            pltpu.make_async_copy(h2_scr.at[0], xs_out.at[pl.ds(0, TM), :], sem).wait()

    @pl.when(i == 0)
    def _():
        run_ref[...] = jnp.zeros((NE, 128), F32)
        h2_scr[1] = jnp.zeros((TM, D), F32)
        pos_v[...] = E_ROWS + _iota2((8, TM), 0) * TM + _iota2((8, TM), 1)
        positions_to_smem()

    def tile(yc, x, gt, sh, sc):
        scatter(1 - par)
        mix = jnp.dot(yc, wo_ref[...], preferred_element_type=F32)
        x1 = x + gt * mix
        x1_ref[...] = x1
        h2 = _rms(x1, g_ref[...]) * (1.0 + sc) + sh
        h2_scr[par] = h2
        _route(h2, wr_ref, br_ref, run_ref, pos_v, rw_ref, cnt_ref)

    @pl.when(i < N_TILES_P)
    def _():
        tile(ycp_ref[...], xp_ref[...], _prompt_row(gtp_ref), _prompt_row(shp_ref), _prompt_row(scp_ref))

    @pl.when(i >= N_TILES_P)
    def _():
        tile(ycs_ref[...], xs_ref[...], _sample_rows(gts_ref), _sample_rows(shs_ref), _sample_rows(scs_ref))

    pos_ref[0] = pos_v[...]
    drain()
    positions_to_smem()

    @pl.when(i == N_TILES - 1)
    def _():
        scatter(par)
        drain()


def _outproj(ycp, ycs, xp, xs, mod, w_out_b, g_ffn, w_r, b_r):
    return pl.pallas_call(
        _outproj_kernel,
        out_shape=(jax.ShapeDtypeStruct((T, D), F32), jax.ShapeDtypeStruct((N_TILES, 8, TM), I32),
                   jax.ShapeDtypeStruct((T, 128), F32), jax.ShapeDtypeStruct((NE, 128), I32),
                   jax.ShapeDtypeStruct((E_ROWS + 2 * TM, D), F32)),
        grid=(N_TILES,),
        in_specs=[pl.BlockSpec((TM, D), lambda i: (_p_tile(i), 0)), pl.BlockSpec((TM, D), lambda i: (_s_tile(i), 0)),
                  pl.BlockSpec((TM, D), lambda i: (_p_tile(i), 0)), pl.BlockSpec((TM, D), lambda i: (_s_tile(i), 0)),
                  *_mod_specs(2), *_mod_specs(3), *_mod_specs(4),
                  _const_spec((D, D)), _const_spec((1, D)), _const_spec((R_ROWS, D)), _const_spec((R_ROWS, 1))],
        out_specs=(pl.BlockSpec((TM, D), lambda i: (i, 0)), pl.BlockSpec((1, 8, TM), lambda i: (i, 0, 0)),
                   pl.BlockSpec((TM, 128), lambda i: (i, 0)), pl.BlockSpec((NE, 128), lambda i: (0, 0)),
                   pl.BlockSpec(memory_space=pl.ANY)),
        scratch_shapes=[pltpu.VMEM((NE, 128), F32), pltpu.VMEM((2, TM, D), F32), pltpu.VMEM((8, TM), I32),
                        pltpu.SMEM((8, TM), I32), pltpu.SemaphoreType.DMA(()), pltpu.SemaphoreType.DMA(())],
        compiler_params=_cparams(),
        name="outproj",
    )(ycp, ycs, xp, xs, mod, mod, mod, mod, mod, mod, w_out_b, g_ffn, w_r, b_r)


def _moe_kernel(tb_ref, te_ref, first_ref, nval_ref, nxt_ref, slot_ref, nt_ref,
                x_ref, wg_hbm, wu_hbm, wd_hbm, y_ref, wg_st, wu_st, wd_st, wgu_b, wd_b, wsem):
    i = pl.program_id(0)

    def weight_copies(e, s):
        return (pltpu.make_async_copy(wg_hbm.at[e], wg_st.at[s], wsem.at[s]),
                pltpu.make_async_copy(wu_hbm.at[e], wu_st.at[s], wsem.at[s]),
                pltpu.make_async_copy(wd_hbm.at[e], wd_st.at[s], wsem.at[s]))

    @pl.when(i < nt_ref[0])
    def _():
        @pl.when(first_ref[i] == 1)
        def _():
            s = slot_ref[i]

            @pl.when(i == 0)
            def _():
                for cp in weight_copies(te_ref[0], 0):
                    cp.start()

            for cp in weight_copies(te_ref[i], s):
                cp.wait()

            @pl.when(nxt_ref[i] >= 0)
            def _():
                for cp in weight_copies(nxt_ref[i], 1 - s):
                    cp.start()

            wgu_b[:, 0:DE] = wg_st[s].astype(BF16)
            wgu_b[:, DE:2 * DE] = wu_st[s].astype(BF16)
            wd_b[...] = wd_st[s].astype(BF16)

        valid = _iota2((E_TILE, D), 0) < nval_ref[i]
        x = jnp.where(valid, x_ref[...], 0.0).astype(BF16)
        ab = jnp.dot(x, wgu_b[...], preferred_element_type=F32)
        hid = (_silu(ab[:, 0:DE]) * ab[:, DE:2 * DE]).astype(BF16)
        y_ref[...] = jnp.dot(hid, wd_b[...], preferred_element_type=F32)


def _moe(plan, xs, w_eg, w_eu, w_ed):
    def tile_map(i, tb, *_):
        return (tb[i], 0)

    grid_spec = pltpu.PrefetchScalarGridSpec(
        num_scalar_prefetch=len(plan),
        grid=(E_TILES_MAX,),
        in_specs=[pl.BlockSpec((E_TILE, D), tile_map),
                  pl.BlockSpec(memory_space=pl.ANY), pl.BlockSpec(memory_space=pl.ANY),
                  pl.BlockSpec(memory_space=pl.ANY)],
        out_specs=pl.BlockSpec((E_TILE, D), tile_map),
        scratch_shapes=[pltpu.VMEM((2, D, DE), F32), pltpu.VMEM((2, D, DE), F32), pltpu.VMEM((2, DE, D), F32),
                        pltpu.VMEM((D, 2 * DE), BF16), pltpu.VMEM((DE, D), BF16), pltpu.SemaphoreType.DMA((2,))],
    )
    return pl.pallas_call(
        _moe_kernel,
        out_shape=jax.ShapeDtypeStruct((E_ROWS, D), F32),
        grid_spec=grid_spec,
        compiler_params=_cparams(),
        name="moe",
    )(*plan, xs, w_eg, w_eu, w_ed)


def _plan(counts):
    ntile = (counts + E_TILE - 1) // E_TILE
    tend = jnp.cumsum(ntile)
    tbeg = tend - ntile
    n_tiles = tend[-1]
    tid = jnp.minimum(jnp.arange(E_TILES_MAX, dtype=I32), n_tiles - 1)
    tile_expert = jnp.minimum(jnp.sum((tend[None, :] <= tid[:, None]).astype(I32), axis=1), NE - 1)
    tile_in_expert = tid - tbeg[tile_expert]
    tile_block = tile_expert * E_CAP_TILES + tile_in_expert
    tile_first = (tile_in_expert == 0).astype(I32)
    tile_nvalid = jnp.minimum(counts[tile_expert] - tile_in_expert * E_TILE, E_TILE)
    ids = jnp.arange(NE, dtype=I32)
    occupied = ntile > 0
    later = occupied[None, :] & (ids[None, :] > ids[:, None])
    next_expert = jnp.where(jnp.any(later, axis=1), jnp.min(jnp.where(later, ids[None, :], NE), axis=1), -1)
    rank = jnp.cumsum(occupied.astype(I32)) - 1
    return (tile_block, tile_expert, tile_first, tile_nvalid, next_expert[tile_expert].astype(I32),
            (rank[tile_expert] % 2).astype(I32), n_tiles.reshape(1))


def _final_kernel(pos0_ref, posn_ref, x1_ref, rw_ref, gtp_ref, gts_ref, g_ref, ys_ref, yp_ref, ysm_ref, buf, sem):
    i = pl.program_id(0)
    par = i % 2

    def gather(p_ref, p):
        for r in range(TM):
            for slot in range(2):
                pltpu.make_async_copy(ys_ref.at[pl.ds(p_ref[0, slot, r], 1), :], buf.at[p, slot, pl.ds(r, 1), :],
                                      sem.at[p]).start(priority=r % 2)

    @pl.when(i == 0)
    def _():
        gather(pos0_ref, 0)

    @pl.when(i < N_TILES - 1)
    def _():
        gather(posn_ref, 1 - par)

    for slot in range(2):
        pltpu.make_async_copy(ys_ref.at[pl.ds(0, TM), :], buf.at[par, slot], sem.at[par]).wait()

    moe = rw_ref[:, 0:1] * buf[par, 0] + rw_ref[:, 1:2] * buf[par, 1]

    @pl.when(i < N_TILES_P)
    def _():
        yp_ref[...] = _rms(x1_ref[...] + _prompt_row(gtp_ref) * moe, g_ref[...])

    @pl.when(i >= N_TILES_P)
    def _():
        ysm_ref[...] = _rms(x1_ref[...] + _sample_rows(gts_ref) * moe, g_ref[...])


def _final(pos, x1, rw, mod, g_fin, ys):
    return pl.pallas_call(
        _final_kernel,
        out_shape=(jax.ShapeDtypeStruct((T_P, D), F32), jax.ShapeDtypeStruct((T_S, D), F32)),
        grid=(N_TILES,),
        in_specs=[pl.BlockSpec((1, 8, TM), lambda i: (0, 0, 0), memory_space=pltpu.SMEM),
                  pl.BlockSpec((1, 8, TM), lambda i: (jnp.minimum(i + 1, N_TILES - 1), 0, 0),
                               memory_space=pltpu.SMEM),
                  pl.BlockSpec((TM, D), lambda i: (i, 0)),
                  pl.BlockSpec((TM, 128), lambda i: (i, 0)),
                  *_mod_specs(5),
                  _const_spec((1, D)),
                  pl.BlockSpec(memory_space=pl.ANY)],
        out_specs=(pl.BlockSpec((TM, D), lambda i: (_p_tile(i), 0)),
                   pl.BlockSpec((TM, D), lambda i: (_s_tile(i), 0))),
        scratch_shapes=[pltpu.VMEM((2, 2, TM, D), F32), pltpu.SemaphoreType.DMA((2,))],
        compiler_params=_cparams(),
        name="final",
    )(pos, pos, x1, rw, mod, mod, g_fin, ys)


def kernel(x_prompt, x_sample, c_prompt, c_sample, state_conv, state_gla, w_ada, b_ada, norm_mix, w_in, conv_w,
           conv_b, w_gate_up, b_gate, norm_conv, norm_gla, w_out, norm_ffn, w_coarse, b_coarse, w_fine, b_fine,
           w_exp_gate, w_exp_up, w_exp_down, norm_final):
    xp = x_prompt.reshape(T_P, D)
    xs = x_sample.reshape(T_S, D)
    c_all = jnp.concatenate([c_sample, c_prompt, jnp.zeros((C_ROWS - B_S - B_P, D), F32)], axis=0)
    mod = _ada(c_all, w_ada[0], b_ada[0][None, :])

    w_low_b = jnp.pad(w_in[0, :, PROJ_MAIN:], ((0, 0), (0, 128 - RANK))).astype(BF16)
    w_out_b = w_out[0].astype(BF16)
    wgu_b = jnp.pad(w_gate_up[0], ((0, 128 - RANK), (0, 0))).astype(BF16)
    w_r = jnp.concatenate([w_coarse[0].T, jnp.zeros((8 - N_GROUPS, D), F32), w_fine[0].T], axis=0)
    b_r = jnp.concatenate([b_coarse[0], jnp.zeros((8 - N_GROUPS,), F32), b_fine[0]])[:, None]
    g_mix, g_ffn, g_fin = norm_mix[0][None, :], norm_ffn[0][None, :], norm_final[None, :]
    mix_consts = (conv_w[0], conv_b[0][None, :], wgu_b, b_gate[0][None, :], norm_conv[0][None, :],
                  norm_gla[0][None, :])

    proj = _inproj(xp, xs, mod, g_mix, w_in, w_low_b)
    ycat_p, nconv_p, nst_p = _mixer_p(proj, *mix_consts)
    ycat_s, nconv_s, nst_s = _mixer_s(proj, state_conv[0], state_gla[0].reshape(B_S, HK, DV), *mix_consts)

    x1, pos, rw, counts, x_sorted = _outproj(ycat_p, ycat_s, xp, xs, mod, w_out_b, g_ffn, w_r.astype(BF16), b_r)
    y_sorted = _moe(_plan(counts[:, 0]), x_sorted, w_exp_gate[0], w_exp_up[0], w_exp_down[0])
    y_p, y_s = _final(pos, x1, rw, mod, g_fin, y_sorted)

    new_gla_p = nst_p.reshape(B_P, DV, H, DK).transpose(0, 2, 3, 1)[None]
    new_gla_s = nst_s.reshape(1, B_S, H, DK, DV)
    return (y_p.reshape(B_P, L_P, D), y_s.reshape(B_S, L_S, D), nconv_p[None], new_gla_p,
            nconv_s[None], new_gla_s)
```

```python
import functools

import jax
import jax.numpy as jnp
from jax import lax
from jax.experimental import pallas as pl
from jax.experimental.pallas import tpu as pltpu

F32 = jnp.float32
BF16 = jnp.bfloat16
I32 = jnp.int32

D = 2048
CW = 1024
H = 8
DK = 64
DV = 128
HK = H * DK
HV = H * DV
RANK = 16
TAU = 16.0
N_GROUPS = 4
EPG = 8
NE = N_GROUPS * EPG
DE = 512
EPS = 1e-6

B_P, L_P = 4, 2048
B_S, L_S = 128, 8
T_P = B_P * L_P
T_S = B_S * L_S
T = T_P + T_S

TM = 256
CH = 64
N_TILES = T // TM
N_TILES_P = T_P // TM
TILES_PER_SEQ = L_P // TM
SEQ_PER_TILE = TM // L_S
SEQ_PER_CHUNK = CH // L_S

PROJ_MAIN = 3 * CW + 2 * HK + 2 * HV
PROJ_PAD = PROJ_MAIN + 128
C_ROWS = 136
P_ROW_BLOCK = B_S // 8
R_ROWS = 40

E_TILE = 256
E_TILES_MAX = (2 * T) // E_TILE + NE
E_CAP = T
E_CAP_TILES = E_CAP // E_TILE
E_ROWS = NE * E_CAP
VMEM_LIMIT = 60 * 1024 * 1024


def _cparams(n_axes=1, vmem=VMEM_LIMIT):
    return pltpu.CompilerParams(dimension_semantics=("arbitrary",) * n_axes, vmem_limit_bytes=vmem)


def _rms(x, g):
    return x * lax.rsqrt(jnp.mean(x * x, axis=-1, keepdims=True) + EPS) * g


def _sigmoid(x):
    return 1.0 / (1.0 + jnp.exp(-x))


def _silu(x):
    return x * _sigmoid(x)


def _log_sigmoid(x):
    return jnp.minimum(x, 0.0) - jnp.log(1.0 + jnp.exp(-jnp.abs(x)))


def _iota2(shape, axis):
    return lax.broadcasted_iota(I32, shape, axis)


def _expand_rows(ref, n, reps):
    return jnp.concatenate([jnp.broadcast_to(ref[j:j + 1, :], (reps, ref.shape[-1])) for j in range(n)], axis=0)


def _prompt_row(ref):
    return ref[pl.ds(pl.program_id(0) // TILES_PER_SEQ, 1), :]


def _sample_rows(ref):
    return _expand_rows(ref, SEQ_PER_TILE, L_S)


def _p_tile(i):
    return jnp.minimum(i, N_TILES_P - 1)


def _s_tile(i):
    return jnp.maximum(i - N_TILES_P, 0)


def _mod_specs(col):
    return [pl.BlockSpec((8, D), lambda i, *_: (P_ROW_BLOCK, col)),
            pl.BlockSpec((SEQ_PER_TILE, D), lambda i, *_: (_s_tile(i), col))]


def _const_spec(shape):
    zeros = (0,) * len(shape)
    return pl.BlockSpec(shape, lambda *_: zeros, pipeline_mode=pl.Buffered(1))


def _ada_kernel(c_ref, w_ref, b_ref, o_ref):
    s = _silu(c_ref[...]).astype(BF16)
    o_ref[...] = jnp.dot(s, w_ref[...].astype(BF16), preferred_element_type=F32) + b_ref[...]


def _ada(c_all, w_ada, b_ada):
    tn = 512
    return pl.pallas_call(
        _ada_kernel,
        out_shape=jax.ShapeDtypeStruct((C_ROWS, 6 * D), F32),
        grid=(6 * D // tn,),
        in_specs=[pl.BlockSpec((C_ROWS, D), lambda j: (0, 0)),
                  pl.BlockSpec((D, tn), lambda j: (0, j)),
                  pl.BlockSpec((1, tn), lambda j: (0, j))],
        out_specs=pl.BlockSpec((C_ROWS, tn), lambda j: (0, j)),
        compiler_params=_cparams(),
        name="ada",
    )(c_all, w_ada, b_ada)


W_CHUNK = 256
N_W_CHUNKS = PROJ_MAIN // W_CHUNK
NT_DIMS = (((1,), (1,)), ((), ()))


def _inproj_kernel(xp_ref, xs_ref, shp_ref, shs_ref, scp_ref, scs_ref, g_ref, wt_hbm, wa_ref, o_ref,
                   h_scr, w_scr, stage, wsem):
    i = pl.program_id(0)

    @pl.when(i == 0)
    def _():
        def chunk(c):
            return pltpu.make_async_copy(wt_hbm.at[0, pl.ds(c * W_CHUNK, W_CHUNK), :], stage.at[c % 2],
                                         wsem.at[c % 2])

        chunk(0).start()
        for c in range(N_W_CHUNKS):
            if c + 1 < N_W_CHUNKS:
                chunk(c + 1).start()
            chunk(c).wait()
            w_scr[c * W_CHUNK:(c + 1) * W_CHUNK, :] = stage[c % 2].astype(BF16)

    @pl.when(i < N_TILES_P)
    def _():
        h = _rms(xp_ref[...], g_ref[...]) * (1.0 + _prompt_row(scp_ref)) + _prompt_row(shp_ref)
        h_scr[...] = h.astype(BF16)

    @pl.when(i >= N_TILES_P)
    def _():
        h = _rms(xs_ref[...], g_ref[...]) * (1.0 + _sample_rows(scs_ref)) + _sample_rows(shs_ref)
        h_scr[...] = h.astype(BF16)

    h = h_scr[...]
    o_ref[:, 0:PROJ_MAIN] = lax.dot_general(h, w_scr[...], NT_DIMS, preferred_element_type=F32)
    o_ref[:, PROJ_MAIN:PROJ_PAD] = lax.dot_general(h, wa_ref[...], NT_DIMS, preferred_element_type=F32)


def _inproj(xp, xs, mod, g_mix, w_in_t, w_low_t):
    return pl.pallas_call(
        _inproj_kernel,
        out_shape=jax.ShapeDtypeStruct((T, PROJ_PAD), F32),
        grid=(N_TILES,),
        in_specs=[pl.BlockSpec((TM, D), lambda i: (_p_tile(i), 0)),
                  pl.BlockSpec((TM, D), lambda i: (_s_tile(i), 0)),
                  *_mod_specs(0), *_mod_specs(1),
                  _const_spec((1, D)), pl.BlockSpec(memory_space=pl.ANY), _const_spec((128, D))],
        out_specs=pl.BlockSpec((TM, PROJ_PAD), lambda i: (i, 0)),
        scratch_shapes=[pltpu.VMEM((TM, D), BF16), pltpu.VMEM((PROJ_MAIN, D), BF16),
                        pltpu.VMEM((2, W_CHUNK, D), F32), pltpu.SemaphoreType.DMA((2,))],
        compiler_params=_cparams(),
        name="inproj",
    )(xp, xs, mod, mod, mod, mod, g_mix, w_in_t, w_low_t)


def _gla_prep(p_ref, wgu_ref, bg_ref, rows, seg):
    q = p_ref[:, 3 * CW:3 * CW + HK]
    k = p_ref[:, 3 * CW + HK:3 * CW + 2 * HK]
    a = p_ref[:, PROJ_MAIN:PROJ_PAD].astype(BF16)
    z = jnp.dot(a, wgu_ref[...], preferred_element_type=F32) + bg_ref[...]
    la = _log_sigmoid(z) * (1.0 / TAU)
    la0 = la.astype(BF16)
    rem = la - la0.astype(F32)
    la1 = rem.astype(BF16)
    la2 = (rem - la1.astype(F32)).astype(BF16)
    r = _iota2((2 * rows, rows), 0)
    c = _iota2((2 * rows, rows), 1)
    same = ((r % rows) // seg) == (c // seg)
    sel = jnp.where(same & ((r >= rows) | (c <= r)), 1.0, 0.0).astype(BF16)
    sums = sum(jnp.dot(sel, part, preferred_element_type=F32) for part in (la0, la1, la2))
    b, bl = sums[0:rows], sums[rows:2 * rows]
    qt = q * jnp.exp(b) * (DK ** -0.5)
    kt = k * jnp.exp(-b)
    kend = kt * jnp.exp(bl)
    return qt, kt, kend, bl


def _head_stack(x):
    rows = x.shape[0]
    t = jnp.concatenate([x] * H, axis=0)
    keep = (_iota2((H * rows, HK), 0) // rows) == (_iota2((H * rows, HK), 1) // DK)
    return jnp.where(keep, t, 0.0)


def _gla_out(o_h, g_h, ngl):
    on = o_h * lax.rsqrt(jnp.mean(o_h * o_h, axis=-1, keepdims=True) + EPS) * ngl
    return on * _silu(g_h)


def _conv_out(bg, u, um1, um2, cw_ref, cb_ref, ncv_ref):
    conv_y = cb_ref[...] + cw_ref[0:1, :] * um2 + cw_ref[1:2, :] * um1 + cw_ref[2:3, :] * u
    return _rms(bg * conv_y, ncv_ref[...])


V_OFF = 3 * CW + 2 * HK
G_OFF = V_OFF + HV


def _mixer_p_kernel(p_ref, cw_ref, cb_ref, wgu_ref, bg_ref, ncv_ref, ngl_ref,
                    y_ref, nconv_ref, nst_ref, ubuf, st_ref):
    j = pl.program_id(0) % TILES_PER_SEQ

    @pl.when(j == 0)
    def _():
        ubuf[0:8, :] = jnp.zeros((8, CW), F32)
        st_ref[...] = jnp.zeros((DV, HK), F32)

    bgate = p_ref[:, 0:CW]
    u = p_ref[:, CW:2 * CW] * p_ref[:, 2 * CW:3 * CW]
    ubuf[8:8 + TM, :] = u
    um1 = ubuf[7:7 + TM, :]
    um2 = ubuf[6:6 + TM, :]
    y_ref[:, 0:CW] = _conv_out(bgate, u, um1, um2, cw_ref, cb_ref, ncv_ref).astype(BF16)
    ubuf[6:8, :] = u[TM - 2:TM, :]

    @pl.when(j == TILES_PER_SEQ - 1)
    def _():
        nconv_ref[0] = u[TM - 2:TM, :]

    qt, kt, kend, bl = _gla_prep(p_ref, wgu_ref, bg_ref, TM, CH)
    causal = _iota2((2 * CH, CH), 0) % CH >= _iota2((2 * CH, CH), 1)
    first_head = _iota2((CH, 2 * DK), 1) < DK
    ngl = ngl_ref[...]

    def pair_stack(x):
        return jnp.concatenate([jnp.where(first_head, x, 0.0), jnp.where(first_head, 0.0, x)], axis=0).astype(BF16)

    for c in range(TM // CH):
        r0 = c * CH
        for m in range(H // 2):
            lanes = slice(m * 2 * DK, (m + 1) * 2 * DK)
            lhs = pair_stack(qt[r0:r0 + CH, lanes])
            sc = lax.dot_general(lhs, kt[r0:r0 + CH, lanes].astype(BF16), (((1,), (1,)), ((), ())),
                                 preferred_element_type=F32)
            sc = jnp.where(causal, sc, 0.0).astype(BF16)
            st = st_ref[:, lanes]
            o_inter = lax.dot_general(lhs, st.astype(BF16), (((1,), (1,)), ((), ())),
                                      preferred_element_type=F32)
            vs = []
            for hh in range(2):
                h = 2 * m + hh
                v_h = p_ref[r0:r0 + CH, V_OFF + h * DV:V_OFF + (h + 1) * DV].astype(BF16)
                g_h = p_ref[r0:r0 + CH, G_OFF + h * DV:G_OFF + (h + 1) * DV]
                vs.append(v_h)
                o_h = (jnp.dot(sc[hh * CH:(hh + 1) * CH], v_h, preferred_element_type=F32)
                       + o_inter[hh * CH:(hh + 1) * CH])
                y_ref[r0:r0 + CH, CW + h * DV:CW + (h + 1) * DV] = _gla_out(o_h, g_h, ngl).astype(BF16)
            kv_t = lax.dot_general(jnp.concatenate(vs, axis=0), pair_stack(kend[r0:r0 + CH, lanes]),
                                   (((0,), (0,)), ((), ())), preferred_element_type=F32)
            st_ref[:, lanes] = st * jnp.exp(bl[r0:r0 + 1, lanes]) + kv_t

    @pl.when(j == TILES_PER_SEQ - 1)
    def _():
        nst_ref[0] = st_ref[...]


def _mixer_p(proj, conv_w, conv_b, wgu_b, b_gate, n_conv, n_gla):
    return pl.pallas_call(
        _mixer_p_kernel,
        out_shape=(jax.ShapeDtypeStruct((T_P, D), BF16),
                   jax.ShapeDtypeStruct((B_P, 2, CW), F32),
                   jax.ShapeDtypeStruct((B_P, DV, HK), F32)),
        grid=(N_TILES_P,),
        in_specs=[pl.BlockSpec((TM, PROJ_PAD), lambda i: (i, 0)),
                  _const_spec((3, CW)), _const_spec((1, CW)), _const_spec((128, HK)), _const_spec((1, HK)),
                  _const_spec((1, CW)), _const_spec((1, DV))],
        out_specs=(pl.BlockSpec((TM, D), lambda i: (i, 0)),
                   pl.BlockSpec((1, 2, CW), lambda i: (i // TILES_PER_SEQ, 0, 0)),
                   pl.BlockSpec((1, DV, HK), lambda i: (i // TILES_PER_SEQ, 0, 0))),
        scratch_shapes=[pltpu.VMEM((8 + TM, CW), F32), pltpu.VMEM((DV, HK), F32)],
        compiler_params=_cparams(),
        name="mixer_p",
    )(proj, conv_w, conv_b, wgu_b, b_gate, n_conv, n_gla)


def _mixer_s_kernel(p_ref, sconv_ref, sst_ref, cw_ref, cb_ref, wgu_ref, bg_ref, ncv_ref, ngl_ref,
                    y_ref, nconv_ref, nst_ref, ubuf):
    nseq = SEQ_PER_CHUNK
    bgate = p_ref[:, 0:CW]
    u = p_ref[:, CW:2 * CW] * p_ref[:, 2 * CW:3 * CW]
    ubuf[0:8, :] = jnp.zeros((8, CW), F32)
    ubuf[8:8 + CH, :] = u
    tpos = _iota2((CH, CW), 0) % L_S
    s0 = jnp.concatenate([jnp.broadcast_to(sconv_ref[s, 0:1, :], (L_S, CW)) for s in range(nseq)], axis=0)
    s1 = jnp.concatenate([jnp.broadcast_to(sconv_ref[s, 1:2, :], (L_S, CW)) for s in range(nseq)], axis=0)
    um1 = jnp.where(tpos == 0, s1, ubuf[7:7 + CH, :])
    um2 = jnp.where(tpos == 0, s0, jnp.where(tpos == 1, s1, ubuf[6:6 + CH, :]))
    y_ref[:, 0:CW] = _conv_out(bgate, u, um1, um2, cw_ref, cb_ref, ncv_ref).astype(BF16)
    for s in range(nseq):
        nconv_ref[s] = u[s * L_S + L_S - 2:(s + 1) * L_S, :]

    qt, kt, kend, bl = _gla_prep(p_ref, wgu_ref, bg_ref, CH, L_S)
    rr = _iota2((H * CH, CH), 0) % CH
    cc = _iota2((H * CH, CH), 1)
    causal = (rr >= cc) & ((rr // L_S) == (cc // L_S))
    ngl = ngl_ref[...]
    lhs_f = _head_stack(qt)
    sc = lax.dot_general(lhs_f.astype(BF16), kt.astype(BF16), (((1,), (1,)), ((), ())),
                         preferred_element_type=F32)
    sc = jnp.where(causal, sc, 0.0).astype(BF16)
    kstack_f = _head_stack(kend)
    v_all = p_ref[:, V_OFF:V_OFF + HV]
    decay_t = jnp.exp(jnp.concatenate([bl, jnp.zeros((128 - CH, HK), F32)], axis=0).T)
    o_inter = []
    for s in range(nseq):
        rs = [slice(h * CH + s * L_S, h * CH + (s + 1) * L_S) for h in range(H)]
        lhs_s = jnp.concatenate([lhs_f[r] for r in rs], axis=0).astype(BF16)
        k_s = jnp.concatenate([kstack_f[r] for r in rs], axis=0).astype(BF16)
        v_s = jnp.concatenate([v_all[s * L_S:(s + 1) * L_S, h * DV:(h + 1) * DV] for h in range(H)],
                              axis=0).astype(BF16)
        st = sst_ref[s]
        o_inter.append(jnp.dot(lhs_s, st.astype(BF16), preferred_element_type=F32))
        kv = lax.dot_general(k_s, v_s, (((0,), (0,)), ((), ())), preferred_element_type=F32)
        nst_ref[s] = st * decay_t[:, s * L_S:s * L_S + 1] + kv
    for h in range(H):
        v_h = v_all[:, h * DV:(h + 1) * DV].astype(BF16)
        g_h = p_ref[:, G_OFF + h * DV:G_OFF + (h + 1) * DV]
        oi_h = jnp.concatenate([o_inter[s][h * L_S:(h + 1) * L_S] for s in range(nseq)], axis=0)
        o_h = jnp.dot(sc[h * CH:(h + 1) * CH], v_h, preferred_element_type=F32) + oi_h
        y_ref[:, CW + h * DV:CW + (h + 1) * DV] = _gla_out(o_h, g_h, ngl).astype(BF16)


def _mixer_s(proj, sconv, sst, conv_w, conv_b, wgu_b, b_gate, n_conv, n_gla):
    nseq = SEQ_PER_CHUNK
    off = T_P // CH
    return pl.pallas_call(
        _mixer_s_kernel,
        out_shape=(jax.ShapeDtypeStruct((T_S, D), BF16),
                   jax.ShapeDtypeStruct((B_S, 2, CW), F32),
                   jax.ShapeDtypeStruct((B_S, HK, DV), F32)),
        grid=(T_S // CH,),
        in_specs=[pl.BlockSpec((CH, PROJ_PAD), lambda i: (i + off, 0)),
                  pl.BlockSpec((nseq, 2, CW), lambda i: (i, 0, 0)),
                  pl.BlockSpec((nseq, HK, DV), lambda i: (i, 0, 0)),
                  _const_spec((3, CW)), _const_spec((1, CW)), _const_spec((128, HK)), _const_spec((1, HK)),
                  _const_spec((1, CW)), _const_spec((1, DV))],
        out_specs=(pl.BlockSpec((CH, D), lambda i: (i, 0)),
                   pl.BlockSpec((nseq, 2, CW), lambda i: (i, 0, 0)),
                   pl.BlockSpec((nseq, HK, DV), lambda i: (i, 0, 0))),
        scratch_shapes=[pltpu.VMEM((8 + CH, CW), F32)],
        compiler_params=_cparams(),
        name="mixer_s",
    )(proj, sconv, sst, conv_w, conv_b, wgu_b, b_gate, n_conv, n_gla)


def _route(h2, wr_ref, br_ref, run_ref, pos_ref, rw_ref, cnt_ref):
    lt = lax.dot_general(wr_ref[...], h2.astype(BF16), (((1,), (1,)), ((), ())),
                         preferred_element_type=F32) + br_ref[...]

    coarse = lt[0:N_GROUPS]
    cmax = jnp.max(coarse, axis=0, keepdims=True)
    gi = _iota2((N_GROUPS, TM), 0)
    grp = jnp.min(jnp.where(coarse == cmax, gi, N_GROUPS), axis=0, keepdims=True)
    p_sel = 1.0 / jnp.sum(jnp.exp(coarse - cmax), axis=0, keepdims=True)
    fine = jnp.zeros((EPG, TM), F32)
    for g in range(N_GROUPS):
        fine = jnp.where(grp == g, lt[8 + g * EPG:8 + (g + 1) * EPG], fine)
    ei = _iota2((EPG, TM), 0)
    f1 = jnp.max(fine, axis=0, keepdims=True)
    i1 = jnp.min(jnp.where(fine == f1, ei, EPG), axis=0, keepdims=True)
    rest = jnp.where(ei == i1, -jnp.inf, fine)
    f2 = jnp.max(rest, axis=0, keepdims=True)
    i2 = jnp.min(jnp.where(rest == f2, ei, EPG), axis=0, keepdims=True)
    e2 = jnp.exp(f2 - f1)
    w1 = p_sel / (1.0 + e2)
    w2 = p_sel * e2 / (1.0 + e2)
    x1 = grp * EPG + i1
    x2 = grp * EPG + i2

    er = _iota2((NE, TM), 0)
    oh1 = jnp.where(er == x1, 1.0, 0.0)
    oh2 = jnp.where(er == x2, 1.0, 0.0)
    oh = oh1 + oh2
    before = jnp.where(_iota2((TM, TM), 0) < _iota2((TM, TM), 1), 1.0, 0.0).astype(BF16)
    cum = jnp.dot(oh.astype(BF16), before, preferred_element_type=F32) + run_ref[:, 0:1]
    r1 = jnp.sum(oh1 * cum, axis=0, keepdims=True).astype(I32)
    r2 = jnp.sum(oh2 * cum, axis=0, keepdims=True).astype(I32)
    run_new = run_ref[...] + jnp.sum(oh, axis=1, keepdims=True)
    run_ref[...] = run_new
    cnt_ref[...] = run_new.astype(I32)

    row = _iota2((8, TM), 0)
    pos_ref[...] = jnp.where(row == 0, x1 * E_CAP + r1, jnp.where(row == 1, x2 * E_CAP + r2, 0))
    wrow = _iota2((128, TM), 0)
    rw_ref[...] = jnp.where(wrow == 0, w1, jnp.where(wrow == 1, w2, 0.0)).T


def _outproj_kernel(ycp_ref, ycs_ref, xp_ref, xs_ref, gtp_ref, gts_ref, shp_ref, shs_ref, scp_ref, scs_ref,
                    wo_ref, g_ref, wr_ref, br_ref,
                    x1_ref, pos_ref, rw_ref, cnt_ref, xs_out,
                    run_ref, h2_scr, pos_v, pos_s, sem, psem):
    i = pl.program_id(0)
    par = i % 2

    def positions_to_smem():
        cp = pltpu.make_async_copy(pos_v, pos_s, psem)
        cp.start()
        cp.wait()

    def scatter(p):
        for r in range(TM):
            for slot in range(2):
                pltpu.make_async_copy(h2_scr.at[p, pl.ds(r, 1), :], xs_out.at[pl.ds(pos_s[slot, r], 1), :],
                                      sem).start(priority=r % 2)

    def drain():
        for _ in range(2):
            pltpu.make_async_copy(h2_scr.at[0], xs_out.at[pl.ds(0, TM), :], sem).wait()

    @pl.when(i == 0)
    def _():
        run_ref[...] = jnp.zeros((NE, 128), F32)
        h2_scr[1] = jnp.zeros((TM, D), F32)
        pos_v[...] = E_ROWS + _iota2((8, TM), 0) * TM + _iota2((8, TM), 1)
        positions_to_smem()

    def tile(yc, x, gt, sh, sc):
        scatter(1 - par)
        mix = jnp.dot(yc, wo_ref[...], preferred_element_type=F32)
        x1 = x + gt * mix
        x1_ref[...] = x1
        h2 = _rms(x1, g_ref[...]) * (1.0 + sc) + sh
        h2_scr[par] = h2
        _route(h2, wr_ref, br_ref, run_ref, pos_v, rw_ref, cnt_ref)

    @pl.when(i < N_TILES_P)
    def _():
        tile(ycp_ref[...], xp_ref[...], _prompt_row(gtp_ref), _prompt_row(shp_ref), _prompt_row(scp_ref))

    @pl.when(i >= N_TILES_P)
    def _():
        tile(ycs_ref[...], xs_ref[...], _sample_rows(gts_ref), _sample_rows(shs_ref), _sample_rows(scs_ref))

    pos_ref[0] = pos_v[...]
    drain()
    positions_to_smem()

    @pl.when(i == N_TILES - 1)
    def _():
        scatter(par)
        drain()


def _outproj(ycp, ycs, xp, xs, mod, w_out_b, g_ffn, w_r, b_r):
    return pl.pallas_call(
        _outproj_kernel,
        out_shape=(jax.ShapeDtypeStruct((T, D), F32), jax.ShapeDtypeStruct((N_TILES, 8, TM), I32),
                   jax.ShapeDtypeStruct((T, 128), F32), jax.ShapeDtypeStruct((NE, 128), I32),
                   jax.ShapeDtypeStruct((E_ROWS + 2 * TM, D), F32)),
        grid=(N_TILES,),
        in_specs=[pl.BlockSpec((TM, D), lambda i: (_p_tile(i), 0)), pl.BlockSpec((TM, D), lambda i: (_s_tile(i), 0)),
                  pl.BlockSpec((TM, D), lambda i: (_p_tile(i), 0)), pl.BlockSpec((TM, D), lambda i: (_s_tile(i), 0)),
                  *_mod_specs(2), *_mod_specs(3), *_mod_specs(4),
                  _const_spec((D, D)), _const_spec((1, D)), _const_spec((R_ROWS, D)), _const_spec((R_ROWS, 1))],
        out_specs=(pl.BlockSpec((TM, D), lambda i: (i, 0)), pl.BlockSpec((1, 8, TM), lambda i: (i, 0, 0)),
                   pl.BlockSpec((TM, 128), lambda i: (i, 0)), pl.BlockSpec((NE, 128), lambda i: (0, 0)),
                   pl.BlockSpec(memory_space=pl.ANY)),
        scratch_shapes=[pltpu.VMEM((NE, 128), F32), pltpu.VMEM((2, TM, D), F32), pltpu.VMEM((8, TM), I32),
                        pltpu.SMEM((8, TM), I32), pltpu.SemaphoreType.DMA(()), pltpu.SemaphoreType.DMA(())],
        compiler_params=_cparams(),
        name="outproj",
    )(ycp, ycs, xp, xs, mod, mod, mod, mod, mod, mod, w_out_b, g_ffn, w_r, b_r)


def _moe_kernel(tb_ref, te_ref, first_ref, nval_ref, nxt_ref, slot_ref, nt_ref,
                x_ref, wg_hbm, wu_hbm, wd_hbm, y_ref, wg_st, wu_st, wd_st, wgu_b, wd_b, wsem):
    i = pl.program_id(0)

    def weight_copies(e, s):
        return (pltpu.make_async_copy(wg_hbm.at[e], wg_st.at[s], wsem.at[s]),
                pltpu.make_async_copy(wu_hbm.at[e], wu_st.at[s], wsem.at[s]),
                pltpu.make_async_copy(wd_hbm.at[e], wd_st.at[s], wsem.at[s]))

    @pl.when(i < nt_ref[0])
    def _():
        @pl.when(first_ref[i] == 1)
        def _():
            s = slot_ref[i]

            @pl.when(i == 0)
            def _():
                for cp in weight_copies(te_ref[0], 0):
                    cp.start()

            for cp in weight_copies(te_ref[i], s):
                cp.wait()

            @pl.when(nxt_ref[i] >= 0)
            def _():
                for cp in weight_copies(nxt_ref[i], 1 - s):
                    cp.start()

            wgu_b[:, 0:DE] = wg_st[s].astype(BF16)
            wgu_b[:, DE:2 * DE] = wu_st[s].astype(BF16)
            wd_b[...] = wd_st[s].astype(BF16)

        valid = _iota2((E_TILE, D), 0) < nval_ref[i]
        x = jnp.where(valid, x_ref[...], 0.0).astype(BF16)
        ab = jnp.dot(x, wgu_b[...], preferred_element_type=F32)
        hid = (_silu(ab[:, 0:DE]) * ab[:, DE:2 * DE]).astype(BF16)
        y_ref[...] = jnp.dot(hid, wd_b[...], preferred_element_type=F32)


def _moe(plan, xs, w_eg, w_eu, w_ed):
    def tile_map(i, tb, *_):
        return (tb[i], 0)

    grid_spec = pltpu.PrefetchScalarGridSpec(
        num_scalar_prefetch=len(plan),
        grid=(E_TILES_MAX,),
        in_specs=[pl.BlockSpec((E_TILE, D), tile_map),
                  pl.BlockSpec(memory_space=pl.ANY), pl.BlockSpec(memory_space=pl.ANY),
                  pl.BlockSpec(memory_space=pl.ANY)],
        out_specs=pl.BlockSpec((E_TILE, D), tile_map),
        scratch_shapes=[pltpu.VMEM((2, D, DE), F32), pltpu.VMEM((2, D, DE), F32), pltpu.VMEM((2, DE, D), F32),
                        pltpu.VMEM((D, 2 * DE), BF16), pltpu.VMEM((DE, D), BF16), pltpu.SemaphoreType.DMA((2,))],
    )
    return pl.pallas_call(
        _moe_kernel,
        out_shape=jax.ShapeDtypeStruct((E_ROWS, D), F32),
        grid_spec=grid_spec,
        compiler_params=_cparams(),
        name="moe",
    )(*plan, xs, w_eg, w_eu, w_ed)


def _plan(counts):
    ntile = (counts + E_TILE - 1) // E_TILE
    tend = jnp.cumsum(ntile)
    tbeg = tend - ntile
    n_tiles = tend[-1]
    tid = jnp.minimum(jnp.arange(E_TILES_MAX, dtype=I32), n_tiles - 1)
    tile_expert = jnp.minimum(jnp.sum((tend[None, :] <= tid[:, None]).astype(I32), axis=1), NE - 1)
    tile_in_expert = tid - tbeg[tile_expert]
    tile_block = tile_expert * E_CAP_TILES + tile_in_expert
    tile_first = (tile_in_expert == 0).astype(I32)
    tile_nvalid = jnp.minimum(counts[tile_expert] - tile_in_expert * E_TILE, E_TILE)
    ids = jnp.arange(NE, dtype=I32)
    occupied = ntile > 0
    later = occupied[None, :] & (ids[None, :] > ids[:, None])
    next_expert = jnp.where(jnp.any(later, axis=1), jnp.min(jnp.where(later, ids[None, :], NE), axis=1), -1)
    rank = jnp.cumsum(occupied.astype(I32)) - 1
    return (tile_block, tile_expert, tile_first, tile_nvalid, next_expert[tile_expert].astype(I32),
            (rank[tile_expert] % 2).astype(I32), n_tiles.reshape(1))


def _final_kernel(pos0_ref, posn_ref, x1_ref, rw_ref, gtp_ref, gts_ref, g_ref, ys_ref, yp_ref, ysm_ref, buf, sem):
    i = pl.program_id(0)
    par = i % 2

    def gather(p_ref, p):
        for r in range(TM):
            for slot in range(2):
                pltpu.make_async_copy(ys_ref.at[pl.ds(p_ref[0, slot, r], 1), :], buf.at[p, slot, pl.ds(r, 1), :],
                                      sem.at[p]).start(priority=r % 2)

    @pl.when(i == 0)
    def _():
        gather(pos0_ref, 0)

    @pl.when(i < N_TILES - 1)
    def _():
        gather(posn_ref, 1 - par)

    for slot in range(2):
        pltpu.make_async_copy(ys_ref.at[pl.ds(0, TM), :], buf.at[par, slot], sem.at[par]).wait()

    moe = rw_ref[:, 0:1] * buf[par, 0] + rw_ref[:, 1:2] * buf[par, 1]

    @pl.when(i < N_TILES_P)
    def _():
        yp_ref[...] = _rms(x1_ref[...] + _prompt_row(gtp_ref) * moe, g_ref[...])

    @pl.when(i >= N_TILES_P)
    def _():
        ysm_ref[...] = _rms(x1_ref[...] + _sample_rows(gts_ref) * moe, g_ref[...])


def _final(pos, x1, rw, mod, g_fin, ys):
    return pl.pallas_call(
        _final_kernel,
        out_shape=(jax.ShapeDtypeStruct((T_P, D), F32), jax.ShapeDtypeStruct((T_S, D), F32)),
        grid=(N_TILES,),
        in_specs=[pl.BlockSpec((1, 8, TM), lambda i: (0, 0, 0), memory_space=pltpu.SMEM),
                  pl.BlockSpec((1, 8, TM), lambda i: (jnp.minimum(i + 1, N_TILES - 1), 0, 0),
                               memory_space=pltpu.SMEM),
                  pl.BlockSpec((TM, D), lambda i: (i, 0)),
                  pl.BlockSpec((TM, 128), lambda i: (i, 0)),
                  *_mod_specs(5),
                  _const_spec((1, D)),
                  pl.BlockSpec(memory_space=pl.ANY)],
        out_specs=(pl.BlockSpec((TM, D), lambda i: (_p_tile(i), 0)),
                   pl.BlockSpec((TM, D), lambda i: (_s_tile(i), 0))),
        scratch_shapes=[pltpu.VMEM((2, 2, TM, D), F32), pltpu.SemaphoreType.DMA((2,))],
        compiler_params=_cparams(),
        name="final",
    )(pos, pos, x1, rw, mod, mod, g_fin, ys)


def kernel(x_prompt, x_sample, c_prompt, c_sample, state_conv, state_gla, w_ada, b_ada, norm_mix, w_in, conv_w,
           conv_b, w_gate_up, b_gate, norm_conv, norm_gla, w_out, norm_ffn, w_coarse, b_coarse, w_fine, b_fine,
           w_exp_gate, w_exp_up, w_exp_down, norm_final):
    xp = x_prompt.reshape(T_P, D)
    xs = x_sample.reshape(T_S, D)
    c_all = jnp.concatenate([c_sample, c_prompt, jnp.zeros((C_ROWS - B_S - B_P, D), F32)], axis=0)
    mod = _ada(c_all, w_ada[0], b_ada[0][None, :])

    w_in_t = jnp.swapaxes(w_in, 1, 2)
    w_low_t = jnp.pad(w_in_t[0, PROJ_MAIN:, :], ((0, 128 - RANK), (0, 0))).astype(BF16)
    w_out_b = w_out[0].astype(BF16)
    wgu_b = jnp.pad(w_gate_up[0], ((0, 128 - RANK), (0, 0))).astype(BF16)
    w_r = jnp.concatenate([w_coarse[0].T, jnp.zeros((8 - N_GROUPS, D), F32), w_fine[0].T], axis=0)
    b_r = jnp.concatenate([b_coarse[0], jnp.zeros((8 - N_GROUPS,), F32), b_fine[0]])[:, None]
    g_mix, g_ffn, g_fin = norm_mix[0][None, :], norm_ffn[0][None, :], norm_final[None, :]
    mix_consts = (conv_w[0], conv_b[0][None, :], wgu_b, b_gate[0][None, :], norm_conv[0][None, :],
                  norm_gla[0][None, :])

    proj = _inproj(xp, xs, mod, g_mix, w_in_t, w_low_t)
    ycat_p, nconv_p, nst_p = _mixer_p(proj, *mix_consts)
    ycat_s, nconv_s, nst_s = _mixer_s(proj, state_conv[0], state_gla[0].reshape(B_S, HK, DV), *mix_consts)

    x1, pos, rw, counts, x_sorted = _outproj(ycat_p, ycat_s, xp, xs, mod, w_out_b, g_ffn, w_r.astype(BF16), b_r)
    y_sorted = _moe(_plan(counts[:, 0]), x_sorted, w_exp_gate[0], w_exp_up[0], w_exp_down[0])
    y_p, y_s = _final(pos, x1, rw, mod, g_fin, y_sorted)

    new_gla_p = nst_p.reshape(B_P, DV, H, DK).transpose(0, 2, 3, 1)[None]
    new_gla_s = nst_s.reshape(1, B_S, H, DK, DV)
    return (y_p.reshape(B_P, L_P, D), y_s.reshape(B_S, L_S, D), nconv_p[None], new_gla_p,
            nconv_s[None], new_gla_s)
```

```python
import functools

import jax
import jax.numpy as jnp
from jax import lax
from jax.experimental import pallas as pl
from jax.experimental.pallas import tpu as pltpu

F32 = jnp.float32
BF16 = jnp.bfloat16
I32 = jnp.int32

D = 2048
CW = 1024
H = 8
DK = 64
DV = 128
HK = H * DK
HV = H * DV
RANK = 16
TAU = 16.0
N_GROUPS = 4
EPG = 8
NE = N_GROUPS * EPG
DE = 512
EPS = 1e-6

B_P, L_P = 4, 2048
B_S, L_S = 128, 8
T_P = B_P * L_P
T_S = B_S * L_S
T = T_P + T_S

TM = 256
CH = 64
N_TILES = T // TM
N_TILES_P = T_P // TM
TILES_PER_SEQ = L_P // TM
SEQ_PER_TILE = TM // L_S
SEQ_PER_CHUNK = CH // L_S

PROJ_MAIN = 3 * CW + 2 * HK + 2 * HV
PROJ_PAD = PROJ_MAIN + 128
C_ROWS = 136
P_ROW_BLOCK = B_S // 8
R_ROWS = 40

E_TILE = 256
E_TILES_MAX = (2 * T) // E_TILE + NE
E_CAP = T
E_CAP_TILES = E_CAP // E_TILE
E_ROWS = NE * E_CAP
VMEM_LIMIT = 60 * 1024 * 1024


def _cparams(n_axes=1, vmem=VMEM_LIMIT):
    return pltpu.CompilerParams(dimension_semantics=("arbitrary",) * n_axes, vmem_limit_bytes=vmem)


def _rms(x, g):
    return x * lax.rsqrt(jnp.mean(x * x, axis=-1, keepdims=True) + EPS) * g


def _sigmoid(x):
    return 1.0 / (1.0 + jnp.exp(-x))


def _silu(x):
    return x * _sigmoid(x)


def _log_sigmoid(x):
    return jnp.minimum(x, 0.0) - jnp.log(1.0 + jnp.exp(-jnp.abs(x)))


def _iota2(shape, axis):
    return lax.broadcasted_iota(I32, shape, axis)


def _expand_rows(ref, n, reps):
    return jnp.concatenate([jnp.broadcast_to(ref[j:j + 1, :], (reps, ref.shape[-1])) for j in range(n)], axis=0)


def _prompt_row(ref):
    return ref[pl.ds(pl.program_id(0) // TILES_PER_SEQ, 1), :]


def _sample_rows(ref):
    return _expand_rows(ref, SEQ_PER_TILE, L_S)


def _p_tile(i):
    return jnp.minimum(i, N_TILES_P - 1)


def _s_tile(i):
    return jnp.maximum(i - N_TILES_P, 0)


def _mod_specs(col):
    return [pl.BlockSpec((8, D), lambda i, *_: (P_ROW_BLOCK, col)),
            pl.BlockSpec((SEQ_PER_TILE, D), lambda i, *_: (_s_tile(i), col))]


def _const_spec(shape):
    zeros = (0,) * len(shape)
    return pl.BlockSpec(shape, lambda *_: zeros, pipeline_mode=pl.Buffered(1))


def _ada_kernel(c_ref, w_ref, b_ref, o_ref):
    s = _silu(c_ref[...]).astype(BF16)
    o_ref[...] = jnp.dot(s, w_ref[...].astype(BF16), preferred_element_type=F32) + b_ref[...]


def _ada(c_all, w_ada, b_ada):
    tn = 512
    return pl.pallas_call(
        _ada_kernel,
        out_shape=jax.ShapeDtypeStruct((C_ROWS, 6 * D), F32),
        grid=(6 * D // tn,),
        in_specs=[pl.BlockSpec((C_ROWS, D), lambda j: (0, 0)),
                  pl.BlockSpec((D, tn), lambda j: (0, j)),
                  pl.BlockSpec((1, tn), lambda j: (0, j))],
        out_specs=pl.BlockSpec((C_ROWS, tn), lambda j: (0, j)),
        compiler_params=_cparams(),
        name="ada",
    )(c_all, w_ada, b_ada)


W_CHUNK = TM
N_W_CHUNKS = PROJ_MAIN // W_CHUNK
NT_DIMS = (((1,), (1,)), ((), ()))


def _gla_prep(p_ref, wgu_ref, bg_ref, rows, seg):
    q = p_ref[:, 3 * CW:3 * CW + HK]
    k = p_ref[:, 3 * CW + HK:3 * CW + 2 * HK]
    a = p_ref[:, PROJ_MAIN:PROJ_PAD].astype(BF16)
    z = jnp.dot(a, wgu_ref[...], preferred_element_type=F32) + bg_ref[...]
    la = _log_sigmoid(z) * (1.0 / TAU)
    la0 = la.astype(BF16)
    rem = la - la0.astype(F32)
    la1 = rem.astype(BF16)
    la2 = (rem - la1.astype(F32)).astype(BF16)
    r = _iota2((2 * rows, rows), 0)
    c = _iota2((2 * rows, rows), 1)
    same = ((r % rows) // seg) == (c // seg)
    sel = jnp.where(same & ((r >= rows) | (c <= r)), 1.0, 0.0).astype(BF16)
    sums = sum(jnp.dot(sel, part, preferred_element_type=F32) for part in (la0, la1, la2))
    b, bl = sums[0:rows], sums[rows:2 * rows]
    qt = q * jnp.exp(b) * (DK ** -0.5)
    kt = k * jnp.exp(-b)
    kend = kt * jnp.exp(bl)
    return qt, kt, kend, bl


def _head_stack(x):
    rows = x.shape[0]
    t = jnp.concatenate([x] * H, axis=0)
    keep = (_iota2((H * rows, HK), 0) // rows) == (_iota2((H * rows, HK), 1) // DK)
    return jnp.where(keep, t, 0.0)


def _gla_out(o_h, g_h, ngl):
    on = o_h * lax.rsqrt(jnp.mean(o_h * o_h, axis=-1, keepdims=True) + EPS) * ngl
    return on * _silu(g_h)


def _conv_out(bg, u, um1, um2, cw_ref, cb_ref, ncv_ref):
    conv_y = cb_ref[...] + cw_ref[0:1, :] * um2 + cw_ref[1:2, :] * um1 + cw_ref[2:3, :] * u
    return _rms(bg * conv_y, ncv_ref[...])


V_OFF = 3 * CW + 2 * HK
G_OFF = V_OFF + HV


def _mixer_p_tile(j, p_ref, cw_ref, cb_ref, wgu_ref, bg_ref, ncv_ref, ngl_ref,
                  y_ref, nconv_ref, nst_ref, ubuf, st_ref, also):
    @pl.when(j == 0)
    def _():
        ubuf[0:8, :] = jnp.zeros((8, CW), F32)
        st_ref[...] = jnp.zeros((DV, HK), F32)

    bgate = p_ref[:, 0:CW]
    u = p_ref[:, CW:2 * CW] * p_ref[:, 2 * CW:3 * CW]
    ubuf[8:8 + TM, :] = u
    um1 = ubuf[7:7 + TM, :]
    um2 = ubuf[6:6 + TM, :]
    y_ref[:, 0:CW] = _conv_out(bgate, u, um1, um2, cw_ref, cb_ref, ncv_ref).astype(BF16)
    ubuf[6:8, :] = u[TM - 2:TM, :]

    qt, kt, kend, bl = _gla_prep(p_ref, wgu_ref, bg_ref, TM, CH)
    causal = _iota2((2 * CH, CH), 0) % CH >= _iota2((2 * CH, CH), 1)
    first_head = _iota2((CH, 2 * DK), 1) < DK
    ngl = ngl_ref[...]

    def pair_stack(x):
        return jnp.concatenate([jnp.where(first_head, x, 0.0), jnp.where(first_head, 0.0, x)], axis=0).astype(BF16)

    for c in range(TM // CH):
        r0 = c * CH
        for m in range(H // 2):
            lanes = slice(m * 2 * DK, (m + 1) * 2 * DK)
            lhs = pair_stack(qt[r0:r0 + CH, lanes])
            sc = lax.dot_general(lhs, kt[r0:r0 + CH, lanes].astype(BF16), (((1,), (1,)), ((), ())),
                                 preferred_element_type=F32)
            sc = jnp.where(causal, sc, 0.0).astype(BF16)
            st = st_ref[:, lanes]
            o_inter = lax.dot_general(lhs, st.astype(BF16), (((1,), (1,)), ((), ())),
                                      preferred_element_type=F32)
            vs = []
            for hh in range(2):
                h = 2 * m + hh
                v_h = p_ref[r0:r0 + CH, V_OFF + h * DV:V_OFF + (h + 1) * DV].astype(BF16)
                g_h = p_ref[r0:r0 + CH, G_OFF + h * DV:G_OFF + (h + 1) * DV]
                vs.append(v_h)
                o_h = (jnp.dot(sc[hh * CH:(hh + 1) * CH], v_h, preferred_element_type=F32)
                       + o_inter[hh * CH:(hh + 1) * CH])
                y_ref[r0:r0 + CH, CW + h * DV:CW + (h + 1) * DV] = _gla_out(o_h, g_h, ngl).astype(BF16)
            kv_t = lax.dot_general(jnp.concatenate(vs, axis=0), pair_stack(kend[r0:r0 + CH, lanes]),
                                   (((0,), (0,)), ((), ())), preferred_element_type=F32)
            st_ref[:, lanes] = st * jnp.exp(bl[r0:r0 + 1, lanes]) + kv_t

    also()

    @pl.when(j == TILES_PER_SEQ - 1)
    def _():
        nconv_ref[0] = ubuf[6:8, :]
        nst_ref[0] = st_ref[...]


def _inmix_kernel(xp_ref, xs_ref, shp_ref, shs_ref, scp_ref, scs_ref, g_ref, wt_hbm, wa_ref,
                  cw_ref, cb_ref, wgu_ref, bg_ref, ncv_ref, ngl_ref,
                  y_ref, nconv_ref, nst_ref, projs_hbm,
                  h_scr, w_scr, proj, ubuf, st_ref, wsem, psem):
    i = pl.program_id(0)
    par = i % 2

    @pl.when(i == 0)
    def _():
        def chunk(c):
            return pltpu.make_async_copy(wt_hbm.at[0, pl.ds(c * W_CHUNK, W_CHUNK), :],
                                         proj.at[c % 2, :, pl.ds(0, D)], wsem.at[c % 2])

        chunk(0).start()
        for c in range(N_W_CHUNKS):
            if c + 1 < N_W_CHUNKS:
                chunk(c + 1).start()
            chunk(c).wait()
            w_scr[c * W_CHUNK:(c + 1) * W_CHUNK, :] = proj[c % 2, :, 0:D].astype(BF16)

    @pl.when(i < N_TILES_P)
    def _():
        h = _rms(xp_ref[...], g_ref[...]) * (1.0 + _prompt_row(scp_ref)) + _prompt_row(shp_ref)
        h_scr[...] = h.astype(BF16)

    @pl.when(i >= N_TILES_P)
    def _():
        h = _rms(xs_ref[...], g_ref[...]) * (1.0 + _sample_rows(scs_ref)) + _sample_rows(shs_ref)
        h_scr[...] = h.astype(BF16)

    def project():
        h = h_scr[...]
        dst = proj.at[par]
        dst[:, 0:PROJ_MAIN] = lax.dot_general(h, w_scr[...], NT_DIMS, preferred_element_type=F32)
        dst[:, PROJ_MAIN:PROJ_PAD] = lax.dot_general(h, wa_ref[...], NT_DIMS, preferred_element_type=F32)

    has_mixer = (i >= 1) & (i <= N_TILES_P)

    @pl.when(has_mixer)
    def _():
        _mixer_p_tile((i - 1) % TILES_PER_SEQ, proj.at[1 - par], cw_ref, cb_ref, wgu_ref, bg_ref, ncv_ref, ngl_ref,
                      y_ref, nconv_ref, nst_ref, ubuf, st_ref, project)

    @pl.when(jnp.logical_not(has_mixer))
    def _():
        project()

    @pl.when(i >= N_TILES_P)
    def _():
        cp = pltpu.make_async_copy(proj.at[par], projs_hbm.at[pl.ds((i - N_TILES_P) * TM, TM), :], psem)
        cp.start()
        cp.wait()


def _inmix(xp, xs, mod, g_mix, w_in_t, w_low_t, conv_w, conv_b, wgu_b, b_gate, n_conv, n_gla):
    def prev_tile(i):
        return jnp.clip(i - 1, 0, N_TILES_P - 1)

    return pl.pallas_call(
        _inmix_kernel,
        out_shape=(jax.ShapeDtypeStruct((T_P, D), BF16),
                   jax.ShapeDtypeStruct((B_P, 2, CW), F32),
                   jax.ShapeDtypeStruct((B_P, DV, HK), F32),
                   jax.ShapeDtypeStruct((T_S, PROJ_PAD), F32)),
        grid=(N_TILES,),
        in_specs=[pl.BlockSpec((TM, D), lambda i: (_p_tile(i), 0)),
                  pl.BlockSpec((TM, D), lambda i: (_s_tile(i), 0)),
                  *_mod_specs(0), *_mod_specs(1),
                  _const_spec((1, D)), pl.BlockSpec(memory_space=pl.ANY), _const_spec((128, D)),
                  _const_spec((3, CW)), _const_spec((1, CW)), _const_spec((128, HK)), _const_spec((1, HK)),
                  _const_spec((1, CW)), _const_spec((1, DV))],
        out_specs=(pl.BlockSpec((TM, D), lambda i: (prev_tile(i), 0)),
                   pl.BlockSpec((1, 2, CW), lambda i: (prev_tile(i) // TILES_PER_SEQ, 0, 0)),
                   pl.BlockSpec((1, DV, HK), lambda i: (prev_tile(i) // TILES_PER_SEQ, 0, 0)),
                   pl.BlockSpec(memory_space=pl.ANY)),
        scratch_shapes=[pltpu.VMEM((TM, D), BF16), pltpu.VMEM((PROJ_MAIN, D), BF16),
                        pltpu.VMEM((2, TM, PROJ_PAD), F32),
                        pltpu.VMEM((8 + TM, CW), F32), pltpu.VMEM((DV, HK), F32),
                        pltpu.SemaphoreType.DMA((2,)), pltpu.SemaphoreType.DMA(())],
        compiler_params=_cparams(),
        name="inmix",
    )(xp, xs, mod, mod, mod, mod, g_mix, w_in_t, w_low_t, conv_w, conv_b, wgu_b, b_gate, n_conv, n_gla)


def _mixer_s_kernel(p_ref, sconv_ref, sst_ref, cw_ref, cb_ref, wgu_ref, bg_ref, ncv_ref, ngl_ref,
                    y_ref, nconv_ref, nst_ref, ubuf):
    nseq = SEQ_PER_CHUNK
    bgate = p_ref[:, 0:CW]
    u = p_ref[:, CW:2 * CW] * p_ref[:, 2 * CW:3 * CW]
    ubuf[0:8, :] = jnp.zeros((8, CW), F32)
    ubuf[8:8 + CH, :] = u
    tpos = _iota2((CH, CW), 0) % L_S
    s0 = jnp.concatenate([jnp.broadcast_to(sconv_ref[s, 0:1, :], (L_S, CW)) for s in range(nseq)], axis=0)
    s1 = jnp.concatenate([jnp.broadcast_to(sconv_ref[s, 1:2, :], (L_S, CW)) for s in range(nseq)], axis=0)
    um1 = jnp.where(tpos == 0, s1, ubuf[7:7 + CH, :])
    um2 = jnp.where(tpos == 0, s0, jnp.where(tpos == 1, s1, ubuf[6:6 + CH, :]))
    y_ref[:, 0:CW] = _conv_out(bgate, u, um1, um2, cw_ref, cb_ref, ncv_ref).astype(BF16)
    for s in range(nseq):
        nconv_ref[s] = u[s * L_S + L_S - 2:(s + 1) * L_S, :]

    qt, kt, kend, bl = _gla_prep(p_ref, wgu_ref, bg_ref, CH, L_S)
    rr = _iota2((H * CH, CH), 0) % CH
    cc = _iota2((H * CH, CH), 1)
    causal = (rr >= cc) & ((rr // L_S) == (cc // L_S))
    ngl = ngl_ref[...]
    lhs_f = _head_stack(qt)
    sc = lax.dot_general(lhs_f.astype(BF16), kt.astype(BF16), (((1,), (1,)), ((), ())),
                         preferred_element_type=F32)
    sc = jnp.where(causal, sc, 0.0).astype(BF16)
    kstack_f = _head_stack(kend)
    v_all = p_ref[:, V_OFF:V_OFF + HV]
    decay_t = jnp.exp(jnp.concatenate([bl, jnp.zeros((128 - CH, HK), F32)], axis=0).T)
    o_inter = []
    for s in range(nseq):
        rs = [slice(h * CH + s * L_S, h * CH + (s + 1) * L_S) for h in range(H)]
        lhs_s = jnp.concatenate([lhs_f[r] for r in rs], axis=0).astype(BF16)
        k_s = jnp.concatenate([kstack_f[r] for r in rs], axis=0).astype(BF16)
        v_s = jnp.concatenate([v_all[s * L_S:(s + 1) * L_S, h * DV:(h + 1) * DV] for h in range(H)],
                              axis=0).astype(BF16)
        st = sst_ref[s]
        o_inter.append(jnp.dot(lhs_s, st.astype(BF16), preferred_element_type=F32))
        kv = lax.dot_general(k_s, v_s, (((0,), (0,)), ((), ())), preferred_element_type=F32)
        nst_ref[s] = st * decay_t[:, s * L_S:s * L_S + 1] + kv
    for h in range(H):
        v_h = v_all[:, h * DV:(h + 1) * DV].astype(BF16)
        g_h = p_ref[:, G_OFF + h * DV:G_OFF + (h + 1) * DV]
        oi_h = jnp.concatenate([o_inter[s][h * L_S:(h + 1) * L_S] for s in range(nseq)], axis=0)
        o_h = jnp.dot(sc[h * CH:(h + 1) * CH], v_h, preferred_element_type=F32) + oi_h
        y_ref[:, CW + h * DV:CW + (h + 1) * DV] = _gla_out(o_h, g_h, ngl).astype(BF16)


def _mixer_s(proj, sconv, sst, conv_w, conv_b, wgu_b, b_gate, n_conv, n_gla):
    nseq = SEQ_PER_CHUNK
    return pl.pallas_call(
        _mixer_s_kernel,
        out_shape=(jax.ShapeDtypeStruct((T_S, D), BF16),
                   jax.ShapeDtypeStruct((B_S, 2, CW), F32),
                   jax.ShapeDtypeStruct((B_S, HK, DV), F32)),
        grid=(T_S // CH,),
        in_specs=[pl.BlockSpec((CH, PROJ_PAD), lambda i: (i, 0)),
                  pl.BlockSpec((nseq, 2, CW), lambda i: (i, 0, 0)),
                  pl.BlockSpec((nseq, HK, DV), lambda i: (i, 0, 0)),
                  _const_spec((3, CW)), _const_spec((1, CW)), _const_spec((128, HK)), _const_spec((1, HK)),
                  _const_spec((1, CW)), _const_spec((1, DV))],
        out_specs=(pl.BlockSpec((CH, D), lambda i: (i, 0)),
                   pl.BlockSpec((nseq, 2, CW), lambda i: (i, 0, 0)),
                   pl.BlockSpec((nseq, HK, DV), lambda i: (i, 0, 0))),
        scratch_shapes=[pltpu.VMEM((8 + CH, CW), F32)],
        compiler_params=_cparams(),
        name="mixer_s",
    )(proj, sconv, sst, conv_w, conv_b, wgu_b, b_gate, n_conv, n_gla)


def _route(h2, wr_ref, br_ref, run_ref, pos_ref, rw_ref, cnt_ref):
    lt = lax.dot_general(wr_ref[...], h2.astype(BF16), (((1,), (1,)), ((), ())),
                         preferred_element_type=F32) + br_ref[...]

    coarse = lt[0:N_GROUPS]
    cmax = jnp.max(coarse, axis=0, keepdims=True)
    gi = _iota2((N_GROUPS, TM), 0)
    grp = jnp.min(jnp.where(coarse == cmax, gi, N_GROUPS), axis=0, keepdims=True)
    p_sel = 1.0 / jnp.sum(jnp.exp(coarse - cmax), axis=0, keepdims=True)
    fine = jnp.zeros((EPG, TM), F32)
    for g in range(N_GROUPS):
        fine = jnp.where(grp == g, lt[8 + g * EPG:8 + (g + 1) * EPG], fine)
    ei = _iota2((EPG, TM), 0)
    f1 = jnp.max(fine, axis=0, keepdims=True)
    i1 = jnp.min(jnp.where(fine == f1, ei, EPG), axis=0, keepdims=True)
    rest = jnp.where(ei == i1, -jnp.inf, fine)
    f2 = jnp.max(rest, axis=0, keepdims=True)
    i2 = jnp.min(jnp.where(rest == f2, ei, EPG), axis=0, keepdims=True)
    e2 = jnp.exp(f2 - f1)
    w1 = p_sel / (1.0 + e2)
    w2 = p_sel * e2 / (1.0 + e2)
    x1 = grp * EPG + i1
    x2 = grp * EPG + i2

    er = _iota2((NE, TM), 0)
    oh1 = jnp.where(er == x1, 1.0, 0.0)
    oh2 = jnp.where(er == x2, 1.0, 0.0)
    oh = oh1 + oh2
    before = jnp.where(_iota2((TM, TM), 0) < _iota2((TM, TM), 1), 1.0, 0.0).astype(BF16)
    cum = jnp.dot(oh.astype(BF16), before, preferred_element_type=F32) + run_ref[:, 0:1]
    r1 = jnp.sum(oh1 * cum, axis=0, keepdims=True).astype(I32)
    r2 = jnp.sum(oh2 * cum, axis=0, keepdims=True).astype(I32)
    run_new = run_ref[...] + jnp.sum(oh, axis=1, keepdims=True)
    run_ref[...] = run_new
    cnt_ref[...] = run_new.astype(I32)

    row = _iota2((8, TM), 0)
    pos_ref[...] = jnp.where(row == 0, x1 * E_CAP + r1, jnp.where(row == 1, x2 * E_CAP + r2, 0))
    wrow = _iota2((128, TM), 0)
    rw_ref[...] = jnp.where(wrow == 0, w1, jnp.where(wrow == 1, w2, 0.0)).T


def _outproj_kernel(ycp_ref, ycs_ref, xp_ref, xs_ref, gtp_ref, gts_ref, shp_ref, shs_ref, scp_ref, scs_ref,
                    wo_ref, g_ref, wr_ref, br_ref,
                    x1_ref, pos_ref, rw_ref, cnt_ref, xs_out,
                    run_ref, h2_scr, pos_v, pos_s, sem, psem):
    i = pl.program_id(0)
    par = i % 2

    def positions_to_smem():
        cp = pltpu.make_async_copy(pos_v, pos_s, psem)
        cp.start()
        cp.wait()

    def scatter(p):
        for r in range(TM):
            for slot in range(2):
                pltpu.make_async_copy(h2_scr.at[p, pl.ds(r, 1), :], xs_out.at[pl.ds(pos_s[slot, r], 1), :],
                                      sem).start(priority=r % 2)

    def drain():
        for _ in range(2):
            pltpu.make_async_copy(h2_scr.at[0], xs_out.at[pl.ds(0, TM), :], sem).wait()

    @pl.when(i == 0)
    def _():
        run_ref[...] = jnp.zeros((NE, 128), F32)
        h2_scr[1] = jnp.zeros((TM, D), F32)
        pos_v[...] = E_ROWS + _iota2((8, TM), 0) * TM + _iota2((8, TM), 1)
        positions_to_smem()

    def tile(yc, x, gt, sh, sc):
        scatter(1 - par)
        mix = jnp.dot(yc, wo_ref[...], preferred_element_type=F32)
        x1 = x + gt * mix
        x1_ref[...] = x1
        h2 = _rms(x1, g_ref[...]) * (1.0 + sc) + sh
        h2_scr[par] = h2
        _route(h2, wr_ref, br_ref, run_ref, pos_v, rw_ref, cnt_ref)

    @pl.when(i < N_TILES_P)
    def _():
        tile(ycp_ref[...], xp_ref[...], _prompt_row(gtp_ref), _prompt_row(shp_ref), _prompt_row(scp_ref))

    @pl.when(i >= N_TILES_P)
    def _():
        tile(ycs_ref[...], xs_ref[...], _sample_rows(gts_ref), _sample_rows(shs_ref), _sample_rows(scs_ref))

    pos_ref[0] = pos_v[...]
    drain()
    positions_to_smem()

    @pl.when(i == N_TILES - 1)
    def _():
        scatter(par)
        drain()


def _outproj(ycp, ycs, xp, xs, mod, w_out_b, g_ffn, w_r, b_r):
    return pl.pallas_call(
        _outproj_kernel,
        out_shape=(jax.ShapeDtypeStruct((T, D), F32), jax.ShapeDtypeStruct((N_TILES, 8, TM), I32),
                   jax.ShapeDtypeStruct((T, 128), F32), jax.ShapeDtypeStruct((NE, 128), I32),
                   jax.ShapeDtypeStruct((E_ROWS + 2 * TM, D), F32)),
        grid=(N_TILES,),
        in_specs=[pl.BlockSpec((TM, D), lambda i: (_p_tile(i), 0)), pl.BlockSpec((TM, D), lambda i: (_s_tile(i), 0)),
                  pl.BlockSpec((TM, D), lambda i: (_p_tile(i), 0)), pl.BlockSpec((TM, D), lambda i: (_s_tile(i), 0)),
                  *_mod_specs(2), *_mod_specs(3), *_mod_specs(4),
                  _const_spec((D, D)), _const_spec((1, D)), _const_spec((R_ROWS, D)), _const_spec((R_ROWS, 1))],
        out_specs=(pl.BlockSpec((TM, D), lambda i: (i, 0)), pl.BlockSpec((1, 8, TM), lambda i: (i, 0, 0)),
                   pl.BlockSpec((TM, 128), lambda i: (i, 0)), pl.BlockSpec((NE, 128), lambda i: (0, 0)),
                   pl.BlockSpec(memory_space=pl.ANY)),
        scratch_shapes=[pltpu.VMEM((NE, 128), F32), pltpu.VMEM((2, TM, D), F32), pltpu.VMEM((8, TM), I32),
                        pltpu.SMEM((8, TM), I32), pltpu.SemaphoreType.DMA(()), pltpu.SemaphoreType.DMA(())],
        compiler_params=_cparams(),
        name="outproj",
    )(ycp, ycs, xp, xs, mod, mod, mod, mod, mod, mod, w_out_b, g_ffn, w_r, b_r)


def _moe_kernel(tb_ref, te_ref, first_ref, nval_ref, nxt_ref, slot_ref, nt_ref,
                x_ref, wg_hbm, wu_hbm, wd_hbm, y_ref, wg_st, wu_st, wd_st, wgu_b, wd_b, wsem):
    i = pl.program_id(0)

    def weight_copies(e, s):
        return (pltpu.make_async_copy(wg_hbm.at[e], wg_st.at[s], wsem.at[s]),
                pltpu.make_async_copy(wu_hbm.at[e], wu_st.at[s], wsem.at[s]),
                pltpu.make_async_copy(wd_hbm.at[e], wd_st.at[s], wsem.at[s]))

    @pl.when(i < nt_ref[0])
    def _():
        @pl.when(first_ref[i] == 1)
        def _():
            s = slot_ref[i]

            @pl.when(i == 0)
            def _():
                for cp in weight_copies(te_ref[0], 0):
                    cp.start()

            for cp in weight_copies(te_ref[i], s):
                cp.wait()

            @pl.when(nxt_ref[i] >= 0)
            def _():
                for cp in weight_copies(nxt_ref[i], 1 - s):
                    cp.start()

            wgu_b[:, 0:DE] = wg_st[s].astype(BF16)
            wgu_b[:, DE:2 * DE] = wu_st[s].astype(BF16)
            wd_b[...] = wd_st[s].astype(BF16)

        valid = _iota2((E_TILE, D), 0) < nval_ref[i]
        x = jnp.where(valid, x_ref[...], 0.0).astype(BF16)
        ab = jnp.dot(x, wgu_b[...], preferred_element_type=F32)
        hid = (_silu(ab[:, 0:DE]) * ab[:, DE:2 * DE]).astype(BF16)
        y_ref[...] = jnp.dot(hid, wd_b[...], preferred_element_type=F32)


def _moe(plan, xs, w_eg, w_eu, w_ed):
    def tile_map(i, tb, *_):
        return (tb[i], 0)

    grid_spec = pltpu.PrefetchScalarGridSpec(
        num_scalar_prefetch=len(plan),
        grid=(E_TILES_MAX,),
        in_specs=[pl.BlockSpec((E_TILE, D), tile_map),
                  pl.BlockSpec(memory_space=pl.ANY), pl.BlockSpec(memory_space=pl.ANY),
                  pl.BlockSpec(memory_space=pl.ANY)],
        out_specs=pl.BlockSpec((E_TILE, D), tile_map),
        scratch_shapes=[pltpu.VMEM((2, D, DE), F32), pltpu.VMEM((2, D, DE), F32), pltpu.VMEM((2, DE, D), F32),
                        pltpu.VMEM((D, 2 * DE), BF16), pltpu.VMEM((DE, D), BF16), pltpu.SemaphoreType.DMA((2,))],
    )
    return pl.pallas_call(
        _moe_kernel,
        out_shape=jax.ShapeDtypeStruct((E_ROWS, D), F32),
        grid_spec=grid_spec,
        compiler_params=_cparams(),
        name="moe",
    )(*plan, xs, w_eg, w_eu, w_ed)


def _plan(counts):
    ntile = (counts + E_TILE - 1) // E_TILE
    tend = jnp.cumsum(ntile)
    tbeg = tend - ntile
    n_tiles = tend[-1]
    tid = jnp.minimum(jnp.arange(E_TILES_MAX, dtype=I32), n_tiles - 1)
    tile_expert = jnp.minimum(jnp.sum((tend[None, :] <= tid[:, None]).astype(I32), axis=1), NE - 1)
    tile_in_expert = tid - tbeg[tile_expert]
    tile_block = tile_expert * E_CAP_TILES + tile_in_expert
    tile_first = (tile_in_expert == 0).astype(I32)
    tile_nvalid = jnp.minimum(counts[tile_expert] - tile_in_expert * E_TILE, E_TILE)
    ids = jnp.arange(NE, dtype=I32)
    occupied = ntile > 0
    later = occupied[None, :] & (ids[None, :] > ids[:, None])
    next_expert = jnp.where(jnp.any(later, axis=1), jnp.min(jnp.where(later, ids[None, :], NE), axis=1), -1)
    rank = jnp.cumsum(occupied.astype(I32)) - 1
    return (tile_block, tile_expert, tile_first, tile_nvalid, next_expert[tile_expert].astype(I32),
            (rank[tile_expert] % 2).astype(I32), n_tiles.reshape(1))


def _final_kernel(pos0_ref, posn_ref, x1_ref, rw_ref, gtp_ref, gts_ref, g_ref, ys_ref, yp_ref, ysm_ref, buf, sem):
    i = pl.program_id(0)
    par = i % 2

    def gather(p_ref, p):
        for r in range(TM):
            for slot in range(2):
                pltpu.make_async_copy(ys_ref.at[pl.ds(p_ref[0, slot, r], 1), :], buf.at[p, slot, pl.ds(r, 1), :],
                                      sem.at[p]).start(priority=r % 2)

    @pl.when(i == 0)
    def _():
        gather(pos0_ref, 0)

    @pl.when(i < N_TILES - 1)
    def _():
        gather(posn_ref, 1 - par)

    for slot in range(2):
        pltpu.make_async_copy(ys_ref.at[pl.ds(0, TM), :], buf.at[par, slot], sem.at[par]).wait()

    moe = rw_ref[:, 0:1] * buf[par, 0] + rw_ref[:, 1:2] * buf[par, 1]

    @pl.when(i < N_TILES_P)
    def _():
        yp_ref[...] = _rms(x1_ref[...] + _prompt_row(gtp_ref) * moe, g_ref[...])

    @pl.when(i >= N_TILES_P)
    def _():
        ysm_ref[...] = _rms(x1_ref[...] + _sample_rows(gts_ref) * moe, g_ref[...])


def _final(pos, x1, rw, mod, g_fin, ys):
    return pl.pallas_call(
        _final_kernel,
        out_shape=(jax.ShapeDtypeStruct((T_P, D), F32), jax.ShapeDtypeStruct((T_S, D), F32)),
        grid=(N_TILES,),
        in_specs=[pl.BlockSpec((1, 8, TM), lambda i: (0, 0, 0), memory_space=pltpu.SMEM),
                  pl.BlockSpec((1, 8, TM), lambda i: (jnp.minimum(i + 1, N_TILES - 1), 0, 0),
                               memory_space=pltpu.SMEM),
                  pl.BlockSpec((TM, D), lambda i: (i, 0)),
                  pl.BlockSpec((TM, 128), lambda i: (i, 0)),
                  *_mod_specs(5),
                  _const_spec((1, D)),
                  pl.BlockSpec(memory_space=pl.ANY)],
        out_specs=(pl.BlockSpec((TM, D), lambda i: (_p_tile(i), 0)),
                   pl.BlockSpec((TM, D), lambda i: (_s_tile(i), 0))),
        scratch_shapes=[pltpu.VMEM((2, 2, TM, D), F32), pltpu.SemaphoreType.DMA((2,))],
        compiler_params=_cparams(),
        name="final",
    )(pos, pos, x1, rw, mod, mod, g_fin, ys)


def kernel(x_prompt, x_sample, c_prompt, c_sample, state_conv, state_gla, w_ada, b_ada, norm_mix, w_in, conv_w,
           conv_b, w_gate_up, b_gate, norm_conv, norm_gla, w_out, norm_ffn, w_coarse, b_coarse, w_fine, b_fine,
           w_exp_gate, w_exp_up, w_exp_down, norm_final):
    xp = x_prompt.reshape(T_P, D)
    xs = x_sample.reshape(T_S, D)
    c_all = jnp.concatenate([c_sample, c_prompt, jnp.zeros((C_ROWS - B_S - B_P, D), F32)], axis=0)
    mod = _ada(c_all, w_ada[0], b_ada[0][None, :])

    w_in_t = jnp.swapaxes(w_in, 1, 2)
    w_low_t = jnp.pad(w_in_t[0, PROJ_MAIN:, :], ((0, 128 - RANK), (0, 0))).astype(BF16)
    w_out_b = w_out[0].astype(BF16)
    wgu_b = jnp.pad(w_gate_up[0], ((0, 128 - RANK), (0, 0))).astype(BF16)
    w_r = jnp.concatenate([w_coarse[0].T, jnp.zeros((8 - N_GROUPS, D), F32), w_fine[0].T], axis=0)
    b_r = jnp.concatenate([b_coarse[0], jnp.zeros((8 - N_GROUPS,), F32), b_fine[0]])[:, None]
    g_mix, g_ffn, g_fin = norm_mix[0][None, :], norm_ffn[0][None, :], norm_final[None, :]
    mix_consts = (conv_w[0], conv_b[0][None, :], wgu_b, b_gate[0][None, :], norm_conv[0][None, :],
                  norm_gla[0][None, :])

    ycat_p, nconv_p, nst_p, proj_s = _inmix(xp, xs, mod, g_mix, w_in_t, w_low_t, *mix_consts)
    ycat_s, nconv_s, nst_s = _mixer_s(proj_s, state_conv[0], state_gla[0].reshape(B_S, HK, DV), *mix_consts)

    x1, pos, rw, counts, x_sorted = _outproj(ycat_p, ycat_s, xp, xs, mod, w_out_b, g_ffn, w_r.astype(BF16), b_r)
    y_sorted = _moe(_plan(counts[:, 0]), x_sorted, w_exp_gate[0], w_exp_up[0], w_exp_down[0])
    y_p, y_s = _final(pos, x1, rw, mod, g_fin, y_sorted)

    new_gla_p = nst_p.reshape(B_P, DV, H, DK).transpose(0, 2, 3, 1)[None]
    new_gla_s = nst_s.reshape(1, B_S, H, DK, DV)
    return (y_p.reshape(B_P, L_P, D), y_s.reshape(B_S, L_S, D), nconv_p[None], new_gla_p,
            nconv_s[None], new_gla_s)
```

```python
import functools

import jax
import jax.numpy as jnp
from jax import lax
from jax.experimental import pallas as pl
from jax.experimental.pallas import tpu as pltpu

F32 = jnp.float32
BF16 = jnp.bfloat16
I32 = jnp.int32

D = 2048
CW = 1024
H = 8
DK = 64
DV = 128
HK = H * DK
HV = H * DV
RANK = 16
TAU = 16.0
N_GROUPS = 4
EPG = 8
NE = N_GROUPS * EPG
DE = 512
EPS = 1e-6

B_P, L_P = 4, 2048
B_S, L_S = 128, 8
T_P = B_P * L_P
T_S = B_S * L_S
T = T_P + T_S

TM = 256
CH = 64
N_TILES = T // TM
N_TILES_P = T_P // TM
TILES_PER_SEQ = L_P // TM
SEQ_PER_TILE = TM // L_S
SEQ_PER_CHUNK = CH // L_S

PROJ_MAIN = 3 * CW + 2 * HK + 2 * HV
PROJ_PAD = PROJ_MAIN + 128
C_ROWS = 136
P_ROW_BLOCK = B_S // 8
R_ROWS = 40

E_TILE = 256
E_TILES_MAX = (2 * T) // E_TILE + NE
E_CAP = T
E_CAP_TILES = E_CAP // E_TILE
E_ROWS = NE * E_CAP
VMEM_LIMIT = 60 * 1024 * 1024


def _cparams(n_axes=1, vmem=VMEM_LIMIT):
    return pltpu.CompilerParams(dimension_semantics=("arbitrary",) * n_axes, vmem_limit_bytes=vmem)


def _rms(x, g):
    return x * lax.rsqrt(jnp.mean(x * x, axis=-1, keepdims=True) + EPS) * g


def _sigmoid(x):
    return 1.0 / (1.0 + jnp.exp(-x))


def _silu(x):
    return x * _sigmoid(x)


def _log_sigmoid(x):
    return jnp.minimum(x, 0.0) - jnp.log(1.0 + jnp.exp(-jnp.abs(x)))


def _iota2(shape, axis):
    return lax.broadcasted_iota(I32, shape, axis)


def _expand_rows(ref, n, reps):
    return jnp.concatenate([jnp.broadcast_to(ref[j:j + 1, :], (reps, ref.shape[-1])) for j in range(n)], axis=0)


def _prompt_row(ref):
    return ref[pl.ds(pl.program_id(0) // TILES_PER_SEQ, 1), :]


def _sample_rows(ref):
    return _expand_rows(ref, SEQ_PER_TILE, L_S)


def _p_tile(i):
    return jnp.minimum(i, N_TILES_P - 1)


def _s_tile(i):
    return jnp.maximum(i - N_TILES_P, 0)


def _mod_specs(col):
    return [pl.BlockSpec((8, D), lambda i, *_: (P_ROW_BLOCK, col)),
            pl.BlockSpec((SEQ_PER_TILE, D), lambda i, *_: (_s_tile(i), col))]


def _const_spec(shape):
    zeros = (0,) * len(shape)
    return pl.BlockSpec(shape, lambda *_: zeros, pipeline_mode=pl.Buffered(1))


def _ada_kernel(c_ref, w_ref, b_ref, o_ref):
    s = _silu(c_ref[...]).astype(BF16)
    o_ref[...] = jnp.dot(s, w_ref[...].astype(BF16), preferred_element_type=F32) + b_ref[...]


def _ada(c_all, w_ada, b_ada):
    tn = 1024
    return pl.pallas_call(
        _ada_kernel,
        out_shape=jax.ShapeDtypeStruct((C_ROWS, 6 * D), F32),
        grid=(6 * D // tn,),
        in_specs=[pl.BlockSpec((C_ROWS, D), lambda j: (0, 0)),
                  pl.BlockSpec((D, tn), lambda j: (0, j)),
                  pl.BlockSpec((1, tn), lambda j: (0, j))],
        out_specs=pl.BlockSpec((C_ROWS, tn), lambda j: (0, j)),
        compiler_params=_cparams(),
        name="ada",
    )(c_all, w_ada, b_ada)


W_CHUNK = TM
N_W_CHUNKS = PROJ_MAIN // W_CHUNK
NT_DIMS = (((1,), (1,)), ((), ()))


def _gla_prep(p_ref, wgu_ref, bg_ref, rows, seg):
    q = p_ref[:, 3 * CW:3 * CW + HK]
    k = p_ref[:, 3 * CW + HK:3 * CW + 2 * HK]
    a = p_ref[:, PROJ_MAIN:PROJ_PAD].astype(BF16)
    z = jnp.dot(a, wgu_ref[...], preferred_element_type=F32) + bg_ref[...]
    la = _log_sigmoid(z) * (1.0 / TAU)
    la0 = la.astype(BF16)
    rem = la - la0.astype(F32)
    la1 = rem.astype(BF16)
    la2 = (rem - la1.astype(F32)).astype(BF16)
    r = _iota2((2 * rows, rows), 0)
    c = _iota2((2 * rows, rows), 1)
    same = ((r % rows) // seg) == (c // seg)
    sel = jnp.where(same & ((r >= rows) | (c <= r)), 1.0, 0.0).astype(BF16)
    sums = sum(jnp.dot(sel, part, preferred_element_type=F32) for part in (la0, la1, la2))
    b, bl = sums[0:rows], sums[rows:2 * rows]
    qt = q * jnp.exp(b) * (DK ** -0.5)
    kt = k * jnp.exp(-b)
    kend = kt * jnp.exp(bl)
    return qt, kt, kend, bl


def _head_stack(x):
    rows = x.shape[0]
    t = jnp.concatenate([x] * H, axis=0)
    keep = (_iota2((H * rows, HK), 0) // rows) == (_iota2((H * rows, HK), 1) // DK)
    return jnp.where(keep, t, 0.0)


def _gla_out(o_h, g_h, ngl):
    on = o_h * lax.rsqrt(jnp.mean(o_h * o_h, axis=-1, keepdims=True) + EPS) * ngl
    return on * _silu(g_h)


def _conv_out(bg, u, um1, um2, cw_ref, cb_ref, ncv_ref):
    conv_y = cb_ref[...] + cw_ref[0:1, :] * um2 + cw_ref[1:2, :] * um1 + cw_ref[2:3, :] * u
    return _rms(bg * conv_y, ncv_ref[...])


V_OFF = 3 * CW + 2 * HK
G_OFF = V_OFF + HV


def _mixer_p_tile(j, p_ref, cw_ref, cb_ref, wgu_ref, bg_ref, ncv_ref, ngl_ref,
                  y_ref, nconv_ref, nst_ref, ubuf, st_ref, also):
    @pl.when(j == 0)
    def _():
        ubuf[0:8, :] = jnp.zeros((8, CW), F32)
        st_ref[...] = jnp.zeros((DV, HK), F32)

    bgate = p_ref[:, 0:CW]
    u = p_ref[:, CW:2 * CW] * p_ref[:, 2 * CW:3 * CW]
    ubuf[8:8 + TM, :] = u
    um1 = ubuf[7:7 + TM, :]
    um2 = ubuf[6:6 + TM, :]
    y_ref[:, 0:CW] = _conv_out(bgate, u, um1, um2, cw_ref, cb_ref, ncv_ref).astype(BF16)
    ubuf[6:8, :] = u[TM - 2:TM, :]

    qt, kt, kend, bl = _gla_prep(p_ref, wgu_ref, bg_ref, TM, CH)
    causal = _iota2((2 * CH, CH), 0) % CH >= _iota2((2 * CH, CH), 1)
    first_head = _iota2((CH, 2 * DK), 1) < DK
    ngl = ngl_ref[...]

    def pair_stack(x):
        return jnp.concatenate([jnp.where(first_head, x, 0.0), jnp.where(first_head, 0.0, x)], axis=0).astype(BF16)

    for c in range(TM // CH):
        r0 = c * CH
        for m in range(H // 2):
            lanes = slice(m * 2 * DK, (m + 1) * 2 * DK)
            lhs = pair_stack(qt[r0:r0 + CH, lanes])
            sc = lax.dot_general(lhs, kt[r0:r0 + CH, lanes].astype(BF16), (((1,), (1,)), ((), ())),
                                 preferred_element_type=F32)
            sc = jnp.where(causal, sc, 0.0).astype(BF16)
            st = st_ref[:, lanes]
            o_inter = lax.dot_general(lhs, st.astype(BF16), (((1,), (1,)), ((), ())),
                                      preferred_element_type=F32)
            vs = []
            for hh in range(2):
                h = 2 * m + hh
                v_h = p_ref[r0:r0 + CH, V_OFF + h * DV:V_OFF + (h + 1) * DV].astype(BF16)
                g_h = p_ref[r0:r0 + CH, G_OFF + h * DV:G_OFF + (h + 1) * DV]
                vs.append(v_h)
                o_h = (jnp.dot(sc[hh * CH:(hh + 1) * CH], v_h, preferred_element_type=F32)
                       + o_inter[hh * CH:(hh + 1) * CH])
                y_ref[r0:r0 + CH, CW + h * DV:CW + (h + 1) * DV] = _gla_out(o_h, g_h, ngl).astype(BF16)
            kv_t = lax.dot_general(jnp.concatenate(vs, axis=0), pair_stack(kend[r0:r0 + CH, lanes]),
                                   (((0,), (0,)), ((), ())), preferred_element_type=F32)
            st_ref[:, lanes] = st * jnp.exp(bl[r0:r0 + 1, lanes]) + kv_t

    also()

    @pl.when(j == TILES_PER_SEQ - 1)
    def _():
        nconv_ref[0] = ubuf[6:8, :]
        nst_ref[0] = st_ref[...]


def _inmix_kernel(xp_ref, xs_ref, shp_ref, shs_ref, scp_ref, scs_ref, g_ref, wt_hbm, wa_ref,
                  cw_ref, cb_ref, wgu_ref, bg_ref, ncv_ref, ngl_ref,
                  y_ref, nconv_ref, nst_ref, projs_hbm,
                  h_scr, w_scr, proj, ubuf, st_ref, wsem, psem):
    i = pl.program_id(0)
    par = i % 2

    @pl.when(i == 0)
    def _():
        def chunk(c):
            return pltpu.make_async_copy(wt_hbm.at[0, pl.ds(c * W_CHUNK, W_CHUNK), :],
                                         proj.at[c % 2, :, pl.ds(0, D)], wsem.at[c % 2])

        chunk(0).start()
        for c in range(N_W_CHUNKS):
            if c + 1 < N_W_CHUNKS:
                chunk(c + 1).start()
            chunk(c).wait()
            w_scr[c * W_CHUNK:(c + 1) * W_CHUNK, :] = proj[c % 2, :, 0:D].astype(BF16)

    @pl.when(i < N_TILES_P)
    def _():
        h = _rms(xp_ref[...], g_ref[...]) * (1.0 + _prompt_row(scp_ref)) + _prompt_row(shp_ref)
        h_scr[...] = h.astype(BF16)

    @pl.when(i >= N_TILES_P)
    def _():
        h = _rms(xs_ref[...], g_ref[...]) * (1.0 + _sample_rows(scs_ref)) + _sample_rows(shs_ref)
        h_scr[...] = h.astype(BF16)

    def project():
        h = h_scr[...]
        dst = proj.at[par]
        dst[:, 0:PROJ_MAIN] = lax.dot_general(h, w_scr[...], NT_DIMS, preferred_element_type=F32)
        dst[:, PROJ_MAIN:PROJ_PAD] = lax.dot_general(h, wa_ref[...], NT_DIMS, preferred_element_type=F32)

    has_mixer = (i >= 1) & (i <= N_TILES_P)

    @pl.when(has_mixer)
    def _():
        _mixer_p_tile((i - 1) % TILES_PER_SEQ, proj.at[1 - par], cw_ref, cb_ref, wgu_ref, bg_ref, ncv_ref, ngl_ref,
                      y_ref, nconv_ref, nst_ref, ubuf, st_ref, project)

    @pl.when(jnp.logical_not(has_mixer))
    def _():
        project()

    @pl.when(i >= N_TILES_P)
    def _():
        cp = pltpu.make_async_copy(proj.at[par], projs_hbm.at[pl.ds((i - N_TILES_P) * TM, TM), :], psem)
        cp.start()
        cp.wait()


def _inmix(xp, xs, mod, g_mix, w_in_t, w_low_t, conv_w, conv_b, wgu_b, b_gate, n_conv, n_gla):
    def prev_tile(i):
        return jnp.clip(i - 1, 0, N_TILES_P - 1)

    return pl.pallas_call(
        _inmix_kernel,
        out_shape=(jax.ShapeDtypeStruct((T_P, D), BF16),
                   jax.ShapeDtypeStruct((B_P, 2, CW), F32),
                   jax.ShapeDtypeStruct((B_P, DV, HK), F32),
                   jax.ShapeDtypeStruct((T_S, PROJ_PAD), F32)),
        grid=(N_TILES,),
        in_specs=[pl.BlockSpec((TM, D), lambda i: (_p_tile(i), 0)),
                  pl.BlockSpec((TM, D), lambda i: (_s_tile(i), 0)),
                  *_mod_specs(0), *_mod_specs(1),
                  _const_spec((1, D)), pl.BlockSpec(memory_space=pl.ANY), _const_spec((128, D)),
                  _const_spec((3, CW)), _const_spec((1, CW)), _const_spec((128, HK)), _const_spec((1, HK)),
                  _const_spec((1, CW)), _const_spec((1, DV))],
        out_specs=(pl.BlockSpec((TM, D), lambda i: (prev_tile(i), 0)),
                   pl.BlockSpec((1, 2, CW), lambda i: (prev_tile(i) // TILES_PER_SEQ, 0, 0)),
                   pl.BlockSpec((1, DV, HK), lambda i: (prev_tile(i) // TILES_PER_SEQ, 0, 0)),
                   pl.BlockSpec(memory_space=pl.ANY)),
        scratch_shapes=[pltpu.VMEM((TM, D), BF16), pltpu.VMEM((PROJ_MAIN, D), BF16),
                        pltpu.VMEM((2, TM, PROJ_PAD), F32),
                        pltpu.VMEM((8 + TM, CW), F32), pltpu.VMEM((DV, HK), F32),
                        pltpu.SemaphoreType.DMA((2,)), pltpu.SemaphoreType.DMA(())],
        compiler_params=_cparams(),
        name="inmix",
    )(xp, xs, mod, mod, mod, mod, g_mix, w_in_t, w_low_t, conv_w, conv_b, wgu_b, b_gate, n_conv, n_gla)


def _mixer_s_kernel(p_ref, sconv_ref, sst_ref, cw_ref, cb_ref, wgu_ref, bg_ref, ncv_ref, ngl_ref,
                    y_ref, nconv_ref, nst_ref, ubuf):
    nseq = SEQ_PER_CHUNK
    bgate = p_ref[:, 0:CW]
    u = p_ref[:, CW:2 * CW] * p_ref[:, 2 * CW:3 * CW]
    ubuf[0:8, :] = jnp.zeros((8, CW), F32)
    ubuf[8:8 + CH, :] = u
    tpos = _iota2((CH, CW), 0) % L_S
    s0 = jnp.concatenate([jnp.broadcast_to(sconv_ref[s, 0:1, :], (L_S, CW)) for s in range(nseq)], axis=0)
    s1 = jnp.concatenate([jnp.broadcast_to(sconv_ref[s, 1:2, :], (L_S, CW)) for s in range(nseq)], axis=0)
    um1 = jnp.where(tpos == 0, s1, ubuf[7:7 + CH, :])
    um2 = jnp.where(tpos == 0, s0, jnp.where(tpos == 1, s1, ubuf[6:6 + CH, :]))
    y_ref[:, 0:CW] = _conv_out(bgate, u, um1, um2, cw_ref, cb_ref, ncv_ref).astype(BF16)
    for s in range(nseq):
        nconv_ref[s] = u[s * L_S + L_S - 2:(s + 1) * L_S, :]

    qt, kt, kend, bl = _gla_prep(p_ref, wgu_ref, bg_ref, CH, L_S)
    rr = _iota2((H * CH, CH), 0) % CH
    cc = _iota2((H * CH, CH), 1)
    causal = (rr >= cc) & ((rr // L_S) == (cc // L_S))
    ngl = ngl_ref[...]
    lhs_f = _head_stack(qt)
    sc = lax.dot_general(lhs_f.astype(BF16), kt.astype(BF16), (((1,), (1,)), ((), ())),
                         preferred_element_type=F32)
    sc = jnp.where(causal, sc, 0.0).astype(BF16)
    kstack_f = _head_stack(kend)
    v_all = p_ref[:, V_OFF:V_OFF + HV]
    decay_t = jnp.exp(jnp.concatenate([bl, jnp.zeros((128 - CH, HK), F32)], axis=0).T)
    o_inter = []
    for s in range(nseq):
        rs = [slice(h * CH + s * L_S, h * CH + (s + 1) * L_S) for h in range(H)]
        lhs_s = jnp.concatenate([lhs_f[r] for r in rs], axis=0).astype(BF16)
        k_s = jnp.concatenate([kstack_f[r] for r in rs], axis=0).astype(BF16)
        v_s = jnp.concatenate([v_all[s * L_S:(s + 1) * L_S, h * DV:(h + 1) * DV] for h in range(H)],
                              axis=0).astype(BF16)
        st = sst_ref[s]
        o_inter.append(jnp.dot(lhs_s, st.astype(BF16), preferred_element_type=F32))
        kv = lax.dot_general(k_s, v_s, (((0,), (0,)), ((), ())), preferred_element_type=F32)
        nst_ref[s] = st * decay_t[:, s * L_S:s * L_S + 1] + kv
    for h in range(H):
        v_h = v_all[:, h * DV:(h + 1) * DV].astype(BF16)
        g_h = p_ref[:, G_OFF + h * DV:G_OFF + (h + 1) * DV]
        oi_h = jnp.concatenate([o_inter[s][h * L_S:(h + 1) * L_S] for s in range(nseq)], axis=0)
        o_h = jnp.dot(sc[h * CH:(h + 1) * CH], v_h, preferred_element_type=F32) + oi_h
        y_ref[:, CW + h * DV:CW + (h + 1) * DV] = _gla_out(o_h, g_h, ngl).astype(BF16)


def _mixer_s(proj, sconv, sst, conv_w, conv_b, wgu_b, b_gate, n_conv, n_gla):
    nseq = SEQ_PER_CHUNK
    return pl.pallas_call(
        _mixer_s_kernel,
        out_shape=(jax.ShapeDtypeStruct((T_S, D), BF16),
                   jax.ShapeDtypeStruct((B_S, 2, CW), F32),
                   jax.ShapeDtypeStruct((B_S, HK, DV), F32)),
        grid=(T_S // CH,),
        in_specs=[pl.BlockSpec((CH, PROJ_PAD), lambda i: (i, 0)),
                  pl.BlockSpec((nseq, 2, CW), lambda i: (i, 0, 0)),
                  pl.BlockSpec((nseq, HK, DV), lambda i: (i, 0, 0)),
                  _const_spec((3, CW)), _const_spec((1, CW)), _const_spec((128, HK)), _const_spec((1, HK)),
                  _const_spec((1, CW)), _const_spec((1, DV))],
        out_specs=(pl.BlockSpec((CH, D), lambda i: (i, 0)),
                   pl.BlockSpec((nseq, 2, CW), lambda i: (i, 0, 0)),
                   pl.BlockSpec((nseq, HK, DV), lambda i: (i, 0, 0))),
        scratch_shapes=[pltpu.VMEM((8 + CH, CW), F32)],
        compiler_params=_cparams(),
        name="mixer_s",
    )(proj, sconv, sst, conv_w, conv_b, wgu_b, b_gate, n_conv, n_gla)


def _route(h2, wr_ref, br_ref, run_ref, pos_ref, rw_ref, cnt_ref):
    lt = lax.dot_general(wr_ref[...], h2.astype(BF16), (((1,), (1,)), ((), ())),
                         preferred_element_type=F32) + br_ref[...]

    coarse = lt[0:N_GROUPS]
    cmax = jnp.max(coarse, axis=0, keepdims=True)
    gi = _iota2((N_GROUPS, TM), 0)
    grp = jnp.min(jnp.where(coarse == cmax, gi, N_GROUPS), axis=0, keepdims=True)
    p_sel = 1.0 / jnp.sum(jnp.exp(coarse - cmax), axis=0, keepdims=True)
    fine = jnp.zeros((EPG, TM), F32)
    for g in range(N_GROUPS):
        fine = jnp.where(grp == g, lt[8 + g * EPG:8 + (g + 1) * EPG], fine)
    ei = _iota2((EPG, TM), 0)
    f1 = jnp.max(fine, axis=0, keepdims=True)
    i1 = jnp.min(jnp.where(fine == f1, ei, EPG), axis=0, keepdims=True)
    rest = jnp.where(ei == i1, -jnp.inf, fine)
    f2 = jnp.max(rest, axis=0, keepdims=True)
    i2 = jnp.min(jnp.where(rest == f2, ei, EPG), axis=0, keepdims=True)
    e2 = jnp.exp(f2 - f1)
    w1 = p_sel / (1.0 + e2)
    w2 = p_sel * e2 / (1.0 + e2)
    x1 = grp * EPG + i1
    x2 = grp * EPG + i2

    er = _iota2((NE, TM), 0)
    oh1 = jnp.where(er == x1, 1.0, 0.0)
    oh2 = jnp.where(er == x2, 1.0, 0.0)
    oh = oh1 + oh2
    before = jnp.where(_iota2((TM, TM), 0) < _iota2((TM, TM), 1), 1.0, 0.0).astype(BF16)
    cum = jnp.dot(oh.astype(BF16), before, preferred_element_type=F32) + run_ref[:, 0:1]
    r1 = jnp.sum(oh1 * cum, axis=0, keepdims=True).astype(I32)
    r2 = jnp.sum(oh2 * cum, axis=0, keepdims=True).astype(I32)
    run_new = run_ref[...] + jnp.sum(oh, axis=1, keepdims=True)
    run_ref[...] = run_new
    cnt_ref[...] = run_new.astype(I32)

    row = _iota2((8, TM), 0)
    pos_ref[...] = jnp.where(row == 0, x1 * E_CAP + r1, jnp.where(row == 1, x2 * E_CAP + r2, 0))
    wrow = _iota2((128, TM), 0)
    rw_ref[...] = jnp.where(wrow == 0, w1, jnp.where(wrow == 1, w2, 0.0)).T


def _outproj_kernel(ycp_ref, ycs_ref, xp_ref, xs_ref, gtp_ref, gts_ref, shp_ref, shs_ref, scp_ref, scs_ref,
                    wo_ref, g_ref, wr_ref, br_ref,
                    x1_ref, pos_ref, rw_ref, cnt_ref, xs_out,
                    run_ref, h2_scr, pos_v, pos_s, sem, psem):
    i = pl.program_id(0)
    par = i % 2

    def positions_to_smem():
        cp = pltpu.make_async_copy(pos_v, pos_s, psem)
        cp.start()
        cp.wait()

    def scatter(p):
        for r in range(TM):
            for slot in range(2):
                pltpu.make_async_copy(h2_scr.at[p, pl.ds(r, 1), :], xs_out.at[pos_s[slot, r]],
                                      sem).start(priority=r % 2)

    def drain():
        for _ in range(2):
            pltpu.make_async_copy(h2_scr.at[0], xs_out.at[pl.ds(0, TM), 0], sem).wait()

    @pl.when(i == 0)
    def _():
        run_ref[...] = jnp.zeros((NE, 128), F32)
        h2_scr[1] = jnp.zeros((TM, D), F32)
        pos_v[...] = E_ROWS + _iota2((8, TM), 0) * TM + _iota2((8, TM), 1)
        positions_to_smem()

    def tile(yc, x, gt, sh, sc):
        scatter(1 - par)
        mix = jnp.dot(yc, wo_ref[...], preferred_element_type=F32)
        x1 = x + gt * mix
        x1_ref[...] = x1
        h2 = _rms(x1, g_ref[...]) * (1.0 + sc) + sh
        h2_scr[par] = h2
        _route(h2, wr_ref, br_ref, run_ref, pos_v, rw_ref, cnt_ref)

    @pl.when(i < N_TILES_P)
    def _():
        tile(ycp_ref[...], xp_ref[...], _prompt_row(gtp_ref), _prompt_row(shp_ref), _prompt_row(scp_ref))

    @pl.when(i >= N_TILES_P)
    def _():
        tile(ycs_ref[...], xs_ref[...], _sample_rows(gts_ref), _sample_rows(shs_ref), _sample_rows(scs_ref))

    pos_ref[0] = pos_v[...]
    drain()
    positions_to_smem()

    @pl.when(i == N_TILES - 1)
    def _():
        scatter(par)
        drain()


def _outproj(ycp, ycs, xp, xs, mod, w_out_b, g_ffn, w_r, b_r):
    return pl.pallas_call(
        _outproj_kernel,
        out_shape=(jax.ShapeDtypeStruct((T, D), F32), jax.ShapeDtypeStruct((N_TILES, 8, TM), I32),
                   jax.ShapeDtypeStruct((T, 128), F32), jax.ShapeDtypeStruct((NE, 128), I32),
                   jax.ShapeDtypeStruct((E_ROWS + 2 * TM, 1, D), F32)),
        grid=(N_TILES,),
        in_specs=[pl.BlockSpec((TM, D), lambda i: (_p_tile(i), 0)), pl.BlockSpec((TM, D), lambda i: (_s_tile(i), 0)),
                  pl.BlockSpec((TM, D), lambda i: (_p_tile(i), 0)), pl.BlockSpec((TM, D), lambda i: (_s_tile(i), 0)),
                  *_mod_specs(2), *_mod_specs(3), *_mod_specs(4),
                  _const_spec((D, D)), _const_spec((1, D)), _const_spec((R_ROWS, D)), _const_spec((R_ROWS, 1))],
        out_specs=(pl.BlockSpec((TM, D), lambda i: (i, 0)), pl.BlockSpec((1, 8, TM), lambda i: (i, 0, 0)),
                   pl.BlockSpec((TM, 128), lambda i: (i, 0)), pl.BlockSpec((NE, 128), lambda i: (0, 0)),
                   pl.BlockSpec(memory_space=pl.ANY)),
        scratch_shapes=[pltpu.VMEM((NE, 128), F32), pltpu.VMEM((2, TM, D), F32), pltpu.VMEM((8, TM), I32),
                        pltpu.SMEM((8, TM), I32), pltpu.SemaphoreType.DMA(()), pltpu.SemaphoreType.DMA(())],
        compiler_params=_cparams(),
        name="outproj",
    )(ycp, ycs, xp, xs, mod, mod, mod, mod, mod, mod, w_out_b, g_ffn, w_r, b_r)


def _moe_kernel(tb_ref, te_ref, first_ref, nval_ref, nxt_ref, slot_ref, nt_ref,
                x_ref, wg_hbm, wu_hbm, wd_hbm, y_ref, wg_st, wu_st, wd_st, wgu_b, wd_b, wsem):
    i = pl.program_id(0)

    def weight_copies(e, s):
        return (pltpu.make_async_copy(wg_hbm.at[e], wg_st.at[s], wsem.at[s]),
                pltpu.make_async_copy(wu_hbm.at[e], wu_st.at[s], wsem.at[s]),
                pltpu.make_async_copy(wd_hbm.at[e], wd_st.at[s], wsem.at[s]))

    @pl.when(i < nt_ref[0])
    def _():
        @pl.when(first_ref[i] == 1)
        def _():
            s = slot_ref[i]

            @pl.when(i == 0)
            def _():
                for cp in weight_copies(te_ref[0], 0):
                    cp.start()

            for cp in weight_copies(te_ref[i], s):
                cp.wait()

            @pl.when(nxt_ref[i] >= 0)
            def _():
                for cp in weight_copies(nxt_ref[i], 1 - s):
                    cp.start()

            wgu_b[:, 0:DE] = wg_st[s].astype(BF16)
            wgu_b[:, DE:2 * DE] = wu_st[s].astype(BF16)
            wd_b[...] = wd_st[s].astype(BF16)

        valid = _iota2((E_TILE, D), 0) < nval_ref[i]
        x = jnp.where(valid, x_ref[...], 0.0).astype(BF16)
        ab = jnp.dot(x, wgu_b[...], preferred_element_type=F32)
        hid = (_silu(ab[:, 0:DE]) * ab[:, DE:2 * DE]).astype(BF16)
        y_ref[...] = jnp.dot(hid, wd_b[...], preferred_element_type=F32)


def _moe(plan, xs, w_eg, w_eu, w_ed):
    def tile_map(i, tb, *_):
        return (tb[i], 0, 0)

    grid_spec = pltpu.PrefetchScalarGridSpec(
        num_scalar_prefetch=len(plan),
        grid=(E_TILES_MAX,),
        in_specs=[pl.BlockSpec((E_TILE, None, D), tile_map),
                  pl.BlockSpec(memory_space=pl.ANY), pl.BlockSpec(memory_space=pl.ANY),
                  pl.BlockSpec(memory_space=pl.ANY)],
        out_specs=pl.BlockSpec((E_TILE, None, D), tile_map),
        scratch_shapes=[pltpu.VMEM((2, D, DE), F32), pltpu.VMEM((2, D, DE), F32), pltpu.VMEM((2, DE, D), F32),
                        pltpu.VMEM((D, 2 * DE), BF16), pltpu.VMEM((DE, D), BF16), pltpu.SemaphoreType.DMA((2,))],
    )
    return pl.pallas_call(
        _moe_kernel,
        out_shape=jax.ShapeDtypeStruct((E_ROWS, 1, D), F32),
        grid_spec=grid_spec,
        compiler_params=_cparams(),
        name="moe",
    )(*plan, xs, w_eg, w_eu, w_ed)


def _plan(counts):
    ntile = (counts + E_TILE - 1) // E_TILE
    tend = jnp.cumsum(ntile)
    tbeg = tend - ntile
    n_tiles = tend[-1]
    tid = jnp.minimum(jnp.arange(E_TILES_MAX, dtype=I32), n_tiles - 1)
    tile_expert = jnp.minimum(jnp.sum((tend[None, :] <= tid[:, None]).astype(I32), axis=1), NE - 1)
    tile_in_expert = tid - tbeg[tile_expert]
    tile_block = tile_expert * E_CAP_TILES + tile_in_expert
    tile_first = (tile_in_expert == 0).astype(I32)
    tile_nvalid = jnp.minimum(counts[tile_expert] - tile_in_expert * E_TILE, E_TILE)
    ids = jnp.arange(NE, dtype=I32)
    occupied = ntile > 0
    later = occupied[None, :] & (ids[None, :] > ids[:, None])
    next_expert = jnp.where(jnp.any(later, axis=1), jnp.min(jnp.where(later, ids[None, :], NE), axis=1), -1)
    rank = jnp.cumsum(occupied.astype(I32)) - 1
    return (tile_block, tile_expert, tile_first, tile_nvalid, next_expert[tile_expert].astype(I32),
            (rank[tile_expert] % 2).astype(I32), n_tiles.reshape(1))


def _final_kernel(pos0_ref, posn_ref, x1_ref, rw_ref, gtp_ref, gts_ref, g_ref, ys_ref, yp_ref, ysm_ref, buf, sem):
    i = pl.program_id(0)
    par = i % 2

    def gather(p_ref, p):
        for r in range(TM):
            for slot in range(2):
                pltpu.make_async_copy(ys_ref.at[p_ref[0, slot, r]], buf.at[p, slot, pl.ds(r, 1), :],
                                      sem.at[p]).start(priority=r % 2)

    @pl.when(i == 0)
    def _():
        gather(pos0_ref, 0)

    @pl.when(i < N_TILES - 1)
    def _():
        gather(posn_ref, 1 - par)

    for slot in range(2):
        pltpu.make_async_copy(ys_ref.at[pl.ds(0, TM), 0], buf.at[par, slot], sem.at[par]).wait()

    moe = rw_ref[:, 0:1] * buf[par, 0] + rw_ref[:, 1:2] * buf[par, 1]

    @pl.when(i < N_TILES_P)
    def _():
        yp_ref[...] = _rms(x1_ref[...] + _prompt_row(gtp_ref) * moe, g_ref[...])

    @pl.when(i >= N_TILES_P)
    def _():
        ysm_ref[...] = _rms(x1_ref[...] + _sample_rows(gts_ref) * moe, g_ref[...])


def _final(pos, x1, rw, mod, g_fin, ys):
    return pl.pallas_call(
        _final_kernel,
        out_shape=(jax.ShapeDtypeStruct((T_P, D), F32), jax.ShapeDtypeStruct((T_S, D), F32)),
        grid=(N_TILES,),
        in_specs=[pl.BlockSpec((1, 8, TM), lambda i: (0, 0, 0), memory_space=pltpu.SMEM),
                  pl.BlockSpec((1, 8, TM), lambda i: (jnp.minimum(i + 1, N_TILES - 1), 0, 0),
                               memory_space=pltpu.SMEM),
                  pl.BlockSpec((TM, D), lambda i: (i, 0)),
                  pl.BlockSpec((TM, 128), lambda i: (i, 0)),
                  *_mod_specs(5),
                  _const_spec((1, D)),
                  pl.BlockSpec(memory_space=pl.ANY)],
        out_specs=(pl.BlockSpec((TM, D), lambda i: (_p_tile(i), 0)),
                   pl.BlockSpec((TM, D), lambda i: (_s_tile(i), 0))),
        scratch_shapes=[pltpu.VMEM((2, 2, TM, D), F32), pltpu.SemaphoreType.DMA((2,))],
        compiler_params=_cparams(),
        name="final",
    )(pos, pos, x1, rw, mod, mod, g_fin, ys)


def kernel(x_prompt, x_sample, c_prompt, c_sample, state_conv, state_gla, w_ada, b_ada, norm_mix, w_in, conv_w,
           conv_b, w_gate_up, b_gate, norm_conv, norm_gla, w_out, norm_ffn, w_coarse, b_coarse, w_fine, b_fine,
           w_exp_gate, w_exp_up, w_exp_down, norm_final):
    xp = x_prompt.reshape(T_P, D)
    xs = x_sample.reshape(T_S, D)
    c_all = jnp.concatenate([c_sample, c_prompt, jnp.zeros((C_ROWS - B_S - B_P, D), F32)], axis=0)
    mod = _ada(c_all, w_ada[0], b_ada[0][None, :])

    w_in_t = jnp.swapaxes(w_in, 1, 2)
    w_low_t = jnp.pad(w_in_t[0, PROJ_MAIN:, :], ((0, 128 - RANK), (0, 0))).astype(BF16)
    w_out_b = w_out[0].astype(BF16)
    wgu_b = jnp.pad(w_gate_up[0], ((0, 128 - RANK), (0, 0))).astype(BF16)
    w_r = jnp.concatenate([w_coarse[0].T, jnp.zeros((8 - N_GROUPS, D), F32), w_fine[0].T], axis=0)
    b_r = jnp.concatenate([b_coarse[0], jnp.zeros((8 - N_GROUPS,), F32), b_fine[0]])[:, None]
    g_mix, g_ffn, g_fin = norm_mix[0][None, :], norm_ffn[0][None, :], norm_final[None, :]
    mix_consts = (conv_w[0], conv_b[0][None, :], wgu_b, b_gate[0][None, :], norm_conv[0][None, :],
                  norm_gla[0][None, :])

    ycat_p, nconv_p, nst_p, proj_s = _inmix(xp, xs, mod, g_mix, w_in_t, w_low_t, *mix_consts)
    ycat_s, nconv_s, nst_s = _mixer_s(proj_s, state_conv[0], state_gla[0].reshape(B_S, HK, DV), *mix_consts)

    x1, pos, rw, counts, x_sorted = _outproj(ycat_p, ycat_s, xp, xs, mod, w_out_b, g_ffn, w_r.astype(BF16), b_r)
    y_sorted = _moe(_plan(counts[:, 0]), x_sorted, w_exp_gate[0], w_exp_up[0], w_exp_down[0])
    y_p, y_s = _final(pos, x1, rw, mod, g_fin, y_sorted)

    new_gla_p = nst_p.reshape(B_P, DV, H, DK).transpose(0, 2, 3, 1)[None]
    new_gla_s = nst_s.reshape(1, B_S, H, DK, DV)
    return (y_p.reshape(B_P, L_P, D), y_s.reshape(B_S, L_S, D), nconv_p[None], new_gla_p,
            nconv_s[None], new_gla_s)
```

```python
import functools

import jax
import jax.numpy as jnp
from jax import lax
from jax.experimental import pallas as pl
from jax.experimental.pallas import tpu as pltpu

F32 = jnp.float32
BF16 = jnp.bfloat16
I32 = jnp.int32

D = 2048
CW = 1024
H = 8
DK = 64
DV = 128
HK = H * DK
HV = H * DV
RANK = 16
TAU = 16.0
N_GROUPS = 4
EPG = 8
NE = N_GROUPS * EPG
DE = 512
EPS = 1e-6

B_P, L_P = 4, 2048
B_S, L_S = 128, 8
T_P = B_P * L_P
T_S = B_S * L_S
T = T_P + T_S

TM = 256
CH = 64
N_TILES = T // TM
N_TILES_P = T_P // TM
TILES_PER_SEQ = L_P // TM
SEQ_PER_TILE = TM // L_S
SEQ_PER_CHUNK = CH // L_S

PROJ_MAIN = 3 * CW + 2 * HK + 2 * HV
PROJ_PAD = PROJ_MAIN + 128
C_ROWS = 136
P_ROW_BLOCK = B_S // 8
R_ROWS = 40

E_TILE = 256
E_TILES_MAX = (2 * T) // E_TILE + NE
E_CAP = T
E_CAP_TILES = E_CAP // E_TILE
E_ROWS = NE * E_CAP
VMEM_LIMIT = 60 * 1024 * 1024


def _cparams(n_axes=1, vmem=VMEM_LIMIT):
    return pltpu.CompilerParams(dimension_semantics=("arbitrary",) * n_axes, vmem_limit_bytes=vmem)


def _rms(x, g):
    return x * lax.rsqrt(jnp.mean(x * x, axis=-1, keepdims=True) + EPS) * g


def _sigmoid(x):
    return 1.0 / (1.0 + jnp.exp(-x))


def _silu(x):
    return x * _sigmoid(x)


def _log_sigmoid(x):
    return jnp.minimum(x, 0.0) - jnp.log(1.0 + jnp.exp(-jnp.abs(x)))


def _iota2(shape, axis):
    return lax.broadcasted_iota(I32, shape, axis)


def _expand_rows(ref, n, reps):
    return jnp.concatenate([jnp.broadcast_to(ref[j:j + 1, :], (reps, ref.shape[-1])) for j in range(n)], axis=0)


def _prompt_row(ref):
    return ref[pl.ds(pl.program_id(0) // TILES_PER_SEQ, 1), :]


def _sample_rows(ref):
    return _expand_rows(ref, SEQ_PER_TILE, L_S)


def _p_tile(i):
    return jnp.minimum(i, N_TILES_P - 1)


def _s_tile(i):
    return jnp.maximum(i - N_TILES_P, 0)


def _mod_specs(col):
    return [pl.BlockSpec((8, D), lambda i, *_: (P_ROW_BLOCK, col)),
            pl.BlockSpec((SEQ_PER_TILE, D), lambda i, *_: (_s_tile(i), col))]


def _const_spec(shape):
    zeros = (0,) * len(shape)
    return pl.BlockSpec(shape, lambda *_: zeros, pipeline_mode=pl.Buffered(1))


def _ada_kernel(c_ref, w_ref, b_ref, o_ref):
    s = _silu(c_ref[...]).astype(BF16)
    o_ref[...] = jnp.dot(s, w_ref[...].astype(BF16), preferred_element_type=F32) + b_ref[...]


def _ada(c_all, w_ada, b_ada):
    tn = 1024
    return pl.pallas_call(
        _ada_kernel,
        out_shape=jax.ShapeDtypeStruct((C_ROWS, 6 * D), F32),
        grid=(6 * D // tn,),
        in_specs=[pl.BlockSpec((C_ROWS, D), lambda j: (0, 0)),
                  pl.BlockSpec((D, tn), lambda j: (0, j)),
                  pl.BlockSpec((1, tn), lambda j: (0, j))],
        out_specs=pl.BlockSpec((C_ROWS, tn), lambda j: (0, j)),
        compiler_params=_cparams(),
        name="ada",
    )(c_all, w_ada, b_ada)


W_CHUNK = TM
N_W_CHUNKS = PROJ_MAIN // W_CHUNK
NT_DIMS = (((1,), (1,)), ((), ()))


def _gla_prep(p_ref, wgu_ref, bg_ref, rows, seg):
    q = p_ref[:, 3 * CW:3 * CW + HK]
    k = p_ref[:, 3 * CW + HK:3 * CW + 2 * HK]
    a = p_ref[:, PROJ_MAIN:PROJ_PAD].astype(BF16)
    z = jnp.dot(a, wgu_ref[...], preferred_element_type=F32) + bg_ref[...]
    la = _log_sigmoid(z) * (1.0 / TAU)
    la0 = la.astype(BF16)
    rem = la - la0.astype(F32)
    la1 = rem.astype(BF16)
    la2 = (rem - la1.astype(F32)).astype(BF16)
    r = _iota2((2 * rows, rows), 0)
    c = _iota2((2 * rows, rows), 1)
    same = ((r % rows) // seg) == (c // seg)
    sel = jnp.where(same & ((r >= rows) | (c <= r)), 1.0, 0.0).astype(BF16)
    sums = sum(jnp.dot(sel, part, preferred_element_type=F32) for part in (la0, la1, la2))
    b, bl = sums[0:rows], sums[rows:2 * rows]
    qt = q * jnp.exp(b) * (DK ** -0.5)
    kt = k * jnp.exp(-b)
    kend = kt * jnp.exp(bl)
    return qt, kt, kend, bl


def _head_stack(x):
    rows = x.shape[0]
    t = jnp.concatenate([x] * H, axis=0)
    keep = (_iota2((H * rows, HK), 0) // rows) == (_iota2((H * rows, HK), 1) // DK)
    return jnp.where(keep, t, 0.0)


def _gla_out(o_h, g_h, ngl):
    on = o_h * lax.rsqrt(jnp.mean(o_h * o_h, axis=-1, keepdims=True) + EPS) * ngl
    return on * _silu(g_h)


def _conv_out(bg, u, um1, um2, cw_ref, cb_ref, ncv_ref):
    conv_y = cb_ref[...] + cw_ref[0:1, :] * um2 + cw_ref[1:2, :] * um1 + cw_ref[2:3, :] * u
    return _rms(bg * conv_y, ncv_ref[...])


V_OFF = 3 * CW + 2 * HK
G_OFF = V_OFF + HV


def _mixer_p_tile(j, p_ref, cw_ref, cb_ref, wgu_ref, bg_ref, ncv_ref, ngl_ref,
                  y_ref, nconv_ref, nst_ref, ubuf, st_ref, also):
    @pl.when(j == 0)
    def _():
        ubuf[0:8, :] = jnp.zeros((8, CW), F32)
        st_ref[...] = jnp.zeros((DV, HK), F32)

    bgate = p_ref[:, 0:CW]
    u = p_ref[:, CW:2 * CW] * p_ref[:, 2 * CW:3 * CW]
    ubuf[8:8 + TM, :] = u
    um1 = ubuf[7:7 + TM, :]
    um2 = ubuf[6:6 + TM, :]
    y_ref[:, 0:CW] = _conv_out(bgate, u, um1, um2, cw_ref, cb_ref, ncv_ref).astype(BF16)
    ubuf[6:8, :] = u[TM - 2:TM, :]

    qt, kt, kend, bl = _gla_prep(p_ref, wgu_ref, bg_ref, TM, CH)
    causal = _iota2((2 * CH, CH), 0) % CH >= _iota2((2 * CH, CH), 1)
    first_head = _iota2((CH, 2 * DK), 1) < DK
    ngl = ngl_ref[...]

    def pair_stack(x):
        return jnp.concatenate([jnp.where(first_head, x, 0.0), jnp.where(first_head, 0.0, x)], axis=0).astype(BF16)

    for c in range(TM // CH):
        r0 = c * CH
        for m in range(H // 2):
            lanes = slice(m * 2 * DK, (m + 1) * 2 * DK)
            lhs = pair_stack(qt[r0:r0 + CH, lanes])
            sc = lax.dot_general(lhs, kt[r0:r0 + CH, lanes].astype(BF16), (((1,), (1,)), ((), ())),
                                 preferred_element_type=F32)
            sc = jnp.where(causal, sc, 0.0).astype(BF16)
            st = st_ref[:, lanes]
            o_inter = lax.dot_general(lhs, st.astype(BF16), (((1,), (1,)), ((), ())),
                                      preferred_element_type=F32)
            vs = []
            for hh in range(2):
                h = 2 * m + hh
                v_h = p_ref[r0:r0 + CH, V_OFF + h * DV:V_OFF + (h + 1) * DV].astype(BF16)
                g_h = p_ref[r0:r0 + CH, G_OFF + h * DV:G_OFF + (h + 1) * DV]
                vs.append(v_h)
                o_h = (jnp.dot(sc[hh * CH:(hh + 1) * CH], v_h, preferred_element_type=F32)
                       + o_inter[hh * CH:(hh + 1) * CH])
                y_ref[r0:r0 + CH, CW + h * DV:CW + (h + 1) * DV] = _gla_out(o_h, g_h, ngl).astype(BF16)
            kv_t = lax.dot_general(jnp.concatenate(vs, axis=0), pair_stack(kend[r0:r0 + CH, lanes]),
                                   (((0,), (0,)), ((), ())), preferred_element_type=F32)
            st_ref[:, lanes] = st * jnp.exp(bl[r0:r0 + 1, lanes]) + kv_t

    also()

    @pl.when(j == TILES_PER_SEQ - 1)
    def _():
        nconv_ref[0] = ubuf[6:8, :]
        nst_ref[0] = st_ref[...]


def _inmix_kernel(xp_ref, xs_ref, shp_ref, shs_ref, scp_ref, scs_ref, g_ref, wt_hbm, wa_ref,
                  cw_ref, cb_ref, wgu_ref, bg_ref, ncv_ref, ngl_ref,
                  y_ref, nconv_ref, nst_ref, projs_hbm,
                  h_scr, w_scr, proj, ubuf, st_ref, wsem, psem):
    i = pl.program_id(0)
    par = i % 2

    @pl.when(i == 0)
    def _():
        def chunk(c):
            return pltpu.make_async_copy(wt_hbm.at[0, pl.ds(c * W_CHUNK, W_CHUNK), :],
                                         proj.at[c % 2, :, pl.ds(0, D)], wsem.at[c % 2])

        chunk(0).start()
        for c in range(N_W_CHUNKS):
            if c + 1 < N_W_CHUNKS:
                chunk(c + 1).start()
            chunk(c).wait()
            w_scr[c * W_CHUNK:(c + 1) * W_CHUNK, :] = proj[c % 2, :, 0:D].astype(BF16)

    @pl.when(i < N_TILES_P)
    def _():
        h = _rms(xp_ref[...], g_ref[...]) * (1.0 + _prompt_row(scp_ref)) + _prompt_row(shp_ref)
        h_scr[...] = h.astype(BF16)

    @pl.when(i >= N_TILES_P)
    def _():
        h = _rms(xs_ref[...], g_ref[...]) * (1.0 + _sample_rows(scs_ref)) + _sample_rows(shs_ref)
        h_scr[...] = h.astype(BF16)

    def project():
        h = h_scr[...]
        dst = proj.at[par]
        dst[:, 0:PROJ_MAIN] = lax.dot_general(h, w_scr[...], NT_DIMS, preferred_element_type=F32)
        dst[:, PROJ_MAIN:PROJ_PAD] = lax.dot_general(h, wa_ref[...], NT_DIMS, preferred_element_type=F32)

    has_mixer = (i >= 1) & (i <= N_TILES_P)

    @pl.when(has_mixer)
    def _():
        _mixer_p_tile((i - 1) % TILES_PER_SEQ, proj.at[1 - par], cw_ref, cb_ref, wgu_ref, bg_ref, ncv_ref, ngl_ref,
                      y_ref, nconv_ref, nst_ref, ubuf, st_ref, project)

    @pl.when(jnp.logical_not(has_mixer))
    def _():
        project()

    @pl.when(i >= N_TILES_P)
    def _():
        cp = pltpu.make_async_copy(proj.at[par], projs_hbm.at[pl.ds((i - N_TILES_P) * TM, TM), :], psem)
        cp.start()
        cp.wait()


def _inmix(xp, xs, mod, g_mix, w_in_t, w_low_t, conv_w, conv_b, wgu_b, b_gate, n_conv, n_gla):
    def prev_tile(i):
        return jnp.clip(i - 1, 0, N_TILES_P - 1)

    return pl.pallas_call(
        _inmix_kernel,
        out_shape=(jax.ShapeDtypeStruct((T_P, D), BF16),
                   jax.ShapeDtypeStruct((B_P, 2, CW), F32),
                   jax.ShapeDtypeStruct((B_P, DV, HK), F32),
                   jax.ShapeDtypeStruct((T_S, PROJ_PAD), F32)),
        grid=(N_TILES,),
        in_specs=[pl.BlockSpec((TM, D), lambda i: (_p_tile(i), 0)),
                  pl.BlockSpec((TM, D), lambda i: (_s_tile(i), 0)),
                  *_mod_specs(0), *_mod_specs(1),
                  _const_spec((1, D)), pl.BlockSpec(memory_space=pl.ANY), _const_spec((128, D)),
                  _const_spec((3, CW)), _const_spec((1, CW)), _const_spec((128, HK)), _const_spec((1, HK)),
                  _const_spec((1, CW)), _const_spec((1, DV))],
        out_specs=(pl.BlockSpec((TM, D), lambda i: (prev_tile(i), 0)),
                   pl.BlockSpec((1, 2, CW), lambda i: (prev_tile(i) // TILES_PER_SEQ, 0, 0)),
                   pl.BlockSpec((1, DV, HK), lambda i: (prev_tile(i) // TILES_PER_SEQ, 0, 0)),
                   pl.BlockSpec(memory_space=pl.ANY)),
        scratch_shapes=[pltpu.VMEM((TM, D), BF16), pltpu.VMEM((PROJ_MAIN, D), BF16),
                        pltpu.VMEM((2, TM, PROJ_PAD), F32),
                        pltpu.VMEM((8 + TM, CW), F32), pltpu.VMEM((DV, HK), F32),
                        pltpu.SemaphoreType.DMA((2,)), pltpu.SemaphoreType.DMA(())],
        compiler_params=_cparams(),
        name="inmix",
    )(xp, xs, mod, mod, mod, mod, g_mix, w_in_t, w_low_t, conv_w, conv_b, wgu_b, b_gate, n_conv, n_gla)


def _mixer_s_kernel(p_ref, sconv_ref, sst_ref, cw_ref, cb_ref, wgu_ref, bg_ref, ncv_ref, ngl_ref,
                    y_ref, nconv_ref, nst_ref, ubuf):
    nseq = SEQ_PER_CHUNK
    bgate = p_ref[:, 0:CW]
    u = p_ref[:, CW:2 * CW] * p_ref[:, 2 * CW:3 * CW]
    ubuf[0:8, :] = jnp.zeros((8, CW), F32)
    ubuf[8:8 + CH, :] = u
    tpos = _iota2((CH, CW), 0) % L_S
    s0 = jnp.concatenate([jnp.broadcast_to(sconv_ref[s, 0:1, :], (L_S, CW)) for s in range(nseq)], axis=0)
    s1 = jnp.concatenate([jnp.broadcast_to(sconv_ref[s, 1:2, :], (L_S, CW)) for s in range(nseq)], axis=0)
    um1 = jnp.where(tpos == 0, s1, ubuf[7:7 + CH, :])
    um2 = jnp.where(tpos == 0, s0, jnp.where(tpos == 1, s1, ubuf[6:6 + CH, :]))
    y_ref[:, 0:CW] = _conv_out(bgate, u, um1, um2, cw_ref, cb_ref, ncv_ref).astype(BF16)
    for s in range(nseq):
        nconv_ref[s] = u[s * L_S + L_S - 2:(s + 1) * L_S, :]

    qt, kt, kend, bl = _gla_prep(p_ref, wgu_ref, bg_ref, CH, L_S)
    rr = _iota2((H * CH, CH), 0) % CH
    cc = _iota2((H * CH, CH), 1)
    causal = (rr >= cc) & ((rr // L_S) == (cc // L_S))
    ngl = ngl_ref[...]
    lhs_f = _head_stack(qt)
    sc = lax.dot_general(lhs_f.astype(BF16), kt.astype(BF16), (((1,), (1,)), ((), ())),
                         preferred_element_type=F32)
    sc = jnp.where(causal, sc, 0.0).astype(BF16)
    kstack_f = _head_stack(kend)
    v_all = p_ref[:, V_OFF:V_OFF + HV]
    decay_t = jnp.exp(jnp.concatenate([bl, jnp.zeros((128 - CH, HK), F32)], axis=0).T)
    o_inter = []
    for s in range(nseq):
        rs = [slice(h * CH + s * L_S, h * CH + (s + 1) * L_S) for h in range(H)]
        lhs_s = jnp.concatenate([lhs_f[r] for r in rs], axis=0).astype(BF16)
        k_s = jnp.concatenate([kstack_f[r] for r in rs], axis=0).astype(BF16)
        v_s = jnp.concatenate([v_all[s * L_S:(s + 1) * L_S, h * DV:(h + 1) * DV] for h in range(H)],
                              axis=0).astype(BF16)
        st = sst_ref[s]
        o_inter.append(jnp.dot(lhs_s, st.astype(BF16), preferred_element_type=F32))
        kv = lax.dot_general(k_s, v_s, (((0,), (0,)), ((), ())), preferred_element_type=F32)
        nst_ref[s] = st * decay_t[:, s * L_S:s * L_S + 1] + kv
    for h in range(H):
        v_h = v_all[:, h * DV:(h + 1) * DV].astype(BF16)
        g_h = p_ref[:, G_OFF + h * DV:G_OFF + (h + 1) * DV]
        oi_h = jnp.concatenate([o_inter[s][h * L_S:(h + 1) * L_S] for s in range(nseq)], axis=0)
        o_h = jnp.dot(sc[h * CH:(h + 1) * CH], v_h, preferred_element_type=F32) + oi_h
        y_ref[:, CW + h * DV:CW + (h + 1) * DV] = _gla_out(o_h, g_h, ngl).astype(BF16)


def _mixer_s(proj, sconv, sst, conv_w, conv_b, wgu_b, b_gate, n_conv, n_gla):
    nseq = SEQ_PER_CHUNK
    return pl.pallas_call(
        _mixer_s_kernel,
        out_shape=(jax.ShapeDtypeStruct((T_S, D), BF16),
                   jax.ShapeDtypeStruct((B_S, 2, CW), F32),
                   jax.ShapeDtypeStruct((B_S, HK, DV), F32)),
        grid=(T_S // CH,),
        in_specs=[pl.BlockSpec((CH, PROJ_PAD), lambda i: (i, 0)),
                  pl.BlockSpec((nseq, 2, CW), lambda i: (i, 0, 0)),
                  pl.BlockSpec((nseq, HK, DV), lambda i: (i, 0, 0)),
                  _const_spec((3, CW)), _const_spec((1, CW)), _const_spec((128, HK)), _const_spec((1, HK)),
                  _const_spec((1, CW)), _const_spec((1, DV))],
        out_specs=(pl.BlockSpec((CH, D), lambda i: (i, 0)),
                   pl.BlockSpec((nseq, 2, CW), lambda i: (i, 0, 0)),
                   pl.BlockSpec((nseq, HK, DV), lambda i: (i, 0, 0))),
        scratch_shapes=[pltpu.VMEM((8 + CH, CW), F32)],
        compiler_params=_cparams(),
        name="mixer_s",
    )(proj, sconv, sst, conv_w, conv_b, wgu_b, b_gate, n_conv, n_gla)


def _route(h2, wr_ref, br_ref, run_ref, pos_ref, rw_ref, cnt_ref):
    lt = lax.dot_general(wr_ref[...], h2.astype(BF16), (((1,), (1,)), ((), ())),
                         preferred_element_type=F32) + br_ref[...]

    coarse = lt[0:N_GROUPS]
    cmax = jnp.max(coarse, axis=0, keepdims=True)
    gi = _iota2((N_GROUPS, TM), 0)
    grp = jnp.min(jnp.where(coarse == cmax, gi, N_GROUPS), axis=0, keepdims=True)
    p_sel = 1.0 / jnp.sum(jnp.exp(coarse - cmax), axis=0, keepdims=True)
    fine = jnp.zeros((EPG, TM), F32)
    for g in range(N_GROUPS):
        fine = jnp.where(grp == g, lt[8 + g * EPG:8 + (g + 1) * EPG], fine)
    ei = _iota2((EPG, TM), 0)
    f1 = jnp.max(fine, axis=0, keepdims=True)
    i1 = jnp.min(jnp.where(fine == f1, ei, EPG), axis=0, keepdims=True)
    rest = jnp.where(ei == i1, -jnp.inf, fine)
    f2 = jnp.max(rest, axis=0, keepdims=True)
    i2 = jnp.min(jnp.where(rest == f2, ei, EPG), axis=0, keepdims=True)
    e2 = jnp.exp(f2 - f1)
    w1 = p_sel / (1.0 + e2)
    w2 = p_sel * e2 / (1.0 + e2)
    x1 = grp * EPG + i1
    x2 = grp * EPG + i2

    er = _iota2((NE, TM), 0)
    oh1 = jnp.where(er == x1, 1.0, 0.0)
    oh2 = jnp.where(er == x2, 1.0, 0.0)
    oh = oh1 + oh2
    before = jnp.where(_iota2((TM, TM), 0) < _iota2((TM, TM), 1), 1.0, 0.0).astype(BF16)
    cum = jnp.dot(oh.astype(BF16), before, preferred_element_type=F32) + run_ref[:, 0:1]
    r1 = jnp.sum(oh1 * cum, axis=0, keepdims=True).astype(I32)
    r2 = jnp.sum(oh2 * cum, axis=0, keepdims=True).astype(I32)
    run_new = run_ref[...] + jnp.sum(oh, axis=1, keepdims=True)
    run_ref[...] = run_new
    cnt_ref[...] = run_new.astype(I32)

    row = _iota2((8, TM), 0)
    pos_ref[...] = jnp.where(row == 0, x1 * E_CAP + r1, jnp.where(row == 1, x2 * E_CAP + r2, 0))
    wrow = _iota2((128, TM), 0)
    rw_ref[...] = jnp.where(wrow == 0, w1, jnp.where(wrow == 1, w2, 0.0)).T


def _outproj_kernel(ycp_ref, ycs_ref, xp_ref, xs_ref, gtp_ref, gts_ref, shp_ref, shs_ref, scp_ref, scs_ref,
                    wo_ref, g_ref, wr_ref, br_ref,
                    x1_ref, pos_ref, rw_ref, cnt_ref, xs_out,
                    run_ref, h2_scr, pos_v, pos_s, sem, psem):
    i = pl.program_id(0)
    par = i % 2

    to_smem = pltpu.make_async_copy(pos_v, pos_s, psem)

    def scatter(p):
        for r in range(TM):
            for slot in range(2):
                pltpu.make_async_copy(h2_scr.at[p, pl.ds(r, 1), :], xs_out.at[pl.ds(pos_s[slot, r], 1), :],
                                      sem).start(priority=r % 2)

    def drain():
        for _ in range(2):
            pltpu.make_async_copy(h2_scr.at[0], xs_out.at[pl.ds(0, TM), :], sem).wait()

    @pl.when(i == 0)
    def _():
        run_ref[...] = jnp.zeros((NE, 128), F32)
        h2_scr[1] = jnp.zeros((TM, D), F32)
        pos_v[...] = E_ROWS + _iota2((8, TM), 0) * TM + _iota2((8, TM), 1)
        to_smem.start()

    to_smem.wait()

    def tile(yc, x, gt, sh, sc):
        scatter(1 - par)
        mix = jnp.dot(yc, wo_ref[...], preferred_element_type=F32)
        x1 = x + gt * mix
        x1_ref[...] = x1
        h2 = _rms(x1, g_ref[...]) * (1.0 + sc) + sh
        h2_scr[par] = h2
        _route(h2, wr_ref, br_ref, run_ref, pos_v, rw_ref, cnt_ref)

    @pl.when(i < N_TILES_P)
    def _():
        tile(ycp_ref[...], xp_ref[...], _prompt_row(gtp_ref), _prompt_row(shp_ref), _prompt_row(scp_ref))

    @pl.when(i >= N_TILES_P)
    def _():
        tile(ycs_ref[...], xs_ref[...], _sample_rows(gts_ref), _sample_rows(shs_ref), _sample_rows(scs_ref))

    pos_ref[0] = pos_v[...]
    drain()
    to_smem.start()

    @pl.when(i == N_TILES - 1)
    def _():
        to_smem.wait()
        scatter(par)
        drain()


def _outproj(ycp, ycs, xp, xs, mod, w_out_b, g_ffn, w_r, b_r):
    return pl.pallas_call(
        _outproj_kernel,
        out_shape=(jax.ShapeDtypeStruct((T, D), F32), jax.ShapeDtypeStruct((N_TILES, 8, TM), I32),
                   jax.ShapeDtypeStruct((T, 128), F32), jax.ShapeDtypeStruct((NE, 128), I32),
                   jax.ShapeDtypeStruct((E_ROWS + 2 * TM, D), F32)),
        grid=(N_TILES,),
        in_specs=[pl.BlockSpec((TM, D), lambda i: (_p_tile(i), 0)), pl.BlockSpec((TM, D), lambda i: (_s_tile(i), 0)),
                  pl.BlockSpec((TM, D), lambda i: (_p_tile(i), 0)), pl.BlockSpec((TM, D), lambda i: (_s_tile(i), 0)),
                  *_mod_specs(2), *_mod_specs(3), *_mod_specs(4),
                  _const_spec((D, D)), _const_spec((1, D)), _const_spec((R_ROWS, D)), _const_spec((R_ROWS, 1))],
        out_specs=(pl.BlockSpec((TM, D), lambda i: (i, 0)), pl.BlockSpec((1, 8, TM), lambda i: (i, 0, 0)),
                   pl.BlockSpec((TM, 128), lambda i: (i, 0)), pl.BlockSpec((NE, 128), lambda i: (0, 0)),
                   pl.BlockSpec(memory_space=pl.ANY)),
        scratch_shapes=[pltpu.VMEM((NE, 128), F32), pltpu.VMEM((2, TM, D), F32), pltpu.VMEM((8, TM), I32),
                        pltpu.SMEM((8, TM), I32), pltpu.SemaphoreType.DMA(()), pltpu.SemaphoreType.DMA(())],
        compiler_params=_cparams(),
        name="outproj",
    )(ycp, ycs, xp, xs, mod, mod, mod, mod, mod, mod, w_out_b, g_ffn, w_r, b_r)


def _moe_kernel(tb_ref, te_ref, first_ref, nval_ref, nxt_ref, slot_ref, nt_ref,
                x_ref, wg_hbm, wu_hbm, wd_hbm, y_ref, wg_st, wu_st, wd_st, wgu_b, wd_b, wsem):
    i = pl.program_id(0)

    def weight_copies(e, s):
        return (pltpu.make_async_copy(wg_hbm.at[e], wg_st.at[s], wsem.at[s]),
                pltpu.make_async_copy(wu_hbm.at[e], wu_st.at[s], wsem.at[s]),
                pltpu.make_async_copy(wd_hbm.at[e], wd_st.at[s], wsem.at[s]))

    @pl.when(i < nt_ref[0])
    def _():
        @pl.when(first_ref[i] == 1)
        def _():
            s = slot_ref[i]

            @pl.when(i == 0)
            def _():
                for cp in weight_copies(te_ref[0], 0):
                    cp.start()

            for cp in weight_copies(te_ref[i], s):
                cp.wait()

            @pl.when(nxt_ref[i] >= 0)
            def _():
                for cp in weight_copies(nxt_ref[i], 1 - s):
                    cp.start()

            wgu_b[:, 0:DE] = wg_st[s].astype(BF16)
            wgu_b[:, DE:2 * DE] = wu_st[s].astype(BF16)
            wd_b[...] = wd_st[s].astype(BF16)

        valid = _iota2((E_TILE, D), 0) < nval_ref[i]
        x = jnp.where(valid, x_ref[...], 0.0).astype(BF16)
        ab = jnp.dot(x, wgu_b[...], preferred_element_type=F32)
        hid = (_silu(ab[:, 0:DE]) * ab[:, DE:2 * DE]).astype(BF16)
        y_ref[...] = jnp.dot(hid, wd_b[...], preferred_element_type=F32)


def _moe(plan, xs, w_eg, w_eu, w_ed):
    def x_map(i, tb, *_):
        return (tb[i], 0)

    def y_map(i, tb, *_):
        return (tb[i], 0, 0)

    grid_spec = pltpu.PrefetchScalarGridSpec(
        num_scalar_prefetch=len(plan),
        grid=(E_TILES_MAX,),
        in_specs=[pl.BlockSpec((E_TILE, D), x_map),
                  pl.BlockSpec(memory_space=pl.ANY), pl.BlockSpec(memory_space=pl.ANY),
                  pl.BlockSpec(memory_space=pl.ANY)],
        out_specs=pl.BlockSpec((E_TILE, None, D), y_map),
        scratch_shapes=[pltpu.VMEM((2, D, DE), F32), pltpu.VMEM((2, D, DE), F32), pltpu.VMEM((2, DE, D), F32),
                        pltpu.VMEM((D, 2 * DE), BF16), pltpu.VMEM((DE, D), BF16), pltpu.SemaphoreType.DMA((2,))],
    )
    return pl.pallas_call(
        _moe_kernel,
        out_shape=jax.ShapeDtypeStruct((E_ROWS, 1, D), F32),
        grid_spec=grid_spec,
        compiler_params=_cparams(),
        name="moe",
    )(*plan, xs, w_eg, w_eu, w_ed)


def _plan(counts):
    ntile = (counts + E_TILE - 1) // E_TILE
    tend = jnp.cumsum(ntile)
    tbeg = tend - ntile
    n_tiles = tend[-1]
    tid = jnp.minimum(jnp.arange(E_TILES_MAX, dtype=I32), n_tiles - 1)
    tile_expert = jnp.minimum(jnp.sum((tend[None, :] <= tid[:, None]).astype(I32), axis=1), NE - 1)
    tile_in_expert = tid - tbeg[tile_expert]
    tile_block = tile_expert * E_CAP_TILES + tile_in_expert
    tile_first = (tile_in_expert == 0).astype(I32)
    tile_nvalid = jnp.minimum(counts[tile_expert] - tile_in_expert * E_TILE, E_TILE)
    ids = jnp.arange(NE, dtype=I32)
    occupied = ntile > 0
    later = occupied[None, :] & (ids[None, :] > ids[:, None])
    next_expert = jnp.where(jnp.any(later, axis=1), jnp.min(jnp.where(later, ids[None, :], NE), axis=1), -1)
    rank = jnp.cumsum(occupied.astype(I32)) - 1
    return (tile_block, tile_expert, tile_first, tile_nvalid, next_expert[tile_expert].astype(I32),
            (rank[tile_expert] % 2).astype(I32), n_tiles.reshape(1))


def _final_kernel(pos0_ref, posn_ref, x1_ref, rw_ref, gtp_ref, gts_ref, g_ref, ys_ref, yp_ref, ysm_ref, buf, sem):
    i = pl.program_id(0)
    par = i % 2

    def gather(p_ref, p):
        for r in range(TM):
            for slot in range(2):
                pltpu.make_async_copy(ys_ref.at[p_ref[0, slot, r]], buf.at[p, slot, pl.ds(r, 1), :],
                                      sem.at[p]).start(priority=r % 2)

    @pl.when(i == 0)
    def _():
        gather(pos0_ref, 0)

    @pl.when(i < N_TILES - 1)
    def _():
        gather(posn_ref, 1 - par)

    for slot in range(2):
        pltpu.make_async_copy(ys_ref.at[pl.ds(0, TM), 0], buf.at[par, slot], sem.at[par]).wait()

    moe = rw_ref[:, 0:1] * buf[par, 0] + rw_ref[:, 1:2] * buf[par, 1]

    @pl.when(i < N_TILES_P)
    def _():
        yp_ref[...] = _rms(x1_ref[...] + _prompt_row(gtp_ref) * moe, g_ref[...])

    @pl.when(i >= N_TILES_P)
    def _():
        ysm_ref[...] = _rms(x1_ref[...] + _sample_rows(gts_ref) * moe, g_ref[...])


def _final(pos, x1, rw, mod, g_fin, ys):
    return pl.pallas_call(
        _final_kernel,
        out_shape=(jax.ShapeDtypeStruct((T_P, D), F32), jax.ShapeDtypeStruct((T_S, D), F32)),
        grid=(N_TILES,),
        in_specs=[pl.BlockSpec((1, 8, TM), lambda i: (0, 0, 0), memory_space=pltpu.SMEM),
                  pl.BlockSpec((1, 8, TM), lambda i: (jnp.minimum(i + 1, N_TILES - 1), 0, 0),
                               memory_space=pltpu.SMEM),
                  pl.BlockSpec((TM, D), lambda i: (i, 0)),
                  pl.BlockSpec((TM, 128), lambda i: (i, 0)),
                  *_mod_specs(5),
                  _const_spec((1, D)),
                  pl.BlockSpec(memory_space=pl.ANY)],
        out_specs=(pl.BlockSpec((TM, D), lambda i: (_p_tile(i), 0)),
                   pl.BlockSpec((TM, D), lambda i: (_s_tile(i), 0))),
        scratch_shapes=[pltpu.VMEM((2, 2, TM, D), F32), pltpu.SemaphoreType.DMA((2,))],
        compiler_params=_cparams(),
        name="final",
    )(pos, pos, x1, rw, mod, mod, g_fin, ys)


def kernel(x_prompt, x_sample, c_prompt, c_sample, state_conv, state_gla, w_ada, b_ada, norm_mix, w_in, conv_w,
           conv_b, w_gate_up, b_gate, norm_conv, norm_gla, w_out, norm_ffn, w_coarse, b_coarse, w_fine, b_fine,
           w_exp_gate, w_exp_up, w_exp_down, norm_final):
    xp = x_prompt.reshape(T_P, D)
    xs = x_sample.reshape(T_S, D)
    c_all = jnp.concatenate([c_sample, c_prompt, jnp.zeros((C_ROWS - B_S - B_P, D), F32)], axis=0)
    mod = _ada(c_all, w_ada[0], b_ada[0][None, :])

    w_in_t = jnp.swapaxes(w_in, 1, 2)
    w_low_t = jnp.pad(w_in_t[0, PROJ_MAIN:, :], ((0, 128 - RANK), (0, 0))).astype(BF16)
    w_out_b = w_out[0].astype(BF16)
    wgu_b = jnp.pad(w_gate_up[0], ((0, 128 - RANK), (0, 0))).astype(BF16)
    w_r = jnp.concatenate([w_coarse[0].T, jnp.zeros((8 - N_GROUPS, D), F32), w_fine[0].T], axis=0)
    b_r = jnp.concatenate([b_coarse[0], jnp.zeros((8 - N_GROUPS,), F32), b_fine[0]])[:, None]
    g_mix, g_ffn, g_fin = norm_mix[0][None, :], norm_ffn[0][None, :], norm_final[None, :]
    mix_consts = (conv_w[0], conv_b[0][None, :], wgu_b, b_gate[0][None, :], norm_conv[0][None, :],
                  norm_gla[0][None, :])

    ycat_p, nconv_p, nst_p, proj_s = _inmix(xp, xs, mod, g_mix, w_in_t, w_low_t, *mix_consts)
    ycat_s, nconv_s, nst_s = _mixer_s(proj_s, state_conv[0], state_gla[0].reshape(B_S, HK, DV), *mix_consts)

    x1, pos, rw, counts, x_sorted = _outproj(ycat_p, ycat_s, xp, xs, mod, w_out_b, g_ffn, w_r.astype(BF16), b_r)
    y_sorted = _moe(_plan(counts[:, 0]), x_sorted, w_exp_gate[0], w_exp_up[0], w_exp_down[0])
    y_p, y_s = _final(pos, x1, rw, mod, g_fin, y_sorted)

    new_gla_p = nst_p.reshape(B_P, DV, H, DK).transpose(0, 2, 3, 1)[None]
    new_gla_s = nst_s.reshape(1, B_S, H, DK, DV)
    return (y_p.reshape(B_P, L_P, D), y_s.reshape(B_S, L_S, D), nconv_p[None], new_gla_p,
            nconv_s[None], new_gla_s)
```

```python
import functools

import jax
import jax.numpy as jnp
from jax import lax
from jax.experimental import pallas as pl
from jax.experimental.pallas import tpu as pltpu

F32 = jnp.float32
BF16 = jnp.bfloat16
I32 = jnp.int32

D = 2048
CW = 1024
H = 8
DK = 64
DV = 128
HK = H * DK
HV = H * DV
RANK = 16
TAU = 16.0
N_GROUPS = 4
EPG = 8
NE = N_GROUPS * EPG
DE = 512
EPS = 1e-6

B_P, L_P = 4, 2048
B_S, L_S = 128, 8
T_P = B_P * L_P
T_S = B_S * L_S
T = T_P + T_S

TM = 256
CH = 64
N_TILES = T // TM
N_TILES_P = T_P // TM
TILES_PER_SEQ = L_P // TM
SEQ_PER_TILE = TM // L_S
SEQ_PER_CHUNK = CH // L_S

PROJ_MAIN = 3 * CW + 2 * HK + 2 * HV
PROJ_PAD = PROJ_MAIN + 128
C_ROWS = 136
P_ROW_BLOCK = B_S // 8
R_ROWS = 40

E_TILE = 256
E_TILES_MAX = (2 * T) // E_TILE + NE
E_CAP = T
E_CAP_TILES = E_CAP // E_TILE
E_ROWS = NE * E_CAP
VMEM_LIMIT = 60 * 1024 * 1024


def _cparams(n_axes=1, vmem=VMEM_LIMIT):
    return pltpu.CompilerParams(dimension_semantics=("arbitrary",) * n_axes, vmem_limit_bytes=vmem)


def _rms(x, g):
    return x * lax.rsqrt(jnp.mean(x * x, axis=-1, keepdims=True) + EPS) * g


def _sigmoid(x):
    return 1.0 / (1.0 + jnp.exp(-x))


def _silu(x):
    return x * _sigmoid(x)


def _log_sigmoid(x):
    return jnp.minimum(x, 0.0) - jnp.log(1.0 + jnp.exp(-jnp.abs(x)))


def _iota2(shape, axis):
    return lax.broadcasted_iota(I32, shape, axis)


def _expand_rows(ref, n, reps):
    return jnp.concatenate([jnp.broadcast_to(ref[j:j + 1, :], (reps, ref.shape[-1])) for j in range(n)], axis=0)


def _prompt_row(ref):
    return ref[pl.ds(pl.program_id(0) // TILES_PER_SEQ, 1), :]


def _sample_rows(ref):
    return _expand_rows(ref, SEQ_PER_TILE, L_S)


def _p_tile(i):
    return jnp.minimum(i, N_TILES_P - 1)


def _s_tile(i):
    return jnp.maximum(i - N_TILES_P, 0)


def _mod_specs(col):
    return [pl.BlockSpec((8, D), lambda i, *_: (P_ROW_BLOCK, col)),
            pl.BlockSpec((SEQ_PER_TILE, D), lambda i, *_: (_s_tile(i), col))]


def _const_spec(shape):
    zeros = (0,) * len(shape)
    return pl.BlockSpec(shape, lambda *_: zeros, pipeline_mode=pl.Buffered(1))


def _ada_kernel(c_ref, w_ref, b_ref, o_ref):
    s = _silu(c_ref[...]).astype(BF16)
    o_ref[...] = jnp.dot(s, w_ref[...].astype(BF16), preferred_element_type=F32) + b_ref[...]


def _ada(c_all, w_ada, b_ada):
    tn = 1024
    return pl.pallas_call(
        _ada_kernel,
        out_shape=jax.ShapeDtypeStruct((C_ROWS, 6 * D), F32),
        grid=(6 * D // tn,),
        in_specs=[pl.BlockSpec((C_ROWS, D), lambda j: (0, 0)),
                  pl.BlockSpec((D, tn), lambda j: (0, j)),
                  pl.BlockSpec((1, tn), lambda j: (0, j))],
        out_specs=pl.BlockSpec((C_ROWS, tn), lambda j: (0, j)),
        compiler_params=_cparams(),
        name="ada",
    )(c_all, w_ada, b_ada)


W_CHUNK = TM
N_W_CHUNKS = PROJ_MAIN // W_CHUNK
NT_DIMS = (((1,), (1,)), ((), ()))


def _gla_prep(p_ref, wgu_ref, bg_ref, rows, seg):
    q = p_ref[:, 3 * CW:3 * CW + HK]
    k = p_ref[:, 3 * CW + HK:3 * CW + 2 * HK]
    a = p_ref[:, PROJ_MAIN:PROJ_PAD].astype(BF16)
    z = jnp.dot(a, wgu_ref[...], preferred_element_type=F32) + bg_ref[...]
    la = _log_sigmoid(z) * (1.0 / TAU)
    la0 = la.astype(BF16)
    rem = la - la0.astype(F32)
    la1 = rem.astype(BF16)
    la2 = (rem - la1.astype(F32)).astype(BF16)
    r = _iota2((2 * rows, rows), 0)
    c = _iota2((2 * rows, rows), 1)
    same = ((r % rows) // seg) == (c // seg)
    sel = jnp.where(same & ((r >= rows) | (c <= r)), 1.0, 0.0).astype(BF16)
    sums = sum(jnp.dot(sel, part, preferred_element_type=F32) for part in (la0, la1, la2))
    b, bl = sums[0:rows], sums[rows:2 * rows]
    qt = q * jnp.exp(b) * (DK ** -0.5)
    kt = k * jnp.exp(-b)
    kend = kt * jnp.exp(bl)
    return qt, kt, kend, bl


def _head_stack(x):
    rows = x.shape[0]
    t = jnp.concatenate([x] * H, axis=0)
    keep = (_iota2((H * rows, HK), 0) // rows) == (_iota2((H * rows, HK), 1) // DK)
    return jnp.where(keep, t, 0.0)


def _gla_out(o_h, g_h, ngl):
    on = o_h * lax.rsqrt(jnp.mean(o_h * o_h, axis=-1, keepdims=True) + EPS) * ngl
    return on * _silu(g_h)


def _conv_out(bg, u, um1, um2, cw_ref, cb_ref, ncv_ref):
    conv_y = cb_ref[...] + cw_ref[0:1, :] * um2 + cw_ref[1:2, :] * um1 + cw_ref[2:3, :] * u
    return _rms(bg * conv_y, ncv_ref[...])


V_OFF = 3 * CW + 2 * HK
G_OFF = V_OFF + HV


def _mixer_p_tile(j, p_ref, cw_ref, cb_ref, wgu_ref, bg_ref, ncv_ref, ngl_ref,
                  y_ref, nconv_ref, nst_ref, ubuf, st_ref, also):
    @pl.when(j == 0)
    def _():
        ubuf[0:8, :] = jnp.zeros((8, CW), F32)
        st_ref[...] = jnp.zeros((DV, HK), F32)

    bgate = p_ref[:, 0:CW]
    u = p_ref[:, CW:2 * CW] * p_ref[:, 2 * CW:3 * CW]
    ubuf[8:8 + TM, :] = u
    um1 = ubuf[7:7 + TM, :]
    um2 = ubuf[6:6 + TM, :]
    y_ref[:, 0:CW] = _conv_out(bgate, u, um1, um2, cw_ref, cb_ref, ncv_ref).astype(BF16)
    ubuf[6:8, :] = u[TM - 2:TM, :]

    qt, kt, kend, bl = _gla_prep(p_ref, wgu_ref, bg_ref, TM, CH)
    causal = _iota2((2 * CH, CH), 0) % CH >= _iota2((2 * CH, CH), 1)
    first_head = _iota2((CH, 2 * DK), 1) < DK
    ngl = ngl_ref[...]

    def pair_stack(x):
        return jnp.concatenate([jnp.where(first_head, x, 0.0), jnp.where(first_head, 0.0, x)], axis=0).astype(BF16)

    for c in range(TM // CH):
        r0 = c * CH
        for m in range(H // 2):
            lanes = slice(m * 2 * DK, (m + 1) * 2 * DK)
            lhs = pair_stack(qt[r0:r0 + CH, lanes])
            sc = lax.dot_general(lhs, kt[r0:r0 + CH, lanes].astype(BF16), (((1,), (1,)), ((), ())),
                                 preferred_element_type=F32)
            sc = jnp.where(causal, sc, 0.0).astype(BF16)
            st = st_ref[:, lanes]
            o_inter = lax.dot_general(lhs, st.astype(BF16), (((1,), (1,)), ((), ())),
                                      preferred_element_type=F32)
            vs = []
            for hh in range(2):
                h = 2 * m + hh
                v_h = p_ref[r0:r0 + CH, V_OFF + h * DV:V_OFF + (h + 1) * DV].astype(BF16)
                g_h = p_ref[r0:r0 + CH, G_OFF + h * DV:G_OFF + (h + 1) * DV]
                vs.append(v_h)
                o_h = (jnp.dot(sc[hh * CH:(hh + 1) * CH], v_h, preferred_element_type=F32)
                       + o_inter[hh * CH:(hh + 1) * CH])
                y_ref[r0:r0 + CH, CW + h * DV:CW + (h + 1) * DV] = _gla_out(o_h, g_h, ngl).astype(BF16)
            kv_t = lax.dot_general(jnp.concatenate(vs, axis=0), pair_stack(kend[r0:r0 + CH, lanes]),
                                   (((0,), (0,)), ((), ())), preferred_element_type=F32)
            st_ref[:, lanes] = st * jnp.exp(bl[r0:r0 + 1, lanes]) + kv_t

    also()

    @pl.when(j == TILES_PER_SEQ - 1)
    def _():
        nconv_ref[0] = ubuf[6:8, :]
        nst_ref[0] = st_ref[...]


def _inmix_kernel(xp_ref, xs_ref, shp_ref, shs_ref, scp_ref, scs_ref, g_ref, wt_hbm, wa_ref,
                  cw_ref, cb_ref, wgu_ref, bg_ref, ncv_ref, ngl_ref,
                  y_ref, nconv_ref, nst_ref, projs_hbm,
                  h_scr, w_scr, proj, ubuf, st_ref, wsem, psem):
    i = pl.program_id(0)
    par = i % 2

    @pl.when(i == 0)
    def _():
        def chunk(c):
            return pltpu.make_async_copy(wt_hbm.at[0, pl.ds(c * W_CHUNK, W_CHUNK), :],
                                         proj.at[c % 2, :, pl.ds(0, D)], wsem.at[c % 2])

        chunk(0).start()
        for c in range(N_W_CHUNKS):
            if c + 1 < N_W_CHUNKS:
                chunk(c + 1).start()
            chunk(c).wait()
            w_scr[c * W_CHUNK:(c + 1) * W_CHUNK, :] = proj[c % 2, :, 0:D].astype(BF16)

    @pl.when(i < N_TILES_P)
    def _():
        h = _rms(xp_ref[...], g_ref[...]) * (1.0 + _prompt_row(scp_ref)) + _prompt_row(shp_ref)
        h_scr[...] = h.astype(BF16)

    @pl.when(i >= N_TILES_P)
    def _():
        h = _rms(xs_ref[...], g_ref[...]) * (1.0 + _sample_rows(scs_ref)) + _sample_rows(shs_ref)
        h_scr[...] = h.astype(BF16)

    def project():
        h = h_scr[...]
        dst = proj.at[par]
        dst[:, 0:PROJ_MAIN] = lax.dot_general(h, w_scr[...], NT_DIMS, preferred_element_type=F32)
        dst[:, PROJ_MAIN:PROJ_PAD] = lax.dot_general(h, wa_ref[...], NT_DIMS, preferred_element_type=F32)

    has_mixer = (i >= 1) & (i <= N_TILES_P)

    @pl.when(has_mixer)
    def _():
        _mixer_p_tile((i - 1) % TILES_PER_SEQ, proj.at[1 - par], cw_ref, cb_ref, wgu_ref, bg_ref, ncv_ref, ngl_ref,
                      y_ref, nconv_ref, nst_ref, ubuf, st_ref, project)

    @pl.when(jnp.logical_not(has_mixer))
    def _():
        project()

    @pl.when(i >= N_TILES_P)
    def _():
        cp = pltpu.make_async_copy(proj.at[par], projs_hbm.at[pl.ds((i - N_TILES_P) * TM, TM), :], psem)
        cp.start()
        cp.wait()


def _inmix(xp, xs, mod, g_mix, w_in_t, w_low_t, conv_w, conv_b, wgu_b, b_gate, n_conv, n_gla):
    def prev_tile(i):
        return jnp.clip(i - 1, 0, N_TILES_P - 1)

    return pl.pallas_call(
        _inmix_kernel,
        out_shape=(jax.ShapeDtypeStruct((T_P, D), BF16),
                   jax.ShapeDtypeStruct((B_P, 2, CW), F32),
                   jax.ShapeDtypeStruct((B_P, DV, HK), F32),
                   jax.ShapeDtypeStruct((T_S, PROJ_PAD), F32)),
        grid=(N_TILES,),
        in_specs=[pl.BlockSpec((TM, D), lambda i: (_p_tile(i), 0)),
                  pl.BlockSpec((TM, D), lambda i: (_s_tile(i), 0)),
                  *_mod_specs(0), *_mod_specs(1),
                  _const_spec((1, D)), pl.BlockSpec(memory_space=pl.ANY), _const_spec((128, D)),
                  _const_spec((3, CW)), _const_spec((1, CW)), _const_spec((128, HK)), _const_spec((1, HK)),
                  _const_spec((1, CW)), _const_spec((1, DV))],
        out_specs=(pl.BlockSpec((TM, D), lambda i: (prev_tile(i), 0)),
                   pl.BlockSpec((1, 2, CW), lambda i: (prev_tile(i) // TILES_PER_SEQ, 0, 0)),
                   pl.BlockSpec((1, DV, HK), lambda i: (prev_tile(i) // TILES_PER_SEQ, 0, 0)),
                   pl.BlockSpec(memory_space=pl.ANY)),
        scratch_shapes=[pltpu.VMEM((TM, D), BF16), pltpu.VMEM((PROJ_MAIN, D), BF16),
                        pltpu.VMEM((2, TM, PROJ_PAD), F32),
                        pltpu.VMEM((8 + TM, CW), F32), pltpu.VMEM((DV, HK), F32),
                        pltpu.SemaphoreType.DMA((2,)), pltpu.SemaphoreType.DMA(())],
        compiler_params=_cparams(),
        name="inmix",
    )(xp, xs, mod, mod, mod, mod, g_mix, w_in_t, w_low_t, conv_w, conv_b, wgu_b, b_gate, n_conv, n_gla)


def _mixer_s_kernel(p_ref, sconv_ref, sst_ref, cw_ref, cb_ref, wgu_ref, bg_ref, ncv_ref, ngl_ref,
                    y_ref, nconv_ref, nst_ref, ubuf):
    nseq = SEQ_PER_CHUNK
    bgate = p_ref[:, 0:CW]
    u = p_ref[:, CW:2 * CW] * p_ref[:, 2 * CW:3 * CW]
    ubuf[0:8, :] = jnp.zeros((8, CW), F32)
    ubuf[8:8 + CH, :] = u
    tpos = _iota2((CH, CW), 0) % L_S
    s0 = jnp.concatenate([jnp.broadcast_to(sconv_ref[s, 0:1, :], (L_S, CW)) for s in range(nseq)], axis=0)
    s1 = jnp.concatenate([jnp.broadcast_to(sconv_ref[s, 1:2, :], (L_S, CW)) for s in range(nseq)], axis=0)
    um1 = jnp.where(tpos == 0, s1, ubuf[7:7 + CH, :])
    um2 = jnp.where(tpos == 0, s0, jnp.where(tpos == 1, s1, ubuf[6:6 + CH, :]))
    y_ref[:, 0:CW] = _conv_out(bgate, u, um1, um2, cw_ref, cb_ref, ncv_ref).astype(BF16)
    for s in range(nseq):
        nconv_ref[s] = u[s * L_S + L_S - 2:(s + 1) * L_S, :]

    qt, kt, kend, bl = _gla_prep(p_ref, wgu_ref, bg_ref, CH, L_S)
    rr = _iota2((H * CH, CH), 0) % CH
    cc = _iota2((H * CH, CH), 1)
    causal = (rr >= cc) & ((rr // L_S) == (cc // L_S))
    ngl = ngl_ref[...]
    lhs_f = _head_stack(qt)
    sc = lax.dot_general(lhs_f.astype(BF16), kt.astype(BF16), (((1,), (1,)), ((), ())),
                         preferred_element_type=F32)
    sc = jnp.where(causal, sc, 0.0).astype(BF16)
    kstack_f = _head_stack(kend)
    v_all = p_ref[:, V_OFF:V_OFF + HV]
    decay_t = jnp.exp(jnp.concatenate([bl, jnp.zeros((128 - CH, HK), F32)], axis=0).T)
    o_inter = []
    for s in range(nseq):
        rs = [slice(h * CH + s * L_S, h * CH + (s + 1) * L_S) for h in range(H)]
        lhs_s = jnp.concatenate([lhs_f[r] for r in rs], axis=0).astype(BF16)
        k_s = jnp.concatenate([kstack_f[r] for r in rs], axis=0).astype(BF16)
        v_s = jnp.concatenate([v_all[s * L_S:(s + 1) * L_S, h * DV:(h + 1) * DV] for h in range(H)],
                              axis=0).astype(BF16)
        st = sst_ref[s]
        o_inter.append(jnp.dot(lhs_s, st.astype(BF16), preferred_element_type=F32))
        kv = lax.dot_general(k_s, v_s, (((0,), (0,)), ((), ())), preferred_element_type=F32)
        nst_ref[s] = st * decay_t[:, s * L_S:s * L_S + 1] + kv
    for h in range(H):
        v_h = v_all[:, h * DV:(h + 1) * DV].astype(BF16)
        g_h = p_ref[:, G_OFF + h * DV:G_OFF + (h + 1) * DV]
        oi_h = jnp.concatenate([o_inter[s][h * L_S:(h + 1) * L_S] for s in range(nseq)], axis=0)
        o_h = jnp.dot(sc[h * CH:(h + 1) * CH], v_h, preferred_element_type=F32) + oi_h
        y_ref[:, CW + h * DV:CW + (h + 1) * DV] = _gla_out(o_h, g_h, ngl).astype(BF16)


def _mixer_s(proj, sconv, sst, conv_w, conv_b, wgu_b, b_gate, n_conv, n_gla):
    nseq = SEQ_PER_CHUNK
    return pl.pallas_call(
        _mixer_s_kernel,
        out_shape=(jax.ShapeDtypeStruct((T_S, D), BF16),
                   jax.ShapeDtypeStruct((B_S, 2, CW), F32),
                   jax.ShapeDtypeStruct((B_S, HK, DV), F32)),
        grid=(T_S // CH,),
        in_specs=[pl.BlockSpec((CH, PROJ_PAD), lambda i: (i, 0)),
                  pl.BlockSpec((nseq, 2, CW), lambda i: (i, 0, 0)),
                  pl.BlockSpec((nseq, HK, DV), lambda i: (i, 0, 0)),
                  _const_spec((3, CW)), _const_spec((1, CW)), _const_spec((128, HK)), _const_spec((1, HK)),
                  _const_spec((1, CW)), _const_spec((1, DV))],
        out_specs=(pl.BlockSpec((CH, D), lambda i: (i, 0)),
                   pl.BlockSpec((nseq, 2, CW), lambda i: (i, 0, 0)),
                   pl.BlockSpec((nseq, HK, DV), lambda i: (i, 0, 0))),
        scratch_shapes=[pltpu.VMEM((8 + CH, CW), F32)],
        compiler_params=_cparams(),
        name="mixer_s",
    )(proj, sconv, sst, conv_w, conv_b, wgu_b, b_gate, n_conv, n_gla)


def _route(h2, wr_ref, br_ref, run_ref, pos_ref, rw_ref, cnt_ref):
    lt = lax.dot_general(wr_ref[...], h2.astype(BF16), (((1,), (1,)), ((), ())),
                         preferred_element_type=F32) + br_ref[...]

    coarse = lt[0:N_GROUPS]
    cmax = jnp.max(coarse, axis=0, keepdims=True)
    gi = _iota2((N_GROUPS, TM), 0)
    grp = jnp.min(jnp.where(coarse == cmax, gi, N_GROUPS), axis=0, keepdims=True)
    p_sel = 1.0 / jnp.sum(jnp.exp(coarse - cmax), axis=0, keepdims=True)
    fine = jnp.zeros((EPG, TM), F32)
    for g in range(N_GROUPS):
        fine = jnp.where(grp == g, lt[8 + g * EPG:8 + (g + 1) * EPG], fine)
    ei = _iota2((EPG, TM), 0)
    f1 = jnp.max(fine, axis=0, keepdims=True)
    i1 = jnp.min(jnp.where(fine == f1, ei, EPG), axis=0, keepdims=True)
    rest = jnp.where(ei == i1, -jnp.inf, fine)
    f2 = jnp.max(rest, axis=0, keepdims=True)
    i2 = jnp.min(jnp.where(rest == f2, ei, EPG), axis=0, keepdims=True)
    e2 = jnp.exp(f2 - f1)
    w1 = p_sel / (1.0 + e2)
    w2 = p_sel * e2 / (1.0 + e2)
    x1 = grp * EPG + i1
    x2 = grp * EPG + i2

    er = _iota2((NE, TM), 0)
    oh1 = jnp.where(er == x1, 1.0, 0.0)
    oh2 = jnp.where(er == x2, 1.0, 0.0)
    oh = oh1 + oh2
    before = jnp.where(_iota2((TM, TM), 0) < _iota2((TM, TM), 1), 1.0, 0.0).astype(BF16)
    cum = jnp.dot(oh.astype(BF16), before, preferred_element_type=F32) + run_ref[:, 0:1]
    r1 = jnp.sum(oh1 * cum, axis=0, keepdims=True).astype(I32)
    r2 = jnp.sum(oh2 * cum, axis=0, keepdims=True).astype(I32)
    run_new = run_ref[...] + jnp.sum(oh, axis=1, keepdims=True)
    run_ref[...] = run_new
    cnt_ref[...] = run_new.astype(I32)

    row = _iota2((8, TM), 0)
    pos_ref[...] = jnp.where(row == 0, x1 * E_CAP + r1, jnp.where(row == 1, x2 * E_CAP + r2, 0))
    wrow = _iota2((128, TM), 0)
    rw_ref[...] = jnp.where(wrow == 0, w1, jnp.where(wrow == 1, w2, 0.0)).T


def _outproj_kernel(ycp_ref, ycs_ref, xp_ref, xs_ref, gtp_ref, gts_ref, shp_ref, shs_ref, scp_ref, scs_ref,
                    wo_ref, g_ref, wr_ref, br_ref,
                    x1_ref, pos_ref, rw_ref, cnt_ref, xs_out,
                    run_ref, h2_scr, pos_v, pos_s, sem, psem):
    i = pl.program_id(0)
    par = i % 2

    to_smem = pltpu.make_async_copy(pos_v, pos_s, psem)

    def scatter(p):
        for r in range(TM):
            for slot in range(2):
                pltpu.make_async_copy(h2_scr.at[p, pl.ds(r, 1), :], xs_out.at[pl.ds(pos_s[slot, r], 1), :],
                                      sem).start(priority=r % 2)

    def drain():
        for _ in range(2):
            pltpu.make_async_copy(h2_scr.at[0], xs_out.at[pl.ds(0, TM), :], sem).wait()

    @pl.when(i == 0)
    def _():
        run_ref[...] = jnp.zeros((NE, 128), F32)
        h2_scr[1] = jnp.zeros((TM, D), F32)
        pos_v[...] = E_ROWS + _iota2((8, TM), 0) * TM + _iota2((8, TM), 1)
        to_smem.start()

    to_smem.wait()

    def tile(yc, x, gt, sh, sc):
        scatter(1 - par)
        mix = jnp.dot(yc, wo_ref[...], preferred_element_type=F32)
        x1 = x + gt * mix
        x1_ref[...] = x1
        h2 = _rms(x1, g_ref[...]) * (1.0 + sc) + sh
        h2_scr[par] = h2
        _route(h2, wr_ref, br_ref, run_ref, pos_v, rw_ref, cnt_ref)

    @pl.when(i < N_TILES_P)
    def _():
        tile(ycp_ref[...], xp_ref[...], _prompt_row(gtp_ref), _prompt_row(shp_ref), _prompt_row(scp_ref))

    @pl.when(i >= N_TILES_P)
    def _():
        tile(ycs_ref[...], xs_ref[...], _sample_rows(gts_ref), _sample_rows(shs_ref), _sample_rows(scs_ref))

    pos_ref[0] = pos_v[...]
    drain()
    to_smem.start()

    @pl.when(i == N_TILES - 1)
    def _():
        to_smem.wait()
        scatter(par)
        drain()


def _outproj(ycp, ycs, xp, xs, mod, w_out_b, g_ffn, w_r, b_r):
    return pl.pallas_call(
        _outproj_kernel,
        out_shape=(jax.ShapeDtypeStruct((T, D), F32), jax.ShapeDtypeStruct((N_TILES, 8, TM), I32),
                   jax.ShapeDtypeStruct((T, 128), F32), jax.ShapeDtypeStruct((NE, 128), I32),
                   jax.ShapeDtypeStruct((E_ROWS + 2 * TM, D), F32)),
        grid=(N_TILES,),
        in_specs=[pl.BlockSpec((TM, D), lambda i: (_p_tile(i), 0)), pl.BlockSpec((TM, D), lambda i: (_s_tile(i), 0)),
                  pl.BlockSpec((TM, D), lambda i: (_p_tile(i), 0)), pl.BlockSpec((TM, D), lambda i: (_s_tile(i), 0)),
                  *_mod_specs(2), *_mod_specs(3), *_mod_specs(4),
                  _const_spec((D, D)), _const_spec((1, D)), _const_spec((R_ROWS, D)), _const_spec((R_ROWS, 1))],
        out_specs=(pl.BlockSpec((TM, D), lambda i: (i, 0)), pl.BlockSpec((1, 8, TM), lambda i: (i, 0, 0)),
                   pl.BlockSpec((TM, 128), lambda i: (i, 0)), pl.BlockSpec((NE, 128), lambda i: (0, 0)),
                   pl.BlockSpec(memory_space=pl.ANY)),
        scratch_shapes=[pltpu.VMEM((NE, 128), F32), pltpu.VMEM((2, TM, D), F32), pltpu.VMEM((8, TM), I32),
                        pltpu.SMEM((8, TM), I32), pltpu.SemaphoreType.DMA(()), pltpu.SemaphoreType.DMA(())],
        compiler_params=_cparams(),
        name="outproj",
    )(ycp, ycs, xp, xs, mod, mod, mod, mod, mod, mod, w_out_b, g_ffn, w_r, b_r)


E_CHUNK = 64
E_CHUNKS = E_TILE // E_CHUNK


def _moe_kernel(tb_ref, te_ref, first_ref, nval_ref, nxt_ref, slot_ref, nt_ref,
                x_hbm, wg_hbm, wu_hbm, wd_hbm, y_hbm,
                xbuf, ybuf, wg_st, wu_st, wd_st, wgu_b, wd_b, wsem, xsem, ysem):
    i = pl.program_id(0)
    par = i % 2
    nt = nt_ref[0]

    def weight_copies(e, s):
        return (pltpu.make_async_copy(wg_hbm.at[e], wg_st.at[s], wsem.at[s]),
                pltpu.make_async_copy(wu_hbm.at[e], wu_st.at[s], wsem.at[s]),
                pltpu.make_async_copy(wd_hbm.at[e], wd_st.at[s], wsem.at[s]))

    def tile_chunks(t, p, output, fn):
        row0 = tb_ref[t] * E_TILE
        for c in range(E_CHUNKS):
            rows = pl.ds(c * E_CHUNK, E_CHUNK)
            hbm_rows = pl.ds(row0 + c * E_CHUNK, E_CHUNK)

            @pl.when(c * E_CHUNK < nval_ref[t])
            def _():
                if output:
                    fn(pltpu.make_async_copy(ybuf.at[p, rows, :], y_hbm.at[hbm_rows, 0], ysem.at[p]))
                else:
                    fn(pltpu.make_async_copy(x_hbm.at[hbm_rows, :], xbuf.at[p, rows, :], xsem.at[p]))

    def start(cp):
        cp.start()

    def wait(cp):
        cp.wait()

    @pl.when(i == 0)
    def _():
        tile_chunks(0, 0, False, start)

    @pl.when(i + 1 < nt)
    def _():
        tile_chunks(i + 1, 1 - par, False, start)

    @pl.when(i < nt)
    def _():
        tile_chunks(i, par, False, wait)

        @pl.when(first_ref[i] == 1)
        def _():
            s = slot_ref[i]

            @pl.when(i == 0)
            def _():
                for cp in weight_copies(te_ref[0], 0):
                    cp.start()

            for cp in weight_copies(te_ref[i], s):
                cp.wait()

            @pl.when(nxt_ref[i] >= 0)
            def _():
                for cp in weight_copies(nxt_ref[i], 1 - s):
                    cp.start()

            wgu_b[:, 0:DE] = wg_st[s].astype(BF16)
            wgu_b[:, DE:2 * DE] = wu_st[s].astype(BF16)
            wd_b[...] = wd_st[s].astype(BF16)

        valid = _iota2((E_TILE, D), 0) < nval_ref[i]
        x = jnp.where(valid, xbuf[par], 0.0).astype(BF16)
        ab = jnp.dot(x, wgu_b[...], preferred_element_type=F32)
        hid = (_silu(ab[:, 0:DE]) * ab[:, DE:2 * DE]).astype(BF16)
        ybuf[par] = jnp.dot(hid, wd_b[...], preferred_element_type=F32)

        tile_chunks(i, par, True, start)

        @pl.when(i >= 1)
        def _():
            tile_chunks(i - 1, 1 - par, True, wait)

        @pl.when(i == nt - 1)
        def _():
            tile_chunks(i, par, True, wait)


def _moe(plan, xs, w_eg, w_eu, w_ed):
    grid_spec = pltpu.PrefetchScalarGridSpec(
        num_scalar_prefetch=len(plan),
        grid=(E_TILES_MAX,),
        in_specs=[pl.BlockSpec(memory_space=pl.ANY)] * 4,
        out_specs=pl.BlockSpec(memory_space=pl.ANY),
        scratch_shapes=[pltpu.VMEM((2, E_TILE, D), F32), pltpu.VMEM((2, E_TILE, D), F32),
                        pltpu.VMEM((2, D, DE), F32), pltpu.VMEM((2, D, DE), F32), pltpu.VMEM((2, DE, D), F32),
                        pltpu.VMEM((D, 2 * DE), BF16), pltpu.VMEM((DE, D), BF16),
                        pltpu.SemaphoreType.DMA((2,)), pltpu.SemaphoreType.DMA((2,)), pltpu.SemaphoreType.DMA((2,))],
    )
    return pl.pallas_call(
        _moe_kernel,
        out_shape=jax.ShapeDtypeStruct((E_ROWS, 1, D), F32),
        grid_spec=grid_spec,
        compiler_params=_cparams(),
        name="moe",
    )(*plan, xs, w_eg, w_eu, w_ed)


def _plan(counts):
    ntile = (counts + E_TILE - 1) // E_TILE
    tend = jnp.cumsum(ntile)
    tbeg = tend - ntile
    n_tiles = tend[-1]
    tid = jnp.minimum(jnp.arange(E_TILES_MAX, dtype=I32), n_tiles - 1)
    tile_expert = jnp.minimum(jnp.sum((tend[None, :] <= tid[:, None]).astype(I32), axis=1), NE - 1)
    tile_in_expert = tid - tbeg[tile_expert]
    tile_block = tile_expert * E_CAP_TILES + tile_in_expert
    tile_first = (tile_in_expert == 0).astype(I32)
    tile_nvalid = jnp.minimum(counts[tile_expert] - tile_in_expert * E_TILE, E_TILE)
    ids = jnp.arange(NE, dtype=I32)
    occupied = ntile > 0
    later = occupied[None, :] & (ids[None, :] > ids[:, None])
    next_expert = jnp.where(jnp.any(later, axis=1), jnp.min(jnp.where(later, ids[None, :], NE), axis=1), -1)
    rank = jnp.cumsum(occupied.astype(I32)) - 1
    return (tile_block, tile_expert, tile_first, tile_nvalid, next_expert[tile_expert].astype(I32),
            (rank[tile_expert] % 2).astype(I32), n_tiles.reshape(1))


def _final_kernel(pos0_ref, posn_ref, x1_ref, rw_ref, gtp_ref, gts_ref, g_ref, ys_ref, yp_ref, ysm_ref, buf, sem):
    i = pl.program_id(0)
    par = i % 2

    def gather(p_ref, p):
        for r in range(TM):
            for slot in range(2):
                pltpu.make_async_copy(ys_ref.at[p_ref[0, slot, r]], buf.at[p, slot, pl.ds(r, 1), :],
                                      sem.at[p]).start(priority=r % 2)

    @pl.when(i == 0)
    def _():
        gather(pos0_ref, 0)

    @pl.when(i < N_TILES - 1)
    def _():
        gather(posn_ref, 1 - par)

    for slot in range(2):
        pltpu.make_async_copy(ys_ref.at[pl.ds(0, TM), 0], buf.at[par, slot], sem.at[par]).wait()

    moe = rw_ref[:, 0:1] * buf[par, 0] + rw_ref[:, 1:2] * buf[par, 1]

    @pl.when(i < N_TILES_P)
    def _():
        yp_ref[...] = _rms(x1_ref[...] + _prompt_row(gtp_ref) * moe, g_ref[...])

    @pl.when(i >= N_TILES_P)
    def _():
        ysm_ref[...] = _rms(x1_ref[...] + _sample_rows(gts_ref) * moe, g_ref[...])


def _final(pos, x1, rw, mod, g_fin, ys):
    return pl.pallas_call(
        _final_kernel,
        out_shape=(jax.ShapeDtypeStruct((T_P, D), F32), jax.ShapeDtypeStruct((T_S, D), F32)),
        grid=(N_TILES,),
        in_specs=[pl.BlockSpec((1, 8, TM), lambda i: (0, 0, 0), memory_space=pltpu.SMEM),
                  pl.BlockSpec((1, 8, TM), lambda i: (jnp.minimum(i + 1, N_TILES - 1), 0, 0),
                               memory_space=pltpu.SMEM),
                  pl.BlockSpec((TM, D), lambda i: (i, 0)),
                  pl.BlockSpec((TM, 128), lambda i: (i, 0)),
                  *_mod_specs(5),
                  _const_spec((1, D)),
                  pl.BlockSpec(memory_space=pl.ANY)],
        out_specs=(pl.BlockSpec((TM, D), lambda i: (_p_tile(i), 0)),
                   pl.BlockSpec((TM, D), lambda i: (_s_tile(i), 0))),
        scratch_shapes=[pltpu.VMEM((2, 2, TM, D), F32), pltpu.SemaphoreType.DMA((2,))],
        compiler_params=_cparams(),
        name="final",
    )(pos, pos, x1, rw, mod, mod, g_fin, ys)


def kernel(x_prompt, x_sample, c_prompt, c_sample, state_conv, state_gla, w_ada, b_ada, norm_mix, w_in, conv_w,
           conv_b, w_gate_up, b_gate, norm_conv, norm_gla, w_out, norm_ffn, w_coarse, b_coarse, w_fine, b_fine,
           w_exp_gate, w_exp_up, w_exp_down, norm_final):
    xp = x_prompt.reshape(T_P, D)
    xs = x_sample.reshape(T_S, D)
    c_all = jnp.concatenate([c_sample, c_prompt, jnp.zeros((C_ROWS - B_S - B_P, D), F32)], axis=0)
    mod = _ada(c_all, w_ada[0], b_ada[0][None, :])

    w_in_t = jnp.swapaxes(w_in, 1, 2)
    w_low_t = jnp.pad(w_in_t[0, PROJ_MAIN:, :], ((0, 128 - RANK), (0, 0))).astype(BF16)
    w_out_b = w_out[0].astype(BF16)
    wgu_b = jnp.pad(w_gate_up[0], ((0, 128 - RANK), (0, 0))).astype(BF16)
    w_r = jnp.concatenate([w_coarse[0].T, jnp.zeros((8 - N_GROUPS, D), F32), w_fine[0].T], axis=0)
    b_r = jnp.concatenate([b_coarse[0], jnp.zeros((8 - N_GROUPS,), F32), b_fine[0]])[:, None]
    g_mix, g_ffn, g_fin = norm_mix[0][None, :], norm_ffn[0][None, :], norm_final[None, :]
    mix_consts = (conv_w[0], conv_b[0][None, :], wgu_b, b_gate[0][None, :], norm_conv[0][None, :],
                  norm_gla[0][None, :])

    ycat_p, nconv_p, nst_p, proj_s = _inmix(xp, xs, mod, g_mix, w_in_t, w_low_t, *mix_consts)
    ycat_s, nconv_s, nst_s = _mixer_s(proj_s, state_conv[0], state_gla[0].reshape(B_S, HK, DV), *mix_consts)

    x1, pos, rw, counts, x_sorted = _outproj(ycat_p, ycat_s, xp, xs, mod, w_out_b, g_ffn, w_r.astype(BF16), b_r)
    y_sorted = _moe(_plan(counts[:, 0]), x_sorted, w_exp_gate[0], w_exp_up[0], w_exp_down[0])
    y_p, y_s = _final(pos, x1, rw, mod, g_fin, y_sorted)

    new_gla_p = nst_p.reshape(B_P, DV, H, DK).transpose(0, 2, 3, 1)[None]
    new_gla_s = nst_s.reshape(1, B_S, H, DK, DV)
    return (y_p.reshape(B_P, L_P, D), y_s.reshape(B_S, L_S, D), nconv_p[None], new_gla_p,
            nconv_s[None], new_gla_s)
```

```python
import functools

import jax
import jax.numpy as jnp
from jax import lax
from jax.experimental import pallas as pl
from jax.experimental.pallas import tpu as pltpu

F32 = jnp.float32
BF16 = jnp.bfloat16
I32 = jnp.int32

D = 2048
CW = 1024
H = 8
DK = 64
DV = 128
HK = H * DK
HV = H * DV
RANK = 16
TAU = 16.0
N_GROUPS = 4
EPG = 8
NE = N_GROUPS * EPG
DE = 512
EPS = 1e-6

B_P, L_P = 4, 2048
B_S, L_S = 128, 8
T_P = B_P * L_P
T_S = B_S * L_S
T = T_P + T_S

TM = 256
CH = 64
N_TILES = T // TM
N_TILES_P = T_P // TM
TILES_PER_SEQ = L_P // TM
SEQ_PER_TILE = TM // L_S
SEQ_PER_CHUNK = CH // L_S

PROJ_MAIN = 3 * CW + 2 * HK + 2 * HV
PROJ_PAD = PROJ_MAIN + 128
C_ROWS = 136
P_ROW_BLOCK = B_S // 8
R_ROWS = 40

E_TILE = 256
E_TILES_MAX = (2 * T) // E_TILE + NE
E_CAP = T
E_CAP_TILES = E_CAP // E_TILE
E_ROWS = NE * E_CAP
VMEM_LIMIT = 60 * 1024 * 1024


def _cparams(n_axes=1, vmem=VMEM_LIMIT):
    return pltpu.CompilerParams(dimension_semantics=("arbitrary",) * n_axes, vmem_limit_bytes=vmem)


def _rms(x, g):
    return x * lax.rsqrt(jnp.mean(x * x, axis=-1, keepdims=True) + EPS) * g


def _sigmoid(x):
    return 1.0 / (1.0 + jnp.exp(-x))


def _silu(x):
    return x * _sigmoid(x)


def _log_sigmoid(x):
    return jnp.minimum(x, 0.0) - jnp.log(1.0 + jnp.exp(-jnp.abs(x)))


def _iota2(shape, axis):
    return lax.broadcasted_iota(I32, shape, axis)


def _expand_rows(ref, n, reps):
    return jnp.concatenate([jnp.broadcast_to(ref[j:j + 1, :], (reps, ref.shape[-1])) for j in range(n)], axis=0)


def _prompt_row(ref):
    return ref[pl.ds(pl.program_id(0) // TILES_PER_SEQ, 1), :]


def _sample_rows(ref):
    return _expand_rows(ref, SEQ_PER_TILE, L_S)


def _p_tile(i):
    return jnp.minimum(i, N_TILES_P - 1)


def _s_tile(i):
    return jnp.maximum(i - N_TILES_P, 0)


def _mod_specs(col):
    return [pl.BlockSpec((8, D), lambda i, *_: (P_ROW_BLOCK, col)),
            pl.BlockSpec((SEQ_PER_TILE, D), lambda i, *_: (_s_tile(i), col))]


def _const_spec(shape):
    zeros = (0,) * len(shape)
    return pl.BlockSpec(shape, lambda *_: zeros, pipeline_mode=pl.Buffered(1))


def _ada_kernel(c_ref, w_ref, b_ref, o_ref):
    s = _silu(c_ref[...]).astype(BF16)
    o_ref[...] = jnp.dot(s, w_ref[...].astype(BF16), preferred_element_type=F32) + b_ref[...]


def _ada(c_all, w_ada, b_ada):
    tn = 1024
    return pl.pallas_call(
        _ada_kernel,
        out_shape=jax.ShapeDtypeStruct((C_ROWS, 6 * D), F32),
        grid=(6 * D // tn,),
        in_specs=[pl.BlockSpec((C_ROWS, D), lambda j: (0, 0)),
                  pl.BlockSpec((D, tn), lambda j: (0, j)),
                  pl.BlockSpec((1, tn), lambda j: (0, j))],
        out_specs=pl.BlockSpec((C_ROWS, tn), lambda j: (0, j)),
        compiler_params=_cparams(),
        name="ada",
    )(c_all, w_ada, b_ada)


W_CHUNK = TM
N_W_CHUNKS = PROJ_MAIN // W_CHUNK
NT_DIMS = (((1,), (1,)), ((), ()))


def _gla_prep(p_ref, wgu_ref, bg_ref, rows, seg):
    q = p_ref[:, 3 * CW:3 * CW + HK]
    k = p_ref[:, 3 * CW + HK:3 * CW + 2 * HK]
    a = p_ref[:, PROJ_MAIN:PROJ_PAD].astype(BF16)
    z = jnp.dot(a, wgu_ref[...], preferred_element_type=F32) + bg_ref[...]
    la = _log_sigmoid(z) * (1.0 / TAU)
    la0 = la.astype(BF16)
    rem = la - la0.astype(F32)
    la1 = rem.astype(BF16)
    la2 = (rem - la1.astype(F32)).astype(BF16)
    r = _iota2((2 * rows, rows), 0)
    c = _iota2((2 * rows, rows), 1)
    same = ((r % rows) // seg) == (c // seg)
    sel = jnp.where(same & ((r >= rows) | (c <= r)), 1.0, 0.0).astype(BF16)
    sums = sum(jnp.dot(sel, part, preferred_element_type=F32) for part in (la0, la1, la2))
    b, bl = sums[0:rows], sums[rows:2 * rows]
    qt = q * jnp.exp(b) * (DK ** -0.5)
    kt = k * jnp.exp(-b)
    kend = kt * jnp.exp(bl)
    return qt, kt, kend, bl


def _head_stack(x):
    rows = x.shape[0]
    t = jnp.concatenate([x] * H, axis=0)
    keep = (_iota2((H * rows, HK), 0) // rows) == (_iota2((H * rows, HK), 1) // DK)
    return jnp.where(keep, t, 0.0)


def _gla_out(o_h, g_h, ngl):
    on = o_h * lax.rsqrt(jnp.mean(o_h * o_h, axis=-1, keepdims=True) + EPS) * ngl
    return on * _silu(g_h)


def _conv_out(bg, u, um1, um2, cw_ref, cb_ref, ncv_ref):
    conv_y = cb_ref[...] + cw_ref[0:1, :] * um2 + cw_ref[1:2, :] * um1 + cw_ref[2:3, :] * u
    return _rms(bg * conv_y, ncv_ref[...])


V_OFF = 3 * CW + 2 * HK
G_OFF = V_OFF + HV


def _mixer_p_tile(j, p_ref, cw_ref, cb_ref, wgu_ref, bg_ref, ncv_ref, ngl_ref,
                  y_ref, nconv_ref, nst_ref, ubuf, st_ref, also):
    @pl.when(j == 0)
    def _():
        ubuf[0:8, :] = jnp.zeros((8, CW), F32)
        st_ref[...] = jnp.zeros((DV, HK), F32)

    bgate = p_ref[:, 0:CW]
    u = p_ref[:, CW:2 * CW] * p_ref[:, 2 * CW:3 * CW]
    ubuf[8:8 + TM, :] = u
    um1 = ubuf[7:7 + TM, :]
    um2 = ubuf[6:6 + TM, :]
    y_ref[:, 0:CW] = _conv_out(bgate, u, um1, um2, cw_ref, cb_ref, ncv_ref).astype(BF16)
    ubuf[6:8, :] = u[TM - 2:TM, :]

    qt, kt, kend, bl = _gla_prep(p_ref, wgu_ref, bg_ref, TM, CH)
    causal = _iota2((2 * CH, CH), 0) % CH >= _iota2((2 * CH, CH), 1)
    first_head = _iota2((CH, 2 * DK), 1) < DK
    ngl = ngl_ref[...]

    def pair_stack(x):
        return jnp.concatenate([jnp.where(first_head, x, 0.0), jnp.where(first_head, 0.0, x)], axis=0).astype(BF16)

    for c in range(TM // CH):
        r0 = c * CH
        for m in range(H // 2):
            lanes = slice(m * 2 * DK, (m + 1) * 2 * DK)
            lhs = pair_stack(qt[r0:r0 + CH, lanes])
            sc = lax.dot_general(lhs, kt[r0:r0 + CH, lanes].astype(BF16), (((1,), (1,)), ((), ())),
                                 preferred_element_type=F32)
            sc = jnp.where(causal, sc, 0.0).astype(BF16)
            st = st_ref[:, lanes]
            o_inter = lax.dot_general(lhs, st.astype(BF16), (((1,), (1,)), ((), ())),
                                      preferred_element_type=F32)
            vs = []
            for hh in range(2):
                h = 2 * m + hh
                v_h = p_ref[r0:r0 + CH, V_OFF + h * DV:V_OFF + (h + 1) * DV].astype(BF16)
                g_h = p_ref[r0:r0 + CH, G_OFF + h * DV:G_OFF + (h + 1) * DV]
                vs.append(v_h)
                o_h = (jnp.dot(sc[hh * CH:(hh + 1) * CH], v_h, preferred_element_type=F32)
                       + o_inter[hh * CH:(hh + 1) * CH])
                y_ref[r0:r0 + CH, CW + h * DV:CW + (h + 1) * DV] = _gla_out(o_h, g_h, ngl).astype(BF16)
            kv_t = lax.dot_general(jnp.concatenate(vs, axis=0), pair_stack(kend[r0:r0 + CH, lanes]),
                                   (((0,), (0,)), ((), ())), preferred_element_type=F32)
            st_ref[:, lanes] = st * jnp.exp(bl[r0:r0 + 1, lanes]) + kv_t

    also()

    @pl.when(j == TILES_PER_SEQ - 1)
    def _():
        nconv_ref[0] = ubuf[6:8, :]
        nst_ref[0] = st_ref[...]


def _inmix_kernel(xp_ref, xs_ref, shp_ref, shs_ref, scp_ref, scs_ref, g_ref, wt_hbm, wa_ref,
                  cw_ref, cb_ref, wgu_ref, bg_ref, ncv_ref, ngl_ref,
                  y_ref, nconv_ref, nst_ref, projs_hbm,
                  h_scr, w_scr, proj, ubuf, st_ref, wsem, psem):
    i = pl.program_id(0)
    par = i % 2

    @pl.when(i == 0)
    def _():
        def chunk(c):
            return pltpu.make_async_copy(wt_hbm.at[0, pl.ds(c * W_CHUNK, W_CHUNK), :],
                                         proj.at[c % 2, :, pl.ds(0, D)], wsem.at[c % 2])

        chunk(0).start()
        for c in range(N_W_CHUNKS):
            if c + 1 < N_W_CHUNKS:
                chunk(c + 1).start()
            chunk(c).wait()
            w_scr[c * W_CHUNK:(c + 1) * W_CHUNK, :] = proj[c % 2, :, 0:D].astype(BF16)

    @pl.when(i < N_TILES_P)
    def _():
        h = _rms(xp_ref[...], g_ref[...]) * (1.0 + _prompt_row(scp_ref)) + _prompt_row(shp_ref)
        h_scr[...] = h.astype(BF16)

    @pl.when(i >= N_TILES_P)
    def _():
        h = _rms(xs_ref[...], g_ref[...]) * (1.0 + _sample_rows(scs_ref)) + _sample_rows(shs_ref)
        h_scr[...] = h.astype(BF16)

    def project():
        h = h_scr[...]
        dst = proj.at[par]
        dst[:, 0:PROJ_MAIN] = lax.dot_general(h, w_scr[...], NT_DIMS, preferred_element_type=F32)
        dst[:, PROJ_MAIN:PROJ_PAD] = lax.dot_general(h, wa_ref[...], NT_DIMS, preferred_element_type=F32)

    has_mixer = (i >= 1) & (i <= N_TILES_P)

    @pl.when(has_mixer)
    def _():
        _mixer_p_tile((i - 1) % TILES_PER_SEQ, proj.at[1 - par], cw_ref, cb_ref, wgu_ref, bg_ref, ncv_ref, ngl_ref,
                      y_ref, nconv_ref, nst_ref, ubuf, st_ref, project)

    @pl.when(jnp.logical_not(has_mixer))
    def _():
        project()

    @pl.when(i >= N_TILES_P)
    def _():
        cp = pltpu.make_async_copy(proj.at[par], projs_hbm.at[pl.ds((i - N_TILES_P) * TM, TM), :], psem)
        cp.start()
        cp.wait()


def _inmix(xp, xs, mod, g_mix, w_in_t, w_low_t, conv_w, conv_b, wgu_b, b_gate, n_conv, n_gla):
    def prev_tile(i):
        return jnp.clip(i - 1, 0, N_TILES_P - 1)

    return pl.pallas_call(
        _inmix_kernel,
        out_shape=(jax.ShapeDtypeStruct((T_P, D), BF16),
                   jax.ShapeDtypeStruct((B_P, 2, CW), F32),
                   jax.ShapeDtypeStruct((B_P, DV, HK), F32),
                   jax.ShapeDtypeStruct((T_S, PROJ_PAD), F32)),
        grid=(N_TILES,),
        in_specs=[pl.BlockSpec((TM, D), lambda i: (_p_tile(i), 0)),
                  pl.BlockSpec((TM, D), lambda i: (_s_tile(i), 0)),
                  *_mod_specs(0), *_mod_specs(1),
                  _const_spec((1, D)), pl.BlockSpec(memory_space=pl.ANY), _const_spec((128, D)),
                  _const_spec((3, CW)), _const_spec((1, CW)), _const_spec((128, HK)), _const_spec((1, HK)),
                  _const_spec((1, CW)), _const_spec((1, DV))],
        out_specs=(pl.BlockSpec((TM, D), lambda i: (prev_tile(i), 0)),
                   pl.BlockSpec((1, 2, CW), lambda i: (prev_tile(i) // TILES_PER_SEQ, 0, 0)),
                   pl.BlockSpec((1, DV, HK), lambda i: (prev_tile(i) // TILES_PER_SEQ, 0, 0)),
                   pl.BlockSpec(memory_space=pl.ANY)),
        scratch_shapes=[pltpu.VMEM((TM, D), BF16), pltpu.VMEM((PROJ_MAIN, D), BF16),
                        pltpu.VMEM((2, TM, PROJ_PAD), F32),
                        pltpu.VMEM((8 + TM, CW), F32), pltpu.VMEM((DV, HK), F32),
                        pltpu.SemaphoreType.DMA((2,)), pltpu.SemaphoreType.DMA(())],
        compiler_params=_cparams(),
        name="inmix",
    )(xp, xs, mod, mod, mod, mod, g_mix, w_in_t, w_low_t, conv_w, conv_b, wgu_b, b_gate, n_conv, n_gla)


def _mixer_s_kernel(p_ref, sconv_ref, sst_ref, cw_ref, cb_ref, wgu_ref, bg_ref, ncv_ref, ngl_ref,
                    y_ref, nconv_ref, nst_ref, ubuf):
    nseq = SEQ_PER_CHUNK
    bgate = p_ref[:, 0:CW]
    u = p_ref[:, CW:2 * CW] * p_ref[:, 2 * CW:3 * CW]
    ubuf[0:8, :] = jnp.zeros((8, CW), F32)
    ubuf[8:8 + CH, :] = u
    tpos = _iota2((CH, CW), 0) % L_S
    s0 = jnp.concatenate([jnp.broadcast_to(sconv_ref[s, 0:1, :], (L_S, CW)) for s in range(nseq)], axis=0)
    s1 = jnp.concatenate([jnp.broadcast_to(sconv_ref[s, 1:2, :], (L_S, CW)) for s in range(nseq)], axis=0)
    um1 = jnp.where(tpos == 0, s1, ubuf[7:7 + CH, :])
    um2 = jnp.where(tpos == 0, s0, jnp.where(tpos == 1, s1, ubuf[6:6 + CH, :]))
    y_ref[:, 0:CW] = _conv_out(bgate, u, um1, um2, cw_ref, cb_ref, ncv_ref).astype(BF16)
    for s in range(nseq):
        nconv_ref[s] = u[s * L_S + L_S - 2:(s + 1) * L_S, :]

    qt, kt, kend, bl = _gla_prep(p_ref, wgu_ref, bg_ref, CH, L_S)
    rr = _iota2((H * CH, CH), 0) % CH
    cc = _iota2((H * CH, CH), 1)
    causal = (rr >= cc) & ((rr // L_S) == (cc // L_S))
    ngl = ngl_ref[...]
    lhs_f = _head_stack(qt)
    sc = lax.dot_general(lhs_f.astype(BF16), kt.astype(BF16), (((1,), (1,)), ((), ())),
                         preferred_element_type=F32)
    sc = jnp.where(causal, sc, 0.0).astype(BF16)
    kstack_f = _head_stack(kend)
    v_all = p_ref[:, V_OFF:V_OFF + HV]
    decay_t = jnp.exp(jnp.concatenate([bl, jnp.zeros((128 - CH, HK), F32)], axis=0).T)
    o_inter = []
    for s in range(nseq):
        rs = [slice(h * CH + s * L_S, h * CH + (s + 1) * L_S) for h in range(H)]
        lhs_s = jnp.concatenate([lhs_f[r] for r in rs], axis=0).astype(BF16)
        k_s = jnp.concatenate([kstack_f[r] for r in rs], axis=0).astype(BF16)
        v_s = jnp.concatenate([v_all[s * L_S:(s + 1) * L_S, h * DV:(h + 1) * DV] for h in range(H)],
                              axis=0).astype(BF16)
        st = sst_ref[s]
        o_inter.append(jnp.dot(lhs_s, st.astype(BF16), preferred_element_type=F32))
        kv = lax.dot_general(k_s, v_s, (((0,), (0,)), ((), ())), preferred_element_type=F32)
        nst_ref[s] = st * decay_t[:, s * L_S:s * L_S + 1] + kv
    for h in range(H):
        v_h = v_all[:, h * DV:(h + 1) * DV].astype(BF16)
        g_h = p_ref[:, G_OFF + h * DV:G_OFF + (h + 1) * DV]
        oi_h = jnp.concatenate([o_inter[s][h * L_S:(h + 1) * L_S] for s in range(nseq)], axis=0)
        o_h = jnp.dot(sc[h * CH:(h + 1) * CH], v_h, preferred_element_type=F32) + oi_h
        y_ref[:, CW + h * DV:CW + (h + 1) * DV] = _gla_out(o_h, g_h, ngl).astype(BF16)


def _mixer_s(proj, sconv, sst, conv_w, conv_b, wgu_b, b_gate, n_conv, n_gla):
    nseq = SEQ_PER_CHUNK
    return pl.pallas_call(
        _mixer_s_kernel,
        out_shape=(jax.ShapeDtypeStruct((T_S, D), BF16),
                   jax.ShapeDtypeStruct((B_S, 2, CW), F32),
                   jax.ShapeDtypeStruct((B_S, HK, DV), F32)),
        grid=(T_S // CH,),
        in_specs=[pl.BlockSpec((CH, PROJ_PAD), lambda i: (i, 0)),
                  pl.BlockSpec((nseq, 2, CW), lambda i: (i, 0, 0)),
                  pl.BlockSpec((nseq, HK, DV), lambda i: (i, 0, 0)),
                  _const_spec((3, CW)), _const_spec((1, CW)), _const_spec((128, HK)), _const_spec((1, HK)),
                  _const_spec((1, CW)), _const_spec((1, DV))],
        out_specs=(pl.BlockSpec((CH, D), lambda i: (i, 0)),
                   pl.BlockSpec((nseq, 2, CW), lambda i: (i, 0, 0)),
                   pl.BlockSpec((nseq, HK, DV), lambda i: (i, 0, 0))),
        scratch_shapes=[pltpu.VMEM((8 + CH, CW), F32)],
        compiler_params=_cparams(),
        name="mixer_s",
    )(proj, sconv, sst, conv_w, conv_b, wgu_b, b_gate, n_conv, n_gla)


def _route(h2, wr_ref, br_ref, run_ref, pos_ref, rw_ref, cnt_ref):
    lt = lax.dot_general(wr_ref[...], h2.astype(BF16), (((1,), (1,)), ((), ())),
                         preferred_element_type=F32) + br_ref[...]

    coarse = lt[0:N_GROUPS]
    cmax = jnp.max(coarse, axis=0, keepdims=True)
    gi = _iota2((N_GROUPS, TM), 0)
    grp = jnp.min(jnp.where(coarse == cmax, gi, N_GROUPS), axis=0, keepdims=True)
    p_sel = 1.0 / jnp.sum(jnp.exp(coarse - cmax), axis=0, keepdims=True)
    fine = jnp.zeros((EPG, TM), F32)
    for g in range(N_GROUPS):
        fine = jnp.where(grp == g, lt[8 + g * EPG:8 + (g + 1) * EPG], fine)
    ei = _iota2((EPG, TM), 0)
    f1 = jnp.max(fine, axis=0, keepdims=True)
    i1 = jnp.min(jnp.where(fine == f1, ei, EPG), axis=0, keepdims=True)
    rest = jnp.where(ei == i1, -jnp.inf, fine)
    f2 = jnp.max(rest, axis=0, keepdims=True)
    i2 = jnp.min(jnp.where(rest == f2, ei, EPG), axis=0, keepdims=True)
    e2 = jnp.exp(f2 - f1)
    w1 = p_sel / (1.0 + e2)
    w2 = p_sel * e2 / (1.0 + e2)
    x1 = grp * EPG + i1
    x2 = grp * EPG + i2

    er = _iota2((NE, TM), 0)
    oh1 = jnp.where(er == x1, 1.0, 0.0)
    oh2 = jnp.where(er == x2, 1.0, 0.0)
    oh = oh1 + oh2
    before = jnp.where(_iota2((TM, TM), 0) < _iota2((TM, TM), 1), 1.0, 0.0).astype(BF16)
    cum = jnp.dot(oh.astype(BF16), before, preferred_element_type=F32) + run_ref[:, 0:1]
    r1 = jnp.sum(oh1 * cum, axis=0, keepdims=True).astype(I32)
    r2 = jnp.sum(oh2 * cum, axis=0, keepdims=True).astype(I32)
    run_new = run_ref[...] + jnp.sum(oh, axis=1, keepdims=True)
    run_ref[...] = run_new
    cnt_ref[...] = run_new.astype(I32)

    row = _iota2((8, TM), 0)
    pos_ref[...] = jnp.where(row == 0, x1 * E_CAP + r1, jnp.where(row == 1, x2 * E_CAP + r2, 0))
    wrow = _iota2((128, TM), 0)
    rw_ref[...] = jnp.where(wrow == 0, w1, jnp.where(wrow == 1, w2, 0.0)).T


def _outproj_kernel(ycp_ref, ycs_ref, xp_ref, xs_ref, gtp_ref, gts_ref, shp_ref, shs_ref, scp_ref, scs_ref,
                    wo_hbm, g_ref, wr_ref, br_ref,
                    x1_ref, pos_ref, rw_ref, cnt_ref, xs_out,
                    run_ref, h2_scr, wo_ref, pos_v, pos_s, sem, psem, wsem):
    i = pl.program_id(0)
    par = i % 2

    @pl.when(i == 0)
    def _():
        def chunk(c):
            return pltpu.make_async_copy(wo_hbm.at[0, pl.ds(c * TM, TM), :], h2_scr.at[c % 2], wsem.at[c % 2])

        chunk(0).start()
        for c in range(D // TM):
            if c + 1 < D // TM:
                chunk(c + 1).start()
            chunk(c).wait()
            wo_ref[c * TM:(c + 1) * TM, :] = h2_scr[c % 2].astype(BF16)

    to_smem = pltpu.make_async_copy(pos_v, pos_s, psem)

    def scatter(p):
        for r in range(TM):
            for slot in range(2):
                pltpu.make_async_copy(h2_scr.at[p, pl.ds(r, 1), :], xs_out.at[pl.ds(pos_s[slot, r], 1), :],
                                      sem).start(priority=r % 2)

    def drain():
        for _ in range(2):
            pltpu.make_async_copy(h2_scr.at[0], xs_out.at[pl.ds(0, TM), :], sem).wait()

    @pl.when(i == 0)
    def _():
        run_ref[...] = jnp.zeros((NE, 128), F32)
        h2_scr[1] = jnp.zeros((TM, D), F32)
        pos_v[...] = E_ROWS + _iota2((8, TM), 0) * TM + _iota2((8, TM), 1)
        to_smem.start()

    to_smem.wait()

    def tile(yc, x, gt, sh, sc):
        scatter(1 - par)
        mix = jnp.dot(yc, wo_ref[...], preferred_element_type=F32)
        x1 = x + gt * mix
        x1_ref[...] = x1
        h2 = _rms(x1, g_ref[...]) * (1.0 + sc) + sh
        h2_scr[par] = h2
        _route(h2, wr_ref, br_ref, run_ref, pos_v, rw_ref, cnt_ref)

    @pl.when(i < N_TILES_P)
    def _():
        tile(ycp_ref[...], xp_ref[...], _prompt_row(gtp_ref), _prompt_row(shp_ref), _prompt_row(scp_ref))

    @pl.when(i >= N_TILES_P)
    def _():
        tile(ycs_ref[...], xs_ref[...], _sample_rows(gts_ref), _sample_rows(shs_ref), _sample_rows(scs_ref))

    pos_ref[0] = pos_v[...]
    drain()
    to_smem.start()

    @pl.when(i == N_TILES - 1)
    def _():
        to_smem.wait()
        scatter(par)
        drain()


def _outproj(ycp, ycs, xp, xs, mod, w_out, g_ffn, w_r, b_r):
    return pl.pallas_call(
        _outproj_kernel,
        out_shape=(jax.ShapeDtypeStruct((T, D), F32), jax.ShapeDtypeStruct((N_TILES, 8, TM), I32),
                   jax.ShapeDtypeStruct((T, 128), F32), jax.ShapeDtypeStruct((NE, 128), I32),
                   jax.ShapeDtypeStruct((E_ROWS + 2 * TM, D), F32)),
        grid=(N_TILES,),
        in_specs=[pl.BlockSpec((TM, D), lambda i: (_p_tile(i), 0)), pl.BlockSpec((TM, D), lambda i: (_s_tile(i), 0)),
                  pl.BlockSpec((TM, D), lambda i: (_p_tile(i), 0)), pl.BlockSpec((TM, D), lambda i: (_s_tile(i), 0)),
                  *_mod_specs(2), *_mod_specs(3), *_mod_specs(4),
                  pl.BlockSpec(memory_space=pl.ANY), _const_spec((1, D)), _const_spec((R_ROWS, D)),
                  _const_spec((R_ROWS, 1))],
        out_specs=(pl.BlockSpec((TM, D), lambda i: (i, 0)), pl.BlockSpec((1, 8, TM), lambda i: (i, 0, 0)),
                   pl.BlockSpec((TM, 128), lambda i: (i, 0)), pl.BlockSpec((NE, 128), lambda i: (0, 0)),
                   pl.BlockSpec(memory_space=pl.ANY)),
        scratch_shapes=[pltpu.VMEM((NE, 128), F32), pltpu.VMEM((2, TM, D), F32), pltpu.VMEM((D, D), BF16),
                        pltpu.VMEM((8, TM), I32), pltpu.SMEM((8, TM), I32), pltpu.SemaphoreType.DMA(()),
                        pltpu.SemaphoreType.DMA(()), pltpu.SemaphoreType.DMA((2,))],
        compiler_params=_cparams(),
        name="outproj",
    )(ycp, ycs, xp, xs, mod, mod, mod, mod, mod, mod, w_out, g_ffn, w_r, b_r)


E_CHUNK = 64
E_CHUNKS = E_TILE // E_CHUNK


def _moe_kernel(tb_ref, te_ref, first_ref, nval_ref, nxt_ref, slot_ref, nt_ref,
                x_hbm, wg_hbm, wu_hbm, wd_hbm, y_hbm,
                xbuf, ybuf, wg_st, wu_st, wd_st, wgu_b, wd_b, wsem, xsem, ysem):
    i = pl.program_id(0)
    par = i % 2
    nt = nt_ref[0]

    def weight_copies(e, s):
        return (pltpu.make_async_copy(wg_hbm.at[e], wg_st.at[s], wsem.at[s]),
                pltpu.make_async_copy(wu_hbm.at[e], wu_st.at[s], wsem.at[s]),
                pltpu.make_async_copy(wd_hbm.at[e], wd_st.at[s], wsem.at[s]))

    def tile_chunks(t, p, output, fn):
        row0 = tb_ref[t] * E_TILE
        for c in range(E_CHUNKS):
            rows = pl.ds(c * E_CHUNK, E_CHUNK)
            hbm_rows = pl.ds(row0 + c * E_CHUNK, E_CHUNK)

            @pl.when(c * E_CHUNK < nval_ref[t])
            def _():
                if output:
                    fn(pltpu.make_async_copy(ybuf.at[p, rows, :], y_hbm.at[hbm_rows, 0], ysem.at[p]))
                else:
                    fn(pltpu.make_async_copy(x_hbm.at[hbm_rows, :], xbuf.at[p, rows, :], xsem.at[p]))

    def start(cp):
        cp.start()

    def wait(cp):
        cp.wait()

    @pl.when(i == 0)
    def _():
        tile_chunks(0, 0, False, start)

    @pl.when(i + 1 < nt)
    def _():
        tile_chunks(i + 1, 1 - par, False, start)

    @pl.when(i < nt)
    def _():
        tile_chunks(i, par, False, wait)

        @pl.when(first_ref[i] == 1)
        def _():
            s = slot_ref[i]

            @pl.when(i == 0)
            def _():
                for cp in weight_copies(te_ref[0], 0):
                    cp.start(priority=1)

            for cp in weight_copies(te_ref[i], s):
                cp.wait()

            @pl.when(nxt_ref[i] >= 0)
            def _():
                for cp in weight_copies(nxt_ref[i], 1 - s):
                    cp.start(priority=1)

            wgu_b[:, 0:DE] = wg_st[s].astype(BF16)
            wgu_b[:, DE:2 * DE] = wu_st[s].astype(BF16)
            wd_b[...] = wd_st[s].astype(BF16)

        valid = _iota2((E_TILE, D), 0) < nval_ref[i]
        x = jnp.where(valid, xbuf[par], 0.0).astype(BF16)
        ab = jnp.dot(x, wgu_b[...], preferred_element_type=F32)
        hid = (_silu(ab[:, 0:DE]) * ab[:, DE:2 * DE]).astype(BF16)
        ybuf[par] = jnp.dot(hid, wd_b[...], preferred_element_type=F32)

        tile_chunks(i, par, True, start)

        @pl.when(i >= 1)
        def _():
            tile_chunks(i - 1, 1 - par, True, wait)

        @pl.when(i == nt - 1)
        def _():
            tile_chunks(i, par, True, wait)


def _moe(plan, xs, w_eg, w_eu, w_ed):
    grid_spec = pltpu.PrefetchScalarGridSpec(
        num_scalar_prefetch=len(plan),
        grid=(E_TILES_MAX,),
        in_specs=[pl.BlockSpec(memory_space=pl.ANY)] * 4,
        out_specs=pl.BlockSpec(memory_space=pl.ANY),
        scratch_shapes=[pltpu.VMEM((2, E_TILE, D), F32), pltpu.VMEM((2, E_TILE, D), F32),
                        pltpu.VMEM((2, D, DE), F32), pltpu.VMEM((2, D, DE), F32), pltpu.VMEM((2, DE, D), F32),
                        pltpu.VMEM((D, 2 * DE), BF16), pltpu.VMEM((DE, D), BF16),
                        pltpu.SemaphoreType.DMA((2,)), pltpu.SemaphoreType.DMA((2,)), pltpu.SemaphoreType.DMA((2,))],
    )
    return pl.pallas_call(
        _moe_kernel,
        out_shape=jax.ShapeDtypeStruct((E_ROWS, 1, D), F32),
        grid_spec=grid_spec,
        compiler_params=_cparams(),
        name="moe",
    )(*plan, xs, w_eg, w_eu, w_ed)


def _plan(counts):
    ntile = (counts + E_TILE - 1) // E_TILE
    tend = jnp.cumsum(ntile)
    tbeg = tend - ntile
    n_tiles = tend[-1]
    tid = jnp.minimum(jnp.arange(E_TILES_MAX, dtype=I32), n_tiles - 1)
    tile_expert = jnp.minimum(jnp.sum((tend[None, :] <= tid[:, None]).astype(I32), axis=1), NE - 1)
    tile_in_expert = tid - tbeg[tile_expert]
    tile_block = tile_expert * E_CAP_TILES + tile_in_expert
    tile_first = (tile_in_expert == 0).astype(I32)
    tile_nvalid = jnp.minimum(counts[tile_expert] - tile_in_expert * E_TILE, E_TILE)
    ids = jnp.arange(NE, dtype=I32)
    occupied = ntile > 0
    later = occupied[None, :] & (ids[None, :] > ids[:, None])
    next_expert = jnp.where(jnp.any(later, axis=1), jnp.min(jnp.where(later, ids[None, :], NE), axis=1), -1)
    rank = jnp.cumsum(occupied.astype(I32)) - 1
    return (tile_block, tile_expert, tile_first, tile_nvalid, next_expert[tile_expert].astype(I32),
            (rank[tile_expert] % 2).astype(I32), n_tiles.reshape(1))


def _final_kernel(pos0_ref, posn_ref, x1_ref, rw_ref, gtp_ref, gts_ref, g_ref, ys_ref, yp_ref, ysm_ref, buf, sem):
    i = pl.program_id(0)
    par = i % 2

    def gather(p_ref, p):
        for r in range(TM):
            for slot in range(2):
                pltpu.make_async_copy(ys_ref.at[p_ref[0, slot, r]], buf.at[p, slot, pl.ds(r, 1), :],
                                      sem.at[p]).start(priority=r % 2)

    @pl.when(i == 0)
    def _():
        gather(pos0_ref, 0)

    @pl.when(i < N_TILES - 1)
    def _():
        gather(posn_ref, 1 - par)

    for slot in range(2):
        pltpu.make_async_copy(ys_ref.at[pl.ds(0, TM), 0], buf.at[par, slot], sem.at[par]).wait()

    moe = rw_ref[:, 0:1] * buf[par, 0] + rw_ref[:, 1:2] * buf[par, 1]

    @pl.when(i < N_TILES_P)
    def _():
        yp_ref[...] = _rms(x1_ref[...] + _prompt_row(gtp_ref) * moe, g_ref[...])

    @pl.when(i >= N_TILES_P)
    def _():
        ysm_ref[...] = _rms(x1_ref[...] + _sample_rows(gts_ref) * moe, g_ref[...])


def _final(pos, x1, rw, mod, g_fin, ys):
    return pl.pallas_call(
        _final_kernel,
        out_shape=(jax.ShapeDtypeStruct((T_P, D), F32), jax.ShapeDtypeStruct((T_S, D), F32)),
        grid=(N_TILES,),
        in_specs=[pl.BlockSpec((1, 8, TM), lambda i: (0, 0, 0), memory_space=pltpu.SMEM),
                  pl.BlockSpec((1, 8, TM), lambda i: (jnp.minimum(i + 1, N_TILES - 1), 0, 0),
                               memory_space=pltpu.SMEM),
                  pl.BlockSpec((TM, D), lambda i: (i, 0)),
                  pl.BlockSpec((TM, 128), lambda i: (i, 0)),
                  *_mod_specs(5),
                  _const_spec((1, D)),
                  pl.BlockSpec(memory_space=pl.ANY)],
        out_specs=(pl.BlockSpec((TM, D), lambda i: (_p_tile(i), 0)),
                   pl.BlockSpec((TM, D), lambda i: (_s_tile(i), 0))),
        scratch_shapes=[pltpu.VMEM((2, 2, TM, D), F32), pltpu.SemaphoreType.DMA((2,))],
        compiler_params=_cparams(),
        name="final",
    )(pos, pos, x1, rw, mod, mod, g_fin, ys)


def kernel(x_prompt, x_sample, c_prompt, c_sample, state_conv, state_gla, w_ada, b_ada, norm_mix, w_in, conv_w,
           conv_b, w_gate_up, b_gate, norm_conv, norm_gla, w_out, norm_ffn, w_coarse, b_coarse, w_fine, b_fine,
           w_exp_gate, w_exp_up, w_exp_down, norm_final):
    xp = x_prompt.reshape(T_P, D)
    xs = x_sample.reshape(T_S, D)
    c_all = jnp.concatenate([c_sample, c_prompt, jnp.zeros((C_ROWS - B_S - B_P, D), F32)], axis=0)
    mod = _ada(c_all, w_ada[0], b_ada[0][None, :])

    w_in_t = jnp.swapaxes(w_in, 1, 2)
    w_low_t = jnp.pad(w_in_t[0, PROJ_MAIN:, :], ((0, 128 - RANK), (0, 0))).astype(BF16)
    wgu_b = jnp.pad(w_gate_up[0], ((0, 128 - RANK), (0, 0))).astype(BF16)
    w_r = jnp.concatenate([w_coarse[0].T, jnp.zeros((8 - N_GROUPS, D), F32), w_fine[0].T], axis=0)
    b_r = jnp.concatenate([b_coarse[0], jnp.zeros((8 - N_GROUPS,), F32), b_fine[0]])[:, None]
    g_mix, g_ffn, g_fin = norm_mix[0][None, :], norm_ffn[0][None, :], norm_final[None, :]
    mix_consts = (conv_w[0], conv_b[0][None, :], wgu_b, b_gate[0][None, :], norm_conv[0][None, :],
                  norm_gla[0][None, :])

    ycat_p, nconv_p, nst_p, proj_s = _inmix(xp, xs, mod, g_mix, w_in_t, w_low_t, *mix_consts)
    ycat_s, nconv_s, nst_s = _mixer_s(proj_s, state_conv[0], state_gla[0].reshape(B_S, HK, DV), *mix_consts)

    x1, pos, rw, counts, x_sorted = _outproj(ycat_p, ycat_s, xp, xs, mod, w_out, g_ffn, w_r.astype(BF16), b_r)
    y_sorted = _moe(_plan(counts[:, 0]), x_sorted, w_exp_gate[0], w_exp_up[0], w_exp_down[0])
    y_p, y_s = _final(pos, x1, rw, mod, g_fin, y_sorted)

    new_gla_p = nst_p.reshape(B_P, DV, H, DK).transpose(0, 2, 3, 1)[None]
    new_gla_s = nst_s.reshape(1, B_S, H, DK, DV)
    return (y_p.reshape(B_P, L_P, D), y_s.reshape(B_S, L_S, D), nconv_p[None], new_gla_p,
            nconv_s[None], new_gla_s)
```

```python
import functools

import jax
import jax.numpy as jnp
from jax import lax
from jax.experimental import pallas as pl
from jax.experimental.pallas import tpu as pltpu

F32 = jnp.float32
BF16 = jnp.bfloat16
I32 = jnp.int32

D = 2048
CW = 1024
H = 8
DK = 64
DV = 128
HK = H * DK
HV = H * DV
RANK = 16
TAU = 16.0
N_GROUPS = 4
EPG = 8
NE = N_GROUPS * EPG
DE = 512
EPS = 1e-6

B_P, L_P = 4, 2048
B_S, L_S = 128, 8
T_P = B_P * L_P
T_S = B_S * L_S
T = T_P + T_S

TM = 256
CH = 64
N_TILES = T // TM
N_TILES_P = T_P // TM
TILES_PER_SEQ = L_P // TM
SEQ_PER_TILE = TM // L_S
SEQ_PER_CHUNK = CH // L_S

PROJ_MAIN = 3 * CW + 2 * HK + 2 * HV
PROJ_PAD = PROJ_MAIN + 128
C_ROWS = 136
P_ROW_BLOCK = B_S // 8
R_ROWS = 40

E_TILE = 256
E_TILES_MAX = (2 * T) // E_TILE + NE
E_CAP = T
E_CAP_TILES = E_CAP // E_TILE
E_ROWS = NE * E_CAP
VMEM_LIMIT = 60 * 1024 * 1024


def _cparams(n_axes=1, vmem=VMEM_LIMIT):
    return pltpu.CompilerParams(dimension_semantics=("arbitrary",) * n_axes, vmem_limit_bytes=vmem)


def _rms(x, g):
    return x * lax.rsqrt(jnp.mean(x * x, axis=-1, keepdims=True) + EPS) * g


def _sigmoid(x):
    return 1.0 / (1.0 + jnp.exp(-x))


def _silu(x):
    return x * _sigmoid(x)


def _log_sigmoid(x):
    return jnp.minimum(x, 0.0) - jnp.log(1.0 + jnp.exp(-jnp.abs(x)))


def _iota2(shape, axis):
    return lax.broadcasted_iota(I32, shape, axis)


def _expand_rows(ref, n, reps):
    return jnp.concatenate([jnp.broadcast_to(ref[j:j + 1, :], (reps, ref.shape[-1])) for j in range(n)], axis=0)


def _prompt_row(ref):
    return ref[pl.ds(pl.program_id(0) // TILES_PER_SEQ, 1), :]


def _sample_rows(ref):
    return _expand_rows(ref, SEQ_PER_TILE, L_S)


def _p_tile(i):
    return jnp.minimum(i, N_TILES_P - 1)


def _s_tile(i):
    return jnp.maximum(i - N_TILES_P, 0)


def _mod_specs(col):
    return [pl.BlockSpec((8, D), lambda i, *_: (P_ROW_BLOCK, col)),
            pl.BlockSpec((SEQ_PER_TILE, D), lambda i, *_: (_s_tile(i), col))]


def _const_spec(shape):
    zeros = (0,) * len(shape)
    return pl.BlockSpec(shape, lambda *_: zeros, pipeline_mode=pl.Buffered(1))


def _ada_kernel(c_ref, w_ref, b_ref, o_ref):
    s = _silu(c_ref[...]).astype(BF16)
    o_ref[...] = jnp.dot(s, w_ref[...].astype(BF16), preferred_element_type=F32) + b_ref[...]


def _ada(c_all, w_ada, b_ada):
    tn = 1024
    return pl.pallas_call(
        _ada_kernel,
        out_shape=jax.ShapeDtypeStruct((C_ROWS, 6 * D), F32),
        grid=(6 * D // tn,),
        in_specs=[pl.BlockSpec((C_ROWS, D), lambda j: (0, 0)),
                  pl.BlockSpec((D, tn), lambda j: (0, j)),
                  pl.BlockSpec((1, tn), lambda j: (0, j))],
        out_specs=pl.BlockSpec((C_ROWS, tn), lambda j: (0, j)),
        compiler_params=_cparams(),
        name="ada",
    )(c_all, w_ada, b_ada)


W_CHUNK = TM
N_W_CHUNKS = PROJ_MAIN // W_CHUNK
NT_DIMS = (((1,), (1,)), ((), ()))


def _gla_prep(p_ref, wgu_ref, bg_ref, rows, seg):
    q = p_ref[:, 3 * CW:3 * CW + HK]
    k = p_ref[:, 3 * CW + HK:3 * CW + 2 * HK]
    a = p_ref[:, PROJ_MAIN:PROJ_PAD].astype(BF16)
    z = jnp.dot(a, wgu_ref[...], preferred_element_type=F32) + bg_ref[...]
    la = _log_sigmoid(z) * (1.0 / TAU)
    la0 = la.astype(BF16)
    rem = la - la0.astype(F32)
    la1 = rem.astype(BF16)
    la2 = (rem - la1.astype(F32)).astype(BF16)
    r = _iota2((2 * rows, rows), 0)
    c = _iota2((2 * rows, rows), 1)
    same = ((r % rows) // seg) == (c // seg)
    sel = jnp.where(same & ((r >= rows) | (c <= r)), 1.0, 0.0).astype(BF16)
    sums = sum(jnp.dot(sel, part, preferred_element_type=F32) for part in (la0, la1, la2))
    b, bl = sums[0:rows], sums[rows:2 * rows]
    qt = q * jnp.exp(b) * (DK ** -0.5)
    kt = k * jnp.exp(-b)
    kend = kt * jnp.exp(bl)
    return qt, kt, kend, bl


def _head_stack(x):
    rows = x.shape[0]
    t = jnp.concatenate([x] * H, axis=0)
    keep = (_iota2((H * rows, HK), 0) // rows) == (_iota2((H * rows, HK), 1) // DK)
    return jnp.where(keep, t, 0.0)


def _gla_out(o_h, g_h, ngl):
    on = o_h * lax.rsqrt(jnp.mean(o_h * o_h, axis=-1, keepdims=True) + EPS) * ngl
    return on * _silu(g_h)


def _conv_out(bg, u, um1, um2, cw_ref, cb_ref, ncv_ref):
    conv_y = cb_ref[...] + cw_ref[0:1, :] * um2 + cw_ref[1:2, :] * um1 + cw_ref[2:3, :] * u
    return _rms(bg * conv_y, ncv_ref[...])


V_OFF = 3 * CW + 2 * HK
G_OFF = V_OFF + HV


def _mixer_p_tile(j, p_ref, cw_ref, cb_ref, wgu_ref, bg_ref, ncv_ref, ngl_ref,
                  y_ref, nconv_ref, nst_ref, ubuf, st_ref, also):
    @pl.when(j == 0)
    def _():
        ubuf[0:8, :] = jnp.zeros((8, CW), F32)
        st_ref[...] = jnp.zeros((DV, HK), F32)

    bgate = p_ref[:, 0:CW]
    u = p_ref[:, CW:2 * CW] * p_ref[:, 2 * CW:3 * CW]
    ubuf[8:8 + TM, :] = u
    um1 = ubuf[7:7 + TM, :]
    um2 = ubuf[6:6 + TM, :]
    y_ref[:, 0:CW] = _conv_out(bgate, u, um1, um2, cw_ref, cb_ref, ncv_ref).astype(BF16)
    ubuf[6:8, :] = u[TM - 2:TM, :]

    qt, kt, kend, bl = _gla_prep(p_ref, wgu_ref, bg_ref, TM, CH)
    causal = _iota2((2 * CH, CH), 0) % CH >= _iota2((2 * CH, CH), 1)
    first_head = _iota2((CH, 2 * DK), 1) < DK
    ngl = ngl_ref[...]

    def pair_stack(x):
        return jnp.concatenate([jnp.where(first_head, x, 0.0), jnp.where(first_head, 0.0, x)], axis=0).astype(BF16)

    for c in range(TM // CH):
        r0 = c * CH
        for m in range(H // 2):
            lanes = slice(m * 2 * DK, (m + 1) * 2 * DK)
            lhs = pair_stack(qt[r0:r0 + CH, lanes])
            sc = lax.dot_general(lhs, kt[r0:r0 + CH, lanes].astype(BF16), (((1,), (1,)), ((), ())),
                                 preferred_element_type=F32)
            sc = jnp.where(causal, sc, 0.0).astype(BF16)
            st = st_ref[:, lanes]
            o_inter = lax.dot_general(lhs, st.astype(BF16), (((1,), (1,)), ((), ())),
                                      preferred_element_type=F32)
            vs = []
            for hh in range(2):
                h = 2 * m + hh
                v_h = p_ref[r0:r0 + CH, V_OFF + h * DV:V_OFF + (h + 1) * DV].astype(BF16)
                g_h = p_ref[r0:r0 + CH, G_OFF + h * DV:G_OFF + (h + 1) * DV]
                vs.append(v_h)
                o_h = (jnp.dot(sc[hh * CH:(hh + 1) * CH], v_h, preferred_element_type=F32)
                       + o_inter[hh * CH:(hh + 1) * CH])
                y_ref[r0:r0 + CH, CW + h * DV:CW + (h + 1) * DV] = _gla_out(o_h, g_h, ngl).astype(BF16)
            kv_t = lax.dot_general(jnp.concatenate(vs, axis=0), pair_stack(kend[r0:r0 + CH, lanes]),
                                   (((0,), (0,)), ((), ())), preferred_element_type=F32)
            st_ref[:, lanes] = st * jnp.exp(bl[r0:r0 + 1, lanes]) + kv_t

    also()

    @pl.when(j == TILES_PER_SEQ - 1)
    def _():
        nconv_ref[0] = ubuf[6:8, :]
        nst_ref[0] = st_ref[...]


def _inmix_kernel(xp_ref, xs_ref, shp_ref, shs_ref, scp_ref, scs_ref, g_ref, wt_hbm, wa_ref,
                  cw_ref, cb_ref, wgu_ref, bg_ref, ncv_ref, ngl_ref,
                  y_ref, nconv_ref, nst_ref, projs_hbm,
                  h_scr, w_scr, proj, ubuf, st_ref, wsem, psem):
    i = pl.program_id(0)
    par = i % 2

    @pl.when(i == 0)
    def _():
        def chunk(c):
            return pltpu.make_async_copy(wt_hbm.at[0, pl.ds(c * W_CHUNK, W_CHUNK), :],
                                         proj.at[c % 2, :, pl.ds(0, D)], wsem.at[c % 2])

        chunk(0).start()
        for c in range(N_W_CHUNKS):
            if c + 1 < N_W_CHUNKS:
                chunk(c + 1).start()
            chunk(c).wait()
            w_scr[c * W_CHUNK:(c + 1) * W_CHUNK, :] = proj[c % 2, :, 0:D].astype(BF16)

    @pl.when(i < N_TILES_P)
    def _():
        h = _rms(xp_ref[...], g_ref[...]) * (1.0 + _prompt_row(scp_ref)) + _prompt_row(shp_ref)
        h_scr[...] = h.astype(BF16)

    @pl.when(i >= N_TILES_P)
    def _():
        h = _rms(xs_ref[...], g_ref[...]) * (1.0 + _sample_rows(scs_ref)) + _sample_rows(shs_ref)
        h_scr[...] = h.astype(BF16)

    def project():
        h = h_scr[...]
        dst = proj.at[par]
        dst[:, 0:PROJ_MAIN] = lax.dot_general(h, w_scr[...], NT_DIMS, preferred_element_type=F32)
        dst[:, PROJ_MAIN:PROJ_PAD] = lax.dot_general(h, wa_ref[...], NT_DIMS, preferred_element_type=F32)

    has_mixer = (i >= 1) & (i <= N_TILES_P)

    @pl.when(has_mixer)
    def _():
        _mixer_p_tile((i - 1) % TILES_PER_SEQ, proj.at[1 - par], cw_ref, cb_ref, wgu_ref, bg_ref, ncv_ref, ngl_ref,
                      y_ref, nconv_ref, nst_ref, ubuf, st_ref, project)

    @pl.when(jnp.logical_not(has_mixer))
    def _():
        project()

    @pl.when(i >= N_TILES_P)
    def _():
        cp = pltpu.make_async_copy(proj.at[par], projs_hbm.at[pl.ds((i - N_TILES_P) * TM, TM), :], psem)
        cp.start()
        cp.wait()


def _inmix(xp, xs, mod, g_mix, w_in_t, w_low_t, conv_w, conv_b, wgu_b, b_gate, n_conv, n_gla):
    def prev_tile(i):
        return jnp.clip(i - 1, 0, N_TILES_P - 1)

    return pl.pallas_call(
        _inmix_kernel,
        out_shape=(jax.ShapeDtypeStruct((T_P, D), BF16),
                   jax.ShapeDtypeStruct((B_P, 2, CW), F32),
                   jax.ShapeDtypeStruct((B_P, DV, HK), F32),
                   jax.ShapeDtypeStruct((T_S, PROJ_PAD), F32)),
        grid=(N_TILES,),
        in_specs=[pl.BlockSpec((TM, D), lambda i: (_p_tile(i), 0)),
                  pl.BlockSpec((TM, D), lambda i: (_s_tile(i), 0)),
                  *_mod_specs(0), *_mod_specs(1),
                  _const_spec((1, D)), pl.BlockSpec(memory_space=pl.ANY), _const_spec((128, D)),
                  _const_spec((3, CW)), _const_spec((1, CW)), _const_spec((128, HK)), _const_spec((1, HK)),
                  _const_spec((1, CW)), _const_spec((1, DV))],
        out_specs=(pl.BlockSpec((TM, D), lambda i: (prev_tile(i), 0)),
                   pl.BlockSpec((1, 2, CW), lambda i: (prev_tile(i) // TILES_PER_SEQ, 0, 0)),
                   pl.BlockSpec((1, DV, HK), lambda i: (prev_tile(i) // TILES_PER_SEQ, 0, 0)),
                   pl.BlockSpec(memory_space=pl.ANY)),
        scratch_shapes=[pltpu.VMEM((TM, D), BF16), pltpu.VMEM((PROJ_MAIN, D), BF16),
                        pltpu.VMEM((2, TM, PROJ_PAD), F32),
                        pltpu.VMEM((8 + TM, CW), F32), pltpu.VMEM((DV, HK), F32),
                        pltpu.SemaphoreType.DMA((2,)), pltpu.SemaphoreType.DMA(())],
        compiler_params=_cparams(),
        name="inmix",
    )(xp, xs, mod, mod, mod, mod, g_mix, w_in_t, w_low_t, conv_w, conv_b, wgu_b, b_gate, n_conv, n_gla)


def _mixer_s_kernel(p_ref, sconv_ref, sst_ref, cw_ref, cb_ref, wgu_ref, bg_ref, ncv_ref, ngl_ref,
                    y_ref, nconv_ref, nst_ref, ubuf):
    nseq = SEQ_PER_CHUNK
    bgate = p_ref[:, 0:CW]
    u = p_ref[:, CW:2 * CW] * p_ref[:, 2 * CW:3 * CW]
    ubuf[0:8, :] = jnp.zeros((8, CW), F32)
    ubuf[8:8 + CH, :] = u
    tpos = _iota2((CH, CW), 0) % L_S
    s0 = jnp.concatenate([jnp.broadcast_to(sconv_ref[s, 0:1, :], (L_S, CW)) for s in range(nseq)], axis=0)
    s1 = jnp.concatenate([jnp.broadcast_to(sconv_ref[s, 1:2, :], (L_S, CW)) for s in range(nseq)], axis=0)
    um1 = jnp.where(tpos == 0, s1, ubuf[7:7 + CH, :])
    um2 = jnp.where(tpos == 0, s0, jnp.where(tpos == 1, s1, ubuf[6:6 + CH, :]))
    y_ref[:, 0:CW] = _conv_out(bgate, u, um1, um2, cw_ref, cb_ref, ncv_ref).astype(BF16)
    for s in range(nseq):
        nconv_ref[s] = u[s * L_S + L_S - 2:(s + 1) * L_S, :]

    qt, kt, kend, bl = _gla_prep(p_ref, wgu_ref, bg_ref, CH, L_S)
    rr = _iota2((H * CH, CH), 0) % CH
    cc = _iota2((H * CH, CH), 1)
    causal = (rr >= cc) & ((rr // L_S) == (cc // L_S))
    ngl = ngl_ref[...]
    lhs_f = _head_stack(qt)
    sc = lax.dot_general(lhs_f.astype(BF16), kt.astype(BF16), (((1,), (1,)), ((), ())),
                         preferred_element_type=F32)
    sc = jnp.where(causal, sc, 0.0).astype(BF16)
    kstack_f = _head_stack(kend)
    v_all = p_ref[:, V_OFF:V_OFF + HV]
    decay_t = jnp.exp(jnp.concatenate([bl, jnp.zeros((128 - CH, HK), F32)], axis=0).T)
    o_inter = []
    for s in range(nseq):
        rs = [slice(h * CH + s * L_S, h * CH + (s + 1) * L_S) for h in range(H)]
        lhs_s = jnp.concatenate([lhs_f[r] for r in rs], axis=0).astype(BF16)
        k_s = jnp.concatenate([kstack_f[r] for r in rs], axis=0).astype(BF16)
        v_s = jnp.concatenate([v_all[s * L_S:(s + 1) * L_S, h * DV:(h + 1) * DV] for h in range(H)],
                              axis=0).astype(BF16)
        st = sst_ref[s]
        o_inter.append(jnp.dot(lhs_s, st.astype(BF16), preferred_element_type=F32))
        kv = lax.dot_general(k_s, v_s, (((0,), (0,)), ((), ())), preferred_element_type=F32)
        nst_ref[s] = st * decay_t[:, s * L_S:s * L_S + 1] + kv
    for h in range(H):
        v_h = v_all[:, h * DV:(h + 1) * DV].astype(BF16)
        g_h = p_ref[:, G_OFF + h * DV:G_OFF + (h + 1) * DV]
        oi_h = jnp.concatenate([o_inter[s][h * L_S:(h + 1) * L_S] for s in range(nseq)], axis=0)
        o_h = jnp.dot(sc[h * CH:(h + 1) * CH], v_h, preferred_element_type=F32) + oi_h
        y_ref[:, CW + h * DV:CW + (h + 1) * DV] = _gla_out(o_h, g_h, ngl).astype(BF16)


def _mixer_s(proj, sconv, sst, conv_w, conv_b, wgu_b, b_gate, n_conv, n_gla):
    nseq = SEQ_PER_CHUNK
    return pl.pallas_call(
        _mixer_s_kernel,
        out_shape=(jax.ShapeDtypeStruct((T_S, D), BF16),
                   jax.ShapeDtypeStruct((B_S, 2, CW), F32),
                   jax.ShapeDtypeStruct((B_S, HK, DV), F32)),
        grid=(T_S // CH,),
        in_specs=[pl.BlockSpec((CH, PROJ_PAD), lambda i: (i, 0)),
                  pl.BlockSpec((nseq, 2, CW), lambda i: (i, 0, 0)),
                  pl.BlockSpec((nseq, HK, DV), lambda i: (i, 0, 0)),
                  _const_spec((3, CW)), _const_spec((1, CW)), _const_spec((128, HK)), _const_spec((1, HK)),
                  _const_spec((1, CW)), _const_spec((1, DV))],
        out_specs=(pl.BlockSpec((CH, D), lambda i: (i, 0)),
                   pl.BlockSpec((nseq, 2, CW), lambda i: (i, 0, 0)),
                   pl.BlockSpec((nseq, HK, DV), lambda i: (i, 0, 0))),
        scratch_shapes=[pltpu.VMEM((8 + CH, CW), F32)],
        compiler_params=_cparams(),
        name="mixer_s",
    )(proj, sconv, sst, conv_w, conv_b, wgu_b, b_gate, n_conv, n_gla)


def _route(h2, wr_ref, br_ref, run_ref, pos_ref, rw_ref, cnt_ref):
    lt = lax.dot_general(wr_ref[...], h2.astype(BF16), (((1,), (1,)), ((), ())),
                         preferred_element_type=F32) + br_ref[...]

    coarse = lt[0:N_GROUPS]
    cmax = jnp.max(coarse, axis=0, keepdims=True)
    gi = _iota2((N_GROUPS, TM), 0)
    grp = jnp.min(jnp.where(coarse == cmax, gi, N_GROUPS), axis=0, keepdims=True)
    p_sel = 1.0 / jnp.sum(jnp.exp(coarse - cmax), axis=0, keepdims=True)
    fine = jnp.zeros((EPG, TM), F32)
    for g in range(N_GROUPS):
        fine = jnp.where(grp == g, lt[8 + g * EPG:8 + (g + 1) * EPG], fine)
    ei = _iota2((EPG, TM), 0)
    f1 = jnp.max(fine, axis=0, keepdims=True)
    i1 = jnp.min(jnp.where(fine == f1, ei, EPG), axis=0, keepdims=True)
    rest = jnp.where(ei == i1, -jnp.inf, fine)
    f2 = jnp.max(rest, axis=0, keepdims=True)
    i2 = jnp.min(jnp.where(rest == f2, ei, EPG), axis=0, keepdims=True)
    e2 = jnp.exp(f2 - f1)
    w1 = p_sel / (1.0 + e2)
    w2 = p_sel * e2 / (1.0 + e2)
    x1 = grp * EPG + i1
    x2 = grp * EPG + i2

    er = _iota2((NE, TM), 0)
    oh1 = jnp.where(er == x1, 1.0, 0.0)
    oh2 = jnp.where(er == x2, 1.0, 0.0)
    oh = oh1 + oh2
    before = jnp.where(_iota2((TM, TM), 0) < _iota2((TM, TM), 1), 1.0, 0.0).astype(BF16)
    cum = jnp.dot(oh.astype(BF16), before, preferred_element_type=F32) + run_ref[:, 0:1]
    r1 = jnp.sum(oh1 * cum, axis=0, keepdims=True).astype(I32)
    r2 = jnp.sum(oh2 * cum, axis=0, keepdims=True).astype(I32)
    run_new = run_ref[...] + jnp.sum(oh, axis=1, keepdims=True)
    run_ref[...] = run_new
    cnt_ref[...] = run_new.astype(I32)

    row = _iota2((8, TM), 0)
    pos_ref[...] = jnp.where(row == 0, x1 * E_CAP + r1, jnp.where(row == 1, x2 * E_CAP + r2, 0))
    wrow = _iota2((128, TM), 0)
    rw_ref[...] = jnp.where(wrow == 0, w1, jnp.where(wrow == 1, w2, 0.0)).T


def _outproj_kernel(ycp_ref, ycs_ref, xp_ref, xs_ref, gtp_ref, gts_ref, shp_ref, shs_ref, scp_ref, scs_ref,
                    wo_hbm, g_ref, wr_ref, br_ref,
                    x1_ref, pos_ref, rw_ref, cnt_ref, xs_out,
                    run_ref, h2_scr, wo_ref, pos_v, pos_s, sem, psem, wsem):
    i = pl.program_id(0)
    par = i % 2

    @pl.when(i == 0)
    def _():
        def chunk(c):
            return pltpu.make_async_copy(wo_hbm.at[0, pl.ds(c * TM, TM), :], h2_scr.at[c % 2], wsem.at[c % 2])

        chunk(0).start()
        for c in range(D // TM):
            if c + 1 < D // TM:
                chunk(c + 1).start()
            chunk(c).wait()
            wo_ref[c * TM:(c + 1) * TM, :] = h2_scr[c % 2].astype(BF16)

    to_smem = pltpu.make_async_copy(pos_v, pos_s, psem)

    def scatter(p):
        for r in range(TM):
            for slot in range(2):
                pltpu.make_async_copy(h2_scr.at[p, pl.ds(r, 1), :], xs_out.at[pl.ds(pos_s[slot, r], 1), :],
                                      sem).start(priority=1)

    def drain():
        for _ in range(2):
            pltpu.make_async_copy(h2_scr.at[0], xs_out.at[pl.ds(0, TM), :], sem).wait()

    @pl.when(i == 0)
    def _():
        run_ref[...] = jnp.zeros((NE, 128), F32)
        h2_scr[1] = jnp.zeros((TM, D), F32)
        pos_v[...] = E_ROWS + _iota2((8, TM), 0) * TM + _iota2((8, TM), 1)
        to_smem.start()

    to_smem.wait()

    def tile(yc, x, gt, sh, sc):
        scatter(1 - par)
        mix = jnp.dot(yc, wo_ref[...], preferred_element_type=F32)
        x1 = x + gt * mix
        x1_ref[...] = x1
        h2 = _rms(x1, g_ref[...]) * (1.0 + sc) + sh
        h2_scr[par] = h2
        _route(h2, wr_ref, br_ref, run_ref, pos_v, rw_ref, cnt_ref)

    @pl.when(i < N_TILES_P)
    def _():
        tile(ycp_ref[...], xp_ref[...], _prompt_row(gtp_ref), _prompt_row(shp_ref), _prompt_row(scp_ref))

    @pl.when(i >= N_TILES_P)
    def _():
        tile(ycs_ref[...], xs_ref[...], _sample_rows(gts_ref), _sample_rows(shs_ref), _sample_rows(scs_ref))

    pos_ref[0] = pos_v[...]
    drain()
    to_smem.start()

    @pl.when(i == N_TILES - 1)
    def _():
        to_smem.wait()
        scatter(par)
        drain()


def _outproj(ycp, ycs, xp, xs, mod, w_out, g_ffn, w_r, b_r):
    return pl.pallas_call(
        _outproj_kernel,
        out_shape=(jax.ShapeDtypeStruct((T, D), F32), jax.ShapeDtypeStruct((N_TILES, 8, TM), I32),
                   jax.ShapeDtypeStruct((T, 128), F32), jax.ShapeDtypeStruct((NE, 128), I32),
                   jax.ShapeDtypeStruct((E_ROWS + 2 * TM, D), F32)),
        grid=(N_TILES,),
        in_specs=[pl.BlockSpec((TM, D), lambda i: (_p_tile(i), 0)), pl.BlockSpec((TM, D), lambda i: (_s_tile(i), 0)),
                  pl.BlockSpec((TM, D), lambda i: (_p_tile(i), 0)), pl.BlockSpec((TM, D), lambda i: (_s_tile(i), 0)),
                  *_mod_specs(2), *_mod_specs(3), *_mod_specs(4),
                  pl.BlockSpec(memory_space=pl.ANY), _const_spec((1, D)), _const_spec((R_ROWS, D)),
                  _const_spec((R_ROWS, 1))],
        out_specs=(pl.BlockSpec((TM, D), lambda i: (i, 0)), pl.BlockSpec((1, 8, TM), lambda i: (i, 0, 0)),
                   pl.BlockSpec((TM, 128), lambda i: (i, 0)), pl.BlockSpec((NE, 128), lambda i: (0, 0)),
                   pl.BlockSpec(memory_space=pl.ANY)),
        scratch_shapes=[pltpu.VMEM((NE, 128), F32), pltpu.VMEM((2, TM, D), F32), pltpu.VMEM((D, D), BF16),
                        pltpu.VMEM((8, TM), I32), pltpu.SMEM((8, TM), I32), pltpu.SemaphoreType.DMA(()),
                        pltpu.SemaphoreType.DMA(()), pltpu.SemaphoreType.DMA((2,))],
        compiler_params=_cparams(),
        name="outproj",
    )(ycp, ycs, xp, xs, mod, mod, mod, mod, mod, mod, w_out, g_ffn, w_r, b_r)


E_CHUNK = 64
E_CHUNKS = E_TILE // E_CHUNK


def _moe_kernel(tb_ref, te_ref, first_ref, nval_ref, nxt_ref, slot_ref, nt_ref,
                x_hbm, wg_hbm, wu_hbm, wd_hbm, y_hbm,
                xbuf, ybuf, wg_st, wu_st, wd_st, wgu_b, wd_b, wsem, xsem, ysem):
    i = pl.program_id(0)
    par = i % 2
    nt = nt_ref[0]

    def weight_copies(e, s):
        return (pltpu.make_async_copy(wg_hbm.at[e], wg_st.at[s], wsem.at[s]),
                pltpu.make_async_copy(wu_hbm.at[e], wu_st.at[s], wsem.at[s]),
                pltpu.make_async_copy(wd_hbm.at[e], wd_st.at[s], wsem.at[s]))

    def tile_chunks(t, p, output, fn):
        row0 = tb_ref[t] * E_TILE
        for c in range(E_CHUNKS):
            rows = pl.ds(c * E_CHUNK, E_CHUNK)
            hbm_rows = pl.ds(row0 + c * E_CHUNK, E_CHUNK)

            @pl.when(c * E_CHUNK < nval_ref[t])
            def _():
                if output:
                    fn(pltpu.make_async_copy(ybuf.at[p, rows, :], y_hbm.at[hbm_rows, 0], ysem.at[p]))
                else:
                    fn(pltpu.make_async_copy(x_hbm.at[hbm_rows, :], xbuf.at[p, rows, :], xsem.at[p]))

    def start(cp):
        cp.start()

    def wait(cp):
        cp.wait()

    @pl.when(i == 0)
    def _():
        tile_chunks(0, 0, False, start)

    @pl.when(i + 1 < nt)
    def _():
        tile_chunks(i + 1, 1 - par, False, start)

    @pl.when(i < nt)
    def _():
        tile_chunks(i, par, False, wait)

        @pl.when(first_ref[i] == 1)
        def _():
            s = slot_ref[i]

            @pl.when(i == 0)
            def _():
                for cp in weight_copies(te_ref[0], 0):
                    cp.start(priority=1)

            for cp in weight_copies(te_ref[i], s):
                cp.wait()

            @pl.when(nxt_ref[i] >= 0)
            def _():
                for cp in weight_copies(nxt_ref[i], 1 - s):
                    cp.start(priority=1)

            wgu_b[:, 0:DE] = wg_st[s].astype(BF16)
            wgu_b[:, DE:2 * DE] = wu_st[s].astype(BF16)
            wd_b[...] = wd_st[s].astype(BF16)

        valid = _iota2((E_TILE, D), 0) < nval_ref[i]
        x = jnp.where(valid, xbuf[par], 0.0).astype(BF16)
        ab = jnp.dot(x, wgu_b[...], preferred_element_type=F32)
        hid = (_silu(ab[:, 0:DE]) * ab[:, DE:2 * DE]).astype(BF16)
        ybuf[par] = jnp.dot(hid, wd_b[...], preferred_element_type=F32)

        tile_chunks(i, par, True, start)

        @pl.when(i >= 1)
        def _():
            tile_chunks(i - 1, 1 - par, True, wait)

        @pl.when(i == nt - 1)
        def _():
            tile_chunks(i, par, True, wait)


def _moe(plan, xs, w_eg, w_eu, w_ed):
    grid_spec = pltpu.PrefetchScalarGridSpec(
        num_scalar_prefetch=len(plan),
        grid=(E_TILES_MAX,),
        in_specs=[pl.BlockSpec(memory_space=pl.ANY)] * 4,
        out_specs=pl.BlockSpec(memory_space=pl.ANY),
        scratch_shapes=[pltpu.VMEM((2, E_TILE, D), F32), pltpu.VMEM((2, E_TILE, D), F32),
                        pltpu.VMEM((2, D, DE), F32), pltpu.VMEM((2, D, DE), F32), pltpu.VMEM((2, DE, D), F32),
                        pltpu.VMEM((D, 2 * DE), BF16), pltpu.VMEM((DE, D), BF16),
                        pltpu.SemaphoreType.DMA((2,)), pltpu.SemaphoreType.DMA((2,)), pltpu.SemaphoreType.DMA((2,))],
    )
    return pl.pallas_call(
        _moe_kernel,
        out_shape=jax.ShapeDtypeStruct((E_ROWS, 1, D), F32),
        grid_spec=grid_spec,
        compiler_params=_cparams(),
        name="moe",
    )(*plan, xs, w_eg, w_eu, w_ed)


def _plan(counts):
    ntile = (counts + E_TILE - 1) // E_TILE
    tend = jnp.cumsum(ntile)
    tbeg = tend - ntile
    n_tiles = tend[-1]
    tid = jnp.minimum(jnp.arange(E_TILES_MAX, dtype=I32), n_tiles - 1)
    tile_expert = jnp.minimum(jnp.sum((tend[None, :] <= tid[:, None]).astype(I32), axis=1), NE - 1)
    tile_in_expert = tid - tbeg[tile_expert]
    tile_block = tile_expert * E_CAP_TILES + tile_in_expert
    tile_first = (tile_in_expert == 0).astype(I32)
    tile_nvalid = jnp.minimum(counts[tile_expert] - tile_in_expert * E_TILE, E_TILE)
    ids = jnp.arange(NE, dtype=I32)
    occupied = ntile > 0
    later = occupied[None, :] & (ids[None, :] > ids[:, None])
    next_expert = jnp.where(jnp.any(later, axis=1), jnp.min(jnp.where(later, ids[None, :], NE), axis=1), -1)
    rank = jnp.cumsum(occupied.astype(I32)) - 1
    return (tile_block, tile_expert, tile_first, tile_nvalid, next_expert[tile_expert].astype(I32),
            (rank[tile_expert] % 2).astype(I32), n_tiles.reshape(1))


def _final_kernel(pos0_ref, posn_ref, x1_ref, rw_ref, gtp_ref, gts_ref, g_ref, ys_ref, yp_ref, ysm_ref, buf, sem):
    i = pl.program_id(0)
    par = i % 2

    def gather(p_ref, p):
        for r in range(TM):
            for slot in range(2):
                pltpu.make_async_copy(ys_ref.at[p_ref[0, slot, r]], buf.at[p, slot, pl.ds(r, 1), :],
                                      sem.at[p]).start(priority=1)

    @pl.when(i == 0)
    def _():
        gather(pos0_ref, 0)

    @pl.when(i < N_TILES - 1)
    def _():
        gather(posn_ref, 1 - par)

    for slot in range(2):
        pltpu.make_async_copy(ys_ref.at[pl.ds(0, TM), 0], buf.at[par, slot], sem.at[par]).wait()

    moe = rw_ref[:, 0:1] * buf[par, 0] + rw_ref[:, 1:2] * buf[par, 1]

    @pl.when(i < N_TILES_P)
    def _():
        yp_ref[...] = _rms(x1_ref[...] + _prompt_row(gtp_ref) * moe, g_ref[...])

    @pl.when(i >= N_TILES_P)
    def _():
        ysm_ref[...] = _rms(x1_ref[...] + _sample_rows(gts_ref) * moe, g_ref[...])


def _final(pos, x1, rw, mod, g_fin, ys):
    return pl.pallas_call(
        _final_kernel,
        out_shape=(jax.ShapeDtypeStruct((T_P, D), F32), jax.ShapeDtypeStruct((T_S, D), F32)),
        grid=(N_TILES,),
        in_specs=[pl.BlockSpec((1, 8, TM), lambda i: (0, 0, 0), memory_space=pltpu.SMEM),
                  pl.BlockSpec((1, 8, TM), lambda i: (jnp.minimum(i + 1, N_TILES - 1), 0, 0),
                               memory_space=pltpu.SMEM),
                  pl.BlockSpec((TM, D), lambda i: (i, 0)),
                  pl.BlockSpec((TM, 128), lambda i: (i, 0)),
                  *_mod_specs(5),
                  _const_spec((1, D)),
                  pl.BlockSpec(memory_space=pl.ANY)],
        out_specs=(pl.BlockSpec((TM, D), lambda i: (_p_tile(i), 0)),
                   pl.BlockSpec((TM, D), lambda i: (_s_tile(i), 0))),
        scratch_shapes=[pltpu.VMEM((2, 2, TM, D), F32), pltpu.SemaphoreType.DMA((2,))],
        compiler_params=_cparams(),
        name="final",
    )(pos, pos, x1, rw, mod, mod, g_fin, ys)


def kernel(x_prompt, x_sample, c_prompt, c_sample, state_conv, state_gla, w_ada, b_ada, norm_mix, w_in, conv_w,
           conv_b, w_gate_up, b_gate, norm_conv, norm_gla, w_out, norm_ffn, w_coarse, b_coarse, w_fine, b_fine,
           w_exp_gate, w_exp_up, w_exp_down, norm_final):
    xp = x_prompt.reshape(T_P, D)
    xs = x_sample.reshape(T_S, D)
    c_all = jnp.concatenate([c_sample, c_prompt, jnp.zeros((C_ROWS - B_S - B_P, D), F32)], axis=0)
    mod = _ada(c_all, w_ada[0], b_ada[0][None, :])

    w_in_t = jnp.swapaxes(w_in, 1, 2)
    w_low_t = jnp.pad(w_in_t[0, PROJ_MAIN:, :], ((0, 128 - RANK), (0, 0))).astype(BF16)
    wgu_b = jnp.pad(w_gate_up[0], ((0, 128 - RANK), (0, 0))).astype(BF16)
    w_r = jnp.concatenate([w_coarse[0].T, jnp.zeros((8 - N_GROUPS, D), F32), w_fine[0].T], axis=0)
    b_r = jnp.concatenate([b_coarse[0], jnp.zeros((8 - N_GROUPS,), F32), b_fine[0]])[:, None]
    g_mix, g_ffn, g_fin = norm_mix[0][None, :], norm_ffn[0][None, :], norm_final[None, :]
    mix_consts = (conv_w[0], conv_b[0][None, :], wgu_b, b_gate[0][None, :], norm_conv[0][None, :],
                  norm_gla[0][None, :])

    ycat_p, nconv_p, nst_p, proj_s = _inmix(xp, xs, mod, g_mix, w_in_t, w_low_t, *mix_consts)
    ycat_s, nconv_s, nst_s = _mixer_s(proj_s, state_conv[0], state_gla[0].reshape(B_S, HK, DV), *mix_consts)

    x1, pos, rw, counts, x_sorted = _outproj(ycat_p, ycat_s, xp, xs, mod, w_out, g_ffn, w_r.astype(BF16), b_r)
    y_sorted = _moe(_plan(counts[:, 0]), x_sorted, w_exp_gate[0], w_exp_up[0], w_exp_down[0])
    y_p, y_s = _final(pos, x1, rw, mod, g_fin, y_sorted)

    new_gla_p = nst_p.reshape(B_P, DV, H, DK).transpose(0, 2, 3, 1)[None]
    new_gla_s = nst_s.reshape(1, B_S, H, DK, DV)
    return (y_p.reshape(B_P, L_P, D), y_s.reshape(B_S, L_S, D), nconv_p[None], new_gla_p,
            nconv_s[None], new_gla_s)
```

```python
import functools

import jax
import jax.numpy as jnp
from jax import lax
from jax.experimental import pallas as pl
from jax.experimental.pallas import tpu as pltpu

F32 = jnp.float32
BF16 = jnp.bfloat16
I32 = jnp.int32

D = 2048
CW = 1024
H = 8
DK = 64
DV = 128
HK = H * DK
HV = H * DV
RANK = 16
TAU = 16.0
N_GROUPS = 4
EPG = 8
NE = N_GROUPS * EPG
DE = 512
EPS = 1e-6

B_P, L_P = 4, 2048
B_S, L_S = 128, 8
T_P = B_P * L_P
T_S = B_S * L_S
T = T_P + T_S

TM = 256
CH = 64
N_TILES = T // TM
N_TILES_P = T_P // TM
TILES_PER_SEQ = L_P // TM
SEQ_PER_TILE = TM // L_S
SEQ_PER_CHUNK = CH // L_S

PROJ_MAIN = 3 * CW + 2 * HK + 2 * HV
PROJ_PAD = PROJ_MAIN + 128
C_ROWS = 136
P_ROW_BLOCK = B_S // 8
R_ROWS = 40

E_TILE = 256
E_TILES_MAX = (2 * T) // E_TILE + NE
E_CAP = T
E_CAP_TILES = E_CAP // E_TILE
E_ROWS = NE * E_CAP
VMEM_LIMIT = 60 * 1024 * 1024


def _cparams(n_axes=1, vmem=VMEM_LIMIT):
    return pltpu.CompilerParams(dimension_semantics=("arbitrary",) * n_axes, vmem_limit_bytes=vmem)


def _rms(x, g):
    return x * lax.rsqrt(jnp.mean(x * x, axis=-1, keepdims=True) + EPS) * g


def _sigmoid(x):
    return 1.0 / (1.0 + jnp.exp(-x))


def _silu(x):
    return x * _sigmoid(x)


def _log_sigmoid(x):
    return jnp.minimum(x, 0.0) - jnp.log(1.0 + jnp.exp(-jnp.abs(x)))


def _iota2(shape, axis):
    return lax.broadcasted_iota(I32, shape, axis)


def _expand_rows(ref, n, reps):
    return jnp.concatenate([jnp.broadcast_to(ref[j:j + 1, :], (reps, ref.shape[-1])) for j in range(n)], axis=0)


def _prompt_row(ref):
    return ref[pl.ds(pl.program_id(0) // TILES_PER_SEQ, 1), :]


def _sample_rows(ref):
    return _expand_rows(ref, SEQ_PER_TILE, L_S)


def _p_tile(i):
    return jnp.minimum(i, N_TILES_P - 1)


def _s_tile(i):
    return jnp.maximum(i - N_TILES_P, 0)


def _mod_specs(col):
    return [pl.BlockSpec((8, D), lambda i, *_: (P_ROW_BLOCK, col)),
            pl.BlockSpec((SEQ_PER_TILE, D), lambda i, *_: (_s_tile(i), col))]


def _const_spec(shape):
    zeros = (0,) * len(shape)
    return pl.BlockSpec(shape, lambda *_: zeros, pipeline_mode=pl.Buffered(1))


def _ada_kernel(c_ref, w_ref, b_ref, o_ref):
    s = _silu(c_ref[...]).astype(BF16)
    o_ref[...] = jnp.dot(s, w_ref[...].astype(BF16), preferred_element_type=F32) + b_ref[...]


def _ada(c_all, w_ada, b_ada):
    tn = 1024
    return pl.pallas_call(
        _ada_kernel,
        out_shape=jax.ShapeDtypeStruct((C_ROWS, 6 * D), F32),
        grid=(6 * D // tn,),
        in_specs=[pl.BlockSpec((C_ROWS, D), lambda j: (0, 0)),
                  pl.BlockSpec((D, tn), lambda j: (0, j)),
                  pl.BlockSpec((1, tn), lambda j: (0, j))],
        out_specs=pl.BlockSpec((C_ROWS, tn), lambda j: (0, j)),
        compiler_params=_cparams(),
        name="ada",
    )(c_all, w_ada, b_ada)


W_CHUNK = TM
N_W_CHUNKS = PROJ_MAIN // W_CHUNK
NT_DIMS = (((1,), (1,)), ((), ()))


def _gla_prep(p_ref, wgu_ref, bg_ref, rows, seg):
    q = p_ref[:, 3 * CW:3 * CW + HK]
    k = p_ref[:, 3 * CW + HK:3 * CW + 2 * HK]
    a = p_ref[:, PROJ_MAIN:PROJ_PAD].astype(BF16)
    z = jnp.dot(a, wgu_ref[...], preferred_element_type=F32) + bg_ref[...]
    la = _log_sigmoid(z) * (1.0 / TAU)
    la0 = la.astype(BF16)
    rem = la - la0.astype(F32)
    la1 = rem.astype(BF16)
    la2 = (rem - la1.astype(F32)).astype(BF16)
    r = _iota2((2 * rows, rows), 0)
    c = _iota2((2 * rows, rows), 1)
    same = ((r % rows) // seg) == (c // seg)
    sel = jnp.where(same & ((r >= rows) | (c <= r)), 1.0, 0.0).astype(BF16)
    sums = sum(jnp.dot(sel, part, preferred_element_type=F32) for part in (la0, la1, la2))
    b, bl = sums[0:rows], sums[rows:2 * rows]
    qt = q * jnp.exp(b) * (DK ** -0.5)
    kt = k * jnp.exp(-b)
    kend = kt * jnp.exp(bl)
    return qt, kt, kend, bl


def _head_stack(x):
    rows = x.shape[0]
    t = jnp.concatenate([x] * H, axis=0)
    keep = (_iota2((H * rows, HK), 0) // rows) == (_iota2((H * rows, HK), 1) // DK)
    return jnp.where(keep, t, 0.0)


def _gla_out(o_h, g_h, ngl):
    on = o_h * lax.rsqrt(jnp.mean(o_h * o_h, axis=-1, keepdims=True) + EPS) * ngl
    return on * _silu(g_h)


def _conv_out(bg, u, um1, um2, cw_ref, cb_ref, ncv_ref):
    conv_y = cb_ref[...] + cw_ref[0:1, :] * um2 + cw_ref[1:2, :] * um1 + cw_ref[2:3, :] * u
    return _rms(bg * conv_y, ncv_ref[...])


V_OFF = 3 * CW + 2 * HK
G_OFF = V_OFF + HV


def _mixer_p_tile(j, p_ref, cw_ref, cb_ref, wgu_ref, bg_ref, ncv_ref, ngl_ref,
                  y_ref, nconv_ref, nst_ref, ubuf, st_ref, also):
    @pl.when(j == 0)
    def _():
        ubuf[0:8, :] = jnp.zeros((8, CW), F32)
        st_ref[...] = jnp.zeros((DV, HK), F32)

    bgate = p_ref[:, 0:CW]
    u = p_ref[:, CW:2 * CW] * p_ref[:, 2 * CW:3 * CW]
    ubuf[8:8 + TM, :] = u
    um1 = ubuf[7:7 + TM, :]
    um2 = ubuf[6:6 + TM, :]
    y_ref[:, 0:CW] = _conv_out(bgate, u, um1, um2, cw_ref, cb_ref, ncv_ref).astype(BF16)
    ubuf[6:8, :] = u[TM - 2:TM, :]

    qt, kt, kend, bl = _gla_prep(p_ref, wgu_ref, bg_ref, TM, CH)
    causal = _iota2((2 * CH, CH), 0) % CH >= _iota2((2 * CH, CH), 1)
    first_head = _iota2((CH, 2 * DK), 1) < DK
    ngl = ngl_ref[...]

    def pair_stack(x):
        return jnp.concatenate([jnp.where(first_head, x, 0.0), jnp.where(first_head, 0.0, x)], axis=0).astype(BF16)

    for c in range(TM // CH):
        r0 = c * CH
        for m in range(H // 2):
            lanes = slice(m * 2 * DK, (m + 1) * 2 * DK)
            lhs = pair_stack(qt[r0:r0 + CH, lanes])
            sc = lax.dot_general(lhs, kt[r0:r0 + CH, lanes].astype(BF16), (((1,), (1,)), ((), ())),
                                 preferred_element_type=F32)
            sc = jnp.where(causal, sc, 0.0).astype(BF16)
            st = st_ref[:, lanes]
            o_inter = lax.dot_general(lhs, st.astype(BF16), (((1,), (1,)), ((), ())),
                                      preferred_element_type=F32)
            vs = []
            for hh in range(2):
                h = 2 * m + hh
                v_h = p_ref[r0:r0 + CH, V_OFF + h * DV:V_OFF + (h + 1) * DV].astype(BF16)
                g_h = p_ref[r0:r0 + CH, G_OFF + h * DV:G_OFF + (h + 1) * DV]
                vs.append(v_h)
                o_h = (jnp.dot(sc[hh * CH:(hh + 1) * CH], v_h, preferred_element_type=F32)
                       + o_inter[hh * CH:(hh + 1) * CH])
                y_ref[r0:r0 + CH, CW + h * DV:CW + (h + 1) * DV] = _gla_out(o_h, g_h, ngl).astype(BF16)
            kv_t = lax.dot_general(jnp.concatenate(vs, axis=0), pair_stack(kend[r0:r0 + CH, lanes]),
                                   (((0,), (0,)), ((), ())), preferred_element_type=F32)
            st_ref[:, lanes] = st * jnp.exp(bl[r0:r0 + 1, lanes]) + kv_t

    also()

    @pl.when(j == TILES_PER_SEQ - 1)
    def _():
        nconv_ref[0] = ubuf[6:8, :]
        nst_ref[0] = st_ref[...]


def _inmix_kernel(xp_ref, xs_ref, shp_ref, shs_ref, scp_ref, scs_ref, g_ref, wt_hbm, wa_ref,
                  cw_ref, cb_ref, wgu_ref, bg_ref, ncv_ref, ngl_ref,
                  y_ref, nconv_ref, nst_ref, projs_hbm,
                  h_scr, w_scr, proj, ubuf, st_ref, wsem, psem):
    i = pl.program_id(0)
    par = i % 2

    @pl.when(i == 0)
    def _():
        def chunk(c):
            return pltpu.make_async_copy(wt_hbm.at[0, pl.ds(c * W_CHUNK, W_CHUNK), :],
                                         proj.at[c % 2, :, pl.ds(0, D)], wsem.at[c % 2])

        chunk(0).start()
        for c in range(N_W_CHUNKS):
            if c + 1 < N_W_CHUNKS:
                chunk(c + 1).start()
            chunk(c).wait()
            w_scr[c * W_CHUNK:(c + 1) * W_CHUNK, :] = proj[c % 2, :, 0:D].astype(BF16)

    @pl.when(i < N_TILES_P)
    def _():
        h = _rms(xp_ref[...], g_ref[...]) * (1.0 + _prompt_row(scp_ref)) + _prompt_row(shp_ref)
        h_scr[...] = h.astype(BF16)

    @pl.when(i >= N_TILES_P)
    def _():
        h = _rms(xs_ref[...], g_ref[...]) * (1.0 + _sample_rows(scs_ref)) + _sample_rows(shs_ref)
        h_scr[...] = h.astype(BF16)

    def project():
        h = h_scr[...]
        dst = proj.at[par]
        dst[:, 0:PROJ_MAIN] = lax.dot_general(h, w_scr[...], NT_DIMS, preferred_element_type=F32)
        dst[:, PROJ_MAIN:PROJ_PAD] = lax.dot_general(h, wa_ref[...], NT_DIMS, preferred_element_type=F32)

    has_mixer = (i >= 1) & (i <= N_TILES_P)

    @pl.when(has_mixer)
    def _():
        _mixer_p_tile((i - 1) % TILES_PER_SEQ, proj.at[1 - par], cw_ref, cb_ref, wgu_ref, bg_ref, ncv_ref, ngl_ref,
                      y_ref, nconv_ref, nst_ref, ubuf, st_ref, project)

    @pl.when(jnp.logical_not(has_mixer))
    def _():
        project()

    @pl.when(i >= N_TILES_P)
    def _():
        cp = pltpu.make_async_copy(proj.at[par], projs_hbm.at[pl.ds((i - N_TILES_P) * TM, TM), :], psem)
        cp.start()
        cp.wait()


def _inmix(xp, xs, mod, g_mix, w_in_t, w_low_t, conv_w, conv_b, wgu_b, b_gate, n_conv, n_gla):
    def prev_tile(i):
        return jnp.clip(i - 1, 0, N_TILES_P - 1)

    return pl.pallas_call(
        _inmix_kernel,
        out_shape=(jax.ShapeDtypeStruct((T_P, D), BF16),
                   jax.ShapeDtypeStruct((B_P, 2, CW), F32),
                   jax.ShapeDtypeStruct((B_P, DV, HK), F32),
                   jax.ShapeDtypeStruct((T_S, PROJ_PAD), F32)),
        grid=(N_TILES,),
        in_specs=[pl.BlockSpec((TM, D), lambda i: (_p_tile(i), 0)),
                  pl.BlockSpec((TM, D), lambda i: (_s_tile(i), 0)),
                  *_mod_specs(0), *_mod_specs(1),
                  _const_spec((1, D)), pl.BlockSpec(memory_space=pl.ANY), _const_spec((128, D)),
                  _const_spec((3, CW)), _const_spec((1, CW)), _const_spec((128, HK)), _const_spec((1, HK)),
                  _const_spec((1, CW)), _const_spec((1, DV))],
        out_specs=(pl.BlockSpec((TM, D), lambda i: (prev_tile(i), 0)),
                   pl.BlockSpec((1, 2, CW), lambda i: (prev_tile(i) // TILES_PER_SEQ, 0, 0)),
                   pl.BlockSpec((1, DV, HK), lambda i: (prev_tile(i) // TILES_PER_SEQ, 0, 0)),
                   pl.BlockSpec(memory_space=pl.ANY)),
        scratch_shapes=[pltpu.VMEM((TM, D), BF16), pltpu.VMEM((PROJ_MAIN, D), BF16),
                        pltpu.VMEM((2, TM, PROJ_PAD), F32),
                        pltpu.VMEM((8 + TM, CW), F32), pltpu.VMEM((DV, HK), F32),
                        pltpu.SemaphoreType.DMA((2,)), pltpu.SemaphoreType.DMA(())],
        compiler_params=_cparams(),
        name="inmix",
    )(xp, xs, mod, mod, mod, mod, g_mix, w_in_t, w_low_t, conv_w, conv_b, wgu_b, b_gate, n_conv, n_gla)


def _mixer_s_kernel(p_ref, sconv_ref, sst_ref, cw_ref, cb_ref, wgu_ref, bg_ref, ncv_ref, ngl_ref,
                    y_ref, nconv_ref, nst_ref, ubuf):
    nseq = SEQ_PER_CHUNK
    bgate = p_ref[:, 0:CW]
    u = p_ref[:, CW:2 * CW] * p_ref[:, 2 * CW:3 * CW]
    ubuf[0:8, :] = jnp.zeros((8, CW), F32)
    ubuf[8:8 + CH, :] = u
    tpos = _iota2((CH, CW), 0) % L_S
    s0 = jnp.concatenate([jnp.broadcast_to(sconv_ref[s, 0:1, :], (L_S, CW)) for s in range(nseq)], axis=0)
    s1 = jnp.concatenate([jnp.broadcast_to(sconv_ref[s, 1:2, :], (L_S, CW)) for s in range(nseq)], axis=0)
    um1 = jnp.where(tpos == 0, s1, ubuf[7:7 + CH, :])
    um2 = jnp.where(tpos == 0, s0, jnp.where(tpos == 1, s1, ubuf[6:6 + CH, :]))
    y_ref[:, 0:CW] = _conv_out(bgate, u, um1, um2, cw_ref, cb_ref, ncv_ref).astype(BF16)
    for s in range(nseq):
        nconv_ref[s] = u[s * L_S + L_S - 2:(s + 1) * L_S, :]

    qt, kt, kend, bl = _gla_prep(p_ref, wgu_ref, bg_ref, CH, L_S)
    rr = _iota2((H * CH, CH), 0) % CH
    cc = _iota2((H * CH, CH), 1)
    causal = (rr >= cc) & ((rr // L_S) == (cc // L_S))
    ngl = ngl_ref[...]
    lhs_f = _head_stack(qt)
    sc = lax.dot_general(lhs_f.astype(BF16), kt.astype(BF16), (((1,), (1,)), ((), ())),
                         preferred_element_type=F32)
    sc = jnp.where(causal, sc, 0.0).astype(BF16)
    kstack_f = _head_stack(kend)
    v_all = p_ref[:, V_OFF:V_OFF + HV]
    decay_t = jnp.exp(jnp.concatenate([bl, jnp.zeros((128 - CH, HK), F32)], axis=0).T)
    o_inter = []
    for s in range(nseq):
        rs = [slice(h * CH + s * L_S, h * CH + (s + 1) * L_S) for h in range(H)]
        lhs_s = jnp.concatenate([lhs_f[r] for r in rs], axis=0).astype(BF16)
        k_s = jnp.concatenate([kstack_f[r] for r in rs], axis=0).astype(BF16)
        v_s = jnp.concatenate([v_all[s * L_S:(s + 1) * L_S, h * DV:(h + 1) * DV] for h in range(H)],
                              axis=0).astype(BF16)
        st = sst_ref[s]
        o_inter.append(jnp.dot(lhs_s, st.astype(BF16), preferred_element_type=F32))
        kv = lax.dot_general(k_s, v_s, (((0,), (0,)), ((), ())), preferred_element_type=F32)
        nst_ref[s] = st * decay_t[:, s * L_S:s * L_S + 1] + kv
    for h in range(H):
        v_h = v_all[:, h * DV:(h + 1) * DV].astype(BF16)
        g_h = p_ref[:, G_OFF + h * DV:G_OFF + (h + 1) * DV]
        oi_h = jnp.concatenate([o_inter[s][h * L_S:(h + 1) * L_S] for s in range(nseq)], axis=0)
        o_h = jnp.dot(sc[h * CH:(h + 1) * CH], v_h, preferred_element_type=F32) + oi_h
        y_ref[:, CW + h * DV:CW + (h + 1) * DV] = _gla_out(o_h, g_h, ngl).astype(BF16)


def _mixer_s(proj, sconv, sst, conv_w, conv_b, wgu_b, b_gate, n_conv, n_gla):
    nseq = SEQ_PER_CHUNK
    return pl.pallas_call(
        _mixer_s_kernel,
        out_shape=(jax.ShapeDtypeStruct((T_S, D), BF16),
                   jax.ShapeDtypeStruct((B_S, 2, CW), F32),
                   jax.ShapeDtypeStruct((B_S, HK, DV), F32)),
        grid=(T_S // CH,),
        in_specs=[pl.BlockSpec((CH, PROJ_PAD), lambda i: (i, 0)),
                  pl.BlockSpec((nseq, 2, CW), lambda i: (i, 0, 0)),
                  pl.BlockSpec((nseq, HK, DV), lambda i: (i, 0, 0)),
                  _const_spec((3, CW)), _const_spec((1, CW)), _const_spec((128, HK)), _const_spec((1, HK)),
                  _const_spec((1, CW)), _const_spec((1, DV))],
        out_specs=(pl.BlockSpec((CH, D), lambda i: (i, 0)),
                   pl.BlockSpec((nseq, 2, CW), lambda i: (i, 0, 0)),
                   pl.BlockSpec((nseq, HK, DV), lambda i: (i, 0, 0))),
        scratch_shapes=[pltpu.VMEM((8 + CH, CW), F32)],
        compiler_params=_cparams(),
        name="mixer_s",
    )(proj, sconv, sst, conv_w, conv_b, wgu_b, b_gate, n_conv, n_gla)


def _route(h2, wr_ref, br_ref, run_ref, pos_ref, rw_ref, cnt_ref):
    lt = lax.dot_general(wr_ref[...], h2.astype(BF16), (((1,), (1,)), ((), ())),
                         preferred_element_type=F32) + br_ref[...]

    coarse = lt[0:N_GROUPS]
    cmax = jnp.max(coarse, axis=0, keepdims=True)
    gi = _iota2((N_GROUPS, TM), 0)
    grp = jnp.min(jnp.where(coarse == cmax, gi, N_GROUPS), axis=0, keepdims=True)
    p_sel = 1.0 / jnp.sum(jnp.exp(coarse - cmax), axis=0, keepdims=True)
    fine = jnp.zeros((EPG, TM), F32)
    for g in range(N_GROUPS):
        fine = jnp.where(grp == g, lt[8 + g * EPG:8 + (g + 1) * EPG], fine)
    ei = _iota2((EPG, TM), 0)
    f1 = jnp.max(fine, axis=0, keepdims=True)
    i1 = jnp.min(jnp.where(fine == f1, ei, EPG), axis=0, keepdims=True)
    rest = jnp.where(ei == i1, -jnp.inf, fine)
    f2 = jnp.max(rest, axis=0, keepdims=True)
    i2 = jnp.min(jnp.where(rest == f2, ei, EPG), axis=0, keepdims=True)
    e2 = jnp.exp(f2 - f1)
    w1 = p_sel / (1.0 + e2)
    w2 = p_sel * e2 / (1.0 + e2)
    x1 = grp * EPG + i1
    x2 = grp * EPG + i2

    er = _iota2((NE, TM), 0)
    oh1 = jnp.where(er == x1, 1.0, 0.0)
    oh2 = jnp.where(er == x2, 1.0, 0.0)
    oh = oh1 + oh2
    before = jnp.where(_iota2((TM, TM), 0) < _iota2((TM, TM), 1), 1.0, 0.0).astype(BF16)
    cum = jnp.dot(oh.astype(BF16), before, preferred_element_type=F32) + run_ref[:, 0:1]
    r1 = jnp.sum(oh1 * cum, axis=0, keepdims=True).astype(I32)
    r2 = jnp.sum(oh2 * cum, axis=0, keepdims=True).astype(I32)
    run_new = run_ref[...] + jnp.sum(oh, axis=1, keepdims=True)
    run_ref[...] = run_new
    cnt_ref[...] = run_new.astype(I32)

    row = _iota2((8, TM), 0)
    pos_ref[...] = jnp.where(row == 0, x1 * E_CAP + r1, jnp.where(row == 1, x2 * E_CAP + r2, 0))
    wrow = _iota2((128, TM), 0)
    rw_ref[...] = jnp.where(wrow == 0, w1, jnp.where(wrow == 1, w2, 0.0)).T


def _outproj_kernel(ycp_ref, ycs_ref, xp_ref, xs_ref, gtp_ref, gts_ref, shp_ref, shs_ref, scp_ref, scs_ref,
                    wo_hbm, g_ref, wr_ref, br_ref,
                    x1_ref, pos_ref, rw_ref, cnt_ref, xs_out,
                    run_ref, h2_scr, wo_ref, pos_v, pos_s, sem, psem, wsem):
    i = pl.program_id(0)
    par = i % 2

    @pl.when(i == 0)
    def _():
        def chunk(c):
            return pltpu.make_async_copy(wo_hbm.at[0, pl.ds(c * TM, TM), :], h2_scr.at[c % 2], wsem.at[c % 2])

        chunk(0).start()
        for c in range(D // TM):
            if c + 1 < D // TM:
                chunk(c + 1).start()
            chunk(c).wait()
            wo_ref[c * TM:(c + 1) * TM, :] = h2_scr[c % 2].astype(BF16)

    to_smem = pltpu.make_async_copy(pos_v, pos_s, psem)

    def scatter(p):
        for r in range(TM):
            for slot in range(2):
                pltpu.make_async_copy(h2_scr.at[p, pl.ds(r, 1), :], xs_out.at[pl.ds(pos_s[slot, r], 1), :],
                                      sem).start(priority=r % 2)

    def drain():
        for _ in range(2):
            pltpu.make_async_copy(h2_scr.at[0], xs_out.at[pl.ds(0, TM), :], sem).wait()

    @pl.when(i == 0)
    def _():
        run_ref[...] = jnp.zeros((NE, 128), F32)
        h2_scr[1] = jnp.zeros((TM, D), F32)
        pos_v[...] = E_ROWS + _iota2((8, TM), 0) * TM + _iota2((8, TM), 1)
        to_smem.start()

    to_smem.wait()

    def tile(yc, x, gt, sh, sc):
        scatter(1 - par)
        mix = jnp.dot(yc, wo_ref[...], preferred_element_type=F32)
        x1 = x + gt * mix
        x1_ref[...] = x1
        h2 = _rms(x1, g_ref[...]) * (1.0 + sc) + sh
        h2_scr[par] = h2
        _route(h2, wr_ref, br_ref, run_ref, pos_v, rw_ref, cnt_ref)

    @pl.when(i < N_TILES_P)
    def _():
        tile(ycp_ref[...], xp_ref[...], _prompt_row(gtp_ref), _prompt_row(shp_ref), _prompt_row(scp_ref))

    @pl.when(i >= N_TILES_P)
    def _():
        tile(ycs_ref[...], xs_ref[...], _sample_rows(gts_ref), _sample_rows(shs_ref), _sample_rows(scs_ref))

    pos_ref[0] = pos_v[...]
    drain()
    to_smem.start()

    @pl.when(i == N_TILES - 1)
    def _():
        to_smem.wait()
        scatter(par)
        drain()


def _outproj(ycp, ycs, xp, xs, mod, w_out, g_ffn, w_r, b_r):
    return pl.pallas_call(
        _outproj_kernel,
        out_shape=(jax.ShapeDtypeStruct((T, D), F32), jax.ShapeDtypeStruct((N_TILES, 8, TM), I32),
                   jax.ShapeDtypeStruct((T, 128), F32), jax.ShapeDtypeStruct((NE, 128), I32),
                   jax.ShapeDtypeStruct((E_ROWS + 2 * TM, D), F32)),
        grid=(N_TILES,),
        in_specs=[pl.BlockSpec((TM, D), lambda i: (_p_tile(i), 0)), pl.BlockSpec((TM, D), lambda i: (_s_tile(i), 0)),
                  pl.BlockSpec((TM, D), lambda i: (_p_tile(i), 0)), pl.BlockSpec((TM, D), lambda i: (_s_tile(i), 0)),
                  *_mod_specs(2), *_mod_specs(3), *_mod_specs(4),
                  pl.BlockSpec(memory_space=pl.ANY), _const_spec((1, D)), _const_spec((R_ROWS, D)),
                  _const_spec((R_ROWS, 1))],
        out_specs=(pl.BlockSpec((TM, D), lambda i: (i, 0)), pl.BlockSpec((1, 8, TM), lambda i: (i, 0, 0)),
                   pl.BlockSpec((TM, 128), lambda i: (i, 0)), pl.BlockSpec((NE, 128), lambda i: (0, 0)),
                   pl.BlockSpec(memory_space=pl.ANY)),
        scratch_shapes=[pltpu.VMEM((NE, 128), F32), pltpu.VMEM((2, TM, D), F32), pltpu.VMEM((D, D), BF16),
                        pltpu.VMEM((8, TM), I32), pltpu.SMEM((8, TM), I32), pltpu.SemaphoreType.DMA(()),
                        pltpu.SemaphoreType.DMA(()), pltpu.SemaphoreType.DMA((2,))],
        compiler_params=_cparams(),
        name="outproj",
    )(ycp, ycs, xp, xs, mod, mod, mod, mod, mod, mod, w_out, g_ffn, w_r, b_r)


E_CHUNK = 64
E_CHUNKS = E_TILE // E_CHUNK


def _moe_kernel(tb_ref, te_ref, first_ref, nval_ref, nxt_ref, slot_ref, nt_ref,
                x_hbm, wg_hbm, wu_hbm, wd_hbm, y_hbm,
                xbuf, ybuf, wg_st, wu_st, wd_st, wgu_b, wd_b, wsem, xsem, ysem):
    i = pl.program_id(0)
    par = i % 2
    nt = nt_ref[0]

    def weight_copies(e, s):
        return (pltpu.make_async_copy(wg_hbm.at[e], wg_st.at[s], wsem.at[s]),
                pltpu.make_async_copy(wu_hbm.at[e], wu_st.at[s], wsem.at[s]),
                pltpu.make_async_copy(wd_hbm.at[e], wd_st.at[s], wsem.at[s]))

    def tile_chunks(t, p, output, fn):
        row0 = tb_ref[t] * E_TILE
        for c in range(E_CHUNKS):
            rows = pl.ds(c * E_CHUNK, E_CHUNK)
            hbm_rows = pl.ds(row0 + c * E_CHUNK, E_CHUNK)

            @pl.when(c * E_CHUNK < nval_ref[t])
            def _():
                if output:
                    fn(pltpu.make_async_copy(ybuf.at[p, rows, :], y_hbm.at[hbm_rows, 0], ysem.at[p]))
                else:
                    fn(pltpu.make_async_copy(x_hbm.at[hbm_rows, :], xbuf.at[p, rows, :], xsem.at[p]))

    def start(cp):
        cp.start()

    def start_low(cp):
        cp.start(priority=1)

    def wait(cp):
        cp.wait()

    @pl.when(i == 0)
    def _():
        tile_chunks(0, 0, False, start)

    @pl.when(i + 1 < nt)
    def _():
        tile_chunks(i + 1, 1 - par, False, start)

    @pl.when(i < nt)
    def _():
        tile_chunks(i, par, False, wait)

        @pl.when(first_ref[i] == 1)
        def _():
            s = slot_ref[i]

            @pl.when(i == 0)
            def _():
                for cp in weight_copies(te_ref[0], 0):
                    cp.start(priority=1)

            for cp in weight_copies(te_ref[i], s):
                cp.wait()

            @pl.when(nxt_ref[i] >= 0)
            def _():
                for cp in weight_copies(nxt_ref[i], 1 - s):
                    cp.start(priority=1)

            wgu_b[:, 0:DE] = wg_st[s].astype(BF16)
            wgu_b[:, DE:2 * DE] = wu_st[s].astype(BF16)
            wd_b[...] = wd_st[s].astype(BF16)

        valid = _iota2((E_TILE, D), 0) < nval_ref[i]
        x = jnp.where(valid, xbuf[par], 0.0).astype(BF16)
        ab = jnp.dot(x, wgu_b[...], preferred_element_type=F32)
        hid = (_silu(ab[:, 0:DE]) * ab[:, DE:2 * DE]).astype(BF16)
        ybuf[par] = jnp.dot(hid, wd_b[...], preferred_element_type=F32)

        tile_chunks(i, par, True, start_low)

        @pl.when(i >= 1)
        def _():
            tile_chunks(i - 1, 1 - par, True, wait)

        @pl.when(i == nt - 1)
        def _():
            tile_chunks(i, par, True, wait)


def _moe(plan, xs, w_eg, w_eu, w_ed):
    grid_spec = pltpu.PrefetchScalarGridSpec(
        num_scalar_prefetch=len(plan),
        grid=(E_TILES_MAX,),
        in_specs=[pl.BlockSpec(memory_space=pl.ANY)] * 4,
        out_specs=pl.BlockSpec(memory_space=pl.ANY),
        scratch_shapes=[pltpu.VMEM((2, E_TILE, D), F32), pltpu.VMEM((2, E_TILE, D), F32),
                        pltpu.VMEM((2, D, DE), F32), pltpu.VMEM((2, D, DE), F32), pltpu.VMEM((2, DE, D), F32),
                        pltpu.VMEM((D, 2 * DE), BF16), pltpu.VMEM((DE, D), BF16),
                        pltpu.SemaphoreType.DMA((2,)), pltpu.SemaphoreType.DMA((2,)), pltpu.SemaphoreType.DMA((2,))],
    )
    return pl.pallas_call(
        _moe_kernel,
        out_shape=jax.ShapeDtypeStruct((E_ROWS, 1, D), F32),
        grid_spec=grid_spec,
        compiler_params=_cparams(),
        name="moe",
    )(*plan, xs, w_eg, w_eu, w_ed)


def _plan(counts):
    ntile = (counts + E_TILE - 1) // E_TILE
    tend = jnp.cumsum(ntile)
    tbeg = tend - ntile
    n_tiles = tend[-1]
    tid = jnp.minimum(jnp.arange(E_TILES_MAX, dtype=I32), n_tiles - 1)
    tile_expert = jnp.minimum(jnp.sum((tend[None, :] <= tid[:, None]).astype(I32), axis=1), NE - 1)
    tile_in_expert = tid - tbeg[tile_expert]
    tile_block = tile_expert * E_CAP_TILES + tile_in_expert
    tile_first = (tile_in_expert == 0).astype(I32)
    tile_nvalid = jnp.minimum(counts[tile_expert] - tile_in_expert * E_TILE, E_TILE)
    ids = jnp.arange(NE, dtype=I32)
    occupied = ntile > 0
    later = occupied[None, :] & (ids[None, :] > ids[:, None])
    next_expert = jnp.where(jnp.any(later, axis=1), jnp.min(jnp.where(later, ids[None, :], NE), axis=1), -1)
    rank = jnp.cumsum(occupied.astype(I32)) - 1
    return (tile_block, tile_expert, tile_first, tile_nvalid, next_expert[tile_expert].astype(I32),
            (rank[tile_expert] % 2).astype(I32), n_tiles.reshape(1))


def _final_kernel(pos0_ref, posn_ref, x1_ref, rw_ref, gtp_ref, gts_ref, g_ref, ys_ref, yp_ref, ysm_ref, buf, sem):
    i = pl.program_id(0)
    par = i % 2

    def gather(p_ref, p):
        for r in range(TM):
            for slot in range(2):
                pltpu.make_async_copy(ys_ref.at[p_ref[0, slot, r]], buf.at[p, slot, pl.ds(r, 1), :],
                                      sem.at[p]).start(priority=r % 2)

    @pl.when(i == 0)
    def _():
        gather(pos0_ref, 0)

    @pl.when(i < N_TILES - 1)
    def _():
        gather(posn_ref, 1 - par)

    for slot in range(2):
        pltpu.make_async_copy(ys_ref.at[pl.ds(0, TM), 0], buf.at[par, slot], sem.at[par]).wait()

    moe = rw_ref[:, 0:1] * buf[par, 0] + rw_ref[:, 1:2] * buf[par, 1]

    @pl.when(i < N_TILES_P)
    def _():
        yp_ref[...] = _rms(x1_ref[...] + _prompt_row(gtp_ref) * moe, g_ref[...])

    @pl.when(i >= N_TILES_P)
    def _():
        ysm_ref[...] = _rms(x1_ref[...] + _sample_rows(gts_ref) * moe, g_ref[...])


def _final(pos, x1, rw, mod, g_fin, ys):
    return pl.pallas_call(
        _final_kernel,
        out_shape=(jax.ShapeDtypeStruct((T_P, D), F32), jax.ShapeDtypeStruct((T_S, D), F32)),
        grid=(N_TILES,),
        in_specs=[pl.BlockSpec((1, 8, TM), lambda i: (0, 0, 0), memory_space=pltpu.SMEM),
                  pl.BlockSpec((1, 8, TM), lambda i: (jnp.minimum(i + 1, N_TILES - 1), 0, 0),
                               memory_space=pltpu.SMEM),
                  pl.BlockSpec((TM, D), lambda i: (i, 0)),
                  pl.BlockSpec((TM, 128), lambda i: (i, 0)),
                  *_mod_specs(5),
                  _const_spec((1, D)),
                  pl.BlockSpec(memory_space=pl.ANY)],
        out_specs=(pl.BlockSpec((TM, D), lambda i: (_p_tile(i), 0)),
                   pl.BlockSpec((TM, D), lambda i: (_s_tile(i), 0))),
        scratch_shapes=[pltpu.VMEM((2, 2, TM, D), F32), pltpu.SemaphoreType.DMA((2,))],
        compiler_params=_cparams(),
        name="final",
    )(pos, pos, x1, rw, mod, mod, g_fin, ys)


def kernel(x_prompt, x_sample, c_prompt, c_sample, state_conv, state_gla, w_ada, b_ada, norm_mix, w_in, conv_w,
           conv_b, w_gate_up, b_gate, norm_conv, norm_gla, w_out, norm_ffn, w_coarse, b_coarse, w_fine, b_fine,
           w_exp_gate, w_exp_up, w_exp_down, norm_final):
    xp = x_prompt.reshape(T_P, D)
    xs = x_sample.reshape(T_S, D)
    c_all = jnp.concatenate([c_sample, c_prompt, jnp.zeros((C_ROWS - B_S - B_P, D), F32)], axis=0)
    mod = _ada(c_all, w_ada[0], b_ada[0][None, :])

    w_in_t = jnp.swapaxes(w_in, 1, 2)
    w_low_t = jnp.pad(w_in_t[0, PROJ_MAIN:, :], ((0, 128 - RANK), (0, 0))).astype(BF16)
    wgu_b = jnp.pad(w_gate_up[0], ((0, 128 - RANK), (0, 0))).astype(BF16)
    w_r = jnp.concatenate([w_coarse[0].T, jnp.zeros((8 - N_GROUPS, D), F32), w_fine[0].T], axis=0)
    b_r = jnp.concatenate([b_coarse[0], jnp.zeros((8 - N_GROUPS,), F32), b_fine[0]])[:, None]
    g_mix, g_ffn, g_fin = norm_mix[0][None, :], norm_ffn[0][None, :], norm_final[None, :]
    mix_consts = (conv_w[0], conv_b[0][None, :], wgu_b, b_gate[0][None, :], norm_conv[0][None, :],
                  norm_gla[0][None, :])

    ycat_p, nconv_p, nst_p, proj_s = _inmix(xp, xs, mod, g_mix, w_in_t, w_low_t, *mix_consts)
    ycat_s, nconv_s, nst_s = _mixer_s(proj_s, state_conv[0], state_gla[0].reshape(B_S, HK, DV), *mix_consts)

    x1, pos, rw, counts, x_sorted = _outproj(ycat_p, ycat_s, xp, xs, mod, w_out, g_ffn, w_r.astype(BF16), b_r)
    y_sorted = _moe(_plan(counts[:, 0]), x_sorted, w_exp_gate[0], w_exp_up[0], w_exp_down[0])
    y_p, y_s = _final(pos, x1, rw, mod, g_fin, y_sorted)

    new_gla_p = nst_p.reshape(B_P, DV, H, DK).transpose(0, 2, 3, 1)[None]
    new_gla_s = nst_s.reshape(1, B_S, H, DK, DV)
    return (y_p.reshape(B_P, L_P, D), y_s.reshape(B_S, L_S, D), nconv_p[None], new_gla_p,
            nconv_s[None], new_gla_s)
```

```python
import functools

import jax
import jax.numpy as jnp
from jax import lax
from jax.experimental import pallas as pl
from jax.experimental.pallas import tpu as pltpu

F32 = jnp.float32
BF16 = jnp.bfloat16
I32 = jnp.int32

D = 2048
CW = 1024
H = 8
DK = 64
DV = 128
HK = H * DK
HV = H * DV
RANK = 16
TAU = 16.0
N_GROUPS = 4
EPG = 8
NE = N_GROUPS * EPG
DE = 512
EPS = 1e-6

B_P, L_P = 4, 2048
B_S, L_S = 128, 8
T_P = B_P * L_P
T_S = B_S * L_S
T = T_P + T_S

TM = 256
CH = 64
N_TILES = T // TM
N_TILES_P = T_P // TM
TILES_PER_SEQ = L_P // TM
SEQ_PER_TILE = TM // L_S
SEQ_PER_CHUNK = CH // L_S

PROJ_MAIN = 3 * CW + 2 * HK + 2 * HV
PROJ_PAD = PROJ_MAIN + 128
C_ROWS = 136
P_ROW_BLOCK = B_S // 8
R_ROWS = 40

E_TILE = 256
E_TILES_MAX = (2 * T) // E_TILE + NE
E_CAP = T
E_CAP_TILES = E_CAP // E_TILE
E_ROWS = NE * E_CAP
VMEM_LIMIT = 60 * 1024 * 1024


def _cparams(n_axes=1, vmem=VMEM_LIMIT):
    return pltpu.CompilerParams(dimension_semantics=("arbitrary",) * n_axes, vmem_limit_bytes=vmem)


def _rms(x, g):
    return x * lax.rsqrt(jnp.mean(x * x, axis=-1, keepdims=True) + EPS) * g


def _sigmoid(x):
    return 1.0 / (1.0 + jnp.exp(-x))


def _silu(x):
    return x * _sigmoid(x)


def _log_sigmoid(x):
    return jnp.minimum(x, 0.0) - jnp.log(1.0 + jnp.exp(-jnp.abs(x)))


def _iota2(shape, axis):
    return lax.broadcasted_iota(I32, shape, axis)


def _expand_rows(ref, n, reps):
    return jnp.concatenate([jnp.broadcast_to(ref[j:j + 1, :], (reps, ref.shape[-1])) for j in range(n)], axis=0)


def _prompt_row(ref):
    return ref[pl.ds(pl.program_id(0) // TILES_PER_SEQ, 1), :]


def _sample_rows(ref):
    return _expand_rows(ref, SEQ_PER_TILE, L_S)


def _p_tile(i):
    return jnp.minimum(i, N_TILES_P - 1)


def _s_tile(i):
    return jnp.maximum(i - N_TILES_P, 0)


def _mod_specs(col):
    return [pl.BlockSpec((8, D), lambda i, *_: (P_ROW_BLOCK, col)),
            pl.BlockSpec((SEQ_PER_TILE, D), lambda i, *_: (_s_tile(i), col))]


def _const_spec(shape):
    zeros = (0,) * len(shape)
    return pl.BlockSpec(shape, lambda *_: zeros, pipeline_mode=pl.Buffered(1))


def _ada_kernel(c_ref, w_ref, b_ref, o_ref):
    s = _silu(c_ref[...]).astype(BF16)
    o_ref[...] = jnp.dot(s, w_ref[...].astype(BF16), preferred_element_type=F32) + b_ref[...]


def _ada(c_all, w_ada, b_ada):
    tn = 1024
    return pl.pallas_call(
        _ada_kernel,
        out_shape=jax.ShapeDtypeStruct((C_ROWS, 6 * D), F32),
        grid=(6 * D // tn,),
        in_specs=[pl.BlockSpec((C_ROWS, D), lambda j: (0, 0)),
                  pl.BlockSpec((D, tn), lambda j: (0, j)),
                  pl.BlockSpec((1, tn), lambda j: (0, j))],
        out_specs=pl.BlockSpec((C_ROWS, tn), lambda j: (0, j)),
        compiler_params=_cparams(),
        name="ada",
    )(c_all, w_ada, b_ada)


W_CHUNK = TM
N_W_CHUNKS = PROJ_MAIN // W_CHUNK
NT_DIMS = (((1,), (1,)), ((), ()))


def _gla_prep(p_ref, wgu_ref, bg_ref, rows, seg):
    q = p_ref[:, 3 * CW:3 * CW + HK]
    k = p_ref[:, 3 * CW + HK:3 * CW + 2 * HK]
    a = p_ref[:, PROJ_MAIN:PROJ_PAD].astype(BF16)
    z = jnp.dot(a, wgu_ref[...], preferred_element_type=F32) + bg_ref[...]
    la = _log_sigmoid(z) * (1.0 / TAU)
    la0 = la.astype(BF16)
    rem = la - la0.astype(F32)
    la1 = rem.astype(BF16)
    la2 = (rem - la1.astype(F32)).astype(BF16)
    r = _iota2((2 * rows, rows), 0)
    c = _iota2((2 * rows, rows), 1)
    same = ((r % rows) // seg) == (c // seg)
    sel = jnp.where(same & ((r >= rows) | (c <= r)), 1.0, 0.0).astype(BF16)
    sums = sum(jnp.dot(sel, part, preferred_element_type=F32) for part in (la0, la1, la2))
    b, bl = sums[0:rows], sums[rows:2 * rows]
    qt = q * jnp.exp(b) * (DK ** -0.5)
    kt = k * jnp.exp(-b)
    kend = kt * jnp.exp(bl)
    return qt, kt, kend, bl


def _head_stack(x):
    rows = x.shape[0]
    t = jnp.concatenate([x] * H, axis=0)
    keep = (_iota2((H * rows, HK), 0) // rows) == (_iota2((H * rows, HK), 1) // DK)
    return jnp.where(keep, t, 0.0)


def _gla_out(o_h, g_h, ngl):
    on = o_h * lax.rsqrt(jnp.mean(o_h * o_h, axis=-1, keepdims=True) + EPS) * ngl
    return on * _silu(g_h)


def _conv_out(bg, u, um1, um2, cw_ref, cb_ref, ncv_ref):
    conv_y = cb_ref[...] + cw_ref[0:1, :] * um2 + cw_ref[1:2, :] * um1 + cw_ref[2:3, :] * u
    return _rms(bg * conv_y, ncv_ref[...])


V_OFF = 3 * CW + 2 * HK
G_OFF = V_OFF + HV


def _mixer_p_tile(j, p_ref, cw_ref, cb_ref, wgu_ref, bg_ref, ncv_ref, ngl_ref,
                  y_ref, nconv_ref, nst_ref, ubuf, st_ref, also):
    @pl.when(j == 0)
    def _():
        ubuf[0:8, :] = jnp.zeros((8, CW), F32)
        st_ref[...] = jnp.zeros((DV, HK), F32)

    bgate = p_ref[:, 0:CW]
    u = p_ref[:, CW:2 * CW] * p_ref[:, 2 * CW:3 * CW]
    ubuf[8:8 + TM, :] = u
    um1 = ubuf[7:7 + TM, :]
    um2 = ubuf[6:6 + TM, :]
    y_ref[:, 0:CW] = _conv_out(bgate, u, um1, um2, cw_ref, cb_ref, ncv_ref).astype(BF16)
    ubuf[6:8, :] = u[TM - 2:TM, :]

    qt, kt, kend, bl = _gla_prep(p_ref, wgu_ref, bg_ref, TM, CH)
    causal = _iota2((2 * CH, CH), 0) % CH >= _iota2((2 * CH, CH), 1)
    first_head = _iota2((CH, 2 * DK), 1) < DK
    ngl = ngl_ref[...]

    def pair_stack(x):
        return jnp.concatenate([jnp.where(first_head, x, 0.0), jnp.where(first_head, 0.0, x)], axis=0).astype(BF16)

    for c in range(TM // CH):
        r0 = c * CH
        for m in range(H // 2):
            lanes = slice(m * 2 * DK, (m + 1) * 2 * DK)
            lhs = pair_stack(qt[r0:r0 + CH, lanes])
            sc = lax.dot_general(lhs, kt[r0:r0 + CH, lanes].astype(BF16), (((1,), (1,)), ((), ())),
                                 preferred_element_type=F32)
            sc = jnp.where(causal, sc, 0.0).astype(BF16)
            st = st_ref[:, lanes]
            o_inter = lax.dot_general(lhs, st.astype(BF16), (((1,), (1,)), ((), ())),
                                      preferred_element_type=F32)
            vs = []
            for hh in range(2):
                h = 2 * m + hh
                v_h = p_ref[r0:r0 + CH, V_OFF + h * DV:V_OFF + (h + 1) * DV].astype(BF16)
                g_h = p_ref[r0:r0 + CH, G_OFF + h * DV:G_OFF + (h + 1) * DV]
                vs.append(v_h)
                o_h = (jnp.dot(sc[hh * CH:(hh + 1) * CH], v_h, preferred_element_type=F32)
                       + o_inter[hh * CH:(hh + 1) * CH])
                y_ref[r0:r0 + CH, CW + h * DV:CW + (h + 1) * DV] = _gla_out(o_h, g_h, ngl).astype(BF16)
            kv_t = lax.dot_general(jnp.concatenate(vs, axis=0), pair_stack(kend[r0:r0 + CH, lanes]),
                                   (((0,), (0,)), ((), ())), preferred_element_type=F32)
            st_ref[:, lanes] = st * jnp.exp(bl[r0:r0 + 1, lanes]) + kv_t

    also()

    @pl.when(j == TILES_PER_SEQ - 1)
    def _():
        nconv_ref[0] = ubuf[6:8, :]
        nst_ref[0] = st_ref[...]


def _inmix_kernel(xp_ref, xs_ref, shp_ref, shs_ref, scp_ref, scs_ref, g_ref, wt_hbm, wa_ref,
                  cw_ref, cb_ref, wgu_ref, bg_ref, ncv_ref, ngl_ref,
                  y_ref, nconv_ref, nst_ref, projs_hbm,
                  h_scr, w_scr, proj, ubuf, st_ref, wsem, psem):
    i = pl.program_id(0)
    par = i % 2

    @pl.when(i == 0)
    def _():
        def chunk(c):
            return pltpu.make_async_copy(wt_hbm.at[0, pl.ds(c * W_CHUNK, W_CHUNK), :],
                                         proj.at[c % 2, :, pl.ds(0, D)], wsem.at[c % 2])

        chunk(0).start()
        for c in range(N_W_CHUNKS):
            if c + 1 < N_W_CHUNKS:
                chunk(c + 1).start()
            chunk(c).wait()
            w_scr[c * W_CHUNK:(c + 1) * W_CHUNK, :] = proj[c % 2, :, 0:D].astype(BF16)

    @pl.when(i < N_TILES_P)
    def _():
        h = _rms(xp_ref[...], g_ref[...]) * (1.0 + _prompt_row(scp_ref)) + _prompt_row(shp_ref)
        h_scr[...] = h.astype(BF16)

    @pl.when(i >= N_TILES_P)
    def _():
        h = _rms(xs_ref[...], g_ref[...]) * (1.0 + _sample_rows(scs_ref)) + _sample_rows(shs_ref)
        h_scr[...] = h.astype(BF16)

    def project():
        h = h_scr[...]
        dst = proj.at[par]
        dst[:, 0:PROJ_MAIN] = lax.dot_general(h, w_scr[...], NT_DIMS, preferred_element_type=F32)
        dst[:, PROJ_MAIN:PROJ_PAD] = lax.dot_general(h, wa_ref[...], NT_DIMS, preferred_element_type=F32)

    has_mixer = (i >= 1) & (i <= N_TILES_P)

    @pl.when(has_mixer)
    def _():
        _mixer_p_tile((i - 1) % TILES_PER_SEQ, proj.at[1 - par], cw_ref, cb_ref, wgu_ref, bg_ref, ncv_ref, ngl_ref,
                      y_ref, nconv_ref, nst_ref, ubuf, st_ref, project)

    @pl.when(jnp.logical_not(has_mixer))
    def _():
        project()

    @pl.when(i >= N_TILES_P)
    def _():
        cp = pltpu.make_async_copy(proj.at[par], projs_hbm.at[pl.ds((i - N_TILES_P) * TM, TM), :], psem)
        cp.start()
        cp.wait()


def _inmix(xp, xs, mod, g_mix, w_in_t, w_low_t, conv_w, conv_b, wgu_b, b_gate, n_conv, n_gla):
    def prev_tile(i):
        return jnp.clip(i - 1, 0, N_TILES_P - 1)

    return pl.pallas_call(
        _inmix_kernel,
        out_shape=(jax.ShapeDtypeStruct((T_P, D), BF16),
                   jax.ShapeDtypeStruct((B_P, 2, CW), F32),
                   jax.ShapeDtypeStruct((B_P, DV, HK), F32),
                   jax.ShapeDtypeStruct((T_S, PROJ_PAD), F32)),
        grid=(N_TILES,),
        in_specs=[pl.BlockSpec((TM, D), lambda i: (_p_tile(i), 0)),
                  pl.BlockSpec((TM, D), lambda i: (_s_tile(i), 0)),
                  *_mod_specs(0), *_mod_specs(1),
                  _const_spec((1, D)), pl.BlockSpec(memory_space=pl.ANY), _const_spec((128, D)),
                  _const_spec((3, CW)), _const_spec((1, CW)), _const_spec((128, HK)), _const_spec((1, HK)),
                  _const_spec((1, CW)), _const_spec((1, DV))],
        out_specs=(pl.BlockSpec((TM, D), lambda i: (prev_tile(i), 0)),
                   pl.BlockSpec((1, 2, CW), lambda i: (prev_tile(i) // TILES_PER_SEQ, 0, 0)),
                   pl.BlockSpec((1, DV, HK), lambda i: (prev_tile(i) // TILES_PER_SEQ, 0, 0)),
                   pl.BlockSpec(memory_space=pl.ANY)),
        scratch_shapes=[pltpu.VMEM((TM, D), BF16), pltpu.VMEM((PROJ_MAIN, D), BF16),
                        pltpu.VMEM((2, TM, PROJ_PAD), F32),
                        pltpu.VMEM((8 + TM, CW), F32), pltpu.VMEM((DV, HK), F32),
                        pltpu.SemaphoreType.DMA((2,)), pltpu.SemaphoreType.DMA(())],
        compiler_params=_cparams(),
        name="inmix",
    )(xp, xs, mod, mod, mod, mod, g_mix, w_in_t, w_low_t, conv_w, conv_b, wgu_b, b_gate, n_conv, n_gla)


def _mixer_s_kernel(p_ref, sconv_ref, sst_ref, cw_ref, cb_ref, wgu_ref, bg_ref, ncv_ref, ngl_ref,
                    y_ref, nconv_ref, nst_ref, ubuf):
    nseq = SEQ_PER_CHUNK
    bgate = p_ref[:, 0:CW]
    u = p_ref[:, CW:2 * CW] * p_ref[:, 2 * CW:3 * CW]
    ubuf[0:8, :] = jnp.zeros((8, CW), F32)
    ubuf[8:8 + CH, :] = u
    tpos = _iota2((CH, CW), 0) % L_S
    s0 = jnp.concatenate([jnp.broadcast_to(sconv_ref[s, 0:1, :], (L_S, CW)) for s in range(nseq)], axis=0)
    s1 = jnp.concatenate([jnp.broadcast_to(sconv_ref[s, 1:2, :], (L_S, CW)) for s in range(nseq)], axis=0)
    um1 = jnp.where(tpos == 0, s1, ubuf[7:7 + CH, :])
    um2 = jnp.where(tpos == 0, s0, jnp.where(tpos == 1, s1, ubuf[6:6 + CH, :]))
    y_ref[:, 0:CW] = _conv_out(bgate, u, um1, um2, cw_ref, cb_ref, ncv_ref).astype(BF16)
    for s in range(nseq):
        nconv_ref[s] = u[s * L_S + L_S - 2:(s + 1) * L_S, :]

    qt, kt, kend, bl = _gla_prep(p_ref, wgu_ref, bg_ref, CH, L_S)
    rr = _iota2((H * CH, CH), 0) % CH
    cc = _iota2((H * CH, CH), 1)
    causal = (rr >= cc) & ((rr // L_S) == (cc // L_S))
    ngl = ngl_ref[...]
    lhs_f = _head_stack(qt)
    sc = lax.dot_general(lhs_f.astype(BF16), kt.astype(BF16), (((1,), (1,)), ((), ())),
                         preferred_element_type=F32)
    sc = jnp.where(causal, sc, 0.0).astype(BF16)
    kstack_f = _head_stack(kend)
    v_all = p_ref[:, V_OFF:V_OFF + HV]
    decay_t = jnp.exp(jnp.concatenate([bl, jnp.zeros((128 - CH, HK), F32)], axis=0).T)
    o_inter = []
    for s in range(nseq):
        rs = [slice(h * CH + s * L_S, h * CH + (s + 1) * L_S) for h in range(H)]
        lhs_s = jnp.concatenate([lhs_f[r] for r in rs], axis=0).astype(BF16)
        k_s = jnp.concatenate([kstack_f[r] for r in rs], axis=0).astype(BF16)
        v_s = jnp.concatenate([v_all[s * L_S:(s + 1) * L_S, h * DV:(h + 1) * DV] for h in range(H)],
                              axis=0).astype(BF16)
        st = sst_ref[s]
        o_inter.append(jnp.dot(lhs_s, st.astype(BF16), preferred_element_type=F32))
        kv = lax.dot_general(k_s, v_s, (((0,), (0,)), ((), ())), preferred_element_type=F32)
        nst_ref[s] = st * decay_t[:, s * L_S:s * L_S + 1] + kv
    for h in range(H):
        v_h = v_all[:, h * DV:(h + 1) * DV].astype(BF16)
        g_h = p_ref[:, G_OFF + h * DV:G_OFF + (h + 1) * DV]
        oi_h = jnp.concatenate([o_inter[s][h * L_S:(h + 1) * L_S] for s in range(nseq)], axis=0)
        o_h = jnp.dot(sc[h * CH:(h + 1) * CH], v_h, preferred_element_type=F32) + oi_h
        y_ref[:, CW + h * DV:CW + (h + 1) * DV] = _gla_out(o_h, g_h, ngl).astype(BF16)


def _mixer_s(proj, sconv, sst, conv_w, conv_b, wgu_b, b_gate, n_conv, n_gla):
    nseq = SEQ_PER_CHUNK
    return pl.pallas_call(
        _mixer_s_kernel,
        out_shape=(jax.ShapeDtypeStruct((T_S, D), BF16),
                   jax.ShapeDtypeStruct((B_S, 2, CW), F32),
                   jax.ShapeDtypeStruct((B_S, HK, DV), F32)),
        grid=(T_S // CH,),
        in_specs=[pl.BlockSpec((CH, PROJ_PAD), lambda i: (i, 0)),
                  pl.BlockSpec((nseq, 2, CW), lambda i: (i, 0, 0)),
                  pl.BlockSpec((nseq, HK, DV), lambda i: (i, 0, 0)),
                  _const_spec((3, CW)), _const_spec((1, CW)), _const_spec((128, HK)), _const_spec((1, HK)),
                  _const_spec((1, CW)), _const_spec((1, DV))],
        out_specs=(pl.BlockSpec((CH, D), lambda i: (i, 0)),
                   pl.BlockSpec((nseq, 2, CW), lambda i: (i, 0, 0)),
                   pl.BlockSpec((nseq, HK, DV), lambda i: (i, 0, 0))),
        scratch_shapes=[pltpu.VMEM((8 + CH, CW), F32)],
        compiler_params=_cparams(),
        name="mixer_s",
    )(proj, sconv, sst, conv_w, conv_b, wgu_b, b_gate, n_conv, n_gla)


def _route(h2, wr_ref, br_ref, run_ref, pos_ref, rw_ref, cnt_ref):
    lt = lax.dot_general(wr_ref[...], h2.astype(BF16), (((1,), (1,)), ((), ())),
                         preferred_element_type=F32) + br_ref[...]

    coarse = lt[0:N_GROUPS]
    cmax = jnp.max(coarse, axis=0, keepdims=True)
    gi = _iota2((N_GROUPS, TM), 0)
    grp = jnp.min(jnp.where(coarse == cmax, gi, N_GROUPS), axis=0, keepdims=True)
    p_sel = 1.0 / jnp.sum(jnp.exp(coarse - cmax), axis=0, keepdims=True)
    fine = jnp.zeros((EPG, TM), F32)
    for g in range(N_GROUPS):
        fine = jnp.where(grp == g, lt[8 + g * EPG:8 + (g + 1) * EPG], fine)
    ei = _iota2((EPG, TM), 0)
    f1 = jnp.max(fine, axis=0, keepdims=True)
    i1 = jnp.min(jnp.where(fine == f1, ei, EPG), axis=0, keepdims=True)
    rest = jnp.where(ei == i1, -jnp.inf, fine)
    f2 = jnp.max(rest, axis=0, keepdims=True)
    i2 = jnp.min(jnp.where(rest == f2, ei, EPG), axis=0, keepdims=True)
    e2 = jnp.exp(f2 - f1)
    w1 = p_sel / (1.0 + e2)
    w2 = p_sel * e2 / (1.0 + e2)
    x1 = grp * EPG + i1
    x2 = grp * EPG + i2

    er = _iota2((NE, TM), 0)
    oh1 = jnp.where(er == x1, 1.0, 0.0)
    oh2 = jnp.where(er == x2, 1.0, 0.0)
    oh = oh1 + oh2
    before = jnp.where(_iota2((TM, TM), 0) < _iota2((TM, TM), 1), 1.0, 0.0).astype(BF16)
    cum = jnp.dot(oh.astype(BF16), before, preferred_element_type=F32) + run_ref[:, 0:1]
    r1 = jnp.sum(oh1 * cum, axis=0, keepdims=True).astype(I32)
    r2 = jnp.sum(oh2 * cum, axis=0, keepdims=True).astype(I32)
    run_new = run_ref[...] + jnp.sum(oh, axis=1, keepdims=True)
    run_ref[...] = run_new
    cnt_ref[...] = run_new.astype(I32)

    row = _iota2((8, TM), 0)
    pos_ref[...] = jnp.where(row == 0, x1 * E_CAP + r1, jnp.where(row == 1, x2 * E_CAP + r2, 0))
    wrow = _iota2((128, TM), 0)
    rw_ref[...] = jnp.where(wrow == 0, w1, jnp.where(wrow == 1, w2, 0.0)).T


def _outproj_kernel(ycp_ref, ycs_ref, xp_ref, xs_ref, gtp_ref, gts_ref, shp_ref, shs_ref, scp_ref, scs_ref,
                    wo_hbm, g_ref, wr_ref, br_ref,
                    x1_ref, pos_ref, rw_ref, cnt_ref, xs_out,
                    run_ref, h2_scr, wo_ref, pos_v, pos_s, sem, psem, wsem):
    i = pl.program_id(0)
    par = i % 2

    @pl.when(i == 0)
    def _():
        def chunk(c):
            return pltpu.make_async_copy(wo_hbm.at[0, pl.ds(c * TM, TM), :], h2_scr.at[c % 2], wsem.at[c % 2])

        chunk(0).start()
        for c in range(D // TM):
            if c + 1 < D // TM:
                chunk(c + 1).start()
            chunk(c).wait()
            wo_ref[c * TM:(c + 1) * TM, :] = h2_scr[c % 2].astype(BF16)

    to_smem = pltpu.make_async_copy(pos_v, pos_s, psem)

    def scatter(p):
        for r in range(TM):
            for slot in range(2):
                pltpu.make_async_copy(h2_scr.at[p, pl.ds(r, 1), :], xs_out.at[pl.ds(pos_s[slot, r], 1), :],
                                      sem).start(priority=r % 2)

    def drain():
        for _ in range(2):
            pltpu.make_async_copy(h2_scr.at[0], xs_out.at[pl.ds(0, TM), :], sem).wait()

    @pl.when(i == 0)
    def _():
        run_ref[...] = jnp.zeros((NE, 128), F32)
        h2_scr[1] = jnp.zeros((TM, D), F32)
        pos_v[...] = E_ROWS + _iota2((8, TM), 0) * TM + _iota2((8, TM), 1)
        to_smem.start()

    to_smem.wait()

    def tile(yc, x, gt, sh, sc):
        scatter(1 - par)
        mix = jnp.dot(yc, wo_ref[...], preferred_element_type=F32)
        x1 = x + gt * mix
        x1_ref[...] = x1
        h2 = _rms(x1, g_ref[...]) * (1.0 + sc) + sh
        h2_scr[par] = h2
        _route(h2, wr_ref, br_ref, run_ref, pos_v, rw_ref, cnt_ref)

    @pl.when(i < N_TILES_P)
    def _():
        tile(ycp_ref[...], xp_ref[...], _prompt_row(gtp_ref), _prompt_row(shp_ref), _prompt_row(scp_ref))

    @pl.when(i >= N_TILES_P)
    def _():
        tile(ycs_ref[...], xs_ref[...], _sample_rows(gts_ref), _sample_rows(shs_ref), _sample_rows(scs_ref))

    pos_ref[0] = pos_v[...]
    drain()
    to_smem.start()

    @pl.when(i == N_TILES - 1)
    def _():
        to_smem.wait()
        scatter(par)
        drain()


def _outproj(ycp, ycs, xp, xs, mod, w_out, g_ffn, w_r, b_r):
    return pl.pallas_call(
        _outproj_kernel,
        out_shape=(jax.ShapeDtypeStruct((T, D), F32), jax.ShapeDtypeStruct((N_TILES, 8, TM), I32),
                   jax.ShapeDtypeStruct((T, 128), F32), jax.ShapeDtypeStruct((NE, 128), I32),
                   jax.ShapeDtypeStruct((E_ROWS + 2 * TM, D), F32)),
        grid=(N_TILES,),
        in_specs=[pl.BlockSpec((TM, D), lambda i: (_p_tile(i), 0)), pl.BlockSpec((TM, D), lambda i: (_s_tile(i), 0)),
                  pl.BlockSpec((TM, D), lambda i: (_p_tile(i), 0)), pl.BlockSpec((TM, D), lambda i: (_s_tile(i), 0)),
                  *_mod_specs(2), *_mod_specs(3), *_mod_specs(4),
                  pl.BlockSpec(memory_space=pl.ANY), _const_spec((1, D)), _const_spec((R_ROWS, D)),
                  _const_spec((R_ROWS, 1))],
        out_specs=(pl.BlockSpec((TM, D), lambda i: (i, 0)), pl.BlockSpec((1, 8, TM), lambda i: (i, 0, 0)),
                   pl.BlockSpec((TM, 128), lambda i: (i, 0)), pl.BlockSpec((NE, 128), lambda i: (0, 0)),
                   pl.BlockSpec(memory_space=pl.ANY)),
        scratch_shapes=[pltpu.VMEM((NE, 128), F32), pltpu.VMEM((2, TM, D), F32), pltpu.VMEM((D, D), BF16),
                        pltpu.VMEM((8, TM), I32), pltpu.SMEM((8, TM), I32), pltpu.SemaphoreType.DMA(()),
                        pltpu.SemaphoreType.DMA(()), pltpu.SemaphoreType.DMA((2,))],
        compiler_params=_cparams(),
        name="outproj",
    )(ycp, ycs, xp, xs, mod, mod, mod, mod, mod, mod, w_out, g_ffn, w_r, b_r)


E_CHUNK = 64
E_CHUNKS = E_TILE // E_CHUNK


def _plan_tiles(cnt_ref, tb_ref, te_ref, first_ref, nval_ref, nxt_ref, slot_ref, nt_ref, nxte_ref):
    def backward(k, next_occupied):
        e = NE - 1 - k
        nxte_ref[e] = next_occupied
        return jnp.where(cnt_ref[e, 0] > 0, e, next_occupied)

    lax.fori_loop(0, NE, backward, jnp.int32(-1))

    def forward(e, carry):
        t, rank = carry
        cnt = cnt_ref[e, 0]
        n = lax.shift_right_logical(cnt + (E_TILE - 1), E_TILE.bit_length() - 1)

        def tile(k, _):
            tb_ref[t + k] = e * E_CAP_TILES + k
            te_ref[t + k] = e
            first_ref[t + k] = (k == 0).astype(I32)
            nval_ref[t + k] = jnp.minimum(cnt - k * E_TILE, E_TILE)
            nxt_ref[t + k] = nxte_ref[e]
            slot_ref[t + k] = rank & 1
            return 0

        lax.fori_loop(0, n, tile, 0)
        return t + n, rank + (n > 0).astype(I32)

    n_tiles, _ = lax.fori_loop(0, NE, forward, (jnp.int32(0), jnp.int32(0)))
    nt_ref[0] = n_tiles


def _moe_kernel(cnt_ref, x_hbm, wg_hbm, wu_hbm, wd_hbm, y_hbm,
                xbuf, ybuf, wg_st, wu_st, wd_st, wgu_b, wd_b,
                tb_ref, te_ref, first_ref, nval_ref, nxt_ref, slot_ref, nt_ref, nxte_ref, wsem, xsem, ysem):
    i = pl.program_id(0)
    par = i % 2

    @pl.when(i == 0)
    def _():
        _plan_tiles(cnt_ref, tb_ref, te_ref, first_ref, nval_ref, nxt_ref, slot_ref, nt_ref, nxte_ref)

    nt = nt_ref[0]

    def weight_copies(e, s):
        return (pltpu.make_async_copy(wg_hbm.at[e], wg_st.at[s], wsem.at[s]),
                pltpu.make_async_copy(wu_hbm.at[e], wu_st.at[s], wsem.at[s]),
                pltpu.make_async_copy(wd_hbm.at[e], wd_st.at[s], wsem.at[s]))

    def tile_chunks(t, p, output, fn):
        row0 = tb_ref[t] * E_TILE
        for c in range(E_CHUNKS):
            rows = pl.ds(c * E_CHUNK, E_CHUNK)
            hbm_rows = pl.ds(row0 + c * E_CHUNK, E_CHUNK)

            @pl.when(c * E_CHUNK < nval_ref[t])
            def _():
                if output:
                    fn(pltpu.make_async_copy(ybuf.at[p, rows, :], y_hbm.at[hbm_rows, 0], ysem.at[p]))
                else:
                    fn(pltpu.make_async_copy(x_hbm.at[hbm_rows, :], xbuf.at[p, rows, :], xsem.at[p]))

    def start(cp):
        cp.start()

    def wait(cp):
        cp.wait()

    @pl.when(i == 0)
    def _():
        tile_chunks(0, 0, False, start)

    @pl.when(i + 1 < nt)
    def _():
        tile_chunks(i + 1, 1 - par, False, start)

    @pl.when(i < nt)
    def _():
        tile_chunks(i, par, False, wait)

        @pl.when(first_ref[i] == 1)
        def _():
            s = slot_ref[i]

            @pl.when(i == 0)
            def _():
                for cp in weight_copies(te_ref[0], 0):
                    cp.start(priority=1)

            for cp in weight_copies(te_ref[i], s):
                cp.wait()

            @pl.when(nxt_ref[i] >= 0)
            def _():
                for cp in weight_copies(nxt_ref[i], 1 - s):
                    cp.start(priority=1)

            wgu_b[:, 0:DE] = wg_st[s].astype(BF16)
            wgu_b[:, DE:2 * DE] = wu_st[s].astype(BF16)
            wd_b[...] = wd_st[s].astype(BF16)

        valid = _iota2((E_TILE, D), 0) < nval_ref[i]
        x = jnp.where(valid, xbuf[par], 0.0).astype(BF16)
        ab = jnp.dot(x, wgu_b[...], preferred_element_type=F32)
        hid = (_silu(ab[:, 0:DE]) * ab[:, DE:2 * DE]).astype(BF16)
        ybuf[par] = jnp.dot(hid, wd_b[...], preferred_element_type=F32)

        tile_chunks(i, par, True, start)

        @pl.when(i >= 1)
        def _():
            tile_chunks(i - 1, 1 - par, True, wait)

        @pl.when(i == nt - 1)
        def _():
            tile_chunks(i, par, True, wait)


def _moe(counts, xs, w_eg, w_eu, w_ed):
    tile_table = pltpu.SMEM((E_TILES_MAX,), I32)
    grid_spec = pltpu.PrefetchScalarGridSpec(
        num_scalar_prefetch=1,
        grid=(E_TILES_MAX,),
        in_specs=[pl.BlockSpec(memory_space=pl.ANY)] * 4,
        out_specs=pl.BlockSpec(memory_space=pl.ANY),
        scratch_shapes=[pltpu.VMEM((2, E_TILE, D), F32), pltpu.VMEM((2, E_TILE, D), F32),
                        pltpu.VMEM((2, D, DE), F32), pltpu.VMEM((2, D, DE), F32), pltpu.VMEM((2, DE, D), F32),
                        pltpu.VMEM((D, 2 * DE), BF16), pltpu.VMEM((DE, D), BF16),
                        tile_table, tile_table, tile_table, tile_table, tile_table, tile_table,
                        pltpu.SMEM((1,), I32), pltpu.SMEM((NE,), I32),
                        pltpu.SemaphoreType.DMA((2,)), pltpu.SemaphoreType.DMA((2,)), pltpu.SemaphoreType.DMA((2,))],
    )
    return pl.pallas_call(
        _moe_kernel,
        out_shape=jax.ShapeDtypeStruct((E_ROWS, 1, D), F32),
        grid_spec=grid_spec,
        compiler_params=_cparams(),
        name="moe",
    )(counts, xs, w_eg, w_eu, w_ed)


def _final_kernel(pos0_ref, posn_ref, x1_ref, rw_ref, gtp_ref, gts_ref, g_ref, ys_ref, yp_ref, ysm_ref, buf, sem):
    i = pl.program_id(0)
    par = i % 2

    def gather(p_ref, p):
        for r in range(TM):
            for slot in range(2):
                pltpu.make_async_copy(ys_ref.at[p_ref[0, slot, r]], buf.at[p, slot, pl.ds(r, 1), :],
                                      sem.at[p]).start(priority=r % 2)

    @pl.when(i == 0)
    def _():
        gather(pos0_ref, 0)

    @pl.when(i < N_TILES - 1)
    def _():
        gather(posn_ref, 1 - par)

    for slot in range(2):
        pltpu.make_async_copy(ys_ref.at[pl.ds(0, TM), 0], buf.at[par, slot], sem.at[par]).wait()

    moe = rw_ref[:, 0:1] * buf[par, 0] + rw_ref[:, 1:2] * buf[par, 1]

    @pl.when(i < N_TILES_P)
    def _():
        yp_ref[...] = _rms(x1_ref[...] + _prompt_row(gtp_ref) * moe, g_ref[...])

    @pl.when(i >= N_TILES_P)
    def _():
        ysm_ref[...] = _rms(x1_ref[...] + _sample_rows(gts_ref) * moe, g_ref[...])


def _final(pos, x1, rw, mod, g_fin, ys):
    return pl.pallas_call(
        _final_kernel,
        out_shape=(jax.ShapeDtypeStruct((T_P, D), F32), jax.ShapeDtypeStruct((T_S, D), F32)),
        grid=(N_TILES,),
        in_specs=[pl.BlockSpec((1, 8, TM), lambda i: (0, 0, 0), memory_space=pltpu.SMEM),
                  pl.BlockSpec((1, 8, TM), lambda i: (jnp.minimum(i + 1, N_TILES - 1), 0, 0),
                               memory_space=pltpu.SMEM),
                  pl.BlockSpec((TM, D), lambda i: (i, 0)),
                  pl.BlockSpec((TM, 128), lambda i: (i, 0)),
                  *_mod_specs(5),
                  _const_spec((1, D)),
                  pl.BlockSpec(memory_space=pl.ANY)],
        out_specs=(pl.BlockSpec((TM, D), lambda i: (_p_tile(i), 0)),
                   pl.BlockSpec((TM, D), lambda i: (_s_tile(i), 0))),
        scratch_shapes=[pltpu.VMEM((2, 2, TM, D), F32), pltpu.SemaphoreType.DMA((2,))],
        compiler_params=_cparams(),
        name="final",
    )(pos, pos, x1, rw, mod, mod, g_fin, ys)


def kernel(x_prompt, x_sample, c_prompt, c_sample, state_conv, state_gla, w_ada, b_ada, norm_mix, w_in, conv_w,
           conv_b, w_gate_up, b_gate, norm_conv, norm_gla, w_out, norm_ffn, w_coarse, b_coarse, w_fine, b_fine,
           w_exp_gate, w_exp_up, w_exp_down, norm_final):
    xp = x_prompt.reshape(T_P, D)
    xs = x_sample.reshape(T_S, D)
    c_all = jnp.concatenate([c_sample, c_prompt, jnp.zeros((C_ROWS - B_S - B_P, D), F32)], axis=0)
    mod = _ada(c_all, w_ada[0], b_ada[0][None, :])

    w_in_t = jnp.swapaxes(w_in, 1, 2)
    w_low_t = jnp.pad(w_in_t[0, PROJ_MAIN:, :], ((0, 128 - RANK), (0, 0))).astype(BF16)
    wgu_b = jnp.pad(w_gate_up[0], ((0, 128 - RANK), (0, 0))).astype(BF16)
    w_r = jnp.concatenate([w_coarse[0].T, jnp.zeros((8 - N_GROUPS, D), F32), w_fine[0].T], axis=0)
    b_r = jnp.concatenate([b_coarse[0], jnp.zeros((8 - N_GROUPS,), F32), b_fine[0]])[:, None]
    g_mix, g_ffn, g_fin = norm_mix[0][None, :], norm_ffn[0][None, :], norm_final[None, :]
    mix_consts = (conv_w[0], conv_b[0][None, :], wgu_b, b_gate[0][None, :], norm_conv[0][None, :],
                  norm_gla[0][None, :])

    ycat_p, nconv_p, nst_p, proj_s = _inmix(xp, xs, mod, g_mix, w_in_t, w_low_t, *mix_consts)
    ycat_s, nconv_s, nst_s = _mixer_s(proj_s, state_conv[0], state_gla[0].reshape(B_S, HK, DV), *mix_consts)

    x1, pos, rw, counts, x_sorted = _outproj(ycat_p, ycat_s, xp, xs, mod, w_out, g_ffn, w_r.astype(BF16), b_r)
    y_sorted = _moe(counts, x_sorted, w_exp_gate[0], w_exp_up[0], w_exp_down[0])
    y_p, y_s = _final(pos, x1, rw, mod, g_fin, y_sorted)

    new_gla_p = nst_p.reshape(B_P, DV, H, DK).transpose(0, 2, 3, 1)[None]
    new_gla_s = nst_s.reshape(1, B_S, H, DK, DV)
    return (y_p.reshape(B_P, L_P, D), y_s.reshape(B_S, L_S, D), nconv_p[None], new_gla_p,
            nconv_s[None], new_gla_s)
```

```python
import functools

import jax
import jax.numpy as jnp
from jax import lax
from jax.experimental import pallas as pl
from jax.experimental.pallas import tpu as pltpu

F32 = jnp.float32
BF16 = jnp.bfloat16
I32 = jnp.int32

D = 2048
CW = 1024
H = 8
DK = 64
DV = 128
HK = H * DK
HV = H * DV
RANK = 16
TAU = 16.0
N_GROUPS = 4
EPG = 8
NE = N_GROUPS * EPG
DE = 512
EPS = 1e-6

B_P, L_P = 4, 2048
B_S, L_S = 128, 8
T_P = B_P * L_P
T_S = B_S * L_S
T = T_P + T_S

TM = 256
CH = 64
N_TILES = T // TM
N_TILES_P = T_P // TM
TILES_PER_SEQ = L_P // TM
SEQ_PER_TILE = TM // L_S
SEQ_PER_CHUNK = CH // L_S

PROJ_MAIN = 3 * CW + 2 * HK + 2 * HV
PROJ_PAD = PROJ_MAIN + 128
C_ROWS = 136
P_ROW_BLOCK = B_S // 8
R_ROWS = 40

E_TILE = 256
E_TILES_MAX = (2 * T) // E_TILE + NE
E_CAP = T
E_CAP_TILES = E_CAP // E_TILE
E_ROWS = NE * E_CAP
VMEM_LIMIT = 60 * 1024 * 1024


def _cparams(n_axes=1, vmem=VMEM_LIMIT):
    return pltpu.CompilerParams(dimension_semantics=("arbitrary",) * n_axes, vmem_limit_bytes=vmem)


def _rms(x, g):
    return x * lax.rsqrt(jnp.mean(x * x, axis=-1, keepdims=True) + EPS) * g


def _sigmoid(x):
    return 1.0 / (1.0 + jnp.exp(-x))


def _silu(x):
    return x * _sigmoid(x)


def _log_sigmoid(x):
    return jnp.minimum(x, 0.0) - jnp.log(1.0 + jnp.exp(-jnp.abs(x)))


def _iota2(shape, axis):
    return lax.broadcasted_iota(I32, shape, axis)


def _expand_rows(ref, n, reps):
    return jnp.concatenate([jnp.broadcast_to(ref[j:j + 1, :], (reps, ref.shape[-1])) for j in range(n)], axis=0)


def _prompt_row(ref):
    return ref[pl.ds(pl.program_id(0) // TILES_PER_SEQ, 1), :]


def _sample_rows(ref):
    return _expand_rows(ref, SEQ_PER_TILE, L_S)


def _p_tile(i):
    return jnp.minimum(i, N_TILES_P - 1)


def _s_tile(i):
    return jnp.maximum(i - N_TILES_P, 0)


def _mod_specs(col):
    return [pl.BlockSpec((8, D), lambda i, *_: (P_ROW_BLOCK, col)),
            pl.BlockSpec((SEQ_PER_TILE, D), lambda i, *_: (_s_tile(i), col))]


def _const_spec(shape):
    zeros = (0,) * len(shape)
    return pl.BlockSpec(shape, lambda *_: zeros, pipeline_mode=pl.Buffered(1))


def _ada_kernel(c_ref, w_ref, b_ref, o_ref):
    s = _silu(c_ref[...]).astype(BF16)
    o_ref[...] = jnp.dot(s, w_ref[...].astype(BF16), preferred_element_type=F32) + b_ref[...]


def _ada(c_all, w_ada, b_ada):
    tn = 1024
    return pl.pallas_call(
        _ada_kernel,
        out_shape=jax.ShapeDtypeStruct((C_ROWS, 6 * D), F32),
        grid=(6 * D // tn,),
        in_specs=[pl.BlockSpec((C_ROWS, D), lambda j: (0, 0)),
                  pl.BlockSpec((D, tn), lambda j: (0, j)),
                  pl.BlockSpec((1, tn), lambda j: (0, j))],
        out_specs=pl.BlockSpec((C_ROWS, tn), lambda j: (0, j)),
        compiler_params=_cparams(),
        name="ada",
    )(c_all, w_ada, b_ada)


W_CHUNK = TM
N_W_CHUNKS = PROJ_MAIN // W_CHUNK
NT_DIMS = (((1,), (1,)), ((), ()))


def _gla_prep(p_ref, wgu_ref, bg_ref, rows, seg, totals_from_last_row=False):
    q = p_ref[:, 3 * CW:3 * CW + HK]
    k = p_ref[:, 3 * CW + HK:3 * CW + 2 * HK]
    a = p_ref[:, PROJ_MAIN:PROJ_PAD].astype(BF16)
    z = jnp.dot(a, wgu_ref[...], preferred_element_type=F32) + bg_ref[...]
    la = _log_sigmoid(z) * (1.0 / TAU)
    la0 = la.astype(BF16)
    rem = la - la0.astype(F32)
    la1 = rem.astype(BF16)
    la2 = (rem - la1.astype(F32)).astype(BF16)
    sel_rows = rows if totals_from_last_row else 2 * rows
    r = _iota2((sel_rows, rows), 0)
    c = _iota2((sel_rows, rows), 1)
    same = ((r % rows) // seg) == (c // seg)
    sel = jnp.where(same & ((r >= rows) | (c <= r)), 1.0, 0.0).astype(BF16)
    sums = sum(jnp.dot(sel, part, preferred_element_type=F32) for part in (la0, la1, la2))
    b = sums[0:rows]
    if totals_from_last_row:
        ebl = jnp.concatenate([jnp.broadcast_to(jnp.exp(b[s + seg - 1:s + seg, :]), (seg, HK))
                               for s in range(0, rows, seg)], axis=0)
    else:
        ebl = jnp.exp(sums[rows:2 * rows])
    qt = q * jnp.exp(b) * (DK ** -0.5)
    kt = k * jnp.exp(-b)
    kend = kt * ebl
    return qt, kt, kend, ebl


def _head_stack(x):
    rows = x.shape[0]
    t = jnp.concatenate([x] * H, axis=0)
    keep = (_iota2((H * rows, HK), 0) // rows) == (_iota2((H * rows, HK), 1) // DK)
    return jnp.where(keep, t, 0.0)


def _gla_out(o_h, g_h, ngl):
    on = o_h * lax.rsqrt(jnp.mean(o_h * o_h, axis=-1, keepdims=True) + EPS) * ngl
    return on * _silu(g_h)


def _conv_out(bg, u, um1, um2, cw_ref, cb_ref, ncv_ref):
    conv_y = cb_ref[...] + cw_ref[0:1, :] * um2 + cw_ref[1:2, :] * um1 + cw_ref[2:3, :] * u
    return _rms(bg * conv_y, ncv_ref[...])


V_OFF = 3 * CW + 2 * HK
G_OFF = V_OFF + HV


def _mixer_p_tile(j, p_ref, cw_ref, cb_ref, wgu_ref, bg_ref, ncv_ref, ngl_ref,
                  y_ref, nconv_ref, nst_ref, ubuf, st_ref, also):
    @pl.when(j == 0)
    def _():
        ubuf[0:8, :] = jnp.zeros((8, CW), F32)
        st_ref[...] = jnp.zeros((DV, HK), F32)

    bgate = p_ref[:, 0:CW]
    u = p_ref[:, CW:2 * CW] * p_ref[:, 2 * CW:3 * CW]
    ubuf[8:8 + TM, :] = u
    um1 = ubuf[7:7 + TM, :]
    um2 = ubuf[6:6 + TM, :]
    y_ref[:, 0:CW] = _conv_out(bgate, u, um1, um2, cw_ref, cb_ref, ncv_ref).astype(BF16)
    ubuf[6:8, :] = u[TM - 2:TM, :]

    qt, kt, kend, ebl = _gla_prep(p_ref, wgu_ref, bg_ref, TM, CH, totals_from_last_row=True)
    causal = _iota2((2 * CH, CH), 0) % CH >= _iota2((2 * CH, CH), 1)
    first_head = _iota2((CH, 2 * DK), 1) < DK
    ngl = ngl_ref[...]

    def pair_stack(x):
        return jnp.concatenate([jnp.where(first_head, x, 0.0), jnp.where(first_head, 0.0, x)], axis=0).astype(BF16)

    for c in range(TM // CH):
        r0 = c * CH
        for m in range(H // 2):
            lanes = slice(m * 2 * DK, (m + 1) * 2 * DK)
            lhs = pair_stack(qt[r0:r0 + CH, lanes])
            sc = lax.dot_general(lhs, kt[r0:r0 + CH, lanes].astype(BF16), (((1,), (1,)), ((), ())),
                                 preferred_element_type=F32)
            sc = jnp.where(causal, sc, 0.0).astype(BF16)
            st = st_ref[:, lanes]
            o_inter = lax.dot_general(lhs, st.astype(BF16), (((1,), (1,)), ((), ())),
                                      preferred_element_type=F32)
            vs = []
            for hh in range(2):
                h = 2 * m + hh
                v_h = p_ref[r0:r0 + CH, V_OFF + h * DV:V_OFF + (h + 1) * DV].astype(BF16)
                g_h = p_ref[r0:r0 + CH, G_OFF + h * DV:G_OFF + (h + 1) * DV]
                vs.append(v_h)
                o_h = (jnp.dot(sc[hh * CH:(hh + 1) * CH], v_h, preferred_element_type=F32)
                       + o_inter[hh * CH:(hh + 1) * CH])
                y_ref[r0:r0 + CH, CW + h * DV:CW + (h + 1) * DV] = _gla_out(o_h, g_h, ngl).astype(BF16)
            kv_t = lax.dot_general(jnp.concatenate(vs, axis=0), pair_stack(kend[r0:r0 + CH, lanes]),
                                   (((0,), (0,)), ((), ())), preferred_element_type=F32)
            st_ref[:, lanes] = st * ebl[r0:r0 + 1, lanes] + kv_t

    also()

    @pl.when(j == TILES_PER_SEQ - 1)
    def _():
        nconv_ref[0] = ubuf[6:8, :]
        nst_ref[0] = st_ref[...]


def _inmix_kernel(xp_ref, xs_ref, shp_ref, shs_ref, scp_ref, scs_ref, g_ref, wt_hbm, wa_ref,
                  cw_ref, cb_ref, wgu_ref, bg_ref, ncv_ref, ngl_ref,
                  y_ref, nconv_ref, nst_ref, projs_hbm,
                  h_scr, w_scr, proj, ubuf, st_ref, wsem, psem):
    i = pl.program_id(0)
    par = i % 2

    @pl.when(i == 0)
    def _():
        def chunk(c):
            return pltpu.make_async_copy(wt_hbm.at[0, pl.ds(c * W_CHUNK, W_CHUNK), :],
                                         proj.at[c % 2, :, pl.ds(0, D)], wsem.at[c % 2])

        chunk(0).start()
        for c in range(N_W_CHUNKS):
            if c + 1 < N_W_CHUNKS:
                chunk(c + 1).start()
            chunk(c).wait()
            w_scr[c * W_CHUNK:(c + 1) * W_CHUNK, :] = proj[c % 2, :, 0:D].astype(BF16)

    @pl.when(i < N_TILES_P)
    def _():
        h = _rms(xp_ref[...], g_ref[...]) * (1.0 + _prompt_row(scp_ref)) + _prompt_row(shp_ref)
        h_scr[...] = h.astype(BF16)

    @pl.when(i >= N_TILES_P)
    def _():
        h = _rms(xs_ref[...], g_ref[...]) * (1.0 + _sample_rows(scs_ref)) + _sample_rows(shs_ref)
        h_scr[...] = h.astype(BF16)

    def project():
        h = h_scr[...]
        dst = proj.at[par]
        dst[:, 0:PROJ_MAIN] = lax.dot_general(h, w_scr[...], NT_DIMS, preferred_element_type=F32)
        dst[:, PROJ_MAIN:PROJ_PAD] = lax.dot_general(h, wa_ref[...], NT_DIMS, preferred_element_type=F32)

    has_mixer = (i >= 1) & (i <= N_TILES_P)

    @pl.when(has_mixer)
    def _():
        _mixer_p_tile((i - 1) % TILES_PER_SEQ, proj.at[1 - par], cw_ref, cb_ref, wgu_ref, bg_ref, ncv_ref, ngl_ref,
                      y_ref, nconv_ref, nst_ref, ubuf, st_ref, project)

    @pl.when(jnp.logical_not(has_mixer))
    def _():
        project()

    @pl.when(i >= N_TILES_P)
    def _():
        cp = pltpu.make_async_copy(proj.at[par], projs_hbm.at[pl.ds((i - N_TILES_P) * TM, TM), :], psem)
        cp.start()
        cp.wait()


def _inmix(xp, xs, mod, g_mix, w_in_t, w_low_t, conv_w, conv_b, wgu_b, b_gate, n_conv, n_gla):
    def prev_tile(i):
        return jnp.clip(i - 1, 0, N_TILES_P - 1)

    return pl.pallas_call(
        _inmix_kernel,
        out_shape=(jax.ShapeDtypeStruct((T_P, D), BF16),
                   jax.ShapeDtypeStruct((B_P, 2, CW), F32),
                   jax.ShapeDtypeStruct((B_P, DV, HK), F32),
                   jax.ShapeDtypeStruct((T_S, PROJ_PAD), F32)),
        grid=(N_TILES,),
        in_specs=[pl.BlockSpec((TM, D), lambda i: (_p_tile(i), 0)),
                  pl.BlockSpec((TM, D), lambda i: (_s_tile(i), 0)),
                  *_mod_specs(0), *_mod_specs(1),
                  _const_spec((1, D)), pl.BlockSpec(memory_space=pl.ANY), _const_spec((128, D)),
                  _const_spec((3, CW)), _const_spec((1, CW)), _const_spec((128, HK)), _const_spec((1, HK)),
                  _const_spec((1, CW)), _const_spec((1, DV))],
        out_specs=(pl.BlockSpec((TM, D), lambda i: (prev_tile(i), 0)),
                   pl.BlockSpec((1, 2, CW), lambda i: (prev_tile(i) // TILES_PER_SEQ, 0, 0)),
                   pl.BlockSpec((1, DV, HK), lambda i: (prev_tile(i) // TILES_PER_SEQ, 0, 0)),
                   pl.BlockSpec(memory_space=pl.ANY)),
        scratch_shapes=[pltpu.VMEM((TM, D), BF16), pltpu.VMEM((PROJ_MAIN, D), BF16),
                        pltpu.VMEM((2, TM, PROJ_PAD), F32),
                        pltpu.VMEM((8 + TM, CW), F32), pltpu.VMEM((DV, HK), F32),
                        pltpu.SemaphoreType.DMA((2,)), pltpu.SemaphoreType.DMA(())],
        compiler_params=_cparams(),
        name="inmix",
    )(xp, xs, mod, mod, mod, mod, g_mix, w_in_t, w_low_t, conv_w, conv_b, wgu_b, b_gate, n_conv, n_gla)


def _mixer_s_kernel(p_ref, sconv_ref, sst_ref, cw_ref, cb_ref, wgu_ref, bg_ref, ncv_ref, ngl_ref,
                    y_ref, nconv_ref, nst_ref, ubuf):
    nseq = SEQ_PER_CHUNK
    bgate = p_ref[:, 0:CW]
    u = p_ref[:, CW:2 * CW] * p_ref[:, 2 * CW:3 * CW]
    ubuf[0:8, :] = jnp.zeros((8, CW), F32)
    ubuf[8:8 + CH, :] = u
    tpos = _iota2((CH, CW), 0) % L_S
    s0 = jnp.concatenate([jnp.broadcast_to(sconv_ref[s, 0:1, :], (L_S, CW)) for s in range(nseq)], axis=0)
    s1 = jnp.concatenate([jnp.broadcast_to(sconv_ref[s, 1:2, :], (L_S, CW)) for s in range(nseq)], axis=0)
    um1 = jnp.where(tpos == 0, s1, ubuf[7:7 + CH, :])
    um2 = jnp.where(tpos == 0, s0, jnp.where(tpos == 1, s1, ubuf[6:6 + CH, :]))
    y_ref[:, 0:CW] = _conv_out(bgate, u, um1, um2, cw_ref, cb_ref, ncv_ref).astype(BF16)
    for s in range(nseq):
        nconv_ref[s] = u[s * L_S + L_S - 2:(s + 1) * L_S, :]

    qt, kt, kend, ebl = _gla_prep(p_ref, wgu_ref, bg_ref, CH, L_S)
    rr = _iota2((H * CH, CH), 0) % CH
    cc = _iota2((H * CH, CH), 1)
    causal = (rr >= cc) & ((rr // L_S) == (cc // L_S))
    ngl = ngl_ref[...]
    lhs_f = _head_stack(qt)
    sc = lax.dot_general(lhs_f.astype(BF16), kt.astype(BF16), (((1,), (1,)), ((), ())),
                         preferred_element_type=F32)
    sc = jnp.where(causal, sc, 0.0).astype(BF16)
    kstack_f = _head_stack(kend)
    v_all = p_ref[:, V_OFF:V_OFF + HV]
    decay_t = jnp.concatenate([ebl, jnp.zeros((128 - CH, HK), F32)], axis=0).T
    o_inter = []
    for s in range(nseq):
        rs = [slice(h * CH + s * L_S, h * CH + (s + 1) * L_S) for h in range(H)]
        lhs_s = jnp.concatenate([lhs_f[r] for r in rs], axis=0).astype(BF16)
        k_s = jnp.concatenate([kstack_f[r] for r in rs], axis=0).astype(BF16)
        v_s = jnp.concatenate([v_all[s * L_S:(s + 1) * L_S, h * DV:(h + 1) * DV] for h in range(H)],
                              axis=0).astype(BF16)
        st = sst_ref[s]
        o_inter.append(jnp.dot(lhs_s, st.astype(BF16), preferred_element_type=F32))
        kv = lax.dot_general(k_s, v_s, (((0,), (0,)), ((), ())), preferred_element_type=F32)
        nst_ref[s] = st * decay_t[:, s * L_S:s * L_S + 1] + kv
    for h in range(H):
        v_h = v_all[:, h * DV:(h + 1) * DV].astype(BF16)
        g_h = p_ref[:, G_OFF + h * DV:G_OFF + (h + 1) * DV]
        oi_h = jnp.concatenate([o_inter[s][h * L_S:(h + 1) * L_S] for s in range(nseq)], axis=0)
        o_h = jnp.dot(sc[h * CH:(h + 1) * CH], v_h, preferred_element_type=F32) + oi_h
        y_ref[:, CW + h * DV:CW + (h + 1) * DV] = _gla_out(o_h, g_h, ngl).astype(BF16)


def _mixer_s(proj, sconv, sst, conv_w, conv_b, wgu_b, b_gate, n_conv, n_gla):
    nseq = SEQ_PER_CHUNK
    return pl.pallas_call(
        _mixer_s_kernel,
        out_shape=(jax.ShapeDtypeStruct((T_S, D), BF16),
                   jax.ShapeDtypeStruct((B_S, 2, CW), F32),
                   jax.ShapeDtypeStruct((B_S, HK, DV), F32)),
        grid=(T_S // CH,),
        in_specs=[pl.BlockSpec((CH, PROJ_PAD), lambda i: (i, 0)),
                  pl.BlockSpec((nseq, 2, CW), lambda i: (i, 0, 0)),
                  pl.BlockSpec((nseq, HK, DV), lambda i: (i, 0, 0)),
                  _const_spec((3, CW)), _const_spec((1, CW)), _const_spec((128, HK)), _const_spec((1, HK)),
                  _const_spec((1, CW)), _const_spec((1, DV))],
        out_specs=(pl.BlockSpec((CH, D), lambda i: (i, 0)),
                   pl.BlockSpec((nseq, 2, CW), lambda i: (i, 0, 0)),
                   pl.BlockSpec((nseq, HK, DV), lambda i: (i, 0, 0))),
        scratch_shapes=[pltpu.VMEM((8 + CH, CW), F32)],
        compiler_params=_cparams(),
        name="mixer_s",
    )(proj, sconv, sst, conv_w, conv_b, wgu_b, b_gate, n_conv, n_gla)


def _route(h2, wr_ref, br_ref, run_ref, pos_ref, rw_ref, cnt_ref):
    lt = lax.dot_general(wr_ref[...], h2.astype(BF16), (((1,), (1,)), ((), ())),
                         preferred_element_type=F32) + br_ref[...]

    coarse = lt[0:N_GROUPS]
    cmax = jnp.max(coarse, axis=0, keepdims=True)
    gi = _iota2((N_GROUPS, TM), 0)
    grp = jnp.min(jnp.where(coarse == cmax, gi, N_GROUPS), axis=0, keepdims=True)
    p_sel = 1.0 / jnp.sum(jnp.exp(coarse - cmax), axis=0, keepdims=True)
    fine = jnp.zeros((EPG, TM), F32)
    for g in range(N_GROUPS):
        fine = jnp.where(grp == g, lt[8 + g * EPG:8 + (g + 1) * EPG], fine)
    ei = _iota2((EPG, TM), 0)
    f1 = jnp.max(fine, axis=0, keepdims=True)
    i1 = jnp.min(jnp.where(fine == f1, ei, EPG), axis=0, keepdims=True)
    rest = jnp.where(ei == i1, -jnp.inf, fine)
    f2 = jnp.max(rest, axis=0, keepdims=True)
    i2 = jnp.min(jnp.where(rest == f2, ei, EPG), axis=0, keepdims=True)
    e2 = jnp.exp(f2 - f1)
    w1 = p_sel / (1.0 + e2)
    w2 = p_sel * e2 / (1.0 + e2)
    x1 = grp * EPG + i1
    x2 = grp * EPG + i2

    er = _iota2((NE, TM), 0)
    oh1 = jnp.where(er == x1, 1.0, 0.0)
    oh2 = jnp.where(er == x2, 1.0, 0.0)
    oh = oh1 + oh2
    before = jnp.where(_iota2((TM, TM), 0) < _iota2((TM, TM), 1), 1.0, 0.0).astype(BF16)
    cum = jnp.dot(oh.astype(BF16), before, preferred_element_type=F32) + run_ref[:, 0:1]
    r1 = jnp.sum(oh1 * cum, axis=0, keepdims=True).astype(I32)
    r2 = jnp.sum(oh2 * cum, axis=0, keepdims=True).astype(I32)
    run_new = run_ref[...] + jnp.sum(oh, axis=1, keepdims=True)
    run_ref[...] = run_new
    cnt_ref[...] = run_new.astype(I32)

    row = _iota2((8, TM), 0)
    pos_ref[...] = jnp.where(row == 0, x1 * E_CAP + r1, jnp.where(row == 1, x2 * E_CAP + r2, 0))
    wrow = _iota2((128, TM), 0)
    rw_ref[...] = jnp.where(wrow == 0, w1, jnp.where(wrow == 1, w2, 0.0)).T


def _outproj_kernel(ycp_ref, ycs_ref, xp_ref, xs_ref, gtp_ref, gts_ref, shp_ref, shs_ref, scp_ref, scs_ref,
                    wo_hbm, g_ref, wr_ref, br_ref,
                    x1_ref, pos_ref, rw_ref, cnt_ref, xs_out,
                    run_ref, h2_scr, wo_ref, pos_v, pos_s, sem, psem, wsem):
    i = pl.program_id(0)
    par = i % 2

    @pl.when(i == 0)
    def _():
        def chunk(c):
            return pltpu.make_async_copy(wo_hbm.at[0, pl.ds(c * TM, TM), :], h2_scr.at[c % 2], wsem.at[c % 2])

        chunk(0).start()
        for c in range(D // TM):
            if c + 1 < D // TM:
                chunk(c + 1).start()
            chunk(c).wait()
            wo_ref[c * TM:(c + 1) * TM, :] = h2_scr[c % 2].astype(BF16)

    to_smem = pltpu.make_async_copy(pos_v, pos_s, psem)

    def scatter(p):
        for r in range(TM):
            for slot in range(2):
                pltpu.make_async_copy(h2_scr.at[p, pl.ds(r, 1), :], xs_out.at[pl.ds(pos_s[slot, r], 1), :],
                                      sem).start(priority=r % 2)

    def drain():
        for _ in range(2):
            pltpu.make_async_copy(h2_scr.at[0], xs_out.at[pl.ds(0, TM), :], sem).wait()

    @pl.when(i == 0)
    def _():
        run_ref[...] = jnp.zeros((NE, 128), F32)
        h2_scr[1] = jnp.zeros((TM, D), F32)
        pos_v[...] = E_ROWS + _iota2((8, TM), 0) * TM + _iota2((8, TM), 1)
        to_smem.start()

    to_smem.wait()

    def tile(yc, x, gt, sh, sc):
        scatter(1 - par)
        mix = jnp.dot(yc, wo_ref[...], preferred_element_type=F32)
        x1 = x + gt * mix
        x1_ref[...] = x1
        h2 = _rms(x1, g_ref[...]) * (1.0 + sc) + sh
        h2_scr[par] = h2
        _route(h2, wr_ref, br_ref, run_ref, pos_v, rw_ref, cnt_ref)

    @pl.when(i < N_TILES_P)
    def _():
        tile(ycp_ref[...], xp_ref[...], _prompt_row(gtp_ref), _prompt_row(shp_ref), _prompt_row(scp_ref))

    @pl.when(i >= N_TILES_P)
    def _():
        tile(ycs_ref[...], xs_ref[...], _sample_rows(gts_ref), _sample_rows(shs_ref), _sample_rows(scs_ref))

    pos_ref[0] = pos_v[...]
    drain()
    to_smem.start()

    @pl.when(i == N_TILES - 1)
    def _():
        to_smem.wait()
        scatter(par)
        drain()


def _outproj(ycp, ycs, xp, xs, mod, w_out, g_ffn, w_r, b_r):
    return pl.pallas_call(
        _outproj_kernel,
        out_shape=(jax.ShapeDtypeStruct((T, D), F32), jax.ShapeDtypeStruct((N_TILES, 8, TM), I32),
                   jax.ShapeDtypeStruct((T, 128), F32), jax.ShapeDtypeStruct((NE, 128), I32),
                   jax.ShapeDtypeStruct((E_ROWS + 2 * TM, D), F32)),
        grid=(N_TILES,),
        in_specs=[pl.BlockSpec((TM, D), lambda i: (_p_tile(i), 0)), pl.BlockSpec((TM, D), lambda i: (_s_tile(i), 0)),
                  pl.BlockSpec((TM, D), lambda i: (_p_tile(i), 0)), pl.BlockSpec((TM, D), lambda i: (_s_tile(i), 0)),
                  *_mod_specs(2), *_mod_specs(3), *_mod_specs(4),
                  pl.BlockSpec(memory_space=pl.ANY), _const_spec((1, D)), _const_spec((R_ROWS, D)),
                  _const_spec((R_ROWS, 1))],
        out_specs=(pl.BlockSpec((TM, D), lambda i: (i, 0)), pl.BlockSpec((1, 8, TM), lambda i: (i, 0, 0)),
                   pl.BlockSpec((TM, 128), lambda i: (i, 0)), pl.BlockSpec((NE, 128), lambda i: (0, 0)),
                   pl.BlockSpec(memory_space=pl.ANY)),
        scratch_shapes=[pltpu.VMEM((NE, 128), F32), pltpu.VMEM((2, TM, D), F32), pltpu.VMEM((D, D), BF16),
                        pltpu.VMEM((8, TM), I32), pltpu.SMEM((8, TM), I32), pltpu.SemaphoreType.DMA(()),
                        pltpu.SemaphoreType.DMA(()), pltpu.SemaphoreType.DMA((2,))],
        compiler_params=_cparams(),
        name="outproj",
    )(ycp, ycs, xp, xs, mod, mod, mod, mod, mod, mod, w_out, g_ffn, w_r, b_r)


E_CHUNK = 64
E_CHUNKS = E_TILE // E_CHUNK


def _plan_tiles(cnt_ref, tb_ref, te_ref, first_ref, nval_ref, nxt_ref, slot_ref, nt_ref, nxte_ref):
    def backward(k, next_occupied):
        e = NE - 1 - k
        nxte_ref[e] = next_occupied
        return jnp.where(cnt_ref[e, 0] > 0, e, next_occupied)

    lax.fori_loop(0, NE, backward, jnp.int32(-1))

    def forward(e, carry):
        t, rank = carry
        cnt = cnt_ref[e, 0]
        n = lax.shift_right_logical(cnt + (E_TILE - 1), E_TILE.bit_length() - 1)

        def tile(k, _):
            tb_ref[t + k] = e * E_CAP_TILES + k
            te_ref[t + k] = e
            first_ref[t + k] = (k == 0).astype(I32)
            nval_ref[t + k] = jnp.minimum(cnt - k * E_TILE, E_TILE)
            nxt_ref[t + k] = nxte_ref[e]
            slot_ref[t + k] = rank & 1
            return 0

        lax.fori_loop(0, n, tile, 0)
        return t + n, rank + (n > 0).astype(I32)

    n_tiles, _ = lax.fori_loop(0, NE, forward, (jnp.int32(0), jnp.int32(0)))
    nt_ref[0] = n_tiles


def _moe_kernel(cnt_ref, x_hbm, wg_hbm, wu_hbm, wd_hbm, y_hbm,
                xbuf, ybuf, wg_st, wu_st, wd_st, wgu_b, wd_b,
                tb_ref, te_ref, first_ref, nval_ref, nxt_ref, slot_ref, nt_ref, nxte_ref, wsem, xsem, ysem):
    i = pl.program_id(0)
    par = i % 2

    @pl.when(i == 0)
    def _():
        _plan_tiles(cnt_ref, tb_ref, te_ref, first_ref, nval_ref, nxt_ref, slot_ref, nt_ref, nxte_ref)

    nt = nt_ref[0]

    def weight_copies(e, s):
        return (pltpu.make_async_copy(wg_hbm.at[e], wg_st.at[s], wsem.at[s]),
                pltpu.make_async_copy(wu_hbm.at[e], wu_st.at[s], wsem.at[s]),
                pltpu.make_async_copy(wd_hbm.at[e], wd_st.at[s], wsem.at[s]))

    def tile_chunks(t, p, output, fn):
        row0 = tb_ref[t] * E_TILE
        for c in range(E_CHUNKS):
            rows = pl.ds(c * E_CHUNK, E_CHUNK)
            hbm_rows = pl.ds(row0 + c * E_CHUNK, E_CHUNK)

            @pl.when(c * E_CHUNK < nval_ref[t])
            def _():
                if output:
                    fn(pltpu.make_async_copy(ybuf.at[p, rows, :], y_hbm.at[hbm_rows, 0], ysem.at[p]))
                else:
                    fn(pltpu.make_async_copy(x_hbm.at[hbm_rows, :], xbuf.at[p, rows, :], xsem.at[p]))

    def start(cp):
        cp.start()

    def wait(cp):
        cp.wait()

    @pl.when(i == 0)
    def _():
        tile_chunks(0, 0, False, start)

    @pl.when(i + 1 < nt)
    def _():
        tile_chunks(i + 1, 1 - par, False, start)

    @pl.when(i < nt)
    def _():
        tile_chunks(i, par, False, wait)

        @pl.when(first_ref[i] == 1)
        def _():
            s = slot_ref[i]

            @pl.when(i == 0)
            def _():
                for cp in weight_copies(te_ref[0], 0):
                    cp.start(priority=1)

            for cp in weight_copies(te_ref[i], s):
                cp.wait()

            @pl.when(nxt_ref[i] >= 0)
            def _():
                for cp in weight_copies(nxt_ref[i], 1 - s):
                    cp.start(priority=1)

            wgu_b[:, 0:DE] = wg_st[s].astype(BF16)
            wgu_b[:, DE:2 * DE] = wu_st[s].astype(BF16)
            wd_b[...] = wd_st[s].astype(BF16)

        valid = _iota2((E_TILE, D), 0) < nval_ref[i]
        x = jnp.where(valid, xbuf[par], 0.0).astype(BF16)
        ab = jnp.dot(x, wgu_b[...], preferred_element_type=F32)
        hid = (_silu(ab[:, 0:DE]) * ab[:, DE:2 * DE]).astype(BF16)
        ybuf[par] = jnp.dot(hid, wd_b[...], preferred_element_type=F32)

        tile_chunks(i, par, True, start)

        @pl.when(i >= 1)
        def _():
            tile_chunks(i - 1, 1 - par, True, wait)

        @pl.when(i == nt - 1)
        def _():
            tile_chunks(i, par, True, wait)


def _moe(counts, xs, w_eg, w_eu, w_ed):
    tile_table = pltpu.SMEM((E_TILES_MAX,), I32)
    grid_spec = pltpu.PrefetchScalarGridSpec(
        num_scalar_prefetch=1,
        grid=(E_TILES_MAX,),
        in_specs=[pl.BlockSpec(memory_space=pl.ANY)] * 4,
        out_specs=pl.BlockSpec(memory_space=pl.ANY),
        scratch_shapes=[pltpu.VMEM((2, E_TILE, D), F32), pltpu.VMEM((2, E_TILE, D), F32),
                        pltpu.VMEM((2, D, DE), F32), pltpu.VMEM((2, D, DE), F32), pltpu.VMEM((2, DE, D), F32),
                        pltpu.VMEM((D, 2 * DE), BF16), pltpu.VMEM((DE, D), BF16),
                        tile_table, tile_table, tile_table, tile_table, tile_table, tile_table,
                        pltpu.SMEM((1,), I32), pltpu.SMEM((NE,), I32),
                        pltpu.SemaphoreType.DMA((2,)), pltpu.SemaphoreType.DMA((2,)), pltpu.SemaphoreType.DMA((2,))],
    )
    return pl.pallas_call(
        _moe_kernel,
        out_shape=jax.ShapeDtypeStruct((E_ROWS, 1, D), F32),
        grid_spec=grid_spec,
        compiler_params=_cparams(),
        name="moe",
    )(counts, xs, w_eg, w_eu, w_ed)


def _final_kernel(pos0_ref, posn_ref, x1_ref, rw_ref, gtp_ref, gts_ref, g_ref, ys_ref, yp_ref, ysm_ref, buf, sem):
    i = pl.program_id(0)
    par = i % 2

    def gather(p_ref, p):
        for r in range(TM):
            for slot in range(2):
                pltpu.make_async_copy(ys_ref.at[p_ref[0, slot, r]], buf.at[p, slot, pl.ds(r, 1), :],
                                      sem.at[p]).start(priority=r % 2)

    @pl.when(i == 0)
    def _():
        gather(pos0_ref, 0)

    @pl.when(i < N_TILES - 1)
    def _():
        gather(posn_ref, 1 - par)

    for slot in range(2):
        pltpu.make_async_copy(ys_ref.at[pl.ds(0, TM), 0], buf.at[par, slot], sem.at[par]).wait()

    moe = rw_ref[:, 0:1] * buf[par, 0] + rw_ref[:, 1:2] * buf[par, 1]

    @pl.when(i < N_TILES_P)
    def _():
        yp_ref[...] = _rms(x1_ref[...] + _prompt_row(gtp_ref) * moe, g_ref[...])

    @pl.when(i >= N_TILES_P)
    def _():
        ysm_ref[...] = _rms(x1_ref[...] + _sample_rows(gts_ref) * moe, g_ref[...])


def _final(pos, x1, rw, mod, g_fin, ys):
    return pl.pallas_call(
        _final_kernel,
        out_shape=(jax.ShapeDtypeStruct((T_P, D), F32), jax.ShapeDtypeStruct((T_S, D), F32)),
        grid=(N_TILES,),
        in_specs=[pl.BlockSpec((1, 8, TM), lambda i: (0, 0, 0), memory_space=pltpu.SMEM),
                  pl.BlockSpec((1, 8, TM), lambda i: (jnp.minimum(i + 1, N_TILES - 1), 0, 0),
                               memory_space=pltpu.SMEM),
                  pl.BlockSpec((TM, D), lambda i: (i, 0)),
                  pl.BlockSpec((TM, 128), lambda i: (i, 0)),
                  *_mod_specs(5),
                  _const_spec((1, D)),
                  pl.BlockSpec(memory_space=pl.ANY)],
        out_specs=(pl.BlockSpec((TM, D), lambda i: (_p_tile(i), 0)),
                   pl.BlockSpec((TM, D), lambda i: (_s_tile(i), 0))),
        scratch_shapes=[pltpu.VMEM((2, 2, TM, D), F32), pltpu.SemaphoreType.DMA((2,))],
        compiler_params=_cparams(),
        name="final",
    )(pos, pos, x1, rw, mod, mod, g_fin, ys)


def kernel(x_prompt, x_sample, c_prompt, c_sample, state_conv, state_gla, w_ada, b_ada, norm_mix, w_in, conv_w,
           conv_b, w_gate_up, b_gate, norm_conv, norm_gla, w_out, norm_ffn, w_coarse, b_coarse, w_fine, b_fine,
           w_exp_gate, w_exp_up, w_exp_down, norm_final):
    xp = x_prompt.reshape(T_P, D)
    xs = x_sample.reshape(T_S, D)
    c_all = jnp.concatenate([c_sample, c_prompt, jnp.zeros((C_ROWS - B_S - B_P, D), F32)], axis=0)
    mod = _ada(c_all, w_ada[0], b_ada[0][None, :])

    w_in_t = jnp.swapaxes(w_in, 1, 2)
    w_low_t = jnp.pad(w_in_t[0, PROJ_MAIN:, :], ((0, 128 - RANK), (0, 0))).astype(BF16)
    wgu_b = jnp.pad(w_gate_up[0], ((0, 128 - RANK), (0, 0))).astype(BF16)
    w_r = jnp.concatenate([w_coarse[0].T, jnp.zeros((8 - N_GROUPS, D), F32), w_fine[0].T], axis=0)
    b_r = jnp.concatenate([b_coarse[0], jnp.zeros((8 - N_GROUPS,), F32), b_fine[0]])[:, None]
    g_mix, g_ffn, g_fin = norm_mix[0][None, :], norm_ffn[0][None, :], norm_final[None, :]
    mix_consts = (conv_w[0], conv_b[0][None, :], wgu_b, b_gate[0][None, :], norm_conv[0][None, :],
                  norm_gla[0][None, :])

    ycat_p, nconv_p, nst_p, proj_s = _inmix(xp, xs, mod, g_mix, w_in_t, w_low_t, *mix_consts)
    ycat_s, nconv_s, nst_s = _mixer_s(proj_s, state_conv[0], state_gla[0].reshape(B_S, HK, DV), *mix_consts)

    x1, pos, rw, counts, x_sorted = _outproj(ycat_p, ycat_s, xp, xs, mod, w_out, g_ffn, w_r.astype(BF16), b_r)
    y_sorted = _moe(counts, x_sorted, w_exp_gate[0], w_exp_up[0], w_exp_down[0])
    y_p, y_s = _final(pos, x1, rw, mod, g_fin, y_sorted)

    new_gla_p = nst_p.reshape(B_P, DV, H, DK).transpose(0, 2, 3, 1)[None]
    new_gla_s = nst_s.reshape(1, B_S, H, DK, DV)
    return (y_p.reshape(B_P, L_P, D), y_s.reshape(B_S, L_S, D), nconv_p[None], new_gla_p,
            nconv_s[None], new_gla_s)
```

```python
import jax
import jax.numpy as jnp
from jax import lax
from jax.experimental import pallas as pl
from jax.experimental.pallas import tpu as pltpu

F32 = jnp.float32
BF16 = jnp.bfloat16
I32 = jnp.int32

D = 2048
CW = 1024
H = 8
DK = 64
DV = 128
HK = H * DK
HV = H * DV
RANK = 16
TAU = 16.0
N_GROUPS = 4
EPG = 8
NE = N_GROUPS * EPG
DE = 512
EPS = 1e-6

B_P, L_P = 4, 2048
B_S, L_S = 128, 8
T_P = B_P * L_P
T_S = B_S * L_S
T = T_P + T_S

TM = 256
CH = 64
N_TILES = T // TM
N_TILES_P = T_P // TM
TILES_PER_SEQ = L_P // TM
SEQ_PER_TILE = TM // L_S
SEQ_PER_CHUNK = CH // L_S

PROJ_MAIN = 3 * CW + 2 * HK + 2 * HV
PROJ_PAD = PROJ_MAIN + 128
C_ROWS = 136
P_ROW_BLOCK = B_S // 8
R_ROWS = 40

E_TILE = 256
E_TILES_MAX = (2 * T) // E_TILE + NE
E_CAP = T
E_CAP_TILES = E_CAP // E_TILE
E_ROWS = NE * E_CAP
VMEM_LIMIT = 60 * 1024 * 1024


def _cparams(n_axes=1, vmem=VMEM_LIMIT):
    return pltpu.CompilerParams(dimension_semantics=("arbitrary",) * n_axes, vmem_limit_bytes=vmem)


def _rms(x, g):
    return x * lax.rsqrt(jnp.mean(x * x, axis=-1, keepdims=True) + EPS) * g


def _sigmoid(x):
    return 1.0 / (1.0 + jnp.exp(-x))


def _silu(x):
    return x * _sigmoid(x)


def _log_sigmoid(x):
    return jnp.minimum(x, 0.0) - jnp.log(1.0 + jnp.exp(-jnp.abs(x)))


def _iota2(shape, axis):
    return lax.broadcasted_iota(I32, shape, axis)


def _expand_rows(ref, n, reps):
    return jnp.concatenate([jnp.broadcast_to(ref[j:j + 1, :], (reps, ref.shape[-1])) for j in range(n)], axis=0)


def _prompt_row(ref):
    return ref[pl.ds(pl.program_id(0) // TILES_PER_SEQ, 1), :]


def _sample_rows(ref):
    return _expand_rows(ref, SEQ_PER_TILE, L_S)


def _p_tile(i):
    return jnp.minimum(i, N_TILES_P - 1)


def _s_tile(i):
    return jnp.maximum(i - N_TILES_P, 0)


def _mod_specs(col):
    return [pl.BlockSpec((8, D), lambda i, *_: (P_ROW_BLOCK, col)),
            pl.BlockSpec((SEQ_PER_TILE, D), lambda i, *_: (_s_tile(i), col))]


def _const_spec(shape):
    zeros = (0,) * len(shape)
    return pl.BlockSpec(shape, lambda *_: zeros, pipeline_mode=pl.Buffered(1))


def _ada_kernel(c_ref, w_ref, b_ref, o_ref):
    s = _silu(c_ref[...]).astype(BF16)
    o_ref[...] = jnp.dot(s, w_ref[...].astype(BF16), preferred_element_type=F32) + b_ref[...]


def _ada(c_all, w_ada, b_ada):
    tn = 1024
    return pl.pallas_call(
        _ada_kernel,
        out_shape=jax.ShapeDtypeStruct((C_ROWS, 6 * D), F32),
        grid=(6 * D // tn,),
        in_specs=[pl.BlockSpec((C_ROWS, D), lambda j: (0, 0)),
                  pl.BlockSpec((D, tn), lambda j: (0, j)),
                  pl.BlockSpec((1, tn), lambda j: (0, j))],
        out_specs=pl.BlockSpec((C_ROWS, tn), lambda j: (0, j)),
        compiler_params=_cparams(),
        name="ada",
    )(c_all, w_ada, b_ada)


W_CHUNK = TM
N_W_CHUNKS = PROJ_MAIN // W_CHUNK
NT_DIMS = (((1,), (1,)), ((), ()))


def _gla_prep(p_ref, wgu_ref, bg_ref, rows, seg, totals_from_last_row=False):
    q = p_ref[:, 3 * CW:3 * CW + HK]
    k = p_ref[:, 3 * CW + HK:3 * CW + 2 * HK]
    a = p_ref[:, PROJ_MAIN:PROJ_PAD].astype(BF16)
    z = jnp.dot(a, wgu_ref[...], preferred_element_type=F32) + bg_ref[...]
    la = _log_sigmoid(z) * (1.0 / TAU)
    la0 = la.astype(BF16)
    rem = la - la0.astype(F32)
    la1 = rem.astype(BF16)
    la2 = (rem - la1.astype(F32)).astype(BF16)
    sel_rows = rows if totals_from_last_row else 2 * rows
    r = _iota2((sel_rows, rows), 0)
    c = _iota2((sel_rows, rows), 1)
    same = ((r % rows) // seg) == (c // seg)
    sel = jnp.where(same & ((r >= rows) | (c <= r)), 1.0, 0.0).astype(BF16)
    sums = sum(jnp.dot(sel, part, preferred_element_type=F32) for part in (la0, la1, la2))
    b = sums[0:rows]
    if totals_from_last_row:
        ebl = jnp.concatenate([jnp.broadcast_to(jnp.exp(b[s + seg - 1:s + seg, :]), (seg, HK))
                               for s in range(0, rows, seg)], axis=0)
    else:
        ebl = jnp.exp(sums[rows:2 * rows])
    qt = q * jnp.exp(b) * (DK ** -0.5)
    kt = k * jnp.exp(-b)
    kend = kt * ebl
    return qt, kt, kend, ebl


def _head_stack(x):
    rows = x.shape[0]
    t = jnp.concatenate([x] * H, axis=0)
    keep = (_iota2((H * rows, HK), 0) // rows) == (_iota2((H * rows, HK), 1) // DK)
    return jnp.where(keep, t, 0.0)


def _gla_out(o_h, g_h, ngl):
    on = o_h * lax.rsqrt(jnp.mean(o_h * o_h, axis=-1, keepdims=True) + EPS) * ngl
    return on * _silu(g_h)


def _conv_out(bg, u, um1, um2, cw_ref, cb_ref, ncv_ref):
    conv_y = cb_ref[...] + cw_ref[0:1, :] * um2 + cw_ref[1:2, :] * um1 + cw_ref[2:3, :] * u
    return _rms(bg * conv_y, ncv_ref[...])


V_OFF = 3 * CW + 2 * HK
G_OFF = V_OFF + HV


def _mixer_p_tile(j, p_ref, cw_ref, cb_ref, wgu_ref, bg_ref, ncv_ref, ngl_ref,
                  y_ref, nconv_ref, nst_ref, ubuf, st_ref, also):
    @pl.when(j == 0)
    def _():
        ubuf[0:8, :] = jnp.zeros((8, CW), F32)
        st_ref[...] = jnp.zeros((DV, HK), F32)

    bgate = p_ref[:, 0:CW]
    u = p_ref[:, CW:2 * CW] * p_ref[:, 2 * CW:3 * CW]
    ubuf[8:8 + TM, :] = u
    um1 = ubuf[7:7 + TM, :]
    um2 = ubuf[6:6 + TM, :]
    y_ref[:, 0:CW] = _conv_out(bgate, u, um1, um2, cw_ref, cb_ref, ncv_ref).astype(BF16)
    ubuf[6:8, :] = u[TM - 2:TM, :]

    qt, kt, kend, ebl = _gla_prep(p_ref, wgu_ref, bg_ref, TM, CH, totals_from_last_row=True)
    causal = _iota2((2 * CH, CH), 0) % CH >= _iota2((2 * CH, CH), 1)
    first_head = _iota2((CH, 2 * DK), 1) < DK
    ngl = ngl_ref[...]

    def pair_stack(x):
        return jnp.concatenate([jnp.where(first_head, x, 0.0), jnp.where(first_head, 0.0, x)], axis=0).astype(BF16)

    for c in range(TM // CH):
        r0 = c * CH
        for m in range(H // 2):
            lanes = slice(m * 2 * DK, (m + 1) * 2 * DK)
            lhs = pair_stack(qt[r0:r0 + CH, lanes])
            sc = lax.dot_general(lhs, kt[r0:r0 + CH, lanes].astype(BF16), (((1,), (1,)), ((), ())),
                                 preferred_element_type=F32)
            sc = jnp.where(causal, sc, 0.0).astype(BF16)
            st = st_ref[:, lanes]
            o_inter = lax.dot_general(lhs, st.astype(BF16), (((1,), (1,)), ((), ())),
                                      preferred_element_type=F32)
            vs = []
            for hh in range(2):
                h = 2 * m + hh
                v_h = p_ref[r0:r0 + CH, V_OFF + h * DV:V_OFF + (h + 1) * DV].astype(BF16)
                g_h = p_ref[r0:r0 + CH, G_OFF + h * DV:G_OFF + (h + 1) * DV]
                vs.append(v_h)
                o_h = (jnp.dot(sc[hh * CH:(hh + 1) * CH], v_h, preferred_element_type=F32)
                       + o_inter[hh * CH:(hh + 1) * CH])
                y_ref[r0:r0 + CH, CW + h * DV:CW + (h + 1) * DV] = _gla_out(o_h, g_h, ngl).astype(BF16)
            kv_t = lax.dot_general(jnp.concatenate(vs, axis=0), pair_stack(kend[r0:r0 + CH, lanes]),
                                   (((0,), (0,)), ((), ())), preferred_element_type=F32)
            st_ref[:, lanes] = st * ebl[r0:r0 + 1, lanes] + kv_t

    also()

    @pl.when(j == TILES_PER_SEQ - 1)
    def _():
        nconv_ref[0] = ubuf[6:8, :]
        nst_ref[0] = st_ref[...]


def _inmix_kernel(xp_ref, xs_ref, shp_ref, shs_ref, scp_ref, scs_ref, g_ref, wt_hbm, wa_ref,
                  cw_ref, cb_ref, wgu_ref, bg_ref, ncv_ref, ngl_ref,
                  y_ref, nconv_ref, nst_ref, projs_hbm,
                  h_scr, w_scr, proj, ubuf, st_ref, wsem, psem):
    i = pl.program_id(0)
    par = i % 2

    @pl.when(i == 0)
    def _():
        def chunk(c):
            return pltpu.make_async_copy(wt_hbm.at[0, pl.ds(c * W_CHUNK, W_CHUNK), :],
                                         proj.at[c % 2, :, pl.ds(0, D)], wsem.at[c % 2])

        chunk(0).start()
        for c in range(N_W_CHUNKS):
            if c + 1 < N_W_CHUNKS:
                chunk(c + 1).start()
            chunk(c).wait()
            w_scr[c * W_CHUNK:(c + 1) * W_CHUNK, :] = proj[c % 2, :, 0:D].astype(BF16)

    @pl.when(i < N_TILES_P)
    def _():
        h = _rms(xp_ref[...], g_ref[...]) * (1.0 + _prompt_row(scp_ref)) + _prompt_row(shp_ref)
        h_scr[...] = h.astype(BF16)

    @pl.when(i >= N_TILES_P)
    def _():
        h = _rms(xs_ref[...], g_ref[...]) * (1.0 + _sample_rows(scs_ref)) + _sample_rows(shs_ref)
        h_scr[...] = h.astype(BF16)

    def project():
        h = h_scr[...]
        dst = proj.at[par]
        dst[:, 0:PROJ_MAIN] = lax.dot_general(h, w_scr[...], NT_DIMS, preferred_element_type=F32)
        dst[:, PROJ_MAIN:PROJ_PAD] = lax.dot_general(h, wa_ref[...], NT_DIMS, preferred_element_type=F32)

    has_mixer = (i >= 1) & (i <= N_TILES_P)

    @pl.when(has_mixer)
    def _():
        _mixer_p_tile((i - 1) % TILES_PER_SEQ, proj.at[1 - par], cw_ref, cb_ref, wgu_ref, bg_ref, ncv_ref, ngl_ref,
                      y_ref, nconv_ref, nst_ref, ubuf, st_ref, project)

    @pl.when(jnp.logical_not(has_mixer))
    def _():
        project()

    @pl.when(i >= N_TILES_P)
    def _():
        cp = pltpu.make_async_copy(proj.at[par], projs_hbm.at[pl.ds((i - N_TILES_P) * TM, TM), :], psem)
        cp.start()
        cp.wait()


def _inmix(xp, xs, mod, g_mix, w_in_t, w_low_t, conv_w, conv_b, wgu_b, b_gate, n_conv, n_gla):
    def prev_tile(i):
        return jnp.clip(i - 1, 0, N_TILES_P - 1)

    return pl.pallas_call(
        _inmix_kernel,
        out_shape=(jax.ShapeDtypeStruct((T_P, D), BF16),
                   jax.ShapeDtypeStruct((B_P, 2, CW), F32),
                   jax.ShapeDtypeStruct((B_P, DV, HK), F32),
                   jax.ShapeDtypeStruct((T_S, PROJ_PAD), F32)),
        grid=(N_TILES,),
        in_specs=[pl.BlockSpec((TM, D), lambda i: (_p_tile(i), 0)),
                  pl.BlockSpec((TM, D), lambda i: (_s_tile(i), 0)),
                  *_mod_specs(0), *_mod_specs(1),
                  _const_spec((1, D)), pl.BlockSpec(memory_space=pl.ANY), _const_spec((128, D)),
                  _const_spec((3, CW)), _const_spec((1, CW)), _const_spec((128, HK)), _const_spec((1, HK)),
                  _const_spec((1, CW)), _const_spec((1, DV))],
        out_specs=(pl.BlockSpec((TM, D), lambda i: (prev_tile(i), 0)),
                   pl.BlockSpec((1, 2, CW), lambda i: (prev_tile(i) // TILES_PER_SEQ, 0, 0)),
                   pl.BlockSpec((1, DV, HK), lambda i: (prev_tile(i) // TILES_PER_SEQ, 0, 0)),
                   pl.BlockSpec(memory_space=pl.ANY)),
        scratch_shapes=[pltpu.VMEM((TM, D), BF16), pltpu.VMEM((PROJ_MAIN, D), BF16),
                        pltpu.VMEM((2, TM, PROJ_PAD), F32),
                        pltpu.VMEM((8 + TM, CW), F32), pltpu.VMEM((DV, HK), F32),
                        pltpu.SemaphoreType.DMA((2,)), pltpu.SemaphoreType.DMA(())],
        compiler_params=_cparams(),
        name="inmix",
    )(xp, xs, mod, mod, mod, mod, g_mix, w_in_t, w_low_t, conv_w, conv_b, wgu_b, b_gate, n_conv, n_gla)


def _mixer_s_kernel(p_ref, sconv_ref, sst_ref, cw_ref, cb_ref, wgu_ref, bg_ref, ncv_ref, ngl_ref,
                    y_ref, nconv_ref, nst_ref, ubuf):
    nseq = SEQ_PER_CHUNK
    bgate = p_ref[:, 0:CW]
    u = p_ref[:, CW:2 * CW] * p_ref[:, 2 * CW:3 * CW]
    ubuf[0:8, :] = jnp.zeros((8, CW), F32)
    ubuf[8:8 + CH, :] = u
    tpos = _iota2((CH, CW), 0) % L_S
    s0 = jnp.concatenate([jnp.broadcast_to(sconv_ref[s, 0:1, :], (L_S, CW)) for s in range(nseq)], axis=0)
    s1 = jnp.concatenate([jnp.broadcast_to(sconv_ref[s, 1:2, :], (L_S, CW)) for s in range(nseq)], axis=0)
    um1 = jnp.where(tpos == 0, s1, ubuf[7:7 + CH, :])
    um2 = jnp.where(tpos == 0, s0, jnp.where(tpos == 1, s1, ubuf[6:6 + CH, :]))
    y_ref[:, 0:CW] = _conv_out(bgate, u, um1, um2, cw_ref, cb_ref, ncv_ref).astype(BF16)
    for s in range(nseq):
        nconv_ref[s] = u[s * L_S + L_S - 2:(s + 1) * L_S, :]

    qt, kt, kend, ebl = _gla_prep(p_ref, wgu_ref, bg_ref, CH, L_S)
    rr = _iota2((H * CH, CH), 0) % CH
    cc = _iota2((H * CH, CH), 1)
    causal = (rr >= cc) & ((rr // L_S) == (cc // L_S))
    ngl = ngl_ref[...]
    lhs_f = _head_stack(qt)
    sc = lax.dot_general(lhs_f.astype(BF16), kt.astype(BF16), (((1,), (1,)), ((), ())),
                         preferred_element_type=F32)
    sc = jnp.where(causal, sc, 0.0).astype(BF16)
    kstack_f = _head_stack(kend)
    v_all = p_ref[:, V_OFF:V_OFF + HV]
    decay_t = jnp.concatenate([ebl, jnp.zeros((128 - CH, HK), F32)], axis=0).T
    o_inter = []
    for s in range(nseq):
        rs = [slice(h * CH + s * L_S, h * CH + (s + 1) * L_S) for h in range(H)]
        lhs_s = jnp.concatenate([lhs_f[r] for r in rs], axis=0).astype(BF16)
        k_s = jnp.concatenate([kstack_f[r] for r in rs], axis=0).astype(BF16)
        v_s = jnp.concatenate([v_all[s * L_S:(s + 1) * L_S, h * DV:(h + 1) * DV] for h in range(H)],
                              axis=0).astype(BF16)
        st = sst_ref[s]
        o_inter.append(jnp.dot(lhs_s, st.astype(BF16), preferred_element_type=F32))
        kv = lax.dot_general(k_s, v_s, (((0,), (0,)), ((), ())), preferred_element_type=F32)
        nst_ref[s] = st * decay_t[:, s * L_S:s * L_S + 1] + kv
    for h in range(H):
        v_h = v_all[:, h * DV:(h + 1) * DV].astype(BF16)
        g_h = p_ref[:, G_OFF + h * DV:G_OFF + (h + 1) * DV]
        oi_h = jnp.concatenate([o_inter[s][h * L_S:(h + 1) * L_S] for s in range(nseq)], axis=0)
        o_h = jnp.dot(sc[h * CH:(h + 1) * CH], v_h, preferred_element_type=F32) + oi_h
        y_ref[:, CW + h * DV:CW + (h + 1) * DV] = _gla_out(o_h, g_h, ngl).astype(BF16)


def _mixer_s(proj, sconv, sst, conv_w, conv_b, wgu_b, b_gate, n_conv, n_gla):
    nseq = SEQ_PER_CHUNK
    return pl.pallas_call(
        _mixer_s_kernel,
        out_shape=(jax.ShapeDtypeStruct((T_S, D), BF16),
                   jax.ShapeDtypeStruct((B_S, 2, CW), F32),
                   jax.ShapeDtypeStruct((B_S, HK, DV), F32)),
        grid=(T_S // CH,),
        in_specs=[pl.BlockSpec((CH, PROJ_PAD), lambda i: (i, 0)),
                  pl.BlockSpec((nseq, 2, CW), lambda i: (i, 0, 0)),
                  pl.BlockSpec((nseq, HK, DV), lambda i: (i, 0, 0)),
                  _const_spec((3, CW)), _const_spec((1, CW)), _const_spec((128, HK)), _const_spec((1, HK)),
                  _const_spec((1, CW)), _const_spec((1, DV))],
        out_specs=(pl.BlockSpec((CH, D), lambda i: (i, 0)),
                   pl.BlockSpec((nseq, 2, CW), lambda i: (i, 0, 0)),
                   pl.BlockSpec((nseq, HK, DV), lambda i: (i, 0, 0))),
        scratch_shapes=[pltpu.VMEM((8 + CH, CW), F32)],
        compiler_params=_cparams(),
        name="mixer_s",
    )(proj, sconv, sst, conv_w, conv_b, wgu_b, b_gate, n_conv, n_gla)


def _route(h2, wr_ref, br_ref, run_ref, pos_ref, rw_ref, cnt_ref):
    lt = lax.dot_general(wr_ref[...], h2.astype(BF16), (((1,), (1,)), ((), ())),
                         preferred_element_type=F32) + br_ref[...]

    coarse = lt[0:N_GROUPS]
    cmax = jnp.max(coarse, axis=0, keepdims=True)
    gi = _iota2((N_GROUPS, TM), 0)
    grp = jnp.min(jnp.where(coarse == cmax, gi, N_GROUPS), axis=0, keepdims=True)
    p_sel = 1.0 / jnp.sum(jnp.exp(coarse - cmax), axis=0, keepdims=True)
    fine = jnp.zeros((EPG, TM), F32)
    for g in range(N_GROUPS):
        fine = jnp.where(grp == g, lt[8 + g * EPG:8 + (g + 1) * EPG], fine)
    ei = _iota2((EPG, TM), 0)
    f1 = jnp.max(fine, axis=0, keepdims=True)
    i1 = jnp.min(jnp.where(fine == f1, ei, EPG), axis=0, keepdims=True)
    rest = jnp.where(ei == i1, -jnp.inf, fine)
    f2 = jnp.max(rest, axis=0, keepdims=True)
    i2 = jnp.min(jnp.where(rest == f2, ei, EPG), axis=0, keepdims=True)
    e2 = jnp.exp(f2 - f1)
    w1 = p_sel / (1.0 + e2)
    w2 = p_sel * e2 / (1.0 + e2)
    x1 = grp * EPG + i1
    x2 = grp * EPG + i2

    er = _iota2((NE, TM), 0)
    oh1 = jnp.where(er == x1, 1.0, 0.0)
    oh2 = jnp.where(er == x2, 1.0, 0.0)
    oh = oh1 + oh2
    before = jnp.where(_iota2((TM, TM), 0) < _iota2((TM, TM), 1), 1.0, 0.0).astype(BF16)
    cum = jnp.dot(oh.astype(BF16), before, preferred_element_type=F32) + run_ref[:, 0:1]
    r1 = jnp.sum(oh1 * cum, axis=0, keepdims=True).astype(I32)
    r2 = jnp.sum(oh2 * cum, axis=0, keepdims=True).astype(I32)
    run_new = run_ref[...] + jnp.sum(oh, axis=1, keepdims=True)
    run_ref[...] = run_new
    cnt_ref[...] = run_new.astype(I32)

    row = _iota2((8, TM), 0)
    pos_ref[...] = jnp.where(row == 0, x1 * E_CAP + r1, jnp.where(row == 1, x2 * E_CAP + r2, 0))
    wrow = _iota2((128, TM), 0)
    rw_ref[...] = jnp.where(wrow == 0, w1, jnp.where(wrow == 1, w2, 0.0)).T


def _outproj_kernel(ycp_ref, ycs_ref, xp_ref, xs_ref, gtp_ref, gts_ref, shp_ref, shs_ref, scp_ref, scs_ref,
                    wo_hbm, g_ref, wr_ref, br_ref,
                    x1_ref, pos_ref, rw_ref, cnt_ref, xs_out,
                    run_ref, h2_scr, wo_ref, pos_v, pos_s, sem, psem, wsem):
    i = pl.program_id(0)
    par = i % 2

    @pl.when(i == 0)
    def _():
        def chunk(c):
            return pltpu.make_async_copy(wo_hbm.at[0, pl.ds(c * TM, TM), :], h2_scr.at[c % 2], wsem.at[c % 2])

        chunk(0).start()
        for c in range(D // TM):
            if c + 1 < D // TM:
                chunk(c + 1).start()
            chunk(c).wait()
            wo_ref[c * TM:(c + 1) * TM, :] = h2_scr[c % 2].astype(BF16)

    to_smem = pltpu.make_async_copy(pos_v, pos_s, psem)

    def scatter(p):
        for r in range(TM):
            for slot in range(2):
                pltpu.make_async_copy(h2_scr.at[p, pl.ds(r, 1), :], xs_out.at[pl.ds(pos_s[slot, r], 1), :],
                                      sem).start(priority=r % 2)

    def drain():
        for _ in range(2):
            pltpu.make_async_copy(h2_scr.at[0], xs_out.at[pl.ds(0, TM), :], sem).wait()

    @pl.when(i == 0)
    def _():
        run_ref[...] = jnp.zeros((NE, 128), F32)
        h2_scr[1] = jnp.zeros((TM, D), F32)
        pos_v[...] = E_ROWS + _iota2((8, TM), 0) * TM + _iota2((8, TM), 1)
        to_smem.start()

    to_smem.wait()

    def tile(yc, x, gt, sh, sc):
        scatter(1 - par)
        mix = jnp.dot(yc, wo_ref[...], preferred_element_type=F32)
        x1 = x + gt * mix
        x1_ref[...] = x1
        h2 = _rms(x1, g_ref[...]) * (1.0 + sc) + sh
        h2_scr[par] = h2
        _route(h2, wr_ref, br_ref, run_ref, pos_v, rw_ref, cnt_ref)

    @pl.when(i < N_TILES_P)
    def _():
        tile(ycp_ref[...], xp_ref[...], _prompt_row(gtp_ref), _prompt_row(shp_ref), _prompt_row(scp_ref))

    @pl.when(i >= N_TILES_P)
    def _():
        tile(ycs_ref[...], xs_ref[...], _sample_rows(gts_ref), _sample_rows(shs_ref), _sample_rows(scs_ref))

    pos_ref[0] = pos_v[...]
    drain()
    to_smem.start()

    @pl.when(i == N_TILES - 1)
    def _():
        to_smem.wait()
        scatter(par)
        drain()


def _outproj(ycp, ycs, xp, xs, mod, w_out, g_ffn, w_r, b_r):
    return pl.pallas_call(
        _outproj_kernel,
        out_shape=(jax.ShapeDtypeStruct((T, D), F32), jax.ShapeDtypeStruct((N_TILES, 8, TM), I32),
                   jax.ShapeDtypeStruct((T, 128), F32), jax.ShapeDtypeStruct((NE, 128), I32),
                   jax.ShapeDtypeStruct((E_ROWS + 2 * TM, D), F32)),
        grid=(N_TILES,),
        in_specs=[pl.BlockSpec((TM, D), lambda i: (_p_tile(i), 0)), pl.BlockSpec((TM, D), lambda i: (_s_tile(i), 0)),
                  pl.BlockSpec((TM, D), lambda i: (_p_tile(i), 0)), pl.BlockSpec((TM, D), lambda i: (_s_tile(i), 0)),
                  *_mod_specs(2), *_mod_specs(3), *_mod_specs(4),
                  pl.BlockSpec(memory_space=pl.ANY), _const_spec((1, D)), _const_spec((R_ROWS, D)),
                  _const_spec((R_ROWS, 1))],
        out_specs=(pl.BlockSpec((TM, D), lambda i: (i, 0)), pl.BlockSpec((1, 8, TM), lambda i: (i, 0, 0)),
                   pl.BlockSpec((TM, 128), lambda i: (i, 0)), pl.BlockSpec((NE, 128), lambda i: (0, 0)),
                   pl.BlockSpec(memory_space=pl.ANY)),
        scratch_shapes=[pltpu.VMEM((NE, 128), F32), pltpu.VMEM((2, TM, D), F32), pltpu.VMEM((D, D), BF16),
                        pltpu.VMEM((8, TM), I32), pltpu.SMEM((8, TM), I32), pltpu.SemaphoreType.DMA(()),
                        pltpu.SemaphoreType.DMA(()), pltpu.SemaphoreType.DMA((2,))],
        compiler_params=_cparams(),
        name="outproj",
    )(ycp, ycs, xp, xs, mod, mod, mod, mod, mod, mod, w_out, g_ffn, w_r, b_r)


E_CHUNK = 64
E_CHUNKS = E_TILE // E_CHUNK


def _plan_tiles(cnt_ref, tb_ref, te_ref, first_ref, nval_ref, nxt_ref, slot_ref, nt_ref, nxte_ref):
    def backward(k, next_occupied):
        e = NE - 1 - k
        nxte_ref[e] = next_occupied
        return jnp.where(cnt_ref[e, 0] > 0, e, next_occupied)

    lax.fori_loop(0, NE, backward, jnp.int32(-1))

    def forward(e, carry):
        t, rank = carry
        cnt = cnt_ref[e, 0]
        n = lax.shift_right_logical(cnt + (E_TILE - 1), E_TILE.bit_length() - 1)

        def tile(k, _):
            tb_ref[t + k] = e * E_CAP_TILES + k
            te_ref[t + k] = e
            first_ref[t + k] = (k == 0).astype(I32)
            nval_ref[t + k] = jnp.minimum(cnt - k * E_TILE, E_TILE)
            nxt_ref[t + k] = nxte_ref[e]
            slot_ref[t + k] = rank & 1
            return 0

        lax.fori_loop(0, n, tile, 0)
        return t + n, rank + (n > 0).astype(I32)

    n_tiles, _ = lax.fori_loop(0, NE, forward, (jnp.int32(0), jnp.int32(0)))
    nt_ref[0] = n_tiles


def _moe_kernel(cnt_ref, x_hbm, wg_hbm, wu_hbm, wd_hbm, y_hbm,
                xbuf, ybuf, wg_st, wu_st, wd_st, wgu_b, wd_b,
                tb_ref, te_ref, first_ref, nval_ref, nxt_ref, slot_ref, nt_ref, nxte_ref, wsem, xsem, ysem):
    i = pl.program_id(0)
    par = i % 2

    @pl.when(i == 0)
    def _():
        _plan_tiles(cnt_ref, tb_ref, te_ref, first_ref, nval_ref, nxt_ref, slot_ref, nt_ref, nxte_ref)

    nt = nt_ref[0]

    def weight_copies(e, s):
        return (pltpu.make_async_copy(wg_hbm.at[e], wg_st.at[s], wsem.at[s]),
                pltpu.make_async_copy(wu_hbm.at[e], wu_st.at[s], wsem.at[s]),
                pltpu.make_async_copy(wd_hbm.at[e], wd_st.at[s], wsem.at[s]))

    def tile_chunks(t, p, output, fn):
        row0 = tb_ref[t] * E_TILE
        for c in range(E_CHUNKS):
            rows = pl.ds(c * E_CHUNK, E_CHUNK)
            hbm_rows = pl.ds(row0 + c * E_CHUNK, E_CHUNK)

            @pl.when(c * E_CHUNK < nval_ref[t])
            def _():
                if output:
                    fn(pltpu.make_async_copy(ybuf.at[p, rows, :], y_hbm.at[hbm_rows, 0], ysem.at[p]))
                else:
                    fn(pltpu.make_async_copy(x_hbm.at[hbm_rows, :], xbuf.at[p, rows, :], xsem.at[p]))

    def start(cp):
        cp.start()

    def wait(cp):
        cp.wait()

    @pl.when(i == 0)
    def _():
        tile_chunks(0, 0, False, start)

    @pl.when(i + 1 < nt)
    def _():
        tile_chunks(i + 1, 1 - par, False, start)

    @pl.when(i < nt)
    def _():
        tile_chunks(i, par, False, wait)

        @pl.when(first_ref[i] == 1)
        def _():
            s = slot_ref[i]

            @pl.when(i == 0)
            def _():
                for cp in weight_copies(te_ref[0], 0):
                    cp.start(priority=1)

            for cp in weight_copies(te_ref[i], s):
                cp.wait()

            @pl.when(nxt_ref[i] >= 0)
            def _():
                for cp in weight_copies(nxt_ref[i], 1 - s):
                    cp.start(priority=1)

            wgu_b[:, 0:DE] = wg_st[s].astype(BF16)
            wgu_b[:, DE:2 * DE] = wu_st[s].astype(BF16)
            wd_b[...] = wd_st[s].astype(BF16)

        valid = _iota2((E_TILE, D), 0) < nval_ref[i]
        x = jnp.where(valid, xbuf[par], 0.0).astype(BF16)
        ab = jnp.dot(x, wgu_b[...], preferred_element_type=F32)
        hid = (_silu(ab[:, 0:DE]) * ab[:, DE:2 * DE]).astype(BF16)
        ybuf[par] = jnp.dot(hid, wd_b[...], preferred_element_type=F32)

        tile_chunks(i, par, True, start)

        @pl.when(i >= 1)
        def _():
            tile_chunks(i - 1, 1 - par, True, wait)

        @pl.when(i == nt - 1)
        def _():
            tile_chunks(i, par, True, wait)


def _moe(counts, xs, w_eg, w_eu, w_ed):
    tile_table = pltpu.SMEM((E_TILES_MAX,), I32)
    grid_spec = pltpu.PrefetchScalarGridSpec(
        num_scalar_prefetch=1,
        grid=(E_TILES_MAX,),
        in_specs=[pl.BlockSpec(memory_space=pl.ANY)] * 4,
        out_specs=pl.BlockSpec(memory_space=pl.ANY),
        scratch_shapes=[pltpu.VMEM((2, E_TILE, D), F32), pltpu.VMEM((2, E_TILE, D), F32),
                        pltpu.VMEM((2, D, DE), F32), pltpu.VMEM((2, D, DE), F32), pltpu.VMEM((2, DE, D), F32),
                        pltpu.VMEM((D, 2 * DE), BF16), pltpu.VMEM((DE, D), BF16),
                        tile_table, tile_table, tile_table, tile_table, tile_table, tile_table,
                        pltpu.SMEM((1,), I32), pltpu.SMEM((NE,), I32),
                        pltpu.SemaphoreType.DMA((2,)), pltpu.SemaphoreType.DMA((2,)), pltpu.SemaphoreType.DMA((2,))],
    )
    return pl.pallas_call(
        _moe_kernel,
        out_shape=jax.ShapeDtypeStruct((E_ROWS, 1, D), F32),
        grid_spec=grid_spec,
        compiler_params=_cparams(),
        name="moe",
    )(counts, xs, w_eg, w_eu, w_ed)


def _final_kernel(pos0_ref, posn_ref, x1_ref, rw_ref, gtp_ref, gts_ref, g_ref, ys_ref, yp_ref, ysm_ref, buf, sem):
    i = pl.program_id(0)
    par = i % 2

    def gather(p_ref, p):
        for r in range(TM):
            for slot in range(2):
                pltpu.make_async_copy(ys_ref.at[p_ref[0, slot, r]], buf.at[p, slot, pl.ds(r, 1), :],
                                      sem.at[p]).start(priority=r % 2)

    @pl.when(i == 0)
    def _():
        gather(pos0_ref, 0)

    @pl.when(i < N_TILES - 1)
    def _():
        gather(posn_ref, 1 - par)

    for slot in range(2):
        pltpu.make_async_copy(ys_ref.at[pl.ds(0, TM), 0], buf.at[par, slot], sem.at[par]).wait()

    moe = rw_ref[:, 0:1] * buf[par, 0] + rw_ref[:, 1:2] * buf[par, 1]

    @pl.when(i < N_TILES_P)
    def _():
        yp_ref[...] = _rms(x1_ref[...] + _prompt_row(gtp_ref) * moe, g_ref[...])

    @pl.when(i >= N_TILES_P)
    def _():
        ysm_ref[...] = _rms(x1_ref[...] + _sample_rows(gts_ref) * moe, g_ref[...])


def _final(pos, x1, rw, mod, g_fin, ys):
    return pl.pallas_call(
        _final_kernel,
        out_shape=(jax.ShapeDtypeStruct((T_P, D), F32), jax.ShapeDtypeStruct((T_S, D), F32)),
        grid=(N_TILES,),
        in_specs=[pl.BlockSpec((1, 8, TM), lambda i: (0, 0, 0), memory_space=pltpu.SMEM),
                  pl.BlockSpec((1, 8, TM), lambda i: (jnp.minimum(i + 1, N_TILES - 1), 0, 0),
                               memory_space=pltpu.SMEM),
                  pl.BlockSpec((TM, D), lambda i: (i, 0)),
                  pl.BlockSpec((TM, 128), lambda i: (i, 0)),
                  *_mod_specs(5),
                  _const_spec((1, D)),
                  pl.BlockSpec(memory_space=pl.ANY)],
        out_specs=(pl.BlockSpec((TM, D), lambda i: (_p_tile(i), 0)),
                   pl.BlockSpec((TM, D), lambda i: (_s_tile(i), 0))),
        scratch_shapes=[pltpu.VMEM((2, 2, TM, D), F32), pltpu.SemaphoreType.DMA((2,))],
        compiler_params=_cparams(),
        name="final",
    )(pos, pos, x1, rw, mod, mod, g_fin, ys)


def kernel(x_prompt, x_sample, c_prompt, c_sample, state_conv, state_gla, w_ada, b_ada, norm_mix, w_in, conv_w,
           conv_b, w_gate_up, b_gate, norm_conv, norm_gla, w_out, norm_ffn, w_coarse, b_coarse, w_fine, b_fine,
           w_exp_gate, w_exp_up, w_exp_down, norm_final):
    xp = x_prompt.reshape(T_P, D)
    xs = x_sample.reshape(T_S, D)
    c_all = jnp.concatenate([c_sample, c_prompt, jnp.zeros((C_ROWS - B_S - B_P, D), F32)], axis=0)
    mod = _ada(c_all, w_ada[0], b_ada[0][None, :])

    w_in_t = jnp.swapaxes(w_in, 1, 2)
    w_low_t = jnp.pad(w_in_t[0, PROJ_MAIN:, :], ((0, 128 - RANK), (0, 0))).astype(BF16)
    wgu_b = jnp.pad(w_gate_up[0], ((0, 128 - RANK), (0, 0))).astype(BF16)
    w_r = jnp.concatenate([w_coarse[0].T, jnp.zeros((8 - N_GROUPS, D), F32), w_fine[0].T], axis=0)
    b_r = jnp.concatenate([b_coarse[0], jnp.zeros((8 - N_GROUPS,), F32), b_fine[0]])[:, None]
    g_mix, g_ffn, g_fin = norm_mix[0][None, :], norm_ffn[0][None, :], norm_final[None, :]
    mix_consts = (conv_w[0], conv_b[0][None, :], wgu_b, b_gate[0][None, :], norm_conv[0][None, :],
                  norm_gla[0][None, :])

    ycat_p, nconv_p, nst_p, proj_s = _inmix(xp, xs, mod, g_mix, w_in_t, w_low_t, *mix_consts)
    ycat_s, nconv_s, nst_s = _mixer_s(proj_s, state_conv[0], state_gla[0].reshape(B_S, HK, DV), *mix_consts)

    x1, pos, rw, counts, x_sorted = _outproj(ycat_p, ycat_s, xp, xs, mod, w_out, g_ffn, w_r.astype(BF16), b_r)
    y_sorted = _moe(counts, x_sorted, w_exp_gate[0], w_exp_up[0], w_exp_down[0])
    y_p, y_s = _final(pos, x1, rw, mod, g_fin, y_sorted)

    new_gla_p = nst_p.reshape(B_P, DV, H, DK).transpose(0, 2, 3, 1)[None]
    new_gla_s = nst_s.reshape(1, B_S, H, DK, DV)
    return (y_p.reshape(B_P, L_P, D), y_s.reshape(B_S, L_S, D), nconv_p[None], new_gla_p,
            nconv_s[None], new_gla_s)
```

```python
import jax
import jax.numpy as jnp
from jax import lax
from jax.experimental import pallas as pl
from jax.experimental.pallas import tpu as pltpu

F32 = jnp.float32
BF16 = jnp.bfloat16
I32 = jnp.int32

D = 2048
CW = 1024
H = 8
DK = 64
DV = 128
HK = H * DK
HV = H * DV
RANK = 16
TAU = 16.0
N_GROUPS = 4
EPG = 8
NE = N_GROUPS * EPG
DE = 512
EPS = 1e-6

B_P, L_P = 4, 2048
B_S, L_S = 128, 8
T_P = B_P * L_P
T_S = B_S * L_S
T = T_P + T_S

TM = 256
CH = 64
N_TILES = T // TM
N_TILES_P = T_P // TM
TILES_PER_SEQ = L_P // TM
SEQ_PER_TILE = TM // L_S
SEQ_PER_CHUNK = CH // L_S

PROJ_MAIN = 3 * CW + 2 * HK + 2 * HV
PROJ_PAD = PROJ_MAIN + 128
C_ROWS = 136
P_ROW_BLOCK = B_S // 8
R_ROWS = 40

E_TILE = 256
E_TILES_MAX = (2 * T) // E_TILE + NE
E_CAP = T
E_CAP_TILES = E_CAP // E_TILE
E_ROWS = NE * E_CAP
VMEM_LIMIT = 60 * 1024 * 1024


def _cparams(n_axes=1, vmem=VMEM_LIMIT):
    return pltpu.CompilerParams(dimension_semantics=("arbitrary",) * n_axes, vmem_limit_bytes=vmem)


def _rms(x, g):
    return x * lax.rsqrt(jnp.mean(x * x, axis=-1, keepdims=True) + EPS) * g


def _sigmoid(x):
    return 1.0 / (1.0 + jnp.exp(-x))


def _silu(x):
    return x * _sigmoid(x)


def _log_sigmoid(x):
    return jnp.minimum(x, 0.0) - jnp.log(1.0 + jnp.exp(-jnp.abs(x)))


def _iota2(shape, axis):
    return lax.broadcasted_iota(I32, shape, axis)


def _expand_rows(ref, n, reps):
    return jnp.concatenate([jnp.broadcast_to(ref[j:j + 1, :], (reps, ref.shape[-1])) for j in range(n)], axis=0)


def _prompt_row(ref):
    return ref[pl.ds(pl.program_id(0) // TILES_PER_SEQ, 1), :]


def _sample_rows(ref):
    return _expand_rows(ref, SEQ_PER_TILE, L_S)


def _p_tile(i):
    return jnp.minimum(i, N_TILES_P - 1)


def _s_tile(i):
    return jnp.maximum(i - N_TILES_P, 0)


def _mod_specs(col):
    return [pl.BlockSpec((8, D), lambda i, *_: (P_ROW_BLOCK, col)),
            pl.BlockSpec((SEQ_PER_TILE, D), lambda i, *_: (_s_tile(i), col))]


def _const_spec(shape):
    zeros = (0,) * len(shape)
    return pl.BlockSpec(shape, lambda *_: zeros, pipeline_mode=pl.Buffered(1))


def _ada_kernel(c_ref, w_ref, b_ref, o_ref):
    s = _silu(c_ref[...]).astype(BF16)
    o_ref[...] = jnp.dot(s, w_ref[...].astype(BF16), preferred_element_type=F32) + b_ref[...]


def _ada(c_all, w_ada, b_ada):
    tn = 1024
    return pl.pallas_call(
        _ada_kernel,
        out_shape=jax.ShapeDtypeStruct((C_ROWS, 6 * D), F32),
        grid=(6 * D // tn,),
        in_specs=[pl.BlockSpec((C_ROWS, D), lambda j: (0, 0)),
                  pl.BlockSpec((D, tn), lambda j: (0, j)),
                  pl.BlockSpec((1, tn), lambda j: (0, j))],
        out_specs=pl.BlockSpec((C_ROWS, tn), lambda j: (0, j)),
        compiler_params=_cparams(),
        name="ada",
    )(c_all, w_ada, b_ada)


W_CHUNK = TM
N_W_CHUNKS = PROJ_MAIN // W_CHUNK
NT_DIMS = (((1,), (1,)), ((), ()))


def _gla_prep(p_ref, wgu_ref, bg_ref, rows, seg, totals_from_last_row=False):
    q = p_ref[:, 3 * CW:3 * CW + HK]
    k = p_ref[:, 3 * CW + HK:3 * CW + 2 * HK]
    a = p_ref[:, PROJ_MAIN:PROJ_PAD].astype(BF16)
    z = jnp.dot(a, wgu_ref[...], preferred_element_type=F32) + bg_ref[...]
    la = _log_sigmoid(z) * (1.0 / TAU)
    la0 = la.astype(BF16)
    rem = la - la0.astype(F32)
    la1 = rem.astype(BF16)
    la2 = (rem - la1.astype(F32)).astype(BF16)
    sel_rows = rows if totals_from_last_row else 2 * rows
    r = _iota2((sel_rows, rows), 0)
    c = _iota2((sel_rows, rows), 1)
    same = ((r % rows) // seg) == (c // seg)
    sel = jnp.where(same & ((r >= rows) | (c <= r)), 1.0, 0.0).astype(BF16)
    sums = sum(jnp.dot(sel, part, preferred_element_type=F32) for part in (la0, la1, la2))
    b = sums[0:rows]
    if totals_from_last_row:
        ebl = jnp.concatenate([jnp.broadcast_to(jnp.exp(b[s + seg - 1:s + seg, :]), (seg, HK))
                               for s in range(0, rows, seg)], axis=0)
    else:
        ebl = jnp.exp(sums[rows:2 * rows])
    qt = q * jnp.exp(b) * (DK ** -0.5)
    kt = k * jnp.exp(-b)
    kend = kt * ebl
    return qt, kt, kend, ebl


def _head_stack(x):
    rows = x.shape[0]
    t = jnp.concatenate([x] * H, axis=0)
    keep = (_iota2((H * rows, HK), 0) // rows) == (_iota2((H * rows, HK), 1) // DK)
    return jnp.where(keep, t, 0.0)


def _gla_out(o_h, g_h, ngl):
    on = o_h * lax.rsqrt(jnp.mean(o_h * o_h, axis=-1, keepdims=True) + EPS) * ngl
    return on * _silu(g_h)


def _conv_out(bg, u, um1, um2, cw_ref, cb_ref, ncv_ref):
    conv_y = cb_ref[...] + cw_ref[0:1, :] * um2 + cw_ref[1:2, :] * um1 + cw_ref[2:3, :] * u
    return _rms(bg * conv_y, ncv_ref[...])


V_OFF = 3 * CW + 2 * HK
G_OFF = V_OFF + HV


def _mixer_p_tile(j, p_ref, cw_ref, cb_ref, wgu_ref, bg_ref, ncv_ref, ngl_ref,
                  y_ref, nconv_ref, nst_ref, ubuf, st_ref, also):
    @pl.when(j == 0)
    def _():
        ubuf[0:8, :] = jnp.zeros((8, CW), F32)
        st_ref[...] = jnp.zeros((DV, HK), F32)

    bgate = p_ref[:, 0:CW]
    u = p_ref[:, CW:2 * CW] * p_ref[:, 2 * CW:3 * CW]
    ubuf[8:8 + TM, :] = u
    um1 = ubuf[7:7 + TM, :]
    um2 = ubuf[6:6 + TM, :]
    y_ref[:, 0:CW] = _conv_out(bgate, u, um1, um2, cw_ref, cb_ref, ncv_ref).astype(BF16)
    ubuf[6:8, :] = u[TM - 2:TM, :]

    qt, kt, kend, ebl = _gla_prep(p_ref, wgu_ref, bg_ref, TM, CH, totals_from_last_row=True)
    causal = _iota2((2 * CH, 2 * CH), 0) % CH >= _iota2((2 * CH, 2 * CH), 1) % CH
    first_head = _iota2((CH, 2 * DK), 1) < DK
    ngl = ngl_ref[...]

    def pair_stack(x):
        return jnp.concatenate([jnp.where(first_head, x, 0.0), jnp.where(first_head, 0.0, x)], axis=0).astype(BF16)

    for c in range(TM // CH):
        r0 = c * CH
        for m in range(H // 2):
            lanes = slice(m * 2 * DK, (m + 1) * 2 * DK)
            lhs = pair_stack(qt[r0:r0 + CH, lanes])
            sc = lax.dot_general(lhs, pair_stack(kt[r0:r0 + CH, lanes]), (((1,), (1,)), ((), ())),
                                 preferred_element_type=F32)
            sc = jnp.where(causal, sc, 0.0).astype(BF16)
            st = st_ref[:, lanes]
            vstack = jnp.concatenate(
                [p_ref[r0:r0 + CH, V_OFF + h * DV:V_OFF + (h + 1) * DV].astype(BF16) for h in (2 * m, 2 * m + 1)],
                axis=0)
            o_pair = (jnp.dot(sc, vstack, preferred_element_type=F32)
                      + lax.dot_general(lhs, st.astype(BF16), (((1,), (1,)), ((), ())),
                                        preferred_element_type=F32))
            for hh in range(2):
                h = 2 * m + hh
                g_h = p_ref[r0:r0 + CH, G_OFF + h * DV:G_OFF + (h + 1) * DV]
                y_ref[r0:r0 + CH, CW + h * DV:CW + (h + 1) * DV] = _gla_out(
                    o_pair[hh * CH:(hh + 1) * CH], g_h, ngl).astype(BF16)
            kv_t = lax.dot_general(vstack, pair_stack(kend[r0:r0 + CH, lanes]),
                                   (((0,), (0,)), ((), ())), preferred_element_type=F32)
            st_ref[:, lanes] = st * ebl[r0:r0 + 1, lanes] + kv_t

    also()

    @pl.when(j == TILES_PER_SEQ - 1)
    def _():
        nconv_ref[0] = ubuf[6:8, :]
        nst_ref[0] = st_ref[...]


def _inmix_kernel(xp_ref, xs_ref, shp_ref, shs_ref, scp_ref, scs_ref, g_ref, wt_hbm, wa_ref,
                  cw_ref, cb_ref, wgu_ref, bg_ref, ncv_ref, ngl_ref,
                  y_ref, nconv_ref, nst_ref, projs_hbm,
                  h_scr, w_scr, proj, ubuf, st_ref, wsem, psem):
    i = pl.program_id(0)
    par = i % 2

    @pl.when(i == 0)
    def _():
        def chunk(c):
            return pltpu.make_async_copy(wt_hbm.at[0, pl.ds(c * W_CHUNK, W_CHUNK), :],
                                         proj.at[c % 2, :, pl.ds(0, D)], wsem.at[c % 2])

        chunk(0).start()
        for c in range(N_W_CHUNKS):
            if c + 1 < N_W_CHUNKS:
                chunk(c + 1).start()
            chunk(c).wait()
            w_scr[c * W_CHUNK:(c + 1) * W_CHUNK, :] = proj[c % 2, :, 0:D].astype(BF16)

    @pl.when(i < N_TILES_P)
    def _():
        h = _rms(xp_ref[...], g_ref[...]) * (1.0 + _prompt_row(scp_ref)) + _prompt_row(shp_ref)
        h_scr[...] = h.astype(BF16)

    @pl.when(i >= N_TILES_P)
    def _():
        h = _rms(xs_ref[...], g_ref[...]) * (1.0 + _sample_rows(scs_ref)) + _sample_rows(shs_ref)
        h_scr[...] = h.astype(BF16)

    def project():
        h = h_scr[...]
        dst = proj.at[par]
        dst[:, 0:PROJ_MAIN] = lax.dot_general(h, w_scr[...], NT_DIMS, preferred_element_type=F32)
        dst[:, PROJ_MAIN:PROJ_PAD] = lax.dot_general(h, wa_ref[...], NT_DIMS, preferred_element_type=F32)

    has_mixer = (i >= 1) & (i <= N_TILES_P)

    @pl.when(has_mixer)
    def _():
        _mixer_p_tile((i - 1) % TILES_PER_SEQ, proj.at[1 - par], cw_ref, cb_ref, wgu_ref, bg_ref, ncv_ref, ngl_ref,
                      y_ref, nconv_ref, nst_ref, ubuf, st_ref, project)

    @pl.when(jnp.logical_not(has_mixer))
    def _():
        project()

    @pl.when(i >= N_TILES_P)
    def _():
        cp = pltpu.make_async_copy(proj.at[par], projs_hbm.at[pl.ds((i - N_TILES_P) * TM, TM), :], psem)
        cp.start()
        cp.wait()


def _inmix(xp, xs, mod, g_mix, w_in_t, w_low_t, conv_w, conv_b, wgu_b, b_gate, n_conv, n_gla):
    def prev_tile(i):
        return jnp.clip(i - 1, 0, N_TILES_P - 1)

    return pl.pallas_call(
        _inmix_kernel,
        out_shape=(jax.ShapeDtypeStruct((T_P, D), BF16),
                   jax.ShapeDtypeStruct((B_P, 2, CW), F32),
                   jax.ShapeDtypeStruct((B_P, DV, HK), F32),
                   jax.ShapeDtypeStruct((T_S, PROJ_PAD), F32)),
        grid=(N_TILES,),
        in_specs=[pl.BlockSpec((TM, D), lambda i: (_p_tile(i), 0)),
                  pl.BlockSpec((TM, D), lambda i: (_s_tile(i), 0)),
                  *_mod_specs(0), *_mod_specs(1),
                  _const_spec((1, D)), pl.BlockSpec(memory_space=pl.ANY), _const_spec((128, D)),
                  _const_spec((3, CW)), _const_spec((1, CW)), _const_spec((128, HK)), _const_spec((1, HK)),
                  _const_spec((1, CW)), _const_spec((1, DV))],
        out_specs=(pl.BlockSpec((TM, D), lambda i: (prev_tile(i), 0)),
                   pl.BlockSpec((1, 2, CW), lambda i: (prev_tile(i) // TILES_PER_SEQ, 0, 0)),
                   pl.BlockSpec((1, DV, HK), lambda i: (prev_tile(i) // TILES_PER_SEQ, 0, 0)),
                   pl.BlockSpec(memory_space=pl.ANY)),
        scratch_shapes=[pltpu.VMEM((TM, D), BF16), pltpu.VMEM((PROJ_MAIN, D), BF16),
                        pltpu.VMEM((2, TM, PROJ_PAD), F32),
                        pltpu.VMEM((8 + TM, CW), F32), pltpu.VMEM((DV, HK), F32),
                        pltpu.SemaphoreType.DMA((2,)), pltpu.SemaphoreType.DMA(())],
        compiler_params=_cparams(),
        name="inmix",
    )(xp, xs, mod, mod, mod, mod, g_mix, w_in_t, w_low_t, conv_w, conv_b, wgu_b, b_gate, n_conv, n_gla)


def _mixer_s_kernel(p_ref, sconv_ref, sst_ref, cw_ref, cb_ref, wgu_ref, bg_ref, ncv_ref, ngl_ref,
                    y_ref, nconv_ref, nst_ref, ubuf):
    nseq = SEQ_PER_CHUNK
    bgate = p_ref[:, 0:CW]
    u = p_ref[:, CW:2 * CW] * p_ref[:, 2 * CW:3 * CW]
    ubuf[0:8, :] = jnp.zeros((8, CW), F32)
    ubuf[8:8 + CH, :] = u
    tpos = _iota2((CH, CW), 0) % L_S
    s0 = jnp.concatenate([jnp.broadcast_to(sconv_ref[s, 0:1, :], (L_S, CW)) for s in range(nseq)], axis=0)
    s1 = jnp.concatenate([jnp.broadcast_to(sconv_ref[s, 1:2, :], (L_S, CW)) for s in range(nseq)], axis=0)
    um1 = jnp.where(tpos == 0, s1, ubuf[7:7 + CH, :])
    um2 = jnp.where(tpos == 0, s0, jnp.where(tpos == 1, s1, ubuf[6:6 + CH, :]))
    y_ref[:, 0:CW] = _conv_out(bgate, u, um1, um2, cw_ref, cb_ref, ncv_ref).astype(BF16)
    for s in range(nseq):
        nconv_ref[s] = u[s * L_S + L_S - 2:(s + 1) * L_S, :]

    qt, kt, kend, ebl = _gla_prep(p_ref, wgu_ref, bg_ref, CH, L_S)
    rr = _iota2((H * CH, CH), 0) % CH
    cc = _iota2((H * CH, CH), 1)
    causal = (rr >= cc) & ((rr // L_S) == (cc // L_S))
    ngl = ngl_ref[...]
    lhs_f = _head_stack(qt)
    sc = lax.dot_general(lhs_f.astype(BF16), kt.astype(BF16), (((1,), (1,)), ((), ())),
                         preferred_element_type=F32)
    sc = jnp.where(causal, sc, 0.0).astype(BF16)
    kstack_f = _head_stack(kend)
    v_all = p_ref[:, V_OFF:V_OFF + HV]
    decay_t = jnp.concatenate([ebl, jnp.zeros((128 - CH, HK), F32)], axis=0).T
    o_inter = []
    for s in range(nseq):
        rs = [slice(h * CH + s * L_S, h * CH + (s + 1) * L_S) for h in range(H)]
        lhs_s = jnp.concatenate([lhs_f[r] for r in rs], axis=0).astype(BF16)
        k_s = jnp.concatenate([kstack_f[r] for r in rs], axis=0).astype(BF16)
        v_s = jnp.concatenate([v_all[s * L_S:(s + 1) * L_S, h * DV:(h + 1) * DV] for h in range(H)],
                              axis=0).astype(BF16)
        st = sst_ref[s]
        o_inter.append(jnp.dot(lhs_s, st.astype(BF16), preferred_element_type=F32))
        kv = lax.dot_general(k_s, v_s, (((0,), (0,)), ((), ())), preferred_element_type=F32)
        nst_ref[s] = st * decay_t[:, s * L_S:s * L_S + 1] + kv
    for h in range(H):
        v_h = v_all[:, h * DV:(h + 1) * DV].astype(BF16)
        g_h = p_ref[:, G_OFF + h * DV:G_OFF + (h + 1) * DV]
        oi_h = jnp.concatenate([o_inter[s][h * L_S:(h + 1) * L_S] for s in range(nseq)], axis=0)
        o_h = jnp.dot(sc[h * CH:(h + 1) * CH], v_h, preferred_element_type=F32) + oi_h
        y_ref[:, CW + h * DV:CW + (h + 1) * DV] = _gla_out(o_h, g_h, ngl).astype(BF16)


def _mixer_s(proj, sconv, sst, conv_w, conv_b, wgu_b, b_gate, n_conv, n_gla):
    nseq = SEQ_PER_CHUNK
    return pl.pallas_call(
        _mixer_s_kernel,
        out_shape=(jax.ShapeDtypeStruct((T_S, D), BF16),
                   jax.ShapeDtypeStruct((B_S, 2, CW), F32),
                   jax.ShapeDtypeStruct((B_S, HK, DV), F32)),
        grid=(T_S // CH,),
        in_specs=[pl.BlockSpec((CH, PROJ_PAD), lambda i: (i, 0)),
                  pl.BlockSpec((nseq, 2, CW), lambda i: (i, 0, 0)),
                  pl.BlockSpec((nseq, HK, DV), lambda i: (i, 0, 0)),
                  _const_spec((3, CW)), _const_spec((1, CW)), _const_spec((128, HK)), _const_spec((1, HK)),
                  _const_spec((1, CW)), _const_spec((1, DV))],
        out_specs=(pl.BlockSpec((CH, D), lambda i: (i, 0)),
                   pl.BlockSpec((nseq, 2, CW), lambda i: (i, 0, 0)),
                   pl.BlockSpec((nseq, HK, DV), lambda i: (i, 0, 0))),
        scratch_shapes=[pltpu.VMEM((8 + CH, CW), F32)],
        compiler_params=_cparams(),
        name="mixer_s",
    )(proj, sconv, sst, conv_w, conv_b, wgu_b, b_gate, n_conv, n_gla)


def _route(h2, wr_ref, br_ref, run_ref, pos_ref, rw_ref, cnt_ref):
    lt = lax.dot_general(wr_ref[...], h2.astype(BF16), (((1,), (1,)), ((), ())),
                         preferred_element_type=F32) + br_ref[...]

    coarse = lt[0:N_GROUPS]
    cmax = jnp.max(coarse, axis=0, keepdims=True)
    gi = _iota2((N_GROUPS, TM), 0)
    grp = jnp.min(jnp.where(coarse == cmax, gi, N_GROUPS), axis=0, keepdims=True)
    p_sel = 1.0 / jnp.sum(jnp.exp(coarse - cmax), axis=0, keepdims=True)
    fine = jnp.zeros((EPG, TM), F32)
    for g in range(N_GROUPS):
        fine = jnp.where(grp == g, lt[8 + g * EPG:8 + (g + 1) * EPG], fine)
    ei = _iota2((EPG, TM), 0)
    f1 = jnp.max(fine, axis=0, keepdims=True)
    i1 = jnp.min(jnp.where(fine == f1, ei, EPG), axis=0, keepdims=True)
    rest = jnp.where(ei == i1, -jnp.inf, fine)
    f2 = jnp.max(rest, axis=0, keepdims=True)
    i2 = jnp.min(jnp.where(rest == f2, ei, EPG), axis=0, keepdims=True)
    e2 = jnp.exp(f2 - f1)
    w1 = p_sel / (1.0 + e2)
    w2 = p_sel * e2 / (1.0 + e2)
    x1 = grp * EPG + i1
    x2 = grp * EPG + i2

    er = _iota2((NE, TM), 0)
    oh1 = jnp.where(er == x1, 1.0, 0.0)
    oh2 = jnp.where(er == x2, 1.0, 0.0)
    oh = oh1 + oh2
    before = jnp.where(_iota2((TM, TM), 0) < _iota2((TM, TM), 1), 1.0, 0.0).astype(BF16)
    cum = jnp.dot(oh.astype(BF16), before, preferred_element_type=F32) + run_ref[:, 0:1]
    r1 = jnp.sum(oh1 * cum, axis=0, keepdims=True).astype(I32)
    r2 = jnp.sum(oh2 * cum, axis=0, keepdims=True).astype(I32)
    run_new = run_ref[...] + jnp.sum(oh, axis=1, keepdims=True)
    run_ref[...] = run_new
    cnt_ref[...] = run_new.astype(I32)

    row = _iota2((8, TM), 0)
    pos_ref[...] = jnp.where(row == 0, x1 * E_CAP + r1, jnp.where(row == 1, x2 * E_CAP + r2, 0))
    wrow = _iota2((128, TM), 0)
    rw_ref[...] = jnp.where(wrow == 0, w1, jnp.where(wrow == 1, w2, 0.0)).T


def _outproj_kernel(ycp_ref, ycs_ref, xp_ref, xs_ref, gtp_ref, gts_ref, shp_ref, shs_ref, scp_ref, scs_ref,
                    wo_hbm, g_ref, wr_ref, br_ref,
                    x1_ref, pos_ref, rw_ref, cnt_ref, xs_out,
                    run_ref, h2_scr, wo_ref, pos_v, pos_s, sem, psem, wsem):
    i = pl.program_id(0)
    par = i % 2

    @pl.when(i == 0)
    def _():
        def chunk(c):
            return pltpu.make_async_copy(wo_hbm.at[0, pl.ds(c * TM, TM), :], h2_scr.at[c % 2], wsem.at[c % 2])

        chunk(0).start()
        for c in range(D // TM):
            if c + 1 < D // TM:
                chunk(c + 1).start()
            chunk(c).wait()
            wo_ref[c * TM:(c + 1) * TM, :] = h2_scr[c % 2].astype(BF16)

    to_smem = pltpu.make_async_copy(pos_v, pos_s, psem)

    def scatter(p):
        for r in range(TM):
            for slot in range(2):
                pltpu.make_async_copy(h2_scr.at[p, pl.ds(r, 1), :], xs_out.at[pl.ds(pos_s[slot, r], 1), :],
                                      sem).start(priority=r % 2)

    def drain():
        for _ in range(2):
            pltpu.make_async_copy(h2_scr.at[0], xs_out.at[pl.ds(0, TM), :], sem).wait()

    @pl.when(i == 0)
    def _():
        run_ref[...] = jnp.zeros((NE, 128), F32)
        h2_scr[1] = jnp.zeros((TM, D), F32)
        pos_v[...] = E_ROWS + _iota2((8, TM), 0) * TM + _iota2((8, TM), 1)
        to_smem.start()

    to_smem.wait()

    def tile(yc, x, gt, sh, sc):
        scatter(1 - par)
        mix = jnp.dot(yc, wo_ref[...], preferred_element_type=F32)
        x1 = x + gt * mix
        x1_ref[...] = x1
        h2 = _rms(x1, g_ref[...]) * (1.0 + sc) + sh
        h2_scr[par] = h2
        _route(h2, wr_ref, br_ref, run_ref, pos_v, rw_ref, cnt_ref)

    @pl.when(i < N_TILES_P)
    def _():
        tile(ycp_ref[...], xp_ref[...], _prompt_row(gtp_ref), _prompt_row(shp_ref), _prompt_row(scp_ref))

    @pl.when(i >= N_TILES_P)
    def _():
        tile(ycs_ref[...], xs_ref[...], _sample_rows(gts_ref), _sample_rows(shs_ref), _sample_rows(scs_ref))

    pos_ref[0] = pos_v[...]
    drain()
    to_smem.start()

    @pl.when(i == N_TILES - 1)
    def _():
        to_smem.wait()
        scatter(par)
        drain()


def _outproj(ycp, ycs, xp, xs, mod, w_out, g_ffn, w_r, b_r):
    return pl.pallas_call(
        _outproj_kernel,
        out_shape=(jax.ShapeDtypeStruct((T, D), F32), jax.ShapeDtypeStruct((N_TILES, 8, TM), I32),
                   jax.ShapeDtypeStruct((T, 128), F32), jax.ShapeDtypeStruct((NE, 128), I32),
                   jax.ShapeDtypeStruct((E_ROWS + 2 * TM, D), F32)),
        grid=(N_TILES,),
        in_specs=[pl.BlockSpec((TM, D), lambda i: (_p_tile(i), 0)), pl.BlockSpec((TM, D), lambda i: (_s_tile(i), 0)),
                  pl.BlockSpec((TM, D), lambda i: (_p_tile(i), 0)), pl.BlockSpec((TM, D), lambda i: (_s_tile(i), 0)),
                  *_mod_specs(2), *_mod_specs(3), *_mod_specs(4),
                  pl.BlockSpec(memory_space=pl.ANY), _const_spec((1, D)), _const_spec((R_ROWS, D)),
                  _const_spec((R_ROWS, 1))],
        out_specs=(pl.BlockSpec((TM, D), lambda i: (i, 0)), pl.BlockSpec((1, 8, TM), lambda i: (i, 0, 0)),
                   pl.BlockSpec((TM, 128), lambda i: (i, 0)), pl.BlockSpec((NE, 128), lambda i: (0, 0)),
                   pl.BlockSpec(memory_space=pl.ANY)),
        scratch_shapes=[pltpu.VMEM((NE, 128), F32), pltpu.VMEM((2, TM, D), F32), pltpu.VMEM((D, D), BF16),
                        pltpu.VMEM((8, TM), I32), pltpu.SMEM((8, TM), I32), pltpu.SemaphoreType.DMA(()),
                        pltpu.SemaphoreType.DMA(()), pltpu.SemaphoreType.DMA((2,))],
        compiler_params=_cparams(),
        name="outproj",
    )(ycp, ycs, xp, xs, mod, mod, mod, mod, mod, mod, w_out, g_ffn, w_r, b_r)


E_CHUNK = 64
E_CHUNKS = E_TILE // E_CHUNK


def _plan_tiles(cnt_ref, tb_ref, te_ref, first_ref, nval_ref, nxt_ref, slot_ref, nt_ref, nxte_ref):
    def backward(k, next_occupied):
        e = NE - 1 - k
        nxte_ref[e] = next_occupied
        return jnp.where(cnt_ref[e, 0] > 0, e, next_occupied)

    lax.fori_loop(0, NE, backward, jnp.int32(-1))

    def forward(e, carry):
        t, rank = carry
        cnt = cnt_ref[e, 0]
        n = lax.shift_right_logical(cnt + (E_TILE - 1), E_TILE.bit_length() - 1)

        def tile(k, _):
            tb_ref[t + k] = e * E_CAP_TILES + k
            te_ref[t + k] = e
            first_ref[t + k] = (k == 0).astype(I32)
            nval_ref[t + k] = jnp.minimum(cnt - k * E_TILE, E_TILE)
            nxt_ref[t + k] = nxte_ref[e]
            slot_ref[t + k] = rank & 1
            return 0

        lax.fori_loop(0, n, tile, 0)
        return t + n, rank + (n > 0).astype(I32)

    n_tiles, _ = lax.fori_loop(0, NE, forward, (jnp.int32(0), jnp.int32(0)))
    nt_ref[0] = n_tiles


def _moe_kernel(cnt_ref, x_hbm, wg_hbm, wu_hbm, wd_hbm, y_hbm,
                xbuf, ybuf, wg_st, wu_st, wd_st, wgu_b, wd_b,
                tb_ref, te_ref, first_ref, nval_ref, nxt_ref, slot_ref, nt_ref, nxte_ref, wsem, xsem, ysem):
    i = pl.program_id(0)
    par = i % 2

    @pl.when(i == 0)
    def _():
        _plan_tiles(cnt_ref, tb_ref, te_ref, first_ref, nval_ref, nxt_ref, slot_ref, nt_ref, nxte_ref)

    nt = nt_ref[0]

    def weight_copies(e, s):
        return (pltpu.make_async_copy(wg_hbm.at[e], wg_st.at[s], wsem.at[s]),
                pltpu.make_async_copy(wu_hbm.at[e], wu_st.at[s], wsem.at[s]),
                pltpu.make_async_copy(wd_hbm.at[e], wd_st.at[s], wsem.at[s]))

    def tile_chunks(t, p, output, fn):
        row0 = tb_ref[t] * E_TILE
        for c in range(E_CHUNKS):
            rows = pl.ds(c * E_CHUNK, E_CHUNK)
            hbm_rows = pl.ds(row0 + c * E_CHUNK, E_CHUNK)

            @pl.when(c * E_CHUNK < nval_ref[t])
            def _():
                if output:
                    fn(pltpu.make_async_copy(ybuf.at[p, rows, :], y_hbm.at[hbm_rows, 0], ysem.at[p]))
                else:
                    fn(pltpu.make_async_copy(x_hbm.at[hbm_rows, :], xbuf.at[p, rows, :], xsem.at[p]))

    def start(cp):
        cp.start()

    def wait(cp):
        cp.wait()

    @pl.when(i == 0)
    def _():
        tile_chunks(0, 0, False, start)

    @pl.when(i + 1 < nt)
    def _():
        tile_chunks(i + 1, 1 - par, False, start)

    @pl.when(i < nt)
    def _():
        tile_chunks(i, par, False, wait)

        @pl.when(first_ref[i] == 1)
        def _():
            s = slot_ref[i]

            @pl.when(i == 0)
            def _():
                for cp in weight_copies(te_ref[0], 0):
                    cp.start(priority=1)

            for cp in weight_copies(te_ref[i], s):
                cp.wait()

            @pl.when(nxt_ref[i] >= 0)
            def _():
                for cp in weight_copies(nxt_ref[i], 1 - s):
                    cp.start(priority=1)

            wgu_b[:, 0:DE] = wg_st[s].astype(BF16)
            wgu_b[:, DE:2 * DE] = wu_st[s].astype(BF16)
            wd_b[...] = wd_st[s].astype(BF16)

        valid = _iota2((E_TILE, D), 0) < nval_ref[i]
        x = jnp.where(valid, xbuf[par], 0.0).astype(BF16)
        ab = jnp.dot(x, wgu_b[...], preferred_element_type=F32)
        hid = (_silu(ab[:, 0:DE]) * ab[:, DE:2 * DE]).astype(BF16)
        ybuf[par] = jnp.dot(hid, wd_b[...], preferred_element_type=F32)

        tile_chunks(i, par, True, start)

        @pl.when(i >= 1)
        def _():
            tile_chunks(i - 1, 1 - par, True, wait)

        @pl.when(i == nt - 1)
        def _():
            tile_chunks(i, par, True, wait)


def _moe(counts, xs, w_eg, w_eu, w_ed):
    tile_table = pltpu.SMEM((E_TILES_MAX,), I32)
    grid_spec = pltpu.PrefetchScalarGridSpec(
        num_scalar_prefetch=1,
        grid=(E_TILES_MAX,),
        in_specs=[pl.BlockSpec(memory_space=pl.ANY)] * 4,
        out_specs=pl.BlockSpec(memory_space=pl.ANY),
        scratch_shapes=[pltpu.VMEM((2, E_TILE, D), F32), pltpu.VMEM((2, E_TILE, D), F32),
                        pltpu.VMEM((2, D, DE), F32), pltpu.VMEM((2, D, DE), F32), pltpu.VMEM((2, DE, D), F32),
                        pltpu.VMEM((D, 2 * DE), BF16), pltpu.VMEM((DE, D), BF16),
                        tile_table, tile_table, tile_table, tile_table, tile_table, tile_table,
                        pltpu.SMEM((1,), I32), pltpu.SMEM((NE,), I32),
                        pltpu.SemaphoreType.DMA((2,)), pltpu.SemaphoreType.DMA((2,)), pltpu.SemaphoreType.DMA((2,))],
    )
    return pl.pallas_call(
        _moe_kernel,
        out_shape=jax.ShapeDtypeStruct((E_ROWS, 1, D), F32),
        grid_spec=grid_spec,
        compiler_params=_cparams(),
        name="moe",
    )(counts, xs, w_eg, w_eu, w_ed)


def _final_kernel(pos0_ref, posn_ref, x1_ref, rw_ref, gtp_ref, gts_ref, g_ref, ys_ref, yp_ref, ysm_ref, buf, sem):
    i = pl.program_id(0)
    par = i % 2

    def gather(p_ref, p):
        for r in range(TM):
            for slot in range(2):
                pltpu.make_async_copy(ys_ref.at[p_ref[0, slot, r]], buf.at[p, slot, pl.ds(r, 1), :],
                                      sem.at[p]).start(priority=r % 2)

    @pl.when(i == 0)
    def _():
        gather(pos0_ref, 0)

    @pl.when(i < N_TILES - 1)
    def _():
        gather(posn_ref, 1 - par)

    for slot in range(2):
        pltpu.make_async_copy(ys_ref.at[pl.ds(0, TM), 0], buf.at[par, slot], sem.at[par]).wait()

    moe = rw_ref[:, 0:1] * buf[par, 0] + rw_ref[:, 1:2] * buf[par, 1]

    @pl.when(i < N_TILES_P)
    def _():
        yp_ref[...] = _rms(x1_ref[...] + _prompt_row(gtp_ref) * moe, g_ref[...])

    @pl.when(i >= N_TILES_P)
    def _():
        ysm_ref[...] = _rms(x1_ref[...] + _sample_rows(gts_ref) * moe, g_ref[...])


def _final(pos, x1, rw, mod, g_fin, ys):
    return pl.pallas_call(
        _final_kernel,
        out_shape=(jax.ShapeDtypeStruct((T_P, D), F32), jax.ShapeDtypeStruct((T_S, D), F32)),
        grid=(N_TILES,),
        in_specs=[pl.BlockSpec((1, 8, TM), lambda i: (0, 0, 0), memory_space=pltpu.SMEM),
                  pl.BlockSpec((1, 8, TM), lambda i: (jnp.minimum(i + 1, N_TILES - 1), 0, 0),
                               memory_space=pltpu.SMEM),
                  pl.BlockSpec((TM, D), lambda i: (i, 0)),
                  pl.BlockSpec((TM, 128), lambda i: (i, 0)),
                  *_mod_specs(5),
                  _const_spec((1, D)),
                  pl.BlockSpec(memory_space=pl.ANY)],
        out_specs=(pl.BlockSpec((TM, D), lambda i: (_p_tile(i), 0)),
                   pl.BlockSpec((TM, D), lambda i: (_s_tile(i), 0))),
        scratch_shapes=[pltpu.VMEM((2, 2, TM, D), F32), pltpu.SemaphoreType.DMA((2,))],
        compiler_params=_cparams(),
        name="final",
    )(pos, pos, x1, rw, mod, mod, g_fin, ys)


def kernel(x_prompt, x_sample, c_prompt, c_sample, state_conv, state_gla, w_ada, b_ada, norm_mix, w_in, conv_w,
           conv_b, w_gate_up, b_gate, norm_conv, norm_gla, w_out, norm_ffn, w_coarse, b_coarse, w_fine, b_fine,
           w_exp_gate, w_exp_up, w_exp_down, norm_final):
    xp = x_prompt.reshape(T_P, D)
    xs = x_sample.reshape(T_S, D)
    c_all = jnp.concatenate([c_sample, c_prompt, jnp.zeros((C_ROWS - B_S - B_P, D), F32)], axis=0)
    mod = _ada(c_all, w_ada[0], b_ada[0][None, :])

    w_in_t = jnp.swapaxes(w_in, 1, 2)
    w_low_t = jnp.pad(w_in_t[0, PROJ_MAIN:, :], ((0, 128 - RANK), (0, 0))).astype(BF16)
    wgu_b = jnp.pad(w_gate_up[0], ((0, 128 - RANK), (0, 0))).astype(BF16)
    w_r = jnp.concatenate([w_coarse[0].T, jnp.zeros((8 - N_GROUPS, D), F32), w_fine[0].T], axis=0)
    b_r = jnp.concatenate([b_coarse[0], jnp.zeros((8 - N_GROUPS,), F32), b_fine[0]])[:, None]
    g_mix, g_ffn, g_fin = norm_mix[0][None, :], norm_ffn[0][None, :], norm_final[None, :]
    mix_consts = (conv_w[0], conv_b[0][None, :], wgu_b, b_gate[0][None, :], norm_conv[0][None, :],
                  norm_gla[0][None, :])

    ycat_p, nconv_p, nst_p, proj_s = _inmix(xp, xs, mod, g_mix, w_in_t, w_low_t, *mix_consts)
    ycat_s, nconv_s, nst_s = _mixer_s(proj_s, state_conv[0], state_gla[0].reshape(B_S, HK, DV), *mix_consts)

    x1, pos, rw, counts, x_sorted = _outproj(ycat_p, ycat_s, xp, xs, mod, w_out, g_ffn, w_r.astype(BF16), b_r)
    y_sorted = _moe(counts, x_sorted, w_exp_gate[0], w_exp_up[0], w_exp_down[0])
    y_p, y_s = _final(pos, x1, rw, mod, g_fin, y_sorted)

    new_gla_p = nst_p.reshape(B_P, DV, H, DK).transpose(0, 2, 3, 1)[None]
    new_gla_s = nst_s.reshape(1, B_S, H, DK, DV)
    return (y_p.reshape(B_P, L_P, D), y_s.reshape(B_S, L_S, D), nconv_p[None], new_gla_p,
            nconv_s[None], new_gla_s)
```

```python
import jax
import jax.numpy as jnp
from jax import lax
from jax.experimental import pallas as pl
from jax.experimental.pallas import tpu as pltpu

F32 = jnp.float32
BF16 = jnp.bfloat16
I32 = jnp.int32

D = 2048
CW = 1024
H = 8
DK = 64
DV = 128
HK = H * DK
HV = H * DV
RANK = 16
TAU = 16.0
N_GROUPS = 4
EPG = 8
NE = N_GROUPS * EPG
DE = 512
EPS = 1e-6

B_P, L_P = 4, 2048
B_S, L_S = 128, 8
T_P = B_P * L_P
T_S = B_S * L_S
T = T_P + T_S

TM = 256
CH = 64
N_TILES = T // TM
N_TILES_P = T_P // TM
TILES_PER_SEQ = L_P // TM
SEQ_PER_TILE = TM // L_S
SEQ_PER_CHUNK = CH // L_S

PROJ_MAIN = 3 * CW + 2 * HK + 2 * HV
PROJ_PAD = PROJ_MAIN + 128
C_ROWS = 136
P_ROW_BLOCK = B_S // 8
R_ROWS = 40

E_TILE = 256
E_TILES_MAX = (2 * T) // E_TILE + NE
E_CAP = T
E_CAP_TILES = E_CAP // E_TILE
E_ROWS = NE * E_CAP
VMEM_LIMIT = 60 * 1024 * 1024


def _cparams(n_axes=1, vmem=VMEM_LIMIT):
    return pltpu.CompilerParams(dimension_semantics=("arbitrary",) * n_axes, vmem_limit_bytes=vmem)


def _rms(x, g):
    return x * lax.rsqrt(jnp.mean(x * x, axis=-1, keepdims=True) + EPS) * g


def _sigmoid(x):
    return 1.0 / (1.0 + jnp.exp(-x))


def _silu(x):
    return x * _sigmoid(x)


def _log_sigmoid(x):
    return jnp.minimum(x, 0.0) - jnp.log(1.0 + jnp.exp(-jnp.abs(x)))


def _iota2(shape, axis):
    return lax.broadcasted_iota(I32, shape, axis)


def _expand_rows(ref, n, reps):
    return jnp.concatenate([jnp.broadcast_to(ref[j:j + 1, :], (reps, ref.shape[-1])) for j in range(n)], axis=0)


def _prompt_row(ref):
    return ref[pl.ds(pl.program_id(0) // TILES_PER_SEQ, 1), :]


def _sample_rows(ref):
    return _expand_rows(ref, SEQ_PER_TILE, L_S)


def _p_tile(i):
    return jnp.minimum(i, N_TILES_P - 1)


def _s_tile(i):
    return jnp.maximum(i - N_TILES_P, 0)


def _mod_specs(col):
    return [pl.BlockSpec((8, D), lambda i, *_: (P_ROW_BLOCK, col)),
            pl.BlockSpec((SEQ_PER_TILE, D), lambda i, *_: (_s_tile(i), col))]


def _const_spec(shape):
    zeros = (0,) * len(shape)
    return pl.BlockSpec(shape, lambda *_: zeros, pipeline_mode=pl.Buffered(1))


def _ada_kernel(c_ref, w_ref, b_ref, o_ref):
    s = _silu(c_ref[...]).astype(BF16)
    o_ref[...] = jnp.dot(s, w_ref[...].astype(BF16), preferred_element_type=F32) + b_ref[...]


def _ada(c_all, w_ada, b_ada):
    tn = 1024
    return pl.pallas_call(
        _ada_kernel,
        out_shape=jax.ShapeDtypeStruct((C_ROWS, 6 * D), F32),
        grid=(6 * D // tn,),
        in_specs=[pl.BlockSpec((C_ROWS, D), lambda j: (0, 0)),
                  pl.BlockSpec((D, tn), lambda j: (0, j)),
                  pl.BlockSpec((1, tn), lambda j: (0, j))],
        out_specs=pl.BlockSpec((C_ROWS, tn), lambda j: (0, j)),
        compiler_params=_cparams(),
        name="ada",
    )(c_all, w_ada, b_ada)


W_CHUNK = TM
N_W_CHUNKS = PROJ_MAIN // W_CHUNK
NT_DIMS = (((1,), (1,)), ((), ()))


def _gla_prep(p_ref, wgu_ref, bg_ref, rows, seg, totals_from_last_row=False):
    q = p_ref[:, 3 * CW:3 * CW + HK]
    k = p_ref[:, 3 * CW + HK:3 * CW + 2 * HK]
    a = p_ref[:, PROJ_MAIN:PROJ_PAD].astype(BF16)
    z = jnp.dot(a, wgu_ref[...], preferred_element_type=F32) + bg_ref[...]
    la = _log_sigmoid(z) * (1.0 / TAU)
    la0 = la.astype(BF16)
    la1 = (la - la0.astype(F32)).astype(BF16)
    sel_rows = rows if totals_from_last_row else 2 * rows
    r = _iota2((sel_rows, rows), 0)
    c = _iota2((sel_rows, rows), 1)
    same = ((r % rows) // seg) == (c // seg)
    sel = jnp.where(same & ((r >= rows) | (c <= r)), 1.0, 0.0).astype(BF16)
    sums = sum(jnp.dot(sel, part, preferred_element_type=F32) for part in (la0, la1))
    b = sums[0:rows]
    if totals_from_last_row:
        ebl = jnp.concatenate([jnp.broadcast_to(jnp.exp(b[s + seg - 1:s + seg, :]), (seg, HK))
                               for s in range(0, rows, seg)], axis=0)
    else:
        ebl = jnp.exp(sums[rows:2 * rows])
    qt = q * jnp.exp(b) * (DK ** -0.5)
    kt = k * jnp.exp(-b)
    kend = kt * ebl
    return qt, kt, kend, ebl


def _head_stack(x):
    rows = x.shape[0]
    t = jnp.concatenate([x] * H, axis=0)
    keep = (_iota2((H * rows, HK), 0) // rows) == (_iota2((H * rows, HK), 1) // DK)
    return jnp.where(keep, t, 0.0)


def _gla_out(o_h, g_h, ngl):
    on = o_h * lax.rsqrt(jnp.mean(o_h * o_h, axis=-1, keepdims=True) + EPS) * ngl
    return on * _silu(g_h)


def _conv_out(bg, u, um1, um2, cw_ref, cb_ref, ncv_ref):
    conv_y = cb_ref[...] + cw_ref[0:1, :] * um2 + cw_ref[1:2, :] * um1 + cw_ref[2:3, :] * u
    return _rms(bg * conv_y, ncv_ref[...])


V_OFF = 3 * CW + 2 * HK
G_OFF = V_OFF + HV


def _mixer_p_tile(j, p_ref, cw_ref, cb_ref, wgu_ref, bg_ref, ncv_ref, ngl_ref,
                  y_ref, nconv_ref, nst_ref, ubuf, st_ref, also):
    @pl.when(j == 0)
    def _():
        ubuf[0:8, :] = jnp.zeros((8, CW), F32)
        st_ref[...] = jnp.zeros((DV, HK), F32)

    bgate = p_ref[:, 0:CW]
    u = p_ref[:, CW:2 * CW] * p_ref[:, 2 * CW:3 * CW]
    ubuf[8:8 + TM, :] = u
    um1 = ubuf[7:7 + TM, :]
    um2 = ubuf[6:6 + TM, :]
    y_ref[:, 0:CW] = _conv_out(bgate, u, um1, um2, cw_ref, cb_ref, ncv_ref).astype(BF16)
    ubuf[6:8, :] = u[TM - 2:TM, :]

    qt, kt, kend, ebl = _gla_prep(p_ref, wgu_ref, bg_ref, TM, CH, totals_from_last_row=True)
    causal = _iota2((2 * CH, 2 * CH), 0) % CH >= _iota2((2 * CH, 2 * CH), 1) % CH
    first_head = _iota2((CH, 2 * DK), 1) < DK
    ngl = ngl_ref[...]

    def pair_stack(x):
        return jnp.concatenate([jnp.where(first_head, x, 0.0), jnp.where(first_head, 0.0, x)], axis=0).astype(BF16)

    for c in range(TM // CH):
        r0 = c * CH
        for m in range(H // 2):
            lanes = slice(m * 2 * DK, (m + 1) * 2 * DK)
            lhs = pair_stack(qt[r0:r0 + CH, lanes])
            sc = lax.dot_general(lhs, pair_stack(kt[r0:r0 + CH, lanes]), (((1,), (1,)), ((), ())),
                                 preferred_element_type=F32)
            sc = jnp.where(causal, sc, 0.0).astype(BF16)
            st = st_ref[:, lanes]
            vstack = jnp.concatenate(
                [p_ref[r0:r0 + CH, V_OFF + h * DV:V_OFF + (h + 1) * DV].astype(BF16) for h in (2 * m, 2 * m + 1)],
                axis=0)
            o_pair = (jnp.dot(sc, vstack, preferred_element_type=F32)
                      + lax.dot_general(lhs, st.astype(BF16), (((1,), (1,)), ((), ())),
                                        preferred_element_type=F32))
            for hh in range(2):
                h = 2 * m + hh
                g_h = p_ref[r0:r0 + CH, G_OFF + h * DV:G_OFF + (h + 1) * DV]
                y_ref[r0:r0 + CH, CW + h * DV:CW + (h + 1) * DV] = _gla_out(
                    o_pair[hh * CH:(hh + 1) * CH], g_h, ngl).astype(BF16)
            kv_t = lax.dot_general(vstack, pair_stack(kend[r0:r0 + CH, lanes]),
                                   (((0,), (0,)), ((), ())), preferred_element_type=F32)
            st_ref[:, lanes] = st * ebl[r0:r0 + 1, lanes] + kv_t

    also()

    @pl.when(j == TILES_PER_SEQ - 1)
    def _():
        nconv_ref[0] = ubuf[6:8, :]
        nst_ref[0] = st_ref[...]


def _inmix_kernel(xp_ref, xs_ref, shp_ref, shs_ref, scp_ref, scs_ref, g_ref, wt_hbm, wa_ref,
                  cw_ref, cb_ref, wgu_ref, bg_ref, ncv_ref, ngl_ref,
                  y_ref, nconv_ref, nst_ref, projs_hbm,
                  h_scr, w_scr, proj, ubuf, st_ref, wsem, psem):
    i = pl.program_id(0)
    par = i % 2

    @pl.when(i == 0)
    def _():
        def chunk(c):
            return pltpu.make_async_copy(wt_hbm.at[0, pl.ds(c * W_CHUNK, W_CHUNK), :],
                                         proj.at[c % 2, :, pl.ds(0, D)], wsem.at[c % 2])

        chunk(0).start()
        for c in range(N_W_CHUNKS):
            if c + 1 < N_W_CHUNKS:
                chunk(c + 1).start()
            chunk(c).wait()
            w_scr[c * W_CHUNK:(c + 1) * W_CHUNK, :] = proj[c % 2, :, 0:D].astype(BF16)

    @pl.when(i < N_TILES_P)
    def _():
        h = _rms(xp_ref[...], g_ref[...]) * (1.0 + _prompt_row(scp_ref)) + _prompt_row(shp_ref)
        h_scr[...] = h.astype(BF16)

    @pl.when(i >= N_TILES_P)
    def _():
        h = _rms(xs_ref[...], g_ref[...]) * (1.0 + _sample_rows(scs_ref)) + _sample_rows(shs_ref)
        h_scr[...] = h.astype(BF16)

    def project():
        h = h_scr[...]
        dst = proj.at[par]
        dst[:, 0:PROJ_MAIN] = lax.dot_general(h, w_scr[...], NT_DIMS, preferred_element_type=F32)
        dst[:, PROJ_MAIN:PROJ_PAD] = lax.dot_general(h, wa_ref[...], NT_DIMS, preferred_element_type=F32)

    has_mixer = (i >= 1) & (i <= N_TILES_P)

    @pl.when(has_mixer)
    def _():
        _mixer_p_tile((i - 1) % TILES_PER_SEQ, proj.at[1 - par], cw_ref, cb_ref, wgu_ref, bg_ref, ncv_ref, ngl_ref,
                      y_ref, nconv_ref, nst_ref, ubuf, st_ref, project)

    @pl.when(jnp.logical_not(has_mixer))
    def _():
        project()

    @pl.when(i >= N_TILES_P)
    def _():
        cp = pltpu.make_async_copy(proj.at[par], projs_hbm.at[pl.ds((i - N_TILES_P) * TM, TM), :], psem)
        cp.start()
        cp.wait()


def _inmix(xp, xs, mod, g_mix, w_in_t, w_low_t, conv_w, conv_b, wgu_b, b_gate, n_conv, n_gla):
    def prev_tile(i):
        return jnp.clip(i - 1, 0, N_TILES_P - 1)

    return pl.pallas_call(
        _inmix_kernel,
        out_shape=(jax.ShapeDtypeStruct((T_P, D), BF16),
                   jax.ShapeDtypeStruct((B_P, 2, CW), F32),
                   jax.ShapeDtypeStruct((B_P, DV, HK), F32),
                   jax.ShapeDtypeStruct((T_S, PROJ_PAD), F32)),
        grid=(N_TILES,),
        in_specs=[pl.BlockSpec((TM, D), lambda i: (_p_tile(i), 0)),
                  pl.BlockSpec((TM, D), lambda i: (_s_tile(i), 0)),
                  *_mod_specs(0), *_mod_specs(1),
                  _const_spec((1, D)), pl.BlockSpec(memory_space=pl.ANY), _const_spec((128, D)),
                  _const_spec((3, CW)), _const_spec((1, CW)), _const_spec((128, HK)), _const_spec((1, HK)),
                  _const_spec((1, CW)), _const_spec((1, DV))],
        out_specs=(pl.BlockSpec((TM, D), lambda i: (prev_tile(i), 0)),
                   pl.BlockSpec((1, 2, CW), lambda i: (prev_tile(i) // TILES_PER_SEQ, 0, 0)),
                   pl.BlockSpec((1, DV, HK), lambda i: (prev_tile(i) // TILES_PER_SEQ, 0, 0)),
                   pl.BlockSpec(memory_space=pl.ANY)),
        scratch_shapes=[pltpu.VMEM((TM, D), BF16), pltpu.VMEM((PROJ_MAIN, D), BF16),
                        pltpu.VMEM((2, TM, PROJ_PAD), F32),
                        pltpu.VMEM((8 + TM, CW), F32), pltpu.VMEM((DV, HK), F32),
                        pltpu.SemaphoreType.DMA((2,)), pltpu.SemaphoreType.DMA(())],
        compiler_params=_cparams(),
        name="inmix",
    )(xp, xs, mod, mod, mod, mod, g_mix, w_in_t, w_low_t, conv_w, conv_b, wgu_b, b_gate, n_conv, n_gla)


def _mixer_s_kernel(p_ref, sconv_ref, sst_ref, cw_ref, cb_ref, wgu_ref, bg_ref, ncv_ref, ngl_ref,
                    y_ref, nconv_ref, nst_ref, ubuf):
    nseq = SEQ_PER_CHUNK
    bgate = p_ref[:, 0:CW]
    u = p_ref[:, CW:2 * CW] * p_ref[:, 2 * CW:3 * CW]
    ubuf[0:8, :] = jnp.zeros((8, CW), F32)
    ubuf[8:8 + CH, :] = u
    tpos = _iota2((CH, CW), 0) % L_S
    s0 = jnp.concatenate([jnp.broadcast_to(sconv_ref[s, 0:1, :], (L_S, CW)) for s in range(nseq)], axis=0)
    s1 = jnp.concatenate([jnp.broadcast_to(sconv_ref[s, 1:2, :], (L_S, CW)) for s in range(nseq)], axis=0)
    um1 = jnp.where(tpos == 0, s1, ubuf[7:7 + CH, :])
    um2 = jnp.where(tpos == 0, s0, jnp.where(tpos == 1, s1, ubuf[6:6 + CH, :]))
    y_ref[:, 0:CW] = _conv_out(bgate, u, um1, um2, cw_ref, cb_ref, ncv_ref).astype(BF16)
    for s in range(nseq):
        nconv_ref[s] = u[s * L_S + L_S - 2:(s + 1) * L_S, :]

    qt, kt, kend, ebl = _gla_prep(p_ref, wgu_ref, bg_ref, CH, L_S)
    rr = _iota2((H * CH, CH), 0) % CH
    cc = _iota2((H * CH, CH), 1)
    causal = (rr >= cc) & ((rr // L_S) == (cc // L_S))
    ngl = ngl_ref[...]
    lhs_f = _head_stack(qt)
    sc = lax.dot_general(lhs_f.astype(BF16), kt.astype(BF16), (((1,), (1,)), ((), ())),
                         preferred_element_type=F32)
    sc = jnp.where(causal, sc, 0.0).astype(BF16)
    kstack_f = _head_stack(kend)
    v_all = p_ref[:, V_OFF:V_OFF + HV]
    decay_t = jnp.concatenate([ebl, jnp.zeros((128 - CH, HK), F32)], axis=0).T
    o_inter = []
    for s in range(nseq):
        rs = [slice(h * CH + s * L_S, h * CH + (s + 1) * L_S) for h in range(H)]
        lhs_s = jnp.concatenate([lhs_f[r] for r in rs], axis=0).astype(BF16)
        k_s = jnp.concatenate([kstack_f[r] for r in rs], axis=0).astype(BF16)
        v_s = jnp.concatenate([v_all[s * L_S:(s + 1) * L_S, h * DV:(h + 1) * DV] for h in range(H)],
                              axis=0).astype(BF16)
        st = sst_ref[s]
        o_inter.append(jnp.dot(lhs_s, st.astype(BF16), preferred_element_type=F32))
        kv = lax.dot_general(k_s, v_s, (((0,), (0,)), ((), ())), preferred_element_type=F32)
        nst_ref[s] = st * decay_t[:, s * L_S:s * L_S + 1] + kv
    for h in range(H):
        v_h = v_all[:, h * DV:(h + 1) * DV].astype(BF16)
        g_h = p_ref[:, G_OFF + h * DV:G_OFF + (h + 1) * DV]
        oi_h = jnp.concatenate([o_inter[s][h * L_S:(h + 1) * L_S] for s in range(nseq)], axis=0)
        o_h = jnp.dot(sc[h * CH:(h + 1) * CH], v_h, preferred_element_type=F32) + oi_h
        y_ref[:, CW + h * DV:CW + (h + 1) * DV] = _gla_out(o_h, g_h, ngl).astype(BF16)


def _mixer_s(proj, sconv, sst, conv_w, conv_b, wgu_b, b_gate, n_conv, n_gla):
    nseq = SEQ_PER_CHUNK
    return pl.pallas_call(
        _mixer_s_kernel,
        out_shape=(jax.ShapeDtypeStruct((T_S, D), BF16),
                   jax.ShapeDtypeStruct((B_S, 2, CW), F32),
                   jax.ShapeDtypeStruct((B_S, HK, DV), F32)),
        grid=(T_S // CH,),
        in_specs=[pl.BlockSpec((CH, PROJ_PAD), lambda i: (i, 0)),
                  pl.BlockSpec((nseq, 2, CW), lambda i: (i, 0, 0)),
                  pl.BlockSpec((nseq, HK, DV), lambda i: (i, 0, 0)),
                  _const_spec((3, CW)), _const_spec((1, CW)), _const_spec((128, HK)), _const_spec((1, HK)),
                  _const_spec((1, CW)), _const_spec((1, DV))],
        out_specs=(pl.BlockSpec((CH, D), lambda i: (i, 0)),
                   pl.BlockSpec((nseq, 2, CW), lambda i: (i, 0, 0)),
                   pl.BlockSpec((nseq, HK, DV), lambda i: (i, 0, 0))),
        scratch_shapes=[pltpu.VMEM((8 + CH, CW), F32)],
        compiler_params=_cparams(),
        name="mixer_s",
    )(proj, sconv, sst, conv_w, conv_b, wgu_b, b_gate, n_conv, n_gla)


def _route(h2, wr_ref, br_ref, run_ref, pos_ref, rw_ref, cnt_ref):
    lt = lax.dot_general(wr_ref[...], h2.astype(BF16), (((1,), (1,)), ((), ())),
                         preferred_element_type=F32) + br_ref[...]

    coarse = lt[0:N_GROUPS]
    cmax = jnp.max(coarse, axis=0, keepdims=True)
    gi = _iota2((N_GROUPS, TM), 0)
    grp = jnp.min(jnp.where(coarse == cmax, gi, N_GROUPS), axis=0, keepdims=True)
    p_sel = 1.0 / jnp.sum(jnp.exp(coarse - cmax), axis=0, keepdims=True)
    fine = jnp.zeros((EPG, TM), F32)
    for g in range(N_GROUPS):
        fine = jnp.where(grp == g, lt[8 + g * EPG:8 + (g + 1) * EPG], fine)
    ei = _iota2((EPG, TM), 0)
    f1 = jnp.max(fine, axis=0, keepdims=True)
    i1 = jnp.min(jnp.where(fine == f1, ei, EPG), axis=0, keepdims=True)
    rest = jnp.where(ei == i1, -jnp.inf, fine)
    f2 = jnp.max(rest, axis=0, keepdims=True)
    i2 = jnp.min(jnp.where(rest == f2, ei, EPG), axis=0, keepdims=True)
    e2 = jnp.exp(f2 - f1)
    w1 = p_sel / (1.0 + e2)
    w2 = p_sel * e2 / (1.0 + e2)
    x1 = grp * EPG + i1
    x2 = grp * EPG + i2

    er = _iota2((NE, TM), 0)
    oh1 = jnp.where(er == x1, 1.0, 0.0)
    oh2 = jnp.where(er == x2, 1.0, 0.0)
    oh = oh1 + oh2
    before = jnp.where(_iota2((TM, TM), 0) < _iota2((TM, TM), 1), 1.0, 0.0).astype(BF16)
    cum = jnp.dot(oh.astype(BF16), before, preferred_element_type=F32) + run_ref[:, 0:1]
    r1 = jnp.sum(oh1 * cum, axis=0, keepdims=True).astype(I32)
    r2 = jnp.sum(oh2 * cum, axis=0, keepdims=True).astype(I32)
    run_new = run_ref[...] + jnp.sum(oh, axis=1, keepdims=True)
    run_ref[...] = run_new
    cnt_ref[...] = run_new.astype(I32)

    row = _iota2((8, TM), 0)
    pos_ref[...] = jnp.where(row == 0, x1 * E_CAP + r1, jnp.where(row == 1, x2 * E_CAP + r2, 0))
    wrow = _iota2((128, TM), 0)
    rw_ref[...] = jnp.where(wrow == 0, w1, jnp.where(wrow == 1, w2, 0.0)).T


def _outproj_kernel(ycp_ref, ycs_ref, xp_ref, xs_ref, gtp_ref, gts_ref, shp_ref, shs_ref, scp_ref, scs_ref,
                    wo_hbm, g_ref, wr_ref, br_ref,
                    x1_ref, pos_ref, rw_ref, cnt_ref, xs_out,
                    run_ref, h2_scr, wo_ref, pos_v, pos_s, sem, psem, wsem):
    i = pl.program_id(0)
    par = i % 2

    @pl.when(i == 0)
    def _():
        def chunk(c):
            return pltpu.make_async_copy(wo_hbm.at[0, pl.ds(c * TM, TM), :], h2_scr.at[c % 2], wsem.at[c % 2])

        chunk(0).start()
        for c in range(D // TM):
            if c + 1 < D // TM:
                chunk(c + 1).start()
            chunk(c).wait()
            wo_ref[c * TM:(c + 1) * TM, :] = h2_scr[c % 2].astype(BF16)

    to_smem = pltpu.make_async_copy(pos_v, pos_s, psem)

    def scatter(p):
        for r in range(TM):
            for slot in range(2):
                pltpu.make_async_copy(h2_scr.at[p, pl.ds(r, 1), :], xs_out.at[pl.ds(pos_s[slot, r], 1), :],
                                      sem).start(priority=r % 2)

    def drain():
        for _ in range(2):
            pltpu.make_async_copy(h2_scr.at[0], xs_out.at[pl.ds(0, TM), :], sem).wait()

    @pl.when(i == 0)
    def _():
        run_ref[...] = jnp.zeros((NE, 128), F32)
        h2_scr[1] = jnp.zeros((TM, D), F32)
        pos_v[...] = E_ROWS + _iota2((8, TM), 0) * TM + _iota2((8, TM), 1)
        to_smem.start()

    to_smem.wait()

    def tile(yc, x, gt, sh, sc):
        scatter(1 - par)
        mix = jnp.dot(yc, wo_ref[...], preferred_element_type=F32)
        x1 = x + gt * mix
        x1_ref[...] = x1
        h2 = _rms(x1, g_ref[...]) * (1.0 + sc) + sh
        h2_scr[par] = h2
        _route(h2, wr_ref, br_ref, run_ref, pos_v, rw_ref, cnt_ref)

    @pl.when(i < N_TILES_P)
    def _():
        tile(ycp_ref[...], xp_ref[...], _prompt_row(gtp_ref), _prompt_row(shp_ref), _prompt_row(scp_ref))

    @pl.when(i >= N_TILES_P)
    def _():
        tile(ycs_ref[...], xs_ref[...], _sample_rows(gts_ref), _sample_rows(shs_ref), _sample_rows(scs_ref))

    pos_ref[0] = pos_v[...]
    drain()
    to_smem.start()

    @pl.when(i == N_TILES - 1)
    def _():
        to_smem.wait()
        scatter(par)
        drain()


def _outproj(ycp, ycs, xp, xs, mod, w_out, g_ffn, w_r, b_r):
    return pl.pallas_call(
        _outproj_kernel,
        out_shape=(jax.ShapeDtypeStruct((T, D), F32), jax.ShapeDtypeStruct((N_TILES, 8, TM), I32),
                   jax.ShapeDtypeStruct((T, 128), F32), jax.ShapeDtypeStruct((NE, 128), I32),
                   jax.ShapeDtypeStruct((E_ROWS + 2 * TM, D), F32)),
        grid=(N_TILES,),
        in_specs=[pl.BlockSpec((TM, D), lambda i: (_p_tile(i), 0)), pl.BlockSpec((TM, D), lambda i: (_s_tile(i), 0)),
                  pl.BlockSpec((TM, D), lambda i: (_p_tile(i), 0)), pl.BlockSpec((TM, D), lambda i: (_s_tile(i), 0)),
                  *_mod_specs(2), *_mod_specs(3), *_mod_specs(4),
                  pl.BlockSpec(memory_space=pl.ANY), _const_spec((1, D)), _const_spec((R_ROWS, D)),
                  _const_spec((R_ROWS, 1))],
        out_specs=(pl.BlockSpec((TM, D), lambda i: (i, 0)), pl.BlockSpec((1, 8, TM), lambda i: (i, 0, 0)),
                   pl.BlockSpec((TM, 128), lambda i: (i, 0)), pl.BlockSpec((NE, 128), lambda i: (0, 0)),
                   pl.BlockSpec(memory_space=pl.ANY)),
        scratch_shapes=[pltpu.VMEM((NE, 128), F32), pltpu.VMEM((2, TM, D), F32), pltpu.VMEM((D, D), BF16),
                        pltpu.VMEM((8, TM), I32), pltpu.SMEM((8, TM), I32), pltpu.SemaphoreType.DMA(()),
                        pltpu.SemaphoreType.DMA(()), pltpu.SemaphoreType.DMA((2,))],
        compiler_params=_cparams(),
        name="outproj",
    )(ycp, ycs, xp, xs, mod, mod, mod, mod, mod, mod, w_out, g_ffn, w_r, b_r)


E_CHUNK = 64
E_CHUNKS = E_TILE // E_CHUNK


def _plan_tiles(cnt_ref, tb_ref, te_ref, first_ref, nval_ref, nxt_ref, slot_ref, nt_ref, nxte_ref):
    def backward(k, next_occupied):
        e = NE - 1 - k
        nxte_ref[e] = next_occupied
        return jnp.where(cnt_ref[e, 0] > 0, e, next_occupied)

    lax.fori_loop(0, NE, backward, jnp.int32(-1))

    def forward(e, carry):
        t, rank = carry
        cnt = cnt_ref[e, 0]
        n = lax.shift_right_logical(cnt + (E_TILE - 1), E_TILE.bit_length() - 1)

        def tile(k, _):
            tb_ref[t + k] = e * E_CAP_TILES + k
            te_ref[t + k] = e
            first_ref[t + k] = (k == 0).astype(I32)
            nval_ref[t + k] = jnp.minimum(cnt - k * E_TILE, E_TILE)
            nxt_ref[t + k] = nxte_ref[e]
            slot_ref[t + k] = rank & 1
            return 0

        lax.fori_loop(0, n, tile, 0)
        return t + n, rank + (n > 0).astype(I32)

    n_tiles, _ = lax.fori_loop(0, NE, forward, (jnp.int32(0), jnp.int32(0)))
    nt_ref[0] = n_tiles


def _moe_kernel(cnt_ref, x_hbm, wg_hbm, wu_hbm, wd_hbm, y_hbm,
                xbuf, ybuf, wg_st, wu_st, wd_st, wgu_b, wd_b,
                tb_ref, te_ref, first_ref, nval_ref, nxt_ref, slot_ref, nt_ref, nxte_ref, wsem, xsem, ysem):
    i = pl.program_id(0)
    par = i % 2

    @pl.when(i == 0)
    def _():
        _plan_tiles(cnt_ref, tb_ref, te_ref, first_ref, nval_ref, nxt_ref, slot_ref, nt_ref, nxte_ref)

    nt = nt_ref[0]

    def weight_copies(e, s):
        return (pltpu.make_async_copy(wg_hbm.at[e], wg_st.at[s], wsem.at[s]),
                pltpu.make_async_copy(wu_hbm.at[e], wu_st.at[s], wsem.at[s]),
                pltpu.make_async_copy(wd_hbm.at[e], wd_st.at[s], wsem.at[s]))

    def tile_chunks(t, p, output, fn):
        row0 = tb_ref[t] * E_TILE
        for c in range(E_CHUNKS):
            rows = pl.ds(c * E_CHUNK, E_CHUNK)
            hbm_rows = pl.ds(row0 + c * E_CHUNK, E_CHUNK)

            @pl.when(c * E_CHUNK < nval_ref[t])
            def _():
                if output:
                    fn(pltpu.make_async_copy(ybuf.at[p, rows, :], y_hbm.at[hbm_rows, 0], ysem.at[p]))
                else:
                    fn(pltpu.make_async_copy(x_hbm.at[hbm_rows, :], xbuf.at[p, rows, :], xsem.at[p]))

    def start(cp):
        cp.start()

    def wait(cp):
        cp.wait()

    @pl.when(i == 0)
    def _():
        tile_chunks(0, 0, False, start)

    @pl.when(i + 1 < nt)
    def _():
        tile_chunks(i + 1, 1 - par, False, start)

    @pl.when(i < nt)
    def _():
        tile_chunks(i, par, False, wait)

        @pl.when(first_ref[i] == 1)
        def _():
            s = slot_ref[i]

            @pl.when(i == 0)
            def _():
                for cp in weight_copies(te_ref[0], 0):
                    cp.start(priority=1)

            for cp in weight_copies(te_ref[i], s):
                cp.wait()

            @pl.when(nxt_ref[i] >= 0)
            def _():
                for cp in weight_copies(nxt_ref[i], 1 - s):
                    cp.start(priority=1)

            wgu_b[:, 0:DE] = wg_st[s].astype(BF16)
            wgu_b[:, DE:2 * DE] = wu_st[s].astype(BF16)
            wd_b[...] = wd_st[s].astype(BF16)

        valid = _iota2((E_TILE, D), 0) < nval_ref[i]
        x = jnp.where(valid, xbuf[par], 0.0).astype(BF16)
        ab = jnp.dot(x, wgu_b[...], preferred_element_type=F32)
        hid = (_silu(ab[:, 0:DE]) * ab[:, DE:2 * DE]).astype(BF16)
        ybuf[par] = jnp.dot(hid, wd_b[...], preferred_element_type=F32)

        tile_chunks(i, par, True, start)

        @pl.when(i >= 1)
        def _():
            tile_chunks(i - 1, 1 - par, True, wait)

        @pl.when(i == nt - 1)
        def _():
            tile_chunks(i, par, True, wait)


def _moe(counts, xs, w_eg, w_eu, w_ed):
    tile_table = pltpu.SMEM((E_TILES_MAX,), I32)
    grid_spec = pltpu.PrefetchScalarGridSpec(
        num_scalar_prefetch=1,
        grid=(E_TILES_MAX,),
        in_specs=[pl.BlockSpec(memory_space=pl.ANY)] * 4,
        out_specs=pl.BlockSpec(memory_space=pl.ANY),
        scratch_shapes=[pltpu.VMEM((2, E_TILE, D), F32), pltpu.VMEM((2, E_TILE, D), F32),
                        pltpu.VMEM((2, D, DE), F32), pltpu.VMEM((2, D, DE), F32), pltpu.VMEM((2, DE, D), F32),
                        pltpu.VMEM((D, 2 * DE), BF16), pltpu.VMEM((DE, D), BF16),
                        tile_table, tile_table, tile_table, tile_table, tile_table, tile_table,
                        pltpu.SMEM((1,), I32), pltpu.SMEM((NE,), I32),
                        pltpu.SemaphoreType.DMA((2,)), pltpu.SemaphoreType.DMA((2,)), pltpu.SemaphoreType.DMA((2,))],
    )
    return pl.pallas_call(
        _moe_kernel,
        out_shape=jax.ShapeDtypeStruct((E_ROWS, 1, D), F32),
        grid_spec=grid_spec,
        compiler_params=_cparams(),
        name="moe",
    )(counts, xs, w_eg, w_eu, w_ed)


def _final_kernel(pos0_ref, posn_ref, x1_ref, rw_ref, gtp_ref, gts_ref, g_ref, ys_ref, yp_ref, ysm_ref, buf, sem):
    i = pl.program_id(0)
    par = i % 2

    def gather(p_ref, p):
        for r in range(TM):
            for slot in range(2):
                pltpu.make_async_copy(ys_ref.at[p_ref[0, slot, r]], buf.at[p, slot, pl.ds(r, 1), :],
                                      sem.at[p]).start(priority=r % 2)

    @pl.when(i == 0)
    def _():
        gather(pos0_ref, 0)

    @pl.when(i < N_TILES - 1)
    def _():
        gather(posn_ref, 1 - par)

    for slot in range(2):
        pltpu.make_async_copy(ys_ref.at[pl.ds(0, TM), 0], buf.at[par, slot], sem.at[par]).wait()

    moe = rw_ref[:, 0:1] * buf[par, 0] + rw_ref[:, 1:2] * buf[par, 1]

    @pl.when(i < N_TILES_P)
    def _():
        yp_ref[...] = _rms(x1_ref[...] + _prompt_row(gtp_ref) * moe, g_ref[...])

    @pl.when(i >= N_TILES_P)
    def _():
        ysm_ref[...] = _rms(x1_ref[...] + _sample_rows(gts_ref) * moe, g_ref[...])


def _final(pos, x1, rw, mod, g_fin, ys):
    return pl.pallas_call(
        _final_kernel,
        out_shape=(jax.ShapeDtypeStruct((T_P, D), F32), jax.ShapeDtypeStruct((T_S, D), F32)),
        grid=(N_TILES,),
        in_specs=[pl.BlockSpec((1, 8, TM), lambda i: (0, 0, 0), memory_space=pltpu.SMEM),
                  pl.BlockSpec((1, 8, TM), lambda i: (jnp.minimum(i + 1, N_TILES - 1), 0, 0),
                               memory_space=pltpu.SMEM),
                  pl.BlockSpec((TM, D), lambda i: (i, 0)),
                  pl.BlockSpec((TM, 128), lambda i: (i, 0)),
                  *_mod_specs(5),
                  _const_spec((1, D)),
                  pl.BlockSpec(memory_space=pl.ANY)],
        out_specs=(pl.BlockSpec((TM, D), lambda i: (_p_tile(i), 0)),
                   pl.BlockSpec((TM, D), lambda i: (_s_tile(i), 0))),
        scratch_shapes=[pltpu.VMEM((2, 2, TM, D), F32), pltpu.SemaphoreType.DMA((2,))],
        compiler_params=_cparams(),
        name="final",
    )(pos, pos, x1, rw, mod, mod, g_fin, ys)


def kernel(x_prompt, x_sample, c_prompt, c_sample, state_conv, state_gla, w_ada, b_ada, norm_mix, w_in, conv_w,
           conv_b, w_gate_up, b_gate, norm_conv, norm_gla, w_out, norm_ffn, w_coarse, b_coarse, w_fine, b_fine,
           w_exp_gate, w_exp_up, w_exp_down, norm_final):
    xp = x_prompt.reshape(T_P, D)
    xs = x_sample.reshape(T_S, D)
    c_all = jnp.concatenate([c_sample, c_prompt, jnp.zeros((C_ROWS - B_S - B_P, D), F32)], axis=0)
    mod = _ada(c_all, w_ada[0], b_ada[0][None, :])

    w_in_t = jnp.swapaxes(w_in, 1, 2)
    w_low_t = jnp.pad(w_in_t[0, PROJ_MAIN:, :], ((0, 128 - RANK), (0, 0))).astype(BF16)
    wgu_b = jnp.pad(w_gate_up[0], ((0, 128 - RANK), (0, 0))).astype(BF16)
    w_r = jnp.concatenate([w_coarse[0].T, jnp.zeros((8 - N_GROUPS, D), F32), w_fine[0].T], axis=0)
    b_r = jnp.concatenate([b_coarse[0], jnp.zeros((8 - N_GROUPS,), F32), b_fine[0]])[:, None]
    g_mix, g_ffn, g_fin = norm_mix[0][None, :], norm_ffn[0][None, :], norm_final[None, :]
    mix_consts = (conv_w[0], conv_b[0][None, :], wgu_b, b_gate[0][None, :], norm_conv[0][None, :],
                  norm_gla[0][None, :])

    ycat_p, nconv_p, nst_p, proj_s = _inmix(xp, xs, mod, g_mix, w_in_t, w_low_t, *mix_consts)
    ycat_s, nconv_s, nst_s = _mixer_s(proj_s, state_conv[0], state_gla[0].reshape(B_S, HK, DV), *mix_consts)

    x1, pos, rw, counts, x_sorted = _outproj(ycat_p, ycat_s, xp, xs, mod, w_out, g_ffn, w_r.astype(BF16), b_r)
    y_sorted = _moe(counts, x_sorted, w_exp_gate[0], w_exp_up[0], w_exp_down[0])
    y_p, y_s = _final(pos, x1, rw, mod, g_fin, y_sorted)

    new_gla_p = nst_p.reshape(B_P, DV, H, DK).transpose(0, 2, 3, 1)[None]
    new_gla_s = nst_s.reshape(1, B_S, H, DK, DV)
    return (y_p.reshape(B_P, L_P, D), y_s.reshape(B_S, L_S, D), nconv_p[None], new_gla_p,
            nconv_s[None], new_gla_s)
```
